```python
import math
import jax, jax.numpy as jnp
from jax import lax
import numpy as np

D_MODEL = 1024
BATCH = 8
SEQ = 2048
DEPTH = 2
DEC_BATCH = 128
DEC_SEQ = 4
PAST_LEN = 16384
PAGE_SIZE = 128

MIX = D_MODEL
HEAD_DIM = 64
A_W = MIX // 4
B_W = (3 * MIX) // 8
C_W = MIX - A_W - B_W
A_DK = HEAD_DIM
A_DV = HEAD_DIM
A_HEADS = A_W // A_DV
B_BD = HEAD_DIM
B_BLOCKS = B_W // B_BD
C_DK = HEAD_DIM
C_DV = HEAD_DIM
C_HEADS = C_W // C_DV
LRU_C = 8.0
CONV_W = 4
CHUNK = 64
DN_CONV_CH = 2 * C_HEADS * C_DK + C_W
IN_SPLITS = (A_HEADS * A_DK, A_HEADS * A_DK, A_W, A_W, B_W, B_W,
             C_HEADS * C_DK, C_HEADS * C_DK, C_W, C_W, C_HEADS, C_HEADS)
N_IN = sum(IN_SPLITS)
PEER_HEADS = 8
PEER_DQ = 128
PEER_NKEYS = 128
PEER_TOPK = 16
PEER_N = PEER_NKEYS * PEER_NKEYS
PEER_BLOCK = 256
EPS = 1e-6

kernel_name = 'hymba_hgrn2_rglru_gdn_peer_step'


def rmsnorm(x, g):
    xf = x.astype(jnp.float32)
    y = xf * lax.rsqrt(jnp.mean(xf * xf, axis=-1, keepdims=True) + EPS)
    return (y * g.astype(jnp.float32)).astype(x.dtype)


def gated_head_norm(o, z, g):
    of = o.astype(jnp.float32)
    y = of * lax.rsqrt(jnp.mean(of * of, axis=-1, keepdims=True) + EPS)
    return y * g.astype(jnp.float32) * jax.nn.silu(z.astype(jnp.float32))


def l2norm(x):
    return x * lax.rsqrt(jnp.sum(x * x, axis=-1, keepdims=True) + EPS)


def causal_conv(x, buf, w, b=None):
    xp = jnp.concatenate([buf.astype(x.dtype), x], axis=1)
    L = x.shape[1]
    y = xp[:, 0:L] * w[0]
    for k in range(1, CONV_W):
        y = y + xp[:, k:k + L] * w[k]
    if b is not None:
        y = y + b
    return y, xp[:, -(CONV_W - 1):]


def chunk_len(L):
    return CHUNK if L % CHUNK == 0 else L


def to_chunks(x, C):
    B, L = x.shape[:2]
    x = x.reshape((B, L // C, C) + x.shape[2:])
    if x.ndim == 5:
        return jnp.transpose(x, (1, 0, 3, 2, 4))
    return jnp.transpose(x, (1, 0, 3, 2))


def from_chunks(o):
    nc, B, H, C, d = o.shape
    return jnp.transpose(o, (1, 0, 3, 2, 4)).reshape(B, nc * C, H, d)


def hgrn2_chunked(q, k, v, log_f, S0):
    C = chunk_len(q.shape[1])
    mask = jnp.tril(jnp.ones((C, C), bool))[:, :, None]

    def step(S, inp):
        qc, kc, vc, lf = inp
        G = jnp.cumsum(lf, axis=2)
        Gl = G[:, :, -1]
        o_inter = jnp.einsum('bhik,bhkv->bhiv', qc * jnp.exp(G), S)
        diff = G[:, :, :, None, :] - G[:, :, None, :, :]
        decay = jnp.exp(jnp.where(mask, diff, -jnp.inf))
        A = jnp.einsum('bhik,bhjk,bhijk->bhij', qc, kc, decay)
        o = o_inter + jnp.einsum('bhij,bhjv->bhiv', A, vc)
        S = S * jnp.exp(Gl)[..., None] + jnp.einsum('bhjk,bhjv->bhkv', kc * jnp.exp(Gl[:, :, None, :] - G), vc)
        return S, o

    S, o = lax.scan(step, S0, (to_chunks(q, C), to_chunks(k, C), to_chunks(v, C), to_chunks(log_f, C)))
    return from_chunks(o), S


def delta_chunked(q, k, v, beta, g, S0):
    C = chunk_len(q.shape[1])
    strict = jnp.tril(jnp.ones((C, C), bool), -1)
    incl = jnp.tril(jnp.ones((C, C), bool))
    eye = jnp.eye(C, dtype=jnp.float32)

    def step(S, inp):
        qc, kc, vc, bc, gc = inp
        G = jnp.cumsum(gc, axis=-1)
        eG = jnp.exp(G)
        diff = G[..., :, None] - G[..., None, :]
        dec_s = jnp.exp(jnp.where(strict, diff, -jnp.inf))
        dec_i = jnp.exp(jnp.where(incl, diff, -jnp.inf))
        M = eye + bc[..., :, None] * jnp.einsum('bhik,bhjk->bhij', kc, kc) * dec_s
        rhs = bc[..., None] * (vc - eG[..., None] * jnp.einsum('bhik,bhkv->bhiv', kc, S))
        E = lax.linalg.triangular_solve(M, rhs, left_side=True, lower=True)
        qk = jnp.einsum('bhik,bhjk->bhij', qc, kc) * dec_i
        o = eG[..., None] * jnp.einsum('bhik,bhkv->bhiv', qc, S) + jnp.einsum('bhij,bhjv->bhiv', qk, E)
        Gl = G[..., -1]
        S = jnp.exp(Gl)[..., None, None] * S + jnp.einsum('bhjk,bhjv->bhkv', kc * jnp.exp(Gl[..., None] - G)[..., None], E)
        return S, o

    S, o = lax.scan(step, S0, (to_chunks(q, C), to_chunks(k, C), to_chunks(v, C), to_chunks(beta, C), to_chunks(g, C)))
    return from_chunks(o), S


def rglru_scan(a, b, h0):
    b = b.at[:, 0].add(a[:, 0] * h0)

    def comb(l, r):
        return (l[0] * r[0], r[0] * l[1] + r[1])

    _, h = lax.associative_scan(comb, (a, b), axis=1)
    return h, h[:, -1]


def split_cols(z):
    offs = [int(o) for o in np.cumsum(IN_SPLITS)[:-1]]
    return jnp.split(z, offs, axis=-1)


def mixer(h, st, lb, p):
    f32 = jnp.float32
    Bsz, L, _ = h.shape
    dt = h.dtype
    z = h @ p['w_in']
    aq, af, ai, ag, bx, by, cq, ck, cv, cz, cb, ca = split_cols(z)
    f = lb + (1.0 - lb) * jax.nn.sigmoid(af.astype(f32))
    qa = (jax.nn.silu(aq.astype(f32)) * A_DK ** -0.5).reshape(Bsz, L, A_HEADS, A_DK)
    ka = (1.0 - f).reshape(Bsz, L, A_HEADS, A_DK)
    lfa = jnp.log(f).reshape(Bsz, L, A_HEADS, A_DK)
    va = ai.astype(f32).reshape(Bsz, L, A_HEADS, A_DV)
    oa, S_a = hgrn2_chunked(qa, ka, va, lfa, st[0].astype(f32))
    oa = gated_head_norm(oa, ag.reshape(Bsz, L, A_HEADS, A_DV), p['a_norm_g']).reshape(Bsz, L, A_W).astype(dt)
    xc, buf_b = causal_conv(bx, st[2], p['lru_conv_w'], p['lru_conv_b'])
    xcf = xc.astype(f32)
    xblk = xcf.reshape(Bsz, L, B_BLOCKS, B_BD)
    r = jax.nn.sigmoid(jnp.einsum('blnd,nde->blne', xblk, p['lru_wa'].astype(f32)).reshape(Bsz, L, B_W) + p['lru_ba'])
    ig = jax.nn.sigmoid(jnp.einsum('blnd,nde->blne', xblk, p['lru_wx'].astype(f32)).reshape(Bsz, L, B_W) + p['lru_bx'])
    log_a = -LRU_C * r * jax.nn.softplus(-p['lru_L'].astype(f32))
    a = jnp.exp(log_a)
    bt = jnp.sqrt(-jnp.expm1(2.0 * log_a)) * ig * xcf
    hs, hT = rglru_scan(a, bt, st[1].astype(f32))
    ob = (hs * jax.nn.gelu(by.astype(f32))).astype(dt)
    qkv, buf_c = causal_conv(jnp.concatenate([cq, ck, cv], axis=-1), st[4], p['dn_conv_w'])
    qkv = jax.nn.silu(qkv.astype(f32))
    qc, kc, vc = jnp.split(qkv, [C_HEADS * C_DK, 2 * C_HEADS * C_DK], axis=-1)
    qc = l2norm(qc.reshape(Bsz, L, C_HEADS, C_DK)) * C_DK ** -0.5
    kc = l2norm(kc.reshape(Bsz, L, C_HEADS, C_DK))
    vc = vc.reshape(Bsz, L, C_HEADS, C_DV)
    beta = jax.nn.sigmoid(cb.astype(f32))
    gdec = -jnp.exp(p['dn_A_log'].astype(f32)) * jax.nn.softplus(ca.astype(f32) + p['dn_dt_bias'])
    oc, S_c = delta_chunked(qc, kc, vc, beta, gdec, st[3].astype(f32))
    oc = gated_head_norm(oc, cz.reshape(Bsz, L, C_HEADS, C_DV), p['dn_norm_g']).reshape(Bsz, L, C_W).astype(dt)
    out = jnp.concatenate([oa, ob, oc], axis=-1) @ p['w_out']
    new_st = (S_a.astype(st[0].dtype), hT.astype(st[1].dtype), buf_b.astype(st[2].dtype),
              S_c.astype(st[3].dtype), buf_c.astype(st[4].dtype))
    return out, new_st


def peer(h, wq, k1, k2, u, v):
    Bsz, L, D = h.shape
    T = Bsz * L
    nblk = -(-T // PEER_BLOCK)
    hb = jnp.pad(h.reshape(T, D), ((0, nblk * PEER_BLOCK - T), (0, 0))).reshape(nblk, PEER_BLOCK, D)
    half = PEER_DQ // 2

    def blk(xb):
        q = (xb @ wq).reshape(-1, PEER_HEADS, 2, half)
        s1 = jnp.einsum('thd,nd->thn', q[:, :, 0], k1)
        s2 = jnp.einsum('thd,nd->thn', q[:, :, 1], k2)
        v1, i1 = lax.top_k(s1, PEER_TOPK)
        v2, i2 = lax.top_k(s2, PEER_TOPK)
        cand = (v1[..., :, None] + v2[..., None, :]).reshape(-1, PEER_HEADS, PEER_TOPK * PEER_TOPK)
        cidx = (i1[..., :, None] * PEER_NKEYS + i2[..., None, :]).reshape(-1, PEER_HEADS, PEER_TOPK * PEER_TOPK)
        sc, pos = lax.top_k(cand, PEER_TOPK)
        idx = jnp.take_along_axis(cidx, pos, axis=-1)
        gw = jax.nn.softmax(sc.astype(jnp.float32), axis=-1)
        act = jax.nn.gelu(jnp.einsum('thkd,td->thk', jnp.take(u, idx, axis=0), xb).astype(jnp.float32))
        return jnp.einsum('thk,thkd->td', (gw * act).astype(xb.dtype), jnp.take(v, idx, axis=0))

    y = lax.map(blk, hb).reshape(-1, D)[:T]
    return y.reshape(Bsz, L, D)


def trunk(x, c, states, stk, lb_all, final_norm_g):
    new = [[] for _ in range(5)]
    for l in range(DEPTH):
        p = {name: arr[l] for name, arr in stk.items()}
        mod = jax.nn.silu(c) @ p['w_ada'] + p['b_ada']
        sh1, sc1, gt1, sh2, sc2, gt2 = jnp.split(mod, 6, axis=-1)
        h = rmsnorm(x, p['norm1_g']) * (1.0 + sc1[:, None]) + sh1[:, None]
        mo, st = mixer(h, tuple(s[l] for s in states), lb_all[l], p)
        x = x + gt1[:, None] * mo
        h = rmsnorm(x, p['norm2_g']) * (1.0 + sc2[:, None]) + sh2[:, None]
        x = x + gt2[:, None] * peer(h, p['peer_wq'], p['peer_k1'], p['peer_k2'], p['peer_u'], p['peer_v'])
        for j in range(5):
            new[j].append(st[j])
    y = rmsnorm(x, final_norm_g)
    return y, [jnp.stack(n) for n in new]


def setup_inputs(seed: int = 0) -> dict:
    key = jax.random.key(seed)
    ks = iter(jax.random.split(key, 48))
    f32 = jnp.float32

    def nrm(shape, s):
        return jax.random.normal(next(ks), shape, f32) * s

    a0 = jax.random.uniform(next(ks), (DEPTH, B_W), f32, 0.9, 0.999)
    sa = a0 ** (1.0 / LRU_C)
    dtv = jnp.exp(jax.random.uniform(next(ks), (DEPTH, C_HEADS), f32, math.log(1e-3), math.log(0.1)))
    return {
        'x_prompt': nrm((BATCH, SEQ, D_MODEL), 1.0),
        'x_sample': nrm((DEC_BATCH, DEC_SEQ, D_MODEL), 1.0),
        'state_hgrn': nrm((DEPTH, DEC_BATCH, A_HEADS, A_DK, A_DV), 0.5),
        'state_lru_h': nrm((DEPTH, DEC_BATCH, B_W), 0.5),
        'state_lru_conv': nrm((DEPTH, DEC_BATCH, CONV_W - 1, B_W), 1.0),
        'state_dn': nrm((DEPTH, DEC_BATCH, C_HEADS, C_DK, C_DV), 0.3),
        'state_dn_conv': nrm((DEPTH, DEC_BATCH, CONV_W - 1, DN_CONV_CH), 1.0),
        'c_prompt': nrm((BATCH, D_MODEL), 1.0),
        'c_sample': nrm((DEC_BATCH, D_MODEL), 1.0),
        'w_ada': nrm((DEPTH, D_MODEL, 6 * D_MODEL), 0.5 * D_MODEL ** -0.5),
        'b_ada': nrm((DEPTH, 6 * D_MODEL), 0.01),
        'norm1_g': 1.0 + nrm((DEPTH, D_MODEL), 0.01),
        'norm2_g': 1.0 + nrm((DEPTH, D_MODEL), 0.01),
        'w_in': nrm((DEPTH, D_MODEL, N_IN), D_MODEL ** -0.5),
        'lb_param': nrm((DEPTH, A_HEADS * A_DK), 0.1),
        'a_norm_g': 1.0 + nrm((DEPTH, A_DV), 0.01),
        'lru_conv_w': nrm((DEPTH, CONV_W, B_W), CONV_W ** -0.5),
        'lru_conv_b': nrm((DEPTH, B_W), 0.01),
        'lru_wa': nrm((DEPTH, B_BLOCKS, B_BD, B_BD), B_BD ** -0.5),
        'lru_ba': nrm((DEPTH, B_W), 0.01),
        'lru_wx': nrm((DEPTH, B_BLOCKS, B_BD, B_BD), B_BD ** -0.5),
        'lru_bx': nrm((DEPTH, B_W), 0.01),
        'lru_L': jnp.log(sa) - jnp.log1p(-sa),
        'dn_conv_w': nrm((DEPTH, CONV_W, DN_CONV_CH), CONV_W ** -0.5),
        'dn_A_log': jnp.log(jax.random.uniform(next(ks), (DEPTH, C_HEADS), f32, 1.0, 16.0)),
        'dn_dt_bias': dtv + jnp.log(-jnp.expm1(-dtv)),
        'dn_norm_g': 1.0 + nrm((DEPTH, C_DV), 0.01),
        'w_out': nrm((DEPTH, MIX, D_MODEL), MIX ** -0.5),
        'peer_wq': nrm((DEPTH, D_MODEL, PEER_HEADS * PEER_DQ), D_MODEL ** -0.5),
        'peer_k1': nrm((DEPTH, PEER_NKEYS, PEER_DQ // 2), (PEER_DQ // 2) ** -0.5),
        'peer_k2': nrm((DEPTH, PEER_NKEYS, PEER_DQ // 2), (PEER_DQ // 2) ** -0.5),
        'peer_u': nrm((DEPTH, PEER_N, D_MODEL), D_MODEL ** -0.5),
        'peer_v': nrm((DEPTH, PEER_N, D_MODEL), PEER_HEADS ** -0.5),
        'final_norm_g': 1.0 + nrm((D_MODEL,), 0.01),
    }


def reference(x_prompt, x_sample, state_hgrn, state_lru_h, state_lru_conv, state_dn, state_dn_conv,
              c_prompt, c_sample, w_ada, b_ada, norm1_g, norm2_g, w_in, lb_param, a_norm_g,
              lru_conv_w, lru_conv_b, lru_wa, lru_ba, lru_wx, lru_bx, lru_L,
              dn_conv_w, dn_A_log, dn_dt_bias, dn_norm_g, w_out,
              peer_wq, peer_k1, peer_k2, peer_u, peer_v, final_norm_g):
    stk = dict(w_ada=w_ada, b_ada=b_ada, norm1_g=norm1_g, norm2_g=norm2_g, w_in=w_in,
               a_norm_g=a_norm_g, lru_conv_w=lru_conv_w, lru_conv_b=lru_conv_b,
               lru_wa=lru_wa, lru_ba=lru_ba, lru_wx=lru_wx, lru_bx=lru_bx, lru_L=lru_L,
               dn_conv_w=dn_conv_w, dn_A_log=dn_A_log, dn_dt_bias=dn_dt_bias, dn_norm_g=dn_norm_g,
               w_out=w_out, peer_wq=peer_wq, peer_k1=peer_k1, peer_k2=peer_k2,
               peer_u=peer_u, peer_v=peer_v)
    lbs = jax.nn.softmax(lb_param.astype(jnp.float32), axis=0)
    lb_all = jnp.cumsum(lbs, axis=0) - lbs[0]
    sample_states = (state_hgrn, state_lru_h, state_lru_conv, state_dn, state_dn_conv)
    prompt_states = tuple(jnp.zeros((DEPTH, BATCH) + s.shape[2:], s.dtype) for s in sample_states)
    y_prompt, pst = trunk(x_prompt, c_prompt, prompt_states, stk, lb_all, final_norm_g)
    y_sample, sst = trunk(x_sample, c_sample, sample_states, stk, lb_all, final_norm_g)
    p_hgrn, p_lru_h, p_lru_conv, p_dn, p_dn_conv = pst
    s_hgrn, s_lru_h, s_lru_conv, s_dn, s_dn_conv = sst
    return (y_prompt, y_sample, p_hgrn, p_lru_h, p_lru_conv, p_dn, p_dn_conv,
            s_hgrn, s_lru_h, s_lru_conv, s_dn, s_dn_conv)
```

```python
import functools

import jax
import jax.numpy as jnp
import numpy as np
from jax import lax
from jax.experimental import pallas as pl
from jax.experimental.pallas import tpu as pltpu

F32 = jnp.float32
BF16 = jnp.bfloat16
HIGHEST = lax.Precision.HIGHEST

D_MODEL = 1024
DEPTH = 2
HEAD_DIM = 64
A_HEADS = 4
A_W = 256
B_W = 384
C_HEADS = 6
C_W = 384
LRU_C = 8.0
CONV_W = 4
N_IN = 3340
N_IN_PAD = 3456
ZA_W = 1024
ZB_W = 768
ZC_W = 1664
PEER_HEADS = 8
PEER_NKEYS = 128
PEER_TOPK = 16
PEER_N = PEER_NKEYS * PEER_NKEYS
EPS = 1e-6

SUPER_BLOCK = 256
SAMPLE_SEQ_ROWS = 8
SAMPLE_HIST = CONV_W - 1
TOKEN_TILE = 256
PEER_TOKEN_TILE = 512
PEER_EXPERT_BLOCK = 2048
VMEM_LIMIT_BYTES = 56 * 1024 * 1024

_NN = (((1,), (0,)), ((), ()))
_NT = (((1,), (1,)), ((), ()))
_TN = (((0,), (0,)), ((), ()))


def _dot(a, b, dims=_NN, precision=None):
    return lax.dot_general(a, b, dims, precision=precision, preferred_element_type=F32)


def _split2(x):
    hi = x.astype(BF16)
    lo = (x - hi.astype(F32)).astype(BF16)
    return hi, lo


def _split3(x):
    hi = x.astype(BF16)
    r = x - hi.astype(F32)
    mid = r.astype(BF16)
    lo = (r - mid.astype(F32)).astype(BF16)
    return hi, mid, lo


def _dot_x3(a, b, dims=_NN):
    ah, al = _split2(a)
    bh, bl = _split2(b)
    return _dot(ah, bh, dims) + _dot(ah, bl, dims) + _dot(al, bh, dims)


def _dot_w(a, wh, wl, dims=_NN):
    ah, al = _split2(a)
    return _dot(ah, wh, dims) + _dot(al, wh, dims) + _dot(ah, wl, dims)


def _dot_sel(a, sel, dims=_NN, passes=3):
    parts = _split3(a)[:passes]
    out = _dot(parts[0], sel, dims)
    for p in parts[1:]:
        out = out + _dot(p, sel, dims)
    return out


def _softplus(x):
    return jnp.maximum(x, 0.0) + jnp.log(1.0 + jnp.exp(-jnp.abs(x)))


def _silu(x):
    return x * jax.nn.sigmoid(x)


def _gelu(x):
    return 0.5 * x * (1.0 + jnp.tanh(np.sqrt(2.0 / np.pi).astype(np.float32) * (x + 0.044715 * (x * x * x))))


def _rmsnorm(x, g):
    return x * lax.rsqrt(jnp.mean(x * x, axis=-1, keepdims=True) + EPS) * g


def _const_spec(shape):
    nd = len(shape)
    return pl.BlockSpec(shape, lambda *_: (0,) * nd)


def _params(n_grid):
    return pltpu.CompilerParams(dimension_semantics=("arbitrary",) * n_grid,
                                vmem_limit_bytes=VMEM_LIMIT_BYTES)


def _ada_kernel(c_ref, w_ref, b_ref, o_ref):
    c = c_ref[...]
    o_ref[0] = _dot(_silu(c), w_ref[0], precision=HIGHEST) + b_ref[0]


def _ada_call(c_all, w_ada, b_ada):
    nb = c_all.shape[0]
    nt = 6 * D_MODEL // 1024
    return pl.pallas_call(
        _ada_kernel,
        grid=(DEPTH, nt),
        in_specs=[
            _const_spec((nb, D_MODEL)),
            pl.BlockSpec((1, D_MODEL, 1024), lambda l, j: (l, 0, j)),
            pl.BlockSpec((1, 1, 1024), lambda l, j: (l, 0, j)),
        ],
        out_specs=pl.BlockSpec((1, nb, 1024), lambda l, j: (l, 0, j)),
        out_shape=jax.ShapeDtypeStruct((DEPTH, nb, 6 * D_MODEL), F32),
        compiler_params=_params(2),
        name="ada_mod",
    )(c_all, w_ada, b_ada.reshape(DEPTH, 1, 6 * D_MODEL))


def _inproj_kernel(x_ref, g_ref, sc_ref, sh_ref, wh_ref, wl_ref, za_ref, zb_ref, zc_ref):
    h = _rmsnorm(x_ref[...], g_ref[...]) * sc_ref[...] + sh_ref[...]
    hh, hl = _split2(h)
    for o_ref, lo, hi in ((za_ref, 0, ZA_W), (zb_ref, ZA_W, ZA_W + ZB_W), (zc_ref, ZA_W + ZB_W, N_IN_PAD)):
        wh = wh_ref[:, lo:hi]
        wl = wl_ref[:, lo:hi]
        o_ref[...] = _dot(hh, wh) + _dot(hl, wh) + _dot(hh, wl)


def _mod_spec(mod, tiles_per_group):
    _, r, w = mod.shape
    return pl.BlockSpec((None, r, w), lambda i: (i // tiles_per_group, 0, 0))


def _inproj_call(x, g, sc, sh, wh, wl, tiles_per_group):
    t = x.shape[0]
    tm = TOKEN_TILE
    row = lambda w: pl.BlockSpec((tm, w), lambda i: (i, 0))
    return pl.pallas_call(
        _inproj_kernel,
        grid=(t // tm,),
        in_specs=[row(D_MODEL), _const_spec((1, D_MODEL)), _mod_spec(sc, tiles_per_group),
                  _mod_spec(sh, tiles_per_group), _const_spec(wh.shape), _const_spec(wl.shape)],
        out_specs=[row(ZA_W), row(ZB_W), row(ZC_W)],
        out_shape=[jax.ShapeDtypeStruct((t, ZA_W), F32), jax.ShapeDtypeStruct((t, ZB_W), F32),
                   jax.ShapeDtypeStruct((t, ZC_W), F32)],
        compiler_params=_params(1),
        name="inproj",
    )(x, g, sc, sh, wh, wl)


class _SeqCfg:
    def __init__(self, seq_rows, hist, real, chunk):
        self.seq_rows = seq_rows
        self.hist = hist
        self.real = real
        self.chunk = chunk
        self.masked = hist > 0 or hist + real < seq_rows
        self.n_chunks = SUPER_BLOCK // chunk
        self.long_seq = seq_rows > SUPER_BLOCK
        self.sb_per_seq = max(seq_rows // SUPER_BLOCK, 1)
        self.seq_per_sb = max(SUPER_BLOCK // seq_rows, 1)
        assert self.long_seq or chunk == seq_rows


PROMPT_CFG = _SeqCfg(seq_rows=2048, hist=0, real=2048, chunk=16)
SAMPLE_CFG = _SeqCfg(seq_rows=SAMPLE_SEQ_ROWS, hist=SAMPLE_HIST, real=4, chunk=SAMPLE_SEQ_ROWS)


def _row_in_seq(cfg, n_rows):
    r = lax.broadcasted_iota(jnp.int32, (n_rows, 1), 0)
    return r % min(cfg.seq_rows, SUPER_BLOCK)


def _real_mask(cfg, n_rows):
    r = _row_in_seq(cfg, n_rows)
    return (r >= cfg.hist) & (r < cfg.hist + cfg.real)


def _causal_conv(x, prev_ref, w_ref):
    n = x.shape[0]
    ext = jnp.concatenate([prev_ref[...], x], axis=0)
    y = x * w_ref[CONV_W - 1:CONV_W, :]
    for k in range(CONV_W - 1):
        s = CONV_W - 1 - k
        y = y + ext[8 - s:8 - s + n, :] * w_ref[k:k + 1, :]
    prev_ref[...] = x[n - 8:n, :]
    return y


def _expand_state(s_cat, tile_ref, bd_ref):
    return _dot_sel(s_cat, tile_ref[...]) * bd_ref[...]


def _compress_state(s_bd, tile_t_ref):
    return _dot_sel(s_bd, tile_t_ref[...])


def _head_norm_gate(o, z, g, ones_ref):
    ms = _dot_sel(o * o, ones_ref[...], passes=2) * (1.0 / HEAD_DIM)
    return o * lax.rsqrt(ms + EPS) * g * _silu(z)


def _state_io(cfg, sb, k, st_ref, s0_ref, load):
    if cfg.long_seq:
        edge = 0 if load else cfg.sb_per_seq - 1
        kedge = 0 if load else cfg.n_chunks - 1
        return jnp.logical_and(sb % cfg.sb_per_seq == edge, k == kedge), 0
    return None, k


def _hgrn_kernel(z_ref, lbp_ref, g_ref, s0_ref, ltri_ref, ones_ref, tile_ref, tile_t_ref, bd_ref,
                 o_ref, sout_ref,
                 st_ref, q_s, k_s, g_s, qe_s, kt_s, egl_s, o_s, *, cfg, layer):
    sb = pl.program_id(0)
    c = cfg.chunk
    n = SUPER_BLOCK
    w = A_W
    lbp = lbp_ref[...]
    e = jnp.exp(lbp - jnp.max(lbp, axis=0, keepdims=True))
    lbs = e / jnp.sum(e, axis=0, keepdims=True)
    lb = jnp.sum(lbs[0:layer + 1, :], axis=0, keepdims=True) - lbs[0:1, :]

    aq = z_ref[:, 0:w]
    af = z_ref[:, w:2 * w]
    f = lb + (1.0 - lb) * jax.nn.sigmoid(af)
    lf = jnp.log(f)
    kk = 1.0 - f
    if cfg.masked:
        real = _real_mask(cfg, n)
        lf = jnp.where(real, lf, 0.0)
        kk = jnp.where(real, kk, 0.0)
    q = _silu(aq) * (HEAD_DIM ** -0.5)
    gcum = _dot(ltri_ref[...], lf, precision=HIGHEST)
    g3 = gcum.reshape(cfg.n_chunks, c, w)
    gl3 = g3[:, c - 1:c, :]
    q_s[...] = q
    k_s[...] = kk
    g_s[...] = gcum
    qe_s[...] = q * jnp.exp(gcum)
    kt_s[...] = (kk.reshape(cfg.n_chunks, c, w) * jnp.exp(gl3 - g3)).reshape(n, w)
    egl_s[...] = jnp.exp(gl3).reshape(cfg.n_chunks, w)

    jio = lax.broadcasted_iota(jnp.int32, (c, 1), 0)
    ones = ones_ref[...]
    bd = bd_ref[...]

    def chunk(k, carry):
        r0 = pl.multiple_of(k * c, c)
        rows = pl.ds(r0, c)
        pred, idx = _state_io(cfg, sb, k, st_ref, s0_ref, True)
        if pred is None:
            st_ref[...] = _expand_state(s0_ref[idx], tile_ref, bd_ref)
        else:
            @pl.when(pred)
            def _():
                st_ref[...] = _expand_state(s0_ref[idx], tile_ref, bd_ref)
        st = st_ref[...]
        q_c = q_s[rows, :]
        k_c = k_s[rows, :]
        g_c = g_s[rows, :]
        v_c = z_ref[rows, 2 * w:3 * w]
        o_inter = _dot(qe_s[rows, :].astype(BF16), st.astype(BF16), _NT)
        d_rows = []
        for i in range(c):
            dec = jnp.exp(jnp.minimum(g_c[i:i + 1, :] - g_c, 0.0))
            d_rows.append(jnp.where(jio <= i, k_c * dec * q_c[i:i + 1, :], 0.0))
        d = jnp.concatenate(d_rows, axis=0)
        a_b = _dot_sel(d, ones, passes=2)
        o_diag = jnp.sum(a_b.reshape(c, c, w) * v_c[None, :, :], axis=1)
        o_s[rows, :] = o_inter + o_diag
        upd = _dot_x3(v_c, kt_s[rows, :], _TN)
        st_new = st * egl_s[pl.ds(k, 1), :] + bd * upd
        st_ref[...] = st_new
        pred, idx = _state_io(cfg, sb, k, st_ref, s0_ref, False)
        if pred is None:
            sout_ref[idx] = _compress_state(st_new, tile_t_ref)
        else:
            @pl.when(pred)
            def _():
                sout_ref[idx] = _compress_state(st_new, tile_t_ref)
        return carry

    lax.fori_loop(0, cfg.n_chunks, chunk, 0)
    o_ref[...] = _head_norm_gate(o_s[...], z_ref[:, 3 * w:4 * w], g_ref[...], ones_ref)


def _mixer_consts(heads, cfg):
    w = heads * HEAD_DIM
    lane_head = np.arange(w) // HEAD_DIM
    ones = (lane_head[:, None] == lane_head[None, :]).astype(np.float32)
    tile = (np.arange(HEAD_DIM)[:, None] == (np.arange(w) % HEAD_DIM)[None, :]).astype(np.float32)
    r = np.arange(SUPER_BLOCK)
    ltri = ((r[:, None] // cfg.chunk == r[None, :] // cfg.chunk) & (r[None, :] <= r[:, None])).astype(np.float32)
    return dict(ltri=jnp.asarray(ltri, F32), ones=jnp.asarray(ones, BF16), tile=jnp.asarray(tile, BF16),
                tile_t=jnp.asarray(tile.T, BF16), bd=jnp.asarray(ones, F32))


def _state_specs(cfg, w):
    blk = (1 if cfg.long_seq else cfg.seq_per_sb, w, HEAD_DIM)
    if cfg.long_seq:
        imap = lambda i: (i // cfg.sb_per_seq, 0, 0)
    else:
        imap = lambda i: (i, 0, 0)
    return pl.BlockSpec(blk, imap)


def _hgrn_call(z_a, lb_param, g_exp, s0_t, cfg, layer):
    rows = z_a.shape[0]
    w = A_W
    cst = _mixer_consts(A_HEADS, cfg)
    n_seq = s0_t.shape[0]
    vm = lambda shape: pltpu.VMEM(shape, F32)
    kern = functools.partial(_hgrn_kernel, cfg=cfg, layer=layer)
    return pl.pallas_call(
        kern,
        grid=(rows // SUPER_BLOCK,),
        in_specs=[pl.BlockSpec((SUPER_BLOCK, ZA_W), lambda i: (i, 0)), _const_spec((DEPTH, w)),
                  _const_spec((1, w)), _state_specs(cfg, w), _const_spec(cst["ltri"].shape),
                  _const_spec(cst["ones"].shape), _const_spec(cst["tile"].shape),
                  _const_spec(cst["tile_t"].shape), _const_spec(cst["bd"].shape)],
        out_specs=[pl.BlockSpec((SUPER_BLOCK, w), lambda i: (i, 0)), _state_specs(cfg, w)],
        out_shape=[jax.ShapeDtypeStruct((rows, w), F32), jax.ShapeDtypeStruct((n_seq, w, HEAD_DIM), F32)],
        scratch_shapes=[vm((w, w))] + [vm((SUPER_BLOCK, w))] * 5 + [vm((cfg.n_chunks, w)), vm((SUPER_BLOCK, w))],
        compiler_params=_params(1),
        name="hgrn2",
    )(z_a, lb_param, g_exp, s0_t, cst["ltri"], cst["ones"], cst["tile"], cst["tile_t"], cst["bd"])


def _lru_kernel(*refs, cfg):
    if cfg.long_seq:
        (z_ref, cw_ref, cb_ref, wgh_ref, wgl_ref, ba_ref, bx_ref, lp_ref,
         ob_ref, hs_ref, prev_ref, hc_ref) = refs
        inj_ref = None
    else:
        (z_ref, inj_ref, cw_ref, cb_ref, wgh_ref, wgl_ref, ba_ref, bx_ref, lp_ref,
         ob_ref, hs_ref, prev_ref, hc_ref) = refs
    sb = pl.program_id(0)
    n = SUPER_BLOCK
    w = B_W

    @pl.when(sb % cfg.sb_per_seq == 0)
    def _():
        prev_ref[...] = jnp.zeros_like(prev_ref)
        hc_ref[...] = jnp.zeros_like(hc_ref)

    xc = _causal_conv(z_ref[:, 0:w], prev_ref, cw_ref) + cb_ref[...]
    gates = _dot_w(xc, wgh_ref[...], wgl_ref[...])
    r = jax.nn.sigmoid(gates[:, 0:w] + ba_ref[...])
    ig = jax.nn.sigmoid(gates[:, w:2 * w] + bx_ref[...])
    log_a = -LRU_C * r * _softplus(-lp_ref[...])
    a = jnp.exp(log_a)
    b = jnp.sqrt(1.0 - jnp.exp(2.0 * log_a)) * ig * xc
    if cfg.masked:
        real = _real_mask(cfg, n)
        a = jnp.where(real, a, 1.0)
        b = jnp.where(real, b, 0.0)
    if inj_ref is not None:
        b = b + inj_ref[...]
    ris = _row_in_seq(cfg, n)
    d = 1
    while d < min(n, cfg.seq_rows):
        has = ris >= d
        a_sh = jnp.where(has, pltpu.roll(a, d, 0), 1.0)
        b_sh = jnp.where(has, pltpu.roll(b, d, 0), 0.0)
        b = b + a * b_sh
        a = a * a_sh
        d *= 2
    hs = b + a * hc_ref[...]
    if cfg.long_seq:
        hc_ref[...] = hs[n - 1:n, :]
    hs_ref[...] = hs
    ob_ref[...] = hs * _gelu(z_ref[:, w:2 * w])


def _lru_call(z_b, inj, cw, cb, wgh, wgl, ba, bx, lp, cfg):
    rows = z_b.shape[0]
    w = B_W
    row = lambda width: pl.BlockSpec((SUPER_BLOCK, width), lambda i: (i, 0))
    ins = [z_b] + ([] if cfg.long_seq else [inj]) + [cw, cb, wgh, wgl, ba, bx, lp]
    specs = [row(ZB_W)] + ([] if cfg.long_seq else [row(w)]) + [_const_spec(a.shape) for a in ins[-7:]]
    return pl.pallas_call(
        functools.partial(_lru_kernel, cfg=cfg),
        grid=(rows // SUPER_BLOCK,),
        in_specs=specs,
        out_specs=[row(w), row(w)],
        out_shape=[jax.ShapeDtypeStruct((rows, w), F32)] * 2,
        scratch_shapes=[pltpu.VMEM((8, w), F32), pltpu.VMEM((1, w), F32)],
        compiler_params=_params(1),
        name="rglru",
    )(*ins)


def _dn_kernel(z_ref, cw_ref, alog_ref, dtb_ref, g_ref, s0_ref, ltri_ref, ones_ref, tile_ref, tile_t_ref,
               bd_ref, expb_ref, expa_ref,
               o_ref, sout_ref,
               st_ref, prev_ref, q_s, k_s, v_s, g_s, eg_s, beta_s, kt_s, egl_s, o_s, *, cfg):
    sb = pl.program_id(0)
    c = cfg.chunk
    n = SUPER_BLOCK
    w = C_W

    @pl.when(sb % cfg.sb_per_seq == 0)
    def _():
        prev_ref[...] = jnp.zeros_like(prev_ref)

    qkv = _silu(_causal_conv(z_ref[:, 0:3 * w], prev_ref, cw_ref))
    q = qkv[:, 0:w]
    kx = qkv[:, w:2 * w]
    ones = ones_ref[...]
    q = q * lax.rsqrt(_dot_sel(q * q, ones, passes=2) + EPS) * (HEAD_DIM ** -0.5)
    kx = kx * lax.rsqrt(_dot_sel(kx * kx, ones, passes=2) + EPS)
    pc = z_ref[:, 4 * w:4 * w + 128]
    beta = jax.nn.sigmoid(_dot_sel(pc, expb_ref[...]))
    gdec = -jnp.exp(alog_ref[...]) * _softplus(_dot_sel(pc, expa_ref[...]) + dtb_ref[...])
    if cfg.masked:
        real = _real_mask(cfg, n)
        beta = jnp.where(real, beta, 0.0)
        gdec = jnp.where(real, gdec, 0.0)
    gcum = _dot(ltri_ref[...], gdec, precision=HIGHEST)
    g3 = gcum.reshape(cfg.n_chunks, c, w)
    gl3 = g3[:, c - 1:c, :]
    q_s[...] = q
    k_s[...] = kx
    v_s[...] = qkv[:, 2 * w:3 * w]
    g_s[...] = gcum
    eg_s[...] = jnp.exp(gcum)
    beta_s[...] = beta
    kt_s[...] = (kx.reshape(cfg.n_chunks, c, w) * jnp.exp(gl3 - g3)).reshape(n, w)
    egl_s[...] = jnp.exp(gl3).reshape(cfg.n_chunks, w)

    iio = lax.broadcasted_iota(jnp.int32, (c, 1), 0)
    bd = bd_ref[...]

    def chunk(k, carry):
        r0 = pl.multiple_of(k * c, c)
        rows = pl.ds(r0, c)
        pred, idx = _state_io(cfg, sb, k, st_ref, s0_ref, True)
        if pred is None:
            st_ref[...] = _expand_state(s0_ref[idx], tile_ref, bd_ref)
        else:
            @pl.when(pred)
            def _():
                st_ref[...] = _expand_state(s0_ref[idx], tile_ref, bd_ref)
        st = st_ref[...]
        q_c = q_s[rows, :]
        k_c = k_s[rows, :]
        g_c = g_s[rows, :]
        eg_c = eg_s[rows, :]
        beta_c = beta_s[rows, :]
        qk_s = _dot_x3(jnp.concatenate([q_c, k_c], axis=0), st)
        q_st = qk_s[0:c, :]
        k_st = qk_s[c:2 * c, :]
        d_rows = [k_c * k_c[j:j + 1, :] for j in range(c)] + [q_c * k_c[j:j + 1, :] for j in range(c)]
        dots = _dot_sel(jnp.concatenate(d_rows, axis=0), ones, passes=2)
        e = beta_c * (v_s[rows, :] - eg_c * k_st)
        o = eg_c * q_st
        for j in range(c):
            dec = jnp.exp(jnp.minimum(g_c - g_c[j:j + 1, :], 0.0))
            m_col = jnp.where(iio > j, beta_c * dots[j * c:(j + 1) * c, :] * dec, 0.0)
            e_j = e[j:j + 1, :]
            e = e - m_col * e_j
            qk_col = jnp.where(iio >= j, dots[(c + j) * c:(c + j + 1) * c, :] * dec, 0.0)
            o = o + qk_col * e_j
        o_s[rows, :] = o
        upd = _dot_x3(kt_s[rows, :], e, _TN)
        st_new = st * egl_s[pl.ds(k, 1), :] + bd * upd
        st_ref[...] = st_new
        pred, idx = _state_io(cfg, sb, k, st_ref, s0_ref, False)
        if pred is None:
            sout_ref[idx] = _compress_state(st_new, tile_t_ref)
        else:
            @pl.when(pred)
            def _():
                sout_ref[idx] = _compress_state(st_new, tile_t_ref)
        return carry

    lax.fori_loop(0, cfg.n_chunks, chunk, 0)
    o_ref[...] = _head_norm_gate(o_s[...], z_ref[:, 3 * w:4 * w], g_ref[...], ones_ref)


def _dn_call(z_c, cw, alog_exp, dtb_exp, g_exp, s0, cfg):
    rows = z_c.shape[0]
    w = C_W
    cst = _mixer_consts(C_HEADS, cfg)
    lane_head = np.arange(w) // HEAD_DIM
    expb = (np.arange(128)[:, None] == lane_head[None, :]).astype(np.float32)
    expa = (np.arange(128)[:, None] == (lane_head[None, :] + C_HEADS)).astype(np.float32)
    expb = jnp.asarray(expb, BF16)
    expa = jnp.asarray(expa, BF16)
    n_seq = s0.shape[0]
    vm = lambda shape: pltpu.VMEM(shape, F32)
    return pl.pallas_call(
        functools.partial(_dn_kernel, cfg=cfg),
        grid=(rows // SUPER_BLOCK,),
        in_specs=[pl.BlockSpec((SUPER_BLOCK, ZC_W), lambda i: (i, 0)), _const_spec(cw.shape),
                  _const_spec((1, w)), _const_spec((1, w)), _const_spec((1, w)), _state_specs(cfg, w),
                  _const_spec(cst["ltri"].shape), _const_spec(cst["ones"].shape),
                  _const_spec(cst["tile"].shape), _const_spec(cst["tile_t"].shape),
                  _const_spec(cst["bd"].shape), _const_spec(expb.shape), _const_spec(expa.shape)],
        out_specs=[pl.BlockSpec((SUPER_BLOCK, w), lambda i: (i, 0)), _state_specs(cfg, w)],
        out_shape=[jax.ShapeDtypeStruct((rows, w), F32), jax.ShapeDtypeStruct((n_seq, w, HEAD_DIM), F32)],
        scratch_shapes=[vm((w, w)), vm((8, 3 * w))] + [vm((SUPER_BLOCK, w))] * 7
        + [vm((cfg.n_chunks, w)), vm((SUPER_BLOCK, w))],
        compiler_params=_params(1),
        name="deltanet",
    )(z_c, cw, alog_exp, dtb_exp, g_exp, s0, cst["ltri"], cst["ones"], cst["tile"], cst["tile_t"],
      cst["bd"], expb, expa)


def _outproj_kernel(x_ref, oa_ref, ob_ref, oc_ref, gt_ref, g_ref, sc_ref, sh_ref,
                    woh_ref, wol_ref, wqh_ref, wql_ref, x1_ref, h2_ref, q_ref):
    mo = _dot_w(oa_ref[...], woh_ref[0:A_W, :], wol_ref[0:A_W, :])
    mo = mo + _dot_w(ob_ref[...], woh_ref[A_W:A_W + B_W, :], wol_ref[A_W:A_W + B_W, :])
    mo = mo + _dot_w(oc_ref[...], woh_ref[A_W + B_W:, :], wol_ref[A_W + B_W:, :])
    x1 = x_ref[...] + gt_ref[...] * mo
    x1_ref[...] = x1
    h2 = _rmsnorm(x1, g_ref[...]) * sc_ref[...] + sh_ref[...]
    h2_ref[...] = h2.astype(BF16)
    q_ref[...] = _dot_w(h2, wqh_ref[...], wql_ref[...])


def _outproj_call(x, oa, ob, oc, gt, g, sc, sh, woh, wol, wqh, wql, tiles_per_group):
    t = x.shape[0]
    tm = TOKEN_TILE
    row = lambda w: pl.BlockSpec((tm, w), lambda i: (i, 0))
    return pl.pallas_call(
        _outproj_kernel,
        grid=(t // tm,),
        in_specs=[row(D_MODEL), row(A_W), row(B_W), row(C_W), _mod_spec(gt, tiles_per_group),
                  _const_spec((1, D_MODEL)), _mod_spec(sc, tiles_per_group), _mod_spec(sh, tiles_per_group),
                  _const_spec(woh.shape), _const_spec(wol.shape), _const_spec(wqh.shape),
                  _const_spec(wql.shape)],
        out_specs=[row(D_MODEL), row(D_MODEL), row(D_MODEL)],
        out_shape=[jax.ShapeDtypeStruct((t, D_MODEL), F32), jax.ShapeDtypeStruct((t, D_MODEL), BF16),
                   jax.ShapeDtypeStruct((t, D_MODEL), F32)],
        compiler_params=_params(1),
        name="outproj",
    )(x, oa, ob, oc, gt, g, sc, sh, woh, wol, wqh, wql)


def _top16(s):
    rank = jnp.full(s.shape, float(PEER_TOPK), F32)
    vals = []
    for r in range(PEER_TOPK):
        m = jnp.max(s, axis=0, keepdims=True)
        hit = s == m
        rank = jnp.where(hit, float(r), rank)
        s = jnp.where(hit, -jnp.inf, s)
        vals.append(m)
    return jnp.concatenate(vals, axis=0), rank


def _peer_kernel(h2_ref, q_ref, x1_ref, gt_ref, fg_ref, k1_ref, k2_ref, u_ref, vt_ref, o_ref,
                 cnt_s, e1_s, rank2_s, e2_s, coef_s, acc_s, hid_s, *, final):
    j = pl.program_id(1)
    tt = PEER_TOKEN_TILE
    nk = PEER_NKEYS
    a_per_step = PEER_EXPERT_BLOCK // nk

    @pl.when(j == 0)
    def _():
        acc_s[...] = jnp.zeros_like(acc_s)

        def head(h, carry):
            qh = q_ref[:, pl.ds(pl.multiple_of(h * nk, nk), nk)]
            s1 = _dot(k1_ref[...], qh, _NT, precision=HIGHEST)
            s2 = _dot(k2_ref[...], qh, _NT, precision=HIGHEST)
            v1, rank1 = _top16(s1)
            v2, rank2 = _top16(s2)
            c3 = v1[:, None, :] + v2[None, :, :]
            cand = c3.reshape(PEER_TOPK * PEER_TOPK, tt)
            m = None
            for _ in range(PEER_TOPK):
                m = jnp.max(cand, axis=0, keepdims=True)
                cand = jnp.where(cand == m, -jnp.inf, cand)
            tau = m
            sel = c3 >= tau[None, :, :]
            m0 = v1[0:1, :] + v2[0:1, :]
            zsum = jnp.sum(jnp.where(sel, jnp.exp(c3 - m0[None, :, :]), 0.0).reshape(-1, tt), axis=0, keepdims=True)
            n_i = jnp.sum(jnp.where(sel, 1.0, 0.0), axis=1)
            cnt = jnp.zeros((nk, tt), F32)
            for i in range(PEER_TOPK):
                cnt = cnt + jnp.where(rank1 == float(i), n_i[i:i + 1, :], 0.0)
            cnt_s[h] = cnt
            e1_s[h] = jnp.exp(s1 - v1[0:1, :])
            rank2_s[h] = rank2
            e2_s[h] = jnp.exp(s2 - v2[0:1, :]) / zsum
            return carry

        lax.fori_loop(0, PEER_HEADS, head, 0)

    hid_s[...] = _dot(u_ref[...], h2_ref[...], _NT)

    def a_step(al, carry):
        a = j * a_per_step + al
        r0 = pl.multiple_of(al * nk, nk)
        wsum = jnp.zeros((nk, tt), F32)
        for h in range(PEER_HEADS):
            cnt_row = cnt_s[h, pl.ds(a, 1), :]
            e1_row = e1_s[h, pl.ds(a, 1), :]
            wsum = wsum + jnp.where(rank2_s[h] < cnt_row, e2_s[h] * e1_row, 0.0)
        coef_s[pl.ds(r0, nk), :] = (_gelu(hid_s[pl.ds(r0, nk), :]) * wsum).astype(BF16)
        return carry

    lax.fori_loop(0, a_per_step, a_step, 0)
    acc_s[...] += _dot(vt_ref[...], coef_s[...])

    @pl.when(j == pl.num_programs(1) - 1)
    def _():
        x2 = x1_ref[...] + gt_ref[...] * acc_s[...].T
        if final:
            x2 = _rmsnorm(x2, fg_ref[...])
        o_ref[...] = x2


def _peer_call(h2b, q, x1, gt, fg, k1p, k2p, u_b, vt_b, tiles_per_group, final):
    t = h2b.shape[0]
    tt = PEER_TOKEN_TILE
    nb = PEER_EXPERT_BLOCK
    row = lambda w: pl.BlockSpec((tt, w), lambda i, j: (i, 0))
    _, r, w = gt.shape
    tab = lambda dt: pltpu.VMEM((PEER_HEADS, PEER_NKEYS, tt), dt)
    return pl.pallas_call(
        functools.partial(_peer_kernel, final=final),
        grid=(t // tt, PEER_N // nb),
        in_specs=[row(D_MODEL), row(D_MODEL), row(D_MODEL),
                  pl.BlockSpec((None, r, w), lambda i, j: (i // tiles_per_group, 0, 0)),
                  pl.BlockSpec((1, D_MODEL), lambda i, j: (0, 0)),
                  pl.BlockSpec((PEER_NKEYS, PEER_NKEYS), lambda i, j: (0, 0)),
                  pl.BlockSpec((PEER_NKEYS, PEER_NKEYS), lambda i, j: (0, 0)),
                  pl.BlockSpec((nb, D_MODEL), lambda i, j: (j, 0)),
                  pl.BlockSpec((D_MODEL, nb), lambda i, j: (0, j))],
        out_specs=row(D_MODEL),
        out_shape=jax.ShapeDtypeStruct((t, D_MODEL), F32),
        scratch_shapes=[tab(F32), tab(F32), tab(F32), tab(F32),
                        pltpu.VMEM((nb, tt), BF16), pltpu.VMEM((D_MODEL, tt), F32), pltpu.VMEM((nb, tt), F32)],
        compiler_params=_params(2),
        name="peer",
    )(h2b, q, x1, gt, fg, k1p, k2p, u_b, vt_b)


def _block_diag(wblk):
    n, d, e = wblk.shape
    eye = jnp.eye(n, dtype=wblk.dtype)
    return (eye[:, None, :, None] * wblk[:, :, None, :]).reshape(n * d, n * e)


def _hilo(w):
    hi = w.astype(BF16)
    return hi, (w - hi.astype(F32)).astype(BF16)


def _pad_sample(z, hist=None):
    bsz = z.shape[0] // 4
    w = z.shape[1]
    z3 = z.reshape(bsz, 4, w)
    h3 = jnp.zeros((bsz, SAMPLE_HIST, w), F32)
    if hist is not None:
        h3 = h3.at[:, :, :hist.shape[-1]].set(hist)
    return jnp.concatenate([h3, z3, jnp.zeros((bsz, 1, w), F32)], axis=1).reshape(bsz * SAMPLE_SEQ_ROWS, w)


def _unpad_sample(o):
    bsz = o.shape[0] // SAMPLE_SEQ_ROWS
    return o.reshape(bsz, SAMPLE_SEQ_ROWS, -1)[:, SAMPLE_HIST:SAMPLE_HIST + 4].reshape(bsz * 4, -1)


def _layer_weights(l, w_in, w_out, peer_wq, peer_k1, peer_k2, peer_u, peer_v, lru_wa, lru_wx):
    w_in_p = jnp.pad(w_in[l], ((0, 0), (0, N_IN_PAD - N_IN)))
    wih, wil = _hilo(w_in_p)
    woh, wol = _hilo(w_out[l])
    wqh, wql = _hilo(peer_wq[l])
    wg = jnp.concatenate([_block_diag(lru_wa[l]), _block_diag(lru_wx[l])], axis=1)
    wgh, wgl = _hilo(wg)
    half = PEER_NKEYS // 2
    k1p = jnp.pad(peer_k1[l], ((0, 0), (0, half)))
    k2p = jnp.pad(peer_k2[l], ((0, 0), (half, 0)))
    u_b = peer_u[l].astype(BF16)
    vt_b = peer_v[l].T.astype(BF16)
    return dict(wih=wih, wil=wil, woh=woh, wol=wol, wqh=wqh, wql=wql, wgh=wgh, wgl=wgl,
                k1p=k1p, k2p=k2p, u_b=u_b, vt_b=vt_b)


def _trunk(x, mods, states, is_prompt, p, lw, l, final_g):
    t = x.shape[0]
    if is_prompt:
        sh1, sc1, gt1, sh2, sc2, gt2 = mods
    else:
        sh1, sc1, gt1, sh2, sc2 = (m.reshape(-1, TOKEN_TILE, D_MODEL) for m in mods[:5])
        gt2 = mods[5].reshape(-1, PEER_TOKEN_TILE, D_MODEL)
    if is_prompt:
        cfg = PROMPT_CFG
        tiles_tok = 2048 // TOKEN_TILE
        tiles_peer = 2048 // PEER_TOKEN_TILE
        n_seq = t // 2048
    else:
        cfg = SAMPLE_CFG
        tiles_tok = 1
        tiles_peer = 1
        n_seq = t // 4
    row1 = lambda v: v.reshape(1, -1)
    z_a, z_b, z_c = _inproj_call(x, row1(p['norm1_g']), sc1, sh1, lw['wih'], lw['wil'], tiles_tok)
    if is_prompt:
        s_a = jnp.zeros((n_seq, A_W, HEAD_DIM), F32)
        s_c = jnp.zeros((n_seq, C_W, HEAD_DIM), F32)
        za_m, zb_m, zc_m = z_a, z_b, z_c
        inj = None
    else:
        st_hgrn, st_lru_h, st_lru_conv, st_dn, st_dn_conv = states
        s_a = jnp.swapaxes(st_hgrn, -1, -2).reshape(n_seq, A_W, HEAD_DIM)
        s_c = st_dn.reshape(n_seq, C_W, HEAD_DIM)
        za_m = _pad_sample(z_a)
        zb_m = _pad_sample(z_b, st_lru_conv)
        zc_m = _pad_sample(z_c, st_dn_conv)
        inj = jnp.zeros((n_seq, SAMPLE_SEQ_ROWS, B_W), F32).at[:, 0].set(st_lru_h).reshape(-1, B_W)
    rep = lambda v, h: row1(jnp.tile(v, h))
    oa, s_a_new = _hgrn_call(za_m, p['lb_param'], rep(p['a_norm_g'], A_HEADS), s_a, cfg, l)
    ob, hs = _lru_call(zb_m, inj, p['lru_conv_w'], row1(p['lru_conv_b']), lw['wgh'], lw['wgl'],
                       row1(p['lru_ba']), row1(p['lru_bx']), row1(p['lru_L']), cfg)
    oc, s_c_new = _dn_call(zc_m, p['dn_conv_w'], row1(jnp.repeat(p['dn_A_log'], HEAD_DIM)),
                           row1(jnp.repeat(p['dn_dt_bias'], HEAD_DIM)), rep(p['dn_norm_g'], C_HEADS), s_c, cfg)
    if is_prompt:
        seq = 2048
        h_t = hs.reshape(n_seq, seq, B_W)[:, -1]
        buf_b = z_b.reshape(n_seq, seq, ZB_W)[:, -SAMPLE_HIST:, :B_W]
        buf_c = z_c.reshape(n_seq, seq, ZC_W)[:, -SAMPLE_HIST:, :3 * C_W]
    else:
        oa, ob, oc = _unpad_sample(oa), _unpad_sample(ob), _unpad_sample(oc)
        h_t = hs.reshape(n_seq, SAMPLE_SEQ_ROWS, B_W)[:, -1]
        buf_b = z_b.reshape(n_seq, 4, ZB_W)[:, 1:, :B_W]
        buf_c = z_c.reshape(n_seq, 4, ZC_W)[:, 1:, :3 * C_W]
    new_states = (jnp.swapaxes(s_a_new.reshape(n_seq, A_HEADS, HEAD_DIM, HEAD_DIM), -1, -2), h_t, buf_b,
                  s_c_new.reshape(n_seq, C_HEADS, HEAD_DIM, HEAD_DIM), buf_c)
    x1, h2b, q = _outproj_call(x, oa, ob, oc, gt1, row1(p['norm2_g']), sc2, sh2,
                               lw['woh'], lw['wol'], lw['wqh'], lw['wql'], tiles_tok)
    x2 = _peer_call(h2b, q, x1, gt2, row1(final_g), lw['k1p'], lw['k2p'], lw['u_b'], lw['vt_b'],
                    tiles_peer, final=(l == DEPTH - 1))
    return x2, new_states


def kernel(x_prompt, x_sample, state_hgrn, state_lru_h, state_lru_conv, state_dn, state_dn_conv,
           c_prompt, c_sample, w_ada, b_ada, norm1_g, norm2_g, w_in, lb_param, a_norm_g,
           lru_conv_w, lru_conv_b, lru_wa, lru_ba, lru_wx, lru_bx, lru_L,
           dn_conv_w, dn_A_log, dn_dt_bias, dn_norm_g, w_out,
           peer_wq, peer_k1, peer_k2, peer_u, peer_v, final_norm_g):
    n_p, seq, _ = x_prompt.shape
    n_s, dec_seq, _ = x_sample.shape
    mod = _ada_call(jnp.concatenate([c_prompt, c_sample], axis=0), w_ada, b_ada)
    xp = x_prompt.reshape(n_p * seq, D_MODEL)
    xs = x_sample.reshape(n_s * dec_seq, D_MODEL)
    sample_states = (state_hgrn, state_lru_h, state_lru_conv, state_dn, state_dn_conv)
    p_new, s_new = [], []
    for l in range(DEPTH):
        p = dict(norm1_g=norm1_g[l], norm2_g=norm2_g[l], lb_param=lb_param, a_norm_g=a_norm_g[l],
                 lru_conv_w=lru_conv_w[l], lru_conv_b=lru_conv_b[l], lru_ba=lru_ba[l], lru_bx=lru_bx[l],
                 lru_L=lru_L[l], dn_conv_w=dn_conv_w[l], dn_A_log=dn_A_log[l], dn_dt_bias=dn_dt_bias[l],
                 dn_norm_g=dn_norm_g[l])
        lw = _layer_weights(l, w_in, w_out, peer_wq, peer_k1, peer_k2, peer_u, peer_v, lru_wa, lru_wx)
        parts = jnp.split(mod[l], 6, axis=-1)
        parts = [m + 1.0 if i in (1, 4) else m for i, m in enumerate(parts)]
        mods_p = [m[:n_p].reshape(n_p, 1, D_MODEL) for m in parts]
        mods_s = [jnp.repeat(m[n_p:], dec_seq, axis=0) for m in parts]
        xp, st_p = _trunk(xp, mods_p, None, True, p, lw, l, final_norm_g)
        xs, st_s = _trunk(xs, mods_s, tuple(s[l] for s in sample_states), False, p, lw, l, final_norm_g)
        p_new.append(st_p)
        s_new.append(st_s)
    stack = lambda sts: [jnp.stack([s[i] for s in sts]) for i in range(5)]
    p_st = stack(p_new)
    s_st = stack(s_new)
    return (xp.reshape(n_p, seq, D_MODEL), xs.reshape(n_s, dec_seq, D_MODEL), *p_st, *s_st)
```

```python
import functools

import jax
import jax.numpy as jnp
import numpy as np
from jax import lax
from jax.experimental import pallas as pl
from jax.experimental.pallas import tpu as pltpu

F32 = jnp.float32
BF16 = jnp.bfloat16
HIGHEST = lax.Precision.HIGHEST

D_MODEL = 1024
DEPTH = 2
HEAD_DIM = 64
A_HEADS = 4
A_W = 256
B_W = 384
C_HEADS = 6
C_W = 384
LRU_C = 8.0
CONV_W = 4
N_IN = 3340
N_IN_PAD = 3456
ZA_W = 1024
ZB_W = 768
ZC_W = 1664
PEER_HEADS = 8
PEER_NKEYS = 128
PEER_TOPK = 16
PEER_N = PEER_NKEYS * PEER_NKEYS
EPS = 1e-6

SUPER_BLOCK = 256
SAMPLE_SEQ_ROWS = 8
SAMPLE_HIST = CONV_W - 1
TOKEN_TILE = 256
PEER_TOKEN_TILE = 512
PEER_EXPERT_BLOCK = 2048
PEER_SUB_BLOCK = 512
VMEM_LIMIT_BYTES = 56 * 1024 * 1024

_NN = (((1,), (0,)), ((), ()))
_NT = (((1,), (1,)), ((), ()))
_TN = (((0,), (0,)), ((), ()))


def _dot(a, b, dims=_NN, precision=None):
    return lax.dot_general(a, b, dims, precision=precision, preferred_element_type=F32)


def _split2(x):
    hi = x.astype(BF16)
    lo = (x - hi.astype(F32)).astype(BF16)
    return hi, lo


def _split3(x):
    hi = x.astype(BF16)
    r = x - hi.astype(F32)
    mid = r.astype(BF16)
    lo = (r - mid.astype(F32)).astype(BF16)
    return hi, mid, lo


def _dot_x3(a, b, dims=_NN):
    ah, al = _split2(a)
    bh, bl = _split2(b)
    return _dot(ah, bh, dims) + _dot(ah, bl, dims) + _dot(al, bh, dims)


def _dot_w(a, wh, wl, dims=_NN):
    ah, al = _split2(a)
    return _dot(ah, wh, dims) + _dot(al, wh, dims) + _dot(ah, wl, dims)


def _dot_sel(a, sel, dims=_NN, passes=3):
    parts = _split3(a)[:passes]
    out = _dot(parts[0], sel, dims)
    for p in parts[1:]:
        out = out + _dot(p, sel, dims)
    return out


def _softplus(x):
    return jnp.maximum(x, 0.0) + jnp.log(1.0 + jnp.exp(-jnp.abs(x)))


def _silu(x):
    return x * jax.nn.sigmoid(x)


def _gelu(x):
    return 0.5 * x * (1.0 + jnp.tanh(np.sqrt(2.0 / np.pi).astype(np.float32) * (x + 0.044715 * (x * x * x))))


def _rmsnorm(x, g):
    return x * lax.rsqrt(jnp.mean(x * x, axis=-1, keepdims=True) + EPS) * g


def _const_spec(shape):
    nd = len(shape)
    return pl.BlockSpec(shape, lambda *_: (0,) * nd)


def _params(n_grid):
    return pltpu.CompilerParams(dimension_semantics=("arbitrary",) * n_grid,
                                vmem_limit_bytes=VMEM_LIMIT_BYTES)


def _ada_kernel(c_ref, w_ref, b_ref, o_ref):
    c = c_ref[...]
    o_ref[0] = _dot(_silu(c), w_ref[0], precision=HIGHEST) + b_ref[0]


def _ada_call(c_all, w_ada, b_ada):
    nb = c_all.shape[0]
    nt = 6 * D_MODEL // 1024
    return pl.pallas_call(
        _ada_kernel,
        grid=(DEPTH, nt),
        in_specs=[
            _const_spec((nb, D_MODEL)),
            pl.BlockSpec((1, D_MODEL, 1024), lambda l, j: (l, 0, j)),
            pl.BlockSpec((1, 1, 1024), lambda l, j: (l, 0, j)),
        ],
        out_specs=pl.BlockSpec((1, nb, 1024), lambda l, j: (l, 0, j)),
        out_shape=jax.ShapeDtypeStruct((DEPTH, nb, 6 * D_MODEL), F32),
        compiler_params=_params(2),
        name="ada_mod",
    )(c_all, w_ada, b_ada.reshape(DEPTH, 1, 6 * D_MODEL))


def _inproj_kernel(x_ref, g_ref, sc_ref, sh_ref, wh_ref, wl_ref, za_ref, zb_ref, zc_ref):
    h = _rmsnorm(x_ref[...], g_ref[...]) * sc_ref[...] + sh_ref[...]
    hh, hl = _split2(h)
    for o_ref, lo, hi in ((za_ref, 0, ZA_W), (zb_ref, ZA_W, ZA_W + ZB_W), (zc_ref, ZA_W + ZB_W, N_IN_PAD)):
        wh = wh_ref[:, lo:hi]
        wl = wl_ref[:, lo:hi]
        o_ref[...] = _dot(hh, wh) + _dot(hl, wh) + _dot(hh, wl)


def _mod_spec(mod, tiles_per_group):
    _, r, w = mod.shape
    return pl.BlockSpec((None, r, w), lambda i: (i // tiles_per_group, 0, 0))


def _inproj_call(x, g, sc, sh, wh, wl, tiles_per_group):
    t = x.shape[0]
    tm = TOKEN_TILE
    row = lambda w: pl.BlockSpec((tm, w), lambda i: (i, 0))
    return pl.pallas_call(
        _inproj_kernel,
        grid=(t // tm,),
        in_specs=[row(D_MODEL), _const_spec((1, D_MODEL)), _mod_spec(sc, tiles_per_group),
                  _mod_spec(sh, tiles_per_group), _const_spec(wh.shape), _const_spec(wl.shape)],
        out_specs=[row(ZA_W), row(ZB_W), row(ZC_W)],
        out_shape=[jax.ShapeDtypeStruct((t, ZA_W), F32), jax.ShapeDtypeStruct((t, ZB_W), F32),
                   jax.ShapeDtypeStruct((t, ZC_W), F32)],
        compiler_params=_params(1),
        name="inproj",
    )(x, g, sc, sh, wh, wl)


class _SeqCfg:
    def __init__(self, seq_rows, hist, real, chunk):
        self.seq_rows = seq_rows
        self.hist = hist
        self.real = real
        self.chunk = chunk
        self.masked = hist > 0 or hist + real < seq_rows
        self.n_chunks = SUPER_BLOCK // chunk
        self.long_seq = seq_rows > SUPER_BLOCK
        self.sb_per_seq = max(seq_rows // SUPER_BLOCK, 1)
        self.seq_per_sb = max(SUPER_BLOCK // seq_rows, 1)
        assert self.long_seq or chunk == seq_rows


PROMPT_CFG = _SeqCfg(seq_rows=2048, hist=0, real=2048, chunk=16)
SAMPLE_CFG = _SeqCfg(seq_rows=SAMPLE_SEQ_ROWS, hist=SAMPLE_HIST, real=4, chunk=SAMPLE_SEQ_ROWS)


def _row_in_seq(cfg, n_rows):
    r = lax.broadcasted_iota(jnp.int32, (n_rows, 1), 0)
    return r % min(cfg.seq_rows, SUPER_BLOCK)


def _real_mask(cfg, n_rows):
    r = _row_in_seq(cfg, n_rows)
    return (r >= cfg.hist) & (r < cfg.hist + cfg.real)


def _causal_conv(x, prev_ref, w_ref):
    n = x.shape[0]
    ext = jnp.concatenate([prev_ref[...], x], axis=0)
    y = x * w_ref[CONV_W - 1:CONV_W, :]
    for k in range(CONV_W - 1):
        s = CONV_W - 1 - k
        y = y + ext[8 - s:8 - s + n, :] * w_ref[k:k + 1, :]
    prev_ref[...] = x[n - 8:n, :]
    return y


def _expand_state(s_cat, tile_ref, bd_ref):
    return _dot_sel(s_cat, tile_ref[...]) * bd_ref[...]


def _compress_state(s_bd, tile_t_ref):
    return _dot_sel(s_bd, tile_t_ref[...])


def _head_norm_gate(o, z, g, ones_ref):
    ms = _dot_sel(o * o, ones_ref[...], passes=2) * (1.0 / HEAD_DIM)
    return o * lax.rsqrt(ms + EPS) * g * _silu(z)


def _state_io(cfg, sb, k, st_ref, s0_ref, load):
    if cfg.long_seq:
        edge = 0 if load else cfg.sb_per_seq - 1
        kedge = 0 if load else cfg.n_chunks - 1
        return jnp.logical_and(sb % cfg.sb_per_seq == edge, k == kedge), 0
    return None, k


def _hgrn_kernel(z_ref, lbp_ref, g_ref, s0_ref, ltri_ref, ones_ref, tile_ref, tile_t_ref, bd_ref,
                 o_ref, sout_ref,
                 st_ref, q_s, k_s, g_s, qe_s, kt_s, egl_s, o_s, *, cfg, layer):
    sb = pl.program_id(0)
    c = cfg.chunk
    n = SUPER_BLOCK
    w = A_W
    lbp = lbp_ref[...]
    e = jnp.exp(lbp - jnp.max(lbp, axis=0, keepdims=True))
    lbs = e / jnp.sum(e, axis=0, keepdims=True)
    lb = jnp.sum(lbs[0:layer + 1, :], axis=0, keepdims=True) - lbs[0:1, :]

    aq = z_ref[:, 0:w]
    af = z_ref[:, w:2 * w]
    f = lb + (1.0 - lb) * jax.nn.sigmoid(af)
    lf = jnp.log(f)
    kk = 1.0 - f
    if cfg.masked:
        real = _real_mask(cfg, n)
        lf = jnp.where(real, lf, 0.0)
        kk = jnp.where(real, kk, 0.0)
    q = _silu(aq) * (HEAD_DIM ** -0.5)
    gcum = _dot(ltri_ref[...], lf, precision=HIGHEST)
    g3 = gcum.reshape(cfg.n_chunks, c, w)
    gl3 = g3[:, c - 1:c, :]
    q_s[...] = q
    k_s[...] = kk
    g_s[...] = gcum
    qe_s[...] = q * jnp.exp(gcum)
    kt_s[...] = (kk.reshape(cfg.n_chunks, c, w) * jnp.exp(gl3 - g3)).reshape(n, w)
    egl_s[...] = jnp.exp(gl3).reshape(cfg.n_chunks, w)

    jio = lax.broadcasted_iota(jnp.int32, (c, 1), 0)
    ones = ones_ref[...]
    bd = bd_ref[...]

    def chunk(k, carry):
        r0 = pl.multiple_of(k * c, c)
        rows = pl.ds(r0, c)
        pred, idx = _state_io(cfg, sb, k, st_ref, s0_ref, True)
        if pred is None:
            st_ref[...] = _expand_state(s0_ref[idx], tile_ref, bd_ref)
        else:
            @pl.when(pred)
            def _():
                st_ref[...] = _expand_state(s0_ref[idx], tile_ref, bd_ref)
        st = st_ref[...]
        q_c = q_s[rows, :]
        k_c = k_s[rows, :]
        g_c = g_s[rows, :]
        v_c = z_ref[rows, 2 * w:3 * w]
        o_inter = _dot(qe_s[rows, :].astype(BF16), st.astype(BF16), _NT)
        d_rows = []
        for i in range(c):
            dec = jnp.exp(jnp.minimum(g_c[i:i + 1, :] - g_c, 0.0))
            d_rows.append(jnp.where(jio <= i, k_c * dec * q_c[i:i + 1, :], 0.0))
        d = jnp.concatenate(d_rows, axis=0)
        a_b = _dot_sel(d, ones, passes=2)
        o_diag = jnp.sum(a_b.reshape(c, c, w) * v_c[None, :, :], axis=1)
        o_s[rows, :] = o_inter + o_diag
        upd = _dot_x3(v_c, kt_s[rows, :], _TN)
        st_new = st * egl_s[pl.ds(k, 1), :] + bd * upd
        st_ref[...] = st_new
        pred, idx = _state_io(cfg, sb, k, st_ref, s0_ref, False)
        if pred is None:
            sout_ref[idx] = _compress_state(st_new, tile_t_ref)
        else:
            @pl.when(pred)
            def _():
                sout_ref[idx] = _compress_state(st_new, tile_t_ref)
        return carry

    lax.fori_loop(0, cfg.n_chunks, chunk, 0)
    o_ref[...] = _head_norm_gate(o_s[...], z_ref[:, 3 * w:4 * w], g_ref[...], ones_ref)


def _mixer_consts(heads, cfg):
    w = heads * HEAD_DIM
    lane_head = np.arange(w) // HEAD_DIM
    ones = (lane_head[:, None] == lane_head[None, :]).astype(np.float32)
    tile = (np.arange(HEAD_DIM)[:, None] == (np.arange(w) % HEAD_DIM)[None, :]).astype(np.float32)
    r = np.arange(SUPER_BLOCK)
    ltri = ((r[:, None] // cfg.chunk == r[None, :] // cfg.chunk) & (r[None, :] <= r[:, None])).astype(np.float32)
    return dict(ltri=jnp.asarray(ltri, F32), ones=jnp.asarray(ones, BF16), tile=jnp.asarray(tile, BF16),
                tile_t=jnp.asarray(tile.T, BF16), bd=jnp.asarray(ones, F32))


def _state_specs(cfg, w):
    blk = (1 if cfg.long_seq else cfg.seq_per_sb, w, HEAD_DIM)
    if cfg.long_seq:
        imap = lambda i: (i // cfg.sb_per_seq, 0, 0)
    else:
        imap = lambda i: (i, 0, 0)
    return pl.BlockSpec(blk, imap)


def _hgrn_call(z_a, lb_param, g_exp, s0_t, cfg, layer):
    rows = z_a.shape[0]
    w = A_W
    cst = _mixer_consts(A_HEADS, cfg)
    n_seq = s0_t.shape[0]
    vm = lambda shape: pltpu.VMEM(shape, F32)
    kern = functools.partial(_hgrn_kernel, cfg=cfg, layer=layer)
    return pl.pallas_call(
        kern,
        grid=(rows // SUPER_BLOCK,),
        in_specs=[pl.BlockSpec((SUPER_BLOCK, ZA_W), lambda i: (i, 0)), _const_spec((DEPTH, w)),
                  _const_spec((1, w)), _state_specs(cfg, w), _const_spec(cst["ltri"].shape),
                  _const_spec(cst["ones"].shape), _const_spec(cst["tile"].shape),
                  _const_spec(cst["tile_t"].shape), _const_spec(cst["bd"].shape)],
        out_specs=[pl.BlockSpec((SUPER_BLOCK, w), lambda i: (i, 0)), _state_specs(cfg, w)],
        out_shape=[jax.ShapeDtypeStruct((rows, w), F32), jax.ShapeDtypeStruct((n_seq, w, HEAD_DIM), F32)],
        scratch_shapes=[vm((w, w))] + [vm((SUPER_BLOCK, w))] * 5 + [vm((cfg.n_chunks, w)), vm((SUPER_BLOCK, w))],
        compiler_params=_params(1),
        name="hgrn2",
    )(z_a, lb_param, g_exp, s0_t, cst["ltri"], cst["ones"], cst["tile"], cst["tile_t"], cst["bd"])


def _lru_kernel(*refs, cfg):
    if cfg.long_seq:
        (z_ref, cw_ref, cb_ref, wgh_ref, wgl_ref, ba_ref, bx_ref, lp_ref,
         ob_ref, hs_ref, prev_ref, hc_ref) = refs
        inj_ref = None
    else:
        (z_ref, inj_ref, cw_ref, cb_ref, wgh_ref, wgl_ref, ba_ref, bx_ref, lp_ref,
         ob_ref, hs_ref, prev_ref, hc_ref) = refs
    sb = pl.program_id(0)
    n = SUPER_BLOCK
    w = B_W

    @pl.when(sb % cfg.sb_per_seq == 0)
    def _():
        prev_ref[...] = jnp.zeros_like(prev_ref)
        hc_ref[...] = jnp.zeros_like(hc_ref)

    xc = _causal_conv(z_ref[:, 0:w], prev_ref, cw_ref) + cb_ref[...]
    gates = _dot_w(xc, wgh_ref[...], wgl_ref[...])
    r = jax.nn.sigmoid(gates[:, 0:w] + ba_ref[...])
    ig = jax.nn.sigmoid(gates[:, w:2 * w] + bx_ref[...])
    log_a = -LRU_C * r * _softplus(-lp_ref[...])
    a = jnp.exp(log_a)
    b = jnp.sqrt(1.0 - jnp.exp(2.0 * log_a)) * ig * xc
    if cfg.masked:
        real = _real_mask(cfg, n)
        a = jnp.where(real, a, 1.0)
        b = jnp.where(real, b, 0.0)
    if inj_ref is not None:
        b = b + inj_ref[...]
    ris = _row_in_seq(cfg, n)
    d = 1
    while d < min(n, cfg.seq_rows):
        has = ris >= d
        a_sh = jnp.where(has, pltpu.roll(a, d, 0), 1.0)
        b_sh = jnp.where(has, pltpu.roll(b, d, 0), 0.0)
        b = b + a * b_sh
        a = a * a_sh
        d *= 2
    hs = b + a * hc_ref[...]
    if cfg.long_seq:
        hc_ref[...] = hs[n - 1:n, :]
    hs_ref[...] = hs
    ob_ref[...] = hs * _gelu(z_ref[:, w:2 * w])


def _lru_call(z_b, inj, cw, cb, wgh, wgl, ba, bx, lp, cfg):
    rows = z_b.shape[0]
    w = B_W
    row = lambda width: pl.BlockSpec((SUPER_BLOCK, width), lambda i: (i, 0))
    ins = [z_b] + ([] if cfg.long_seq else [inj]) + [cw, cb, wgh, wgl, ba, bx, lp]
    specs = [row(ZB_W)] + ([] if cfg.long_seq else [row(w)]) + [_const_spec(a.shape) for a in ins[-7:]]
    return pl.pallas_call(
        functools.partial(_lru_kernel, cfg=cfg),
        grid=(rows // SUPER_BLOCK,),
        in_specs=specs,
        out_specs=[row(w), row(w)],
        out_shape=[jax.ShapeDtypeStruct((rows, w), F32)] * 2,
        scratch_shapes=[pltpu.VMEM((8, w), F32), pltpu.VMEM((1, w), F32)],
        compiler_params=_params(1),
        name="rglru",
    )(*ins)


def _dn_kernel(z_ref, cw_ref, alog_ref, dtb_ref, g_ref, s0_ref, ltri_ref, ones_ref, tile_ref, tile_t_ref,
               bd_ref, expb_ref, expa_ref,
               o_ref, sout_ref,
               st_ref, prev_ref, q_s, k_s, v_s, g_s, eg_s, beta_s, kt_s, egl_s, o_s, *, cfg):
    sb = pl.program_id(0)
    c = cfg.chunk
    n = SUPER_BLOCK
    w = C_W

    @pl.when(sb % cfg.sb_per_seq == 0)
    def _():
        prev_ref[...] = jnp.zeros_like(prev_ref)

    qkv = _silu(_causal_conv(z_ref[:, 0:3 * w], prev_ref, cw_ref))
    q = qkv[:, 0:w]
    kx = qkv[:, w:2 * w]
    ones = ones_ref[...]
    q = q * lax.rsqrt(_dot_sel(q * q, ones, passes=2) + EPS) * (HEAD_DIM ** -0.5)
    kx = kx * lax.rsqrt(_dot_sel(kx * kx, ones, passes=2) + EPS)
    pc = z_ref[:, 4 * w:4 * w + 128]
    beta = jax.nn.sigmoid(_dot_sel(pc, expb_ref[...]))
    gdec = -jnp.exp(alog_ref[...]) * _softplus(_dot_sel(pc, expa_ref[...]) + dtb_ref[...])
    if cfg.masked:
        real = _real_mask(cfg, n)
        beta = jnp.where(real, beta, 0.0)
        gdec = jnp.where(real, gdec, 0.0)
    gcum = _dot(ltri_ref[...], gdec, precision=HIGHEST)
    g3 = gcum.reshape(cfg.n_chunks, c, w)
    gl3 = g3[:, c - 1:c, :]
    q_s[...] = q
    k_s[...] = kx
    v_s[...] = qkv[:, 2 * w:3 * w]
    g_s[...] = gcum
    eg_s[...] = jnp.exp(gcum)
    beta_s[...] = beta
    kt_s[...] = (kx.reshape(cfg.n_chunks, c, w) * jnp.exp(gl3 - g3)).reshape(n, w)
    egl_s[...] = jnp.exp(gl3).reshape(cfg.n_chunks, w)

    iio = lax.broadcasted_iota(jnp.int32, (c, 1), 0)
    bd = bd_ref[...]

    def chunk(k, carry):
        r0 = pl.multiple_of(k * c, c)
        rows = pl.ds(r0, c)
        pred, idx = _state_io(cfg, sb, k, st_ref, s0_ref, True)
        if pred is None:
            st_ref[...] = _expand_state(s0_ref[idx], tile_ref, bd_ref)
        else:
            @pl.when(pred)
            def _():
                st_ref[...] = _expand_state(s0_ref[idx], tile_ref, bd_ref)
        st = st_ref[...]
        q_c = q_s[rows, :]
        k_c = k_s[rows, :]
        g_c = g_s[rows, :]
        eg_c = eg_s[rows, :]
        beta_c = beta_s[rows, :]
        qk_s = _dot_x3(jnp.concatenate([q_c, k_c], axis=0), st)
        q_st = qk_s[0:c, :]
        k_st = qk_s[c:2 * c, :]
        d_rows = [k_c * k_c[j:j + 1, :] for j in range(c)] + [q_c * k_c[j:j + 1, :] for j in range(c)]
        dots = _dot_sel(jnp.concatenate(d_rows, axis=0), ones, passes=2)
        e = beta_c * (v_s[rows, :] - eg_c * k_st)
        o = eg_c * q_st
        for j in range(c):
            dec = jnp.exp(jnp.minimum(g_c - g_c[j:j + 1, :], 0.0))
            m_col = jnp.where(iio > j, beta_c * dots[j * c:(j + 1) * c, :] * dec, 0.0)
            e_j = e[j:j + 1, :]
            e = e - m_col * e_j
            qk_col = jnp.where(iio >= j, dots[(c + j) * c:(c + j + 1) * c, :] * dec, 0.0)
            o = o + qk_col * e_j
        o_s[rows, :] = o
        upd = _dot_x3(kt_s[rows, :], e, _TN)
        st_new = st * egl_s[pl.ds(k, 1), :] + bd * upd
        st_ref[...] = st_new
        pred, idx = _state_io(cfg, sb, k, st_ref, s0_ref, False)
        if pred is None:
            sout_ref[idx] = _compress_state(st_new, tile_t_ref)
        else:
            @pl.when(pred)
            def _():
                sout_ref[idx] = _compress_state(st_new, tile_t_ref)
        return carry

    lax.fori_loop(0, cfg.n_chunks, chunk, 0)
    o_ref[...] = _head_norm_gate(o_s[...], z_ref[:, 3 * w:4 * w], g_ref[...], ones_ref)


def _dn_call(z_c, cw, alog_exp, dtb_exp, g_exp, s0, cfg):
    rows = z_c.shape[0]
    w = C_W
    cst = _mixer_consts(C_HEADS, cfg)
    lane_head = np.arange(w) // HEAD_DIM
    expb = (np.arange(128)[:, None] == lane_head[None, :]).astype(np.float32)
    expa = (np.arange(128)[:, None] == (lane_head[None, :] + C_HEADS)).astype(np.float32)
    expb = jnp.asarray(expb, BF16)
    expa = jnp.asarray(expa, BF16)
    n_seq = s0.shape[0]
    vm = lambda shape: pltpu.VMEM(shape, F32)
    return pl.pallas_call(
        functools.partial(_dn_kernel, cfg=cfg),
        grid=(rows // SUPER_BLOCK,),
        in_specs=[pl.BlockSpec((SUPER_BLOCK, ZC_W), lambda i: (i, 0)), _const_spec(cw.shape),
                  _const_spec((1, w)), _const_spec((1, w)), _const_spec((1, w)), _state_specs(cfg, w),
                  _const_spec(cst["ltri"].shape), _const_spec(cst["ones"].shape),
                  _const_spec(cst["tile"].shape), _const_spec(cst["tile_t"].shape),
                  _const_spec(cst["bd"].shape), _const_spec(expb.shape), _const_spec(expa.shape)],
        out_specs=[pl.BlockSpec((SUPER_BLOCK, w), lambda i: (i, 0)), _state_specs(cfg, w)],
        out_shape=[jax.ShapeDtypeStruct((rows, w), F32), jax.ShapeDtypeStruct((n_seq, w, HEAD_DIM), F32)],
        scratch_shapes=[vm((w, w)), vm((8, 3 * w))] + [vm((SUPER_BLOCK, w))] * 7
        + [vm((cfg.n_chunks, w)), vm((SUPER_BLOCK, w))],
        compiler_params=_params(1),
        name="deltanet",
    )(z_c, cw, alog_exp, dtb_exp, g_exp, s0, cst["ltri"], cst["ones"], cst["tile"], cst["tile_t"],
      cst["bd"], expb, expa)


def _outproj_kernel(x_ref, oa_ref, ob_ref, oc_ref, gt_ref, g_ref, sc_ref, sh_ref,
                    woh_ref, wol_ref, wqh_ref, wql_ref, x1_ref, h2_ref, q_ref):
    mo = _dot_w(oa_ref[...], woh_ref[0:A_W, :], wol_ref[0:A_W, :])
    mo = mo + _dot_w(ob_ref[...], woh_ref[A_W:A_W + B_W, :], wol_ref[A_W:A_W + B_W, :])
    mo = mo + _dot_w(oc_ref[...], woh_ref[A_W + B_W:, :], wol_ref[A_W + B_W:, :])
    x1 = x_ref[...] + gt_ref[...] * mo
    x1_ref[...] = x1
    h2 = _rmsnorm(x1, g_ref[...]) * sc_ref[...] + sh_ref[...]
    h2_ref[...] = h2.astype(BF16)
    q_ref[...] = _dot_w(h2, wqh_ref[...], wql_ref[...])


def _outproj_call(x, oa, ob, oc, gt, g, sc, sh, woh, wol, wqh, wql, tiles_per_group):
    t = x.shape[0]
    tm = TOKEN_TILE
    row = lambda w: pl.BlockSpec((tm, w), lambda i: (i, 0))
    return pl.pallas_call(
        _outproj_kernel,
        grid=(t // tm,),
        in_specs=[row(D_MODEL), row(A_W), row(B_W), row(C_W), _mod_spec(gt, tiles_per_group),
                  _const_spec((1, D_MODEL)), _mod_spec(sc, tiles_per_group), _mod_spec(sh, tiles_per_group),
                  _const_spec(woh.shape), _const_spec(wol.shape), _const_spec(wqh.shape),
                  _const_spec(wql.shape)],
        out_specs=[row(D_MODEL), row(D_MODEL), row(D_MODEL)],
        out_shape=[jax.ShapeDtypeStruct((t, D_MODEL), F32), jax.ShapeDtypeStruct((t, D_MODEL), BF16),
                   jax.ShapeDtypeStruct((t, D_MODEL), F32)],
        compiler_params=_params(1),
        name="outproj",
    )(x, oa, ob, oc, gt, g, sc, sh, woh, wol, wqh, wql)


def _top16(s):
    rank = jnp.full(s.shape, float(PEER_TOPK), F32)
    vals = []
    for r in range(PEER_TOPK):
        m = jnp.max(s, axis=0, keepdims=True)
        hit = s == m
        rank = jnp.where(hit, float(r), rank)
        s = jnp.where(hit, -jnp.inf, s)
        vals.append(m)
    return jnp.concatenate(vals, axis=0), rank


def _peer_kernel(h2_ref, q_ref, x1_ref, gt_ref, fg_ref, k1_ref, k2_ref, u_ref, vt_ref, o_ref,
                 cnt_s, e1_s, rank2_s, e2_s, acc_s, *, final):
    j = pl.program_id(1)
    tt = PEER_TOKEN_TILE
    nk = PEER_NKEYS
    a_per_step = PEER_EXPERT_BLOCK // nk

    @pl.when(j == 0)
    def _():
        acc_s[...] = jnp.zeros_like(acc_s)

        def head(h, carry):
            qh = q_ref[:, pl.ds(pl.multiple_of(h * nk, nk), nk)]
            s1 = _dot(k1_ref[...], qh, _NT, precision=HIGHEST)
            s2 = _dot(k2_ref[...], qh, _NT, precision=HIGHEST)
            v1, rank1 = _top16(s1)
            v2, rank2 = _top16(s2)
            c3 = v1[:, None, :] + v2[None, :, :]
            cand = c3.reshape(PEER_TOPK * PEER_TOPK, tt)
            m = None
            for _ in range(PEER_TOPK):
                m = jnp.max(cand, axis=0, keepdims=True)
                cand = jnp.where(cand == m, -jnp.inf, cand)
            tau = m
            sel = c3 >= tau[None, :, :]
            m0 = v1[0:1, :] + v2[0:1, :]
            zsum = jnp.sum(jnp.where(sel, jnp.exp(c3 - m0[None, :, :]), 0.0).reshape(-1, tt), axis=0, keepdims=True)
            n_i = jnp.sum(jnp.where(sel, 1.0, 0.0), axis=1)
            cnt = jnp.zeros((nk, tt), F32)
            for i in range(PEER_TOPK):
                cnt = cnt + jnp.where(rank1 == float(i), n_i[i:i + 1, :], 0.0)
            cnt_s[h] = cnt
            e1_s[h] = jnp.exp(s1 - v1[0:1, :])
            rank2_s[h] = rank2.astype(BF16)
            e2_s[h] = (jnp.exp(s2 - v2[0:1, :]) / zsum).astype(BF16)
            return carry

        lax.fori_loop(0, PEER_HEADS, head, 0)

    h2 = h2_ref[...]
    n_sub = PEER_EXPERT_BLOCK // PEER_SUB_BLOCK
    a_per_sub = PEER_SUB_BLOCK // nk
    hidden = lambda s: _dot(u_ref[s * PEER_SUB_BLOCK:(s + 1) * PEER_SUB_BLOCK, :], h2, _NT)
    hid_next = hidden(0)
    for s in range(n_sub):
        e0 = s * PEER_SUB_BLOCK
        hid = hid_next
        if s + 1 < n_sub:
            hid_next = hidden(s + 1)
        coefs = []
        for al in range(a_per_sub):
            a = j * a_per_step + s * a_per_sub + al
            wsum = jnp.zeros((nk, tt), BF16)
            for h in range(PEER_HEADS):
                cnt_row = cnt_s[h, pl.ds(a, 1), :].astype(BF16)
                e1_row = e1_s[h, pl.ds(a, 1), :].astype(BF16)
                wsum = wsum + jnp.where(rank2_s[h] < cnt_row, e2_s[h] * e1_row, jnp.zeros((), BF16))
            coefs.append(_gelu(hid[al * nk:(al + 1) * nk, :]).astype(BF16) * wsum)
        coef = jnp.concatenate(coefs, axis=0)
        acc_s[...] += _dot(vt_ref[:, e0:e0 + PEER_SUB_BLOCK], coef)

    @pl.when(j == pl.num_programs(1) - 1)
    def _():
        x2 = x1_ref[...] + gt_ref[...] * acc_s[...].T
        if final:
            x2 = _rmsnorm(x2, fg_ref[...])
        o_ref[...] = x2


def _peer_call(h2b, q, x1, gt, fg, k1p, k2p, u_b, vt_b, tiles_per_group, final):
    t = h2b.shape[0]
    tt = PEER_TOKEN_TILE
    nb = PEER_EXPERT_BLOCK
    row = lambda w: pl.BlockSpec((tt, w), lambda i, j: (i, 0))
    _, r, w = gt.shape
    tab = lambda dt: pltpu.VMEM((PEER_HEADS, PEER_NKEYS, tt), dt)
    return pl.pallas_call(
        functools.partial(_peer_kernel, final=final),
        grid=(t // tt, PEER_N // nb),
        in_specs=[row(D_MODEL), row(D_MODEL), row(D_MODEL),
                  pl.BlockSpec((None, r, w), lambda i, j: (i // tiles_per_group, 0, 0)),
                  pl.BlockSpec((1, D_MODEL), lambda i, j: (0, 0)),
                  pl.BlockSpec((PEER_NKEYS, PEER_NKEYS), lambda i, j: (0, 0)),
                  pl.BlockSpec((PEER_NKEYS, PEER_NKEYS), lambda i, j: (0, 0)),
                  pl.BlockSpec((nb, D_MODEL), lambda i, j: (j, 0)),
                  pl.BlockSpec((D_MODEL, nb), lambda i, j: (0, j))],
        out_specs=row(D_MODEL),
        out_shape=jax.ShapeDtypeStruct((t, D_MODEL), F32),
        scratch_shapes=[tab(F32), tab(F32), tab(BF16), tab(BF16), pltpu.VMEM((D_MODEL, tt), F32)],
        compiler_params=_params(2),
        name="peer",
    )(h2b, q, x1, gt, fg, k1p, k2p, u_b, vt_b)


def _block_diag(wblk):
    n, d, e = wblk.shape
    eye = jnp.eye(n, dtype=wblk.dtype)
    return (eye[:, None, :, None] * wblk[:, :, None, :]).reshape(n * d, n * e)


def _hilo(w):
    hi = w.astype(BF16)
    return hi, (w - hi.astype(F32)).astype(BF16)


def _pad_sample(z, hist=None):
    bsz = z.shape[0] // 4
    w = z.shape[1]
    z3 = z.reshape(bsz, 4, w)
    h3 = jnp.zeros((bsz, SAMPLE_HIST, w), F32)
    if hist is not None:
        h3 = h3.at[:, :, :hist.shape[-1]].set(hist)
    return jnp.concatenate([h3, z3, jnp.zeros((bsz, 1, w), F32)], axis=1).reshape(bsz * SAMPLE_SEQ_ROWS, w)


def _unpad_sample(o):
    bsz = o.shape[0] // SAMPLE_SEQ_ROWS
    return o.reshape(bsz, SAMPLE_SEQ_ROWS, -1)[:, SAMPLE_HIST:SAMPLE_HIST + 4].reshape(bsz * 4, -1)


def _layer_weights(l, w_in, w_out, peer_wq, peer_k1, peer_k2, peer_u, peer_v, lru_wa, lru_wx):
    w_in_p = jnp.pad(w_in[l], ((0, 0), (0, N_IN_PAD - N_IN)))
    wih, wil = _hilo(w_in_p)
    woh, wol = _hilo(w_out[l])
    wqh, wql = _hilo(peer_wq[l])
    wg = jnp.concatenate([_block_diag(lru_wa[l]), _block_diag(lru_wx[l])], axis=1)
    wgh, wgl = _hilo(wg)
    half = PEER_NKEYS // 2
    k1p = jnp.pad(peer_k1[l], ((0, 0), (0, half)))
    k2p = jnp.pad(peer_k2[l], ((0, 0), (half, 0)))
    u_b = peer_u[l].astype(BF16)
    vt_b = peer_v[l].T.astype(BF16)
    return dict(wih=wih, wil=wil, woh=woh, wol=wol, wqh=wqh, wql=wql, wgh=wgh, wgl=wgl,
                k1p=k1p, k2p=k2p, u_b=u_b, vt_b=vt_b)


def _trunk(x, mods, states, is_prompt, p, lw, l, final_g):
    t = x.shape[0]
    if is_prompt:
        sh1, sc1, gt1, sh2, sc2, gt2 = mods
    else:
        sh1, sc1, gt1, sh2, sc2 = (m.reshape(-1, TOKEN_TILE, D_MODEL) for m in mods[:5])
        gt2 = mods[5].reshape(-1, PEER_TOKEN_TILE, D_MODEL)
    if is_prompt:
        cfg = PROMPT_CFG
        tiles_tok = 2048 // TOKEN_TILE
        tiles_peer = 2048 // PEER_TOKEN_TILE
        n_seq = t // 2048
    else:
        cfg = SAMPLE_CFG
        tiles_tok = 1
        tiles_peer = 1
        n_seq = t // 4
    row1 = lambda v: v.reshape(1, -1)
    z_a, z_b, z_c = _inproj_call(x, row1(p['norm1_g']), sc1, sh1, lw['wih'], lw['wil'], tiles_tok)
    if is_prompt:
        s_a = jnp.zeros((n_seq, A_W, HEAD_DIM), F32)
        s_c = jnp.zeros((n_seq, C_W, HEAD_DIM), F32)
        za_m, zb_m, zc_m = z_a, z_b, z_c
        inj = None
    else:
        st_hgrn, st_lru_h, st_lru_conv, st_dn, st_dn_conv = states
        s_a = jnp.swapaxes(st_hgrn, -1, -2).reshape(n_seq, A_W, HEAD_DIM)
        s_c = st_dn.reshape(n_seq, C_W, HEAD_DIM)
        za_m = _pad_sample(z_a)
        zb_m = _pad_sample(z_b, st_lru_conv)
        zc_m = _pad_sample(z_c, st_dn_conv)
        inj = jnp.zeros((n_seq, SAMPLE_SEQ_ROWS, B_W), F32).at[:, 0].set(st_lru_h).reshape(-1, B_W)
    rep = lambda v, h: row1(jnp.tile(v, h))
    oa, s_a_new = _hgrn_call(za_m, p['lb_param'], rep(p['a_norm_g'], A_HEADS), s_a, cfg, l)
    ob, hs = _lru_call(zb_m, inj, p['lru_conv_w'], row1(p['lru_conv_b']), lw['wgh'], lw['wgl'],
                       row1(p['lru_ba']), row1(p['lru_bx']), row1(p['lru_L']), cfg)
    oc, s_c_new = _dn_call(zc_m, p['dn_conv_w'], row1(jnp.repeat(p['dn_A_log'], HEAD_DIM)),
                           row1(jnp.repeat(p['dn_dt_bias'], HEAD_DIM)), rep(p['dn_norm_g'], C_HEADS), s_c, cfg)
    if is_prompt:
        seq = 2048
        h_t = hs.reshape(n_seq, seq, B_W)[:, -1]
        buf_b = z_b.reshape(n_seq, seq, ZB_W)[:, -SAMPLE_HIST:, :B_W]
        buf_c = z_c.reshape(n_seq, seq, ZC_W)[:, -SAMPLE_HIST:, :3 * C_W]
    else:
        oa, ob, oc = _unpad_sample(oa), _unpad_sample(ob), _unpad_sample(oc)
        h_t = hs.reshape(n_seq, SAMPLE_SEQ_ROWS, B_W)[:, -1]
        buf_b = z_b.reshape(n_seq, 4, ZB_W)[:, 1:, :B_W]
        buf_c = z_c.reshape(n_seq, 4, ZC_W)[:, 1:, :3 * C_W]
    new_states = (jnp.swapaxes(s_a_new.reshape(n_seq, A_HEADS, HEAD_DIM, HEAD_DIM), -1, -2), h_t, buf_b,
                  s_c_new.reshape(n_seq, C_HEADS, HEAD_DIM, HEAD_DIM), buf_c)
    x1, h2b, q = _outproj_call(x, oa, ob, oc, gt1, row1(p['norm2_g']), sc2, sh2,
                               lw['woh'], lw['wol'], lw['wqh'], lw['wql'], tiles_tok)
    x2 = _peer_call(h2b, q, x1, gt2, row1(final_g), lw['k1p'], lw['k2p'], lw['u_b'], lw['vt_b'],
                    tiles_peer, final=(l == DEPTH - 1))
    return x2, new_states


def kernel(x_prompt, x_sample, state_hgrn, state_lru_h, state_lru_conv, state_dn, state_dn_conv,
           c_prompt, c_sample, w_ada, b_ada, norm1_g, norm2_g, w_in, lb_param, a_norm_g,
           lru_conv_w, lru_conv_b, lru_wa, lru_ba, lru_wx, lru_bx, lru_L,
           dn_conv_w, dn_A_log, dn_dt_bias, dn_norm_g, w_out,
           peer_wq, peer_k1, peer_k2, peer_u, peer_v, final_norm_g):
    n_p, seq, _ = x_prompt.shape
    n_s, dec_seq, _ = x_sample.shape
    mod = _ada_call(jnp.concatenate([c_prompt, c_sample], axis=0), w_ada, b_ada)
    xp = x_prompt.reshape(n_p * seq, D_MODEL)
    xs = x_sample.reshape(n_s * dec_seq, D_MODEL)
    sample_states = (state_hgrn, state_lru_h, state_lru_conv, state_dn, state_dn_conv)
    p_new, s_new = [], []
    for l in range(DEPTH):
        p = dict(norm1_g=norm1_g[l], norm2_g=norm2_g[l], lb_param=lb_param, a_norm_g=a_norm_g[l],
                 lru_conv_w=lru_conv_w[l], lru_conv_b=lru_conv_b[l], lru_ba=lru_ba[l], lru_bx=lru_bx[l],
                 lru_L=lru_L[l], dn_conv_w=dn_conv_w[l], dn_A_log=dn_A_log[l], dn_dt_bias=dn_dt_bias[l],
                 dn_norm_g=dn_norm_g[l])
        lw = _layer_weights(l, w_in, w_out, peer_wq, peer_k1, peer_k2, peer_u, peer_v, lru_wa, lru_wx)
        parts = jnp.split(mod[l], 6, axis=-1)
        parts = [m + 1.0 if i in (1, 4) else m for i, m in enumerate(parts)]
        mods_p = [m[:n_p].reshape(n_p, 1, D_MODEL) for m in parts]
        mods_s = [jnp.repeat(m[n_p:], dec_seq, axis=0) for m in parts]
        xp, st_p = _trunk(xp, mods_p, None, True, p, lw, l, final_norm_g)
        xs, st_s = _trunk(xs, mods_s, tuple(s[l] for s in sample_states), False, p, lw, l, final_norm_g)
        p_new.append(st_p)
        s_new.append(st_s)
    stack = lambda sts: [jnp.stack([s[i] for s in sts]) for i in range(5)]
    p_st = stack(p_new)
    s_st = stack(s_new)
    return (xp.reshape(n_p, seq, D_MODEL), xs.reshape(n_s, dec_seq, D_MODEL), *p_st, *s_st)
```

```python
import functools

import jax
import jax.numpy as jnp
import numpy as np
from jax import lax
from jax.experimental import pallas as pl
from jax.experimental.pallas import tpu as pltpu

F32 = jnp.float32
BF16 = jnp.bfloat16
HIGHEST = lax.Precision.HIGHEST

D_MODEL = 1024
DEPTH = 2
HEAD_DIM = 64
A_HEADS = 4
A_W = 256
B_W = 384
C_HEADS = 6
C_W = 384
LRU_C = 8.0
CONV_W = 4
N_IN = 3340
N_IN_PAD = 3456
ZA_W = 1024
ZB_W = 768
ZC_W = 1664
PEER_HEADS = 8
PEER_NKEYS = 128
PEER_TOPK = 16
PEER_N = PEER_NKEYS * PEER_NKEYS
EPS = 1e-6

SUPER_BLOCK = 256
SAMPLE_SEQ_ROWS = 8
SAMPLE_HIST = CONV_W - 1
TOKEN_TILE = 256
PEER_TOKEN_TILE = 512
PEER_EXPERT_BLOCK = 2048
PEER_SUB_BLOCK = 512
VMEM_LIMIT_BYTES = 56 * 1024 * 1024

_NN = (((1,), (0,)), ((), ()))
_NT = (((1,), (1,)), ((), ()))
_TN = (((0,), (0,)), ((), ()))


def _dot(a, b, dims=_NN, precision=None):
    return lax.dot_general(a, b, dims, precision=precision, preferred_element_type=F32)


def _split2(x):
    hi = x.astype(BF16)
    lo = (x - hi.astype(F32)).astype(BF16)
    return hi, lo


def _split3(x):
    hi = x.astype(BF16)
    r = x - hi.astype(F32)
    mid = r.astype(BF16)
    lo = (r - mid.astype(F32)).astype(BF16)
    return hi, mid, lo


def _dot_x3(a, b, dims=_NN):
    ah, al = _split2(a)
    bh, bl = _split2(b)
    return _dot(ah, bh, dims) + _dot(ah, bl, dims) + _dot(al, bh, dims)


def _dot_w(a, wh, wl, dims=_NN):
    ah, al = _split2(a)
    return _dot(ah, wh, dims) + _dot(al, wh, dims) + _dot(ah, wl, dims)


def _dot_b(a, b, dims=_NN):
    return _dot(a.astype(BF16), b.astype(BF16), dims)


def _dot_sel(a, sel, dims=_NN, passes=3):
    parts = _split3(a)[:passes]
    out = _dot(parts[0], sel, dims)
    for p in parts[1:]:
        out = out + _dot(p, sel, dims)
    return out


def _softplus(x):
    return jnp.maximum(x, 0.0) + jnp.log(1.0 + jnp.exp(-jnp.abs(x)))


def _silu(x):
    return x * jax.nn.sigmoid(x)


def _gelu(x):
    return 0.5 * x * (1.0 + jnp.tanh(float(np.sqrt(2.0 / np.pi)) * (x + 0.044715 * (x * x * x))))


def _rmsnorm(x, g):
    return x * lax.rsqrt(jnp.mean(x * x, axis=-1, keepdims=True) + EPS) * g


def _const_spec(shape):
    nd = len(shape)
    return pl.BlockSpec(shape, lambda *_: (0,) * nd)


def _params(n_grid):
    return pltpu.CompilerParams(dimension_semantics=("arbitrary",) * n_grid,
                                vmem_limit_bytes=VMEM_LIMIT_BYTES)


def _ada_kernel(c_ref, w_ref, b_ref, o_ref):
    c = c_ref[...]
    o_ref[0] = _dot(_silu(c), w_ref[0], precision=HIGHEST) + b_ref[0]


def _ada_call(c_all, w_ada, b_ada):
    nb = c_all.shape[0]
    nt = 6 * D_MODEL // 1024
    return pl.pallas_call(
        _ada_kernel,
        grid=(DEPTH, nt),
        in_specs=[
            _const_spec((nb, D_MODEL)),
            pl.BlockSpec((1, D_MODEL, 1024), lambda l, j: (l, 0, j)),
            pl.BlockSpec((1, 1, 1024), lambda l, j: (l, 0, j)),
        ],
        out_specs=pl.BlockSpec((1, nb, 1024), lambda l, j: (l, 0, j)),
        out_shape=jax.ShapeDtypeStruct((DEPTH, nb, 6 * D_MODEL), F32),
        compiler_params=_params(2),
        name="ada_mod",
    )(c_all, w_ada, b_ada.reshape(DEPTH, 1, 6 * D_MODEL))


def _inproj_kernel(x_ref, g_ref, sc_ref, sh_ref, w_ref, za_ref, zb_ref, zc_ref):
    h = (_rmsnorm(x_ref[...], g_ref[...]) * sc_ref[...] + sh_ref[...]).astype(BF16)
    for o_ref, lo, hi in ((za_ref, 0, ZA_W), (zb_ref, ZA_W, ZA_W + ZB_W), (zc_ref, ZA_W + ZB_W, N_IN_PAD)):
        o_ref[...] = _dot(h, w_ref[:, lo:hi])


def _mod_spec(mod, tiles_per_group):
    _, r, w = mod.shape
    return pl.BlockSpec((None, r, w), lambda i: (i // tiles_per_group, 0, 0))


def _inproj_call(x, g, sc, sh, wh, tiles_per_group):
    t = x.shape[0]
    tm = TOKEN_TILE
    row = lambda w: pl.BlockSpec((tm, w), lambda i: (i, 0))
    return pl.pallas_call(
        _inproj_kernel,
        grid=(t // tm,),
        in_specs=[row(D_MODEL), _const_spec((1, D_MODEL)), _mod_spec(sc, tiles_per_group),
                  _mod_spec(sh, tiles_per_group), _const_spec(wh.shape)],
        out_specs=[row(ZA_W), row(ZB_W), row(ZC_W)],
        out_shape=[jax.ShapeDtypeStruct((t, ZA_W), F32), jax.ShapeDtypeStruct((t, ZB_W), F32),
                   jax.ShapeDtypeStruct((t, ZC_W), F32)],
        compiler_params=_params(1),
        name="inproj",
    )(x, g, sc, sh, wh)


class _SeqCfg:
    def __init__(self, seq_rows, hist, real, chunk):
        self.seq_rows = seq_rows
        self.hist = hist
        self.real = real
        self.chunk = chunk
        self.masked = hist > 0 or hist + real < seq_rows
        self.n_chunks = SUPER_BLOCK // chunk
        self.long_seq = seq_rows > SUPER_BLOCK
        self.sb_per_seq = max(seq_rows // SUPER_BLOCK, 1)
        self.seq_per_sb = max(SUPER_BLOCK // seq_rows, 1)
        assert self.long_seq or chunk == seq_rows


PROMPT_CFG = _SeqCfg(seq_rows=2048, hist=0, real=2048, chunk=16)
SAMPLE_CFG = _SeqCfg(seq_rows=SAMPLE_SEQ_ROWS, hist=SAMPLE_HIST, real=4, chunk=SAMPLE_SEQ_ROWS)


def _row_in_seq(cfg, n_rows):
    r = lax.broadcasted_iota(jnp.int32, (n_rows, 1), 0)
    return r % min(cfg.seq_rows, SUPER_BLOCK)


def _real_mask(cfg, n_rows):
    r = _row_in_seq(cfg, n_rows)
    return (r >= cfg.hist) & (r < cfg.hist + cfg.real)


def _causal_conv(x, prev_ref, w_ref):
    n = x.shape[0]
    ext = jnp.concatenate([prev_ref[...], x], axis=0)
    y = x * w_ref[CONV_W - 1:CONV_W, :]
    for k in range(CONV_W - 1):
        s = CONV_W - 1 - k
        y = y + ext[8 - s:8 - s + n, :] * w_ref[k:k + 1, :]
    prev_ref[...] = x[n - 8:n, :]
    return y


def _expand_state(s_cat, tile_ref, bd_ref):
    return _dot_sel(s_cat, tile_ref[...]) * bd_ref[...]


def _compress_state(s_bd, tile_t_ref):
    return _dot_sel(s_bd, tile_t_ref[...])


def _head_norm_gate(o, z, g, ones_ref):
    ms = _dot_sel(o * o, ones_ref[...], passes=2) * (1.0 / HEAD_DIM)
    return o * lax.rsqrt(ms + EPS) * g * _silu(z)


def _state_io(cfg, sb, k, st_ref, s0_ref, load):
    if cfg.long_seq:
        edge = 0 if load else cfg.sb_per_seq - 1
        kedge = 0 if load else cfg.n_chunks - 1
        return jnp.logical_and(sb % cfg.sb_per_seq == edge, k == kedge), 0
    return None, k


def _hgrn_kernel(z_ref, lbp_ref, g_ref, s0_ref, ltri_ref, ones_ref, tile_ref, tile_t_ref, bd_ref,
                 o_ref, sout_ref,
                 st_ref, q_s, k_s, g_s, qe_s, kt_s, egl_s, o_s, *, cfg, layer):
    sb = pl.program_id(0)
    c = cfg.chunk
    n = SUPER_BLOCK
    w = A_W
    lbp = lbp_ref[...]
    e = jnp.exp(lbp - jnp.max(lbp, axis=0, keepdims=True))
    lbs = e / jnp.sum(e, axis=0, keepdims=True)
    lb = jnp.sum(lbs[0:layer + 1, :], axis=0, keepdims=True) - lbs[0:1, :]

    aq = z_ref[:, 0:w]
    af = z_ref[:, w:2 * w]
    f = lb + (1.0 - lb) * jax.nn.sigmoid(af)
    lf = jnp.log(f)
    kk = 1.0 - f
    if cfg.masked:
        real = _real_mask(cfg, n)
        lf = jnp.where(real, lf, 0.0)
        kk = jnp.where(real, kk, 0.0)
    q = _silu(aq) * (HEAD_DIM ** -0.5)
    gcum = _dot(ltri_ref[...], lf, precision=HIGHEST)
    g3 = gcum.reshape(cfg.n_chunks, c, w)
    gl3 = g3[:, c - 1:c, :]
    q_s[...] = q
    k_s[...] = kk
    g_s[...] = gcum
    qe_s[...] = q * jnp.exp(gcum)
    kt_s[...] = (kk.reshape(cfg.n_chunks, c, w) * jnp.exp(gl3 - g3)).reshape(n, w)
    egl_s[...] = jnp.exp(gl3).reshape(cfg.n_chunks, w)

    jio = lax.broadcasted_iota(jnp.int32, (c, 1), 0)
    ones = ones_ref[...]
    bd = bd_ref[...]

    def chunk(k, carry):
        r0 = pl.multiple_of(k * c, c)
        rows = pl.ds(r0, c)
        pred, idx = _state_io(cfg, sb, k, st_ref, s0_ref, True)
        if pred is None:
            st_ref[...] = _expand_state(s0_ref[idx], tile_ref, bd_ref)
        else:
            @pl.when(pred)
            def _():
                st_ref[...] = _expand_state(s0_ref[idx], tile_ref, bd_ref)
        st = st_ref[...]
        q_c = q_s[rows, :]
        k_c = k_s[rows, :]
        g_c = g_s[rows, :]
        v_c = z_ref[rows, 2 * w:3 * w]
        o_inter = _dot(qe_s[rows, :].astype(BF16), st.astype(BF16), _NT)
        d_rows = []
        for i in range(c):
            dec = jnp.exp(jnp.minimum(g_c[i:i + 1, :] - g_c, 0.0))
            d_rows.append(jnp.where(jio <= i, k_c * dec * q_c[i:i + 1, :], 0.0))
        d = jnp.concatenate(d_rows, axis=0)
        a_b = _dot_sel(d, ones, passes=1)
        o_diag = jnp.sum(a_b.reshape(c, c, w) * v_c[None, :, :], axis=1)
        o_s[rows, :] = o_inter + o_diag
        upd = _dot_b(v_c, kt_s[rows, :], _TN)
        st_new = st * egl_s[pl.ds(k, 1), :] + bd * upd
        st_ref[...] = st_new
        pred, idx = _state_io(cfg, sb, k, st_ref, s0_ref, False)
        if pred is None:
            sout_ref[idx] = _compress_state(st_new, tile_t_ref)
        else:
            @pl.when(pred)
            def _():
                sout_ref[idx] = _compress_state(st_new, tile_t_ref)
        return carry

    lax.fori_loop(0, cfg.n_chunks, chunk, 0)
    o_ref[...] = _head_norm_gate(o_s[...], z_ref[:, 3 * w:4 * w], g_ref[...], ones_ref)


def _mixer_consts(heads, cfg):
    w = heads * HEAD_DIM
    lane_head = np.arange(w) // HEAD_DIM
    ones = (lane_head[:, None] == lane_head[None, :]).astype(np.float32)
    tile = (np.arange(HEAD_DIM)[:, None] == (np.arange(w) % HEAD_DIM)[None, :]).astype(np.float32)
    r = np.arange(SUPER_BLOCK)
    ltri = ((r[:, None] // cfg.chunk == r[None, :] // cfg.chunk) & (r[None, :] <= r[:, None])).astype(np.float32)
    return dict(ltri=jnp.asarray(ltri, F32), ones=jnp.asarray(ones, BF16), tile=jnp.asarray(tile, BF16),
                tile_t=jnp.asarray(tile.T, BF16), bd=jnp.asarray(ones, F32))


def _state_specs(cfg, w):
    blk = (1 if cfg.long_seq else cfg.seq_per_sb, w, HEAD_DIM)
    if cfg.long_seq:
        imap = lambda i: (i // cfg.sb_per_seq, 0, 0)
    else:
        imap = lambda i: (i, 0, 0)
    return pl.BlockSpec(blk, imap)


def _hgrn_call(z_a, lb_param, g_exp, s0_t, cfg, layer):
    rows = z_a.shape[0]
    w = A_W
    cst = _mixer_consts(A_HEADS, cfg)
    n_seq = s0_t.shape[0]
    vm = lambda shape: pltpu.VMEM(shape, F32)
    kern = functools.partial(_hgrn_kernel, cfg=cfg, layer=layer)
    return pl.pallas_call(
        kern,
        grid=(rows // SUPER_BLOCK,),
        in_specs=[pl.BlockSpec((SUPER_BLOCK, ZA_W), lambda i: (i, 0)), _const_spec((DEPTH, w)),
                  _const_spec((1, w)), _state_specs(cfg, w), _const_spec(cst["ltri"].shape),
                  _const_spec(cst["ones"].shape), _const_spec(cst["tile"].shape),
                  _const_spec(cst["tile_t"].shape), _const_spec(cst["bd"].shape)],
        out_specs=[pl.BlockSpec((SUPER_BLOCK, w), lambda i: (i, 0)), _state_specs(cfg, w)],
        out_shape=[jax.ShapeDtypeStruct((rows, w), F32), jax.ShapeDtypeStruct((n_seq, w, HEAD_DIM), F32)],
        scratch_shapes=[vm((w, w))] + [vm((SUPER_BLOCK, w))] * 5 + [vm((cfg.n_chunks, w)), vm((SUPER_BLOCK, w))],
        compiler_params=_params(1),
        name="hgrn2",
    )(z_a, lb_param, g_exp, s0_t, cst["ltri"], cst["ones"], cst["tile"], cst["tile_t"], cst["bd"])


def _lru_kernel(*refs, cfg):
    if cfg.long_seq:
        (z_ref, cw_ref, cb_ref, wg_ref, ba_ref, bx_ref, lp_ref,
         ob_ref, hs_ref, prev_ref, hc_ref) = refs
        inj_ref = None
    else:
        (z_ref, inj_ref, cw_ref, cb_ref, wg_ref, ba_ref, bx_ref, lp_ref,
         ob_ref, hs_ref, prev_ref, hc_ref) = refs
    sb = pl.program_id(0)
    n = SUPER_BLOCK
    w = B_W

    @pl.when(sb % cfg.sb_per_seq == 0)
    def _():
        prev_ref[...] = jnp.zeros_like(prev_ref)
        hc_ref[...] = jnp.zeros_like(hc_ref)

    xc = _causal_conv(z_ref[:, 0:w], prev_ref, cw_ref) + cb_ref[...]
    gates = _dot_b(xc, wg_ref[...])
    r = jax.nn.sigmoid(gates[:, 0:w] + ba_ref[...])
    ig = jax.nn.sigmoid(gates[:, w:2 * w] + bx_ref[...])
    log_a = -LRU_C * r * _softplus(-lp_ref[...])
    a = jnp.exp(log_a)
    b = jnp.sqrt(1.0 - jnp.exp(2.0 * log_a)) * ig * xc
    if cfg.masked:
        real = _real_mask(cfg, n)
        a = jnp.where(real, a, 1.0)
        b = jnp.where(real, b, 0.0)
    if inj_ref is not None:
        b = b + inj_ref[...]
    ris = _row_in_seq(cfg, n)
    d = 1
    while d < min(n, cfg.seq_rows):
        has = ris >= d
        a_sh = jnp.where(has, pltpu.roll(a, d, 0), 1.0)
        b_sh = jnp.where(has, pltpu.roll(b, d, 0), 0.0)
        b = b + a * b_sh
        a = a * a_sh
        d *= 2
    hs = b + a * hc_ref[...]
    if cfg.long_seq:
        hc_ref[...] = hs[n - 1:n, :]
    hs_ref[...] = hs
    ob_ref[...] = hs * _gelu(z_ref[:, w:2 * w])


def _lru_call(z_b, inj, cw, cb, wg, ba, bx, lp, cfg):
    rows = z_b.shape[0]
    w = B_W
    row = lambda width: pl.BlockSpec((SUPER_BLOCK, width), lambda i: (i, 0))
    ins = [z_b] + ([] if cfg.long_seq else [inj]) + [cw, cb, wg, ba, bx, lp]
    specs = [row(ZB_W)] + ([] if cfg.long_seq else [row(w)]) + [_const_spec(a.shape) for a in ins[-6:]]
    return pl.pallas_call(
        functools.partial(_lru_kernel, cfg=cfg),
        grid=(rows // SUPER_BLOCK,),
        in_specs=specs,
        out_specs=[row(w), row(w)],
        out_shape=[jax.ShapeDtypeStruct((rows, w), F32)] * 2,
        scratch_shapes=[pltpu.VMEM((8, w), F32), pltpu.VMEM((1, w), F32)],
        compiler_params=_params(1),
        name="rglru",
    )(*ins)


def _dn_kernel(z_ref, cw_ref, alog_ref, dtb_ref, g_ref, s0_ref, ltri_ref, ones_ref, tile_ref, tile_t_ref,
               bd_ref, expb_ref, expa_ref,
               o_ref, sout_ref,
               st_ref, prev_ref, q_s, k_s, v_s, g_s, eg_s, beta_s, kt_s, egl_s, o_s, *, cfg):
    sb = pl.program_id(0)
    c = cfg.chunk
    n = SUPER_BLOCK
    w = C_W

    @pl.when(sb % cfg.sb_per_seq == 0)
    def _():
        prev_ref[...] = jnp.zeros_like(prev_ref)

    qkv = _silu(_causal_conv(z_ref[:, 0:3 * w], prev_ref, cw_ref))
    q = qkv[:, 0:w]
    kx = qkv[:, w:2 * w]
    ones = ones_ref[...]
    q = q * lax.rsqrt(_dot_sel(q * q, ones, passes=2) + EPS) * (HEAD_DIM ** -0.5)
    kx = kx * lax.rsqrt(_dot_sel(kx * kx, ones, passes=2) + EPS)
    pc = z_ref[:, 4 * w:4 * w + 128]
    beta = jax.nn.sigmoid(_dot_sel(pc, expb_ref[...]))
    gdec = -jnp.exp(alog_ref[...]) * _softplus(_dot_sel(pc, expa_ref[...]) + dtb_ref[...])
    if cfg.masked:
        real = _real_mask(cfg, n)
        beta = jnp.where(real, beta, 0.0)
        gdec = jnp.where(real, gdec, 0.0)
    gcum = _dot(ltri_ref[...], gdec, precision=HIGHEST)
    g3 = gcum.reshape(cfg.n_chunks, c, w)
    gl3 = g3[:, c - 1:c, :]
    q_s[...] = q
    k_s[...] = kx
    v_s[...] = qkv[:, 2 * w:3 * w]
    g_s[...] = gcum
    eg_s[...] = jnp.exp(gcum)
    beta_s[...] = beta
    kt_s[...] = (kx.reshape(cfg.n_chunks, c, w) * jnp.exp(gl3 - g3)).reshape(n, w)
    egl_s[...] = jnp.exp(gl3).reshape(cfg.n_chunks, w)

    iio = lax.broadcasted_iota(jnp.int32, (c, 1), 0)
    bd = bd_ref[...]

    def chunk(k, carry):
        r0 = pl.multiple_of(k * c, c)
        rows = pl.ds(r0, c)
        pred, idx = _state_io(cfg, sb, k, st_ref, s0_ref, True)
        if pred is None:
            st_ref[...] = _expand_state(s0_ref[idx], tile_ref, bd_ref)
        else:
            @pl.when(pred)
            def _():
                st_ref[...] = _expand_state(s0_ref[idx], tile_ref, bd_ref)
        st = st_ref[...]
        q_c = q_s[rows, :]
        k_c = k_s[rows, :]
        g_c = g_s[rows, :]
        eg_c = eg_s[rows, :]
        beta_c = beta_s[rows, :]
        qk_s = _dot_b(jnp.concatenate([q_c, k_c], axis=0), st)
        q_st = qk_s[0:c, :]
        k_st = qk_s[c:2 * c, :]
        d_rows = [k_c * k_c[j:j + 1, :] for j in range(c)] + [q_c * k_c[j:j + 1, :] for j in range(c)]
        dots = _dot_sel(jnp.concatenate(d_rows, axis=0), ones, passes=1)
        e = beta_c * (v_s[rows, :] - eg_c * k_st)
        o = eg_c * q_st
        for j in range(c):
            dec = jnp.exp(jnp.minimum(g_c - g_c[j:j + 1, :], 0.0))
            m_col = jnp.where(iio > j, beta_c * dots[j * c:(j + 1) * c, :] * dec, 0.0)
            e_j = e[j:j + 1, :]
            e = e - m_col * e_j
            qk_col = jnp.where(iio >= j, dots[(c + j) * c:(c + j + 1) * c, :] * dec, 0.0)
            o = o + qk_col * e_j
        o_s[rows, :] = o
        upd = _dot_b(kt_s[rows, :], e, _TN)
        st_new = st * egl_s[pl.ds(k, 1), :] + bd * upd
        st_ref[...] = st_new
        pred, idx = _state_io(cfg, sb, k, st_ref, s0_ref, False)
        if pred is None:
            sout_ref[idx] = _compress_state(st_new, tile_t_ref)
        else:
            @pl.when(pred)
            def _():
                sout_ref[idx] = _compress_state(st_new, tile_t_ref)
        return carry

    lax.fori_loop(0, cfg.n_chunks, chunk, 0)
    o_ref[...] = _head_norm_gate(o_s[...], z_ref[:, 3 * w:4 * w], g_ref[...], ones_ref)


def _dn_call(z_c, cw, alog_exp, dtb_exp, g_exp, s0, cfg):
    rows = z_c.shape[0]
    w = C_W
    cst = _mixer_consts(C_HEADS, cfg)
    lane_head = np.arange(w) // HEAD_DIM
    expb = (np.arange(128)[:, None] == lane_head[None, :]).astype(np.float32)
    expa = (np.arange(128)[:, None] == (lane_head[None, :] + C_HEADS)).astype(np.float32)
    expb = jnp.asarray(expb, BF16)
    expa = jnp.asarray(expa, BF16)
    n_seq = s0.shape[0]
    vm = lambda shape: pltpu.VMEM(shape, F32)
    return pl.pallas_call(
        functools.partial(_dn_kernel, cfg=cfg),
        grid=(rows // SUPER_BLOCK,),
        in_specs=[pl.BlockSpec((SUPER_BLOCK, ZC_W), lambda i: (i, 0)), _const_spec(cw.shape),
                  _const_spec((1, w)), _const_spec((1, w)), _const_spec((1, w)), _state_specs(cfg, w),
                  _const_spec(cst["ltri"].shape), _const_spec(cst["ones"].shape),
                  _const_spec(cst["tile"].shape), _const_spec(cst["tile_t"].shape),
                  _const_spec(cst["bd"].shape), _const_spec(expb.shape), _const_spec(expa.shape)],
        out_specs=[pl.BlockSpec((SUPER_BLOCK, w), lambda i: (i, 0)), _state_specs(cfg, w)],
        out_shape=[jax.ShapeDtypeStruct((rows, w), F32), jax.ShapeDtypeStruct((n_seq, w, HEAD_DIM), F32)],
        scratch_shapes=[vm((w, w)), vm((8, 3 * w))] + [vm((SUPER_BLOCK, w))] * 7
        + [vm((cfg.n_chunks, w)), vm((SUPER_BLOCK, w))],
        compiler_params=_params(1),
        name="deltanet",
    )(z_c, cw, alog_exp, dtb_exp, g_exp, s0, cst["ltri"], cst["ones"], cst["tile"], cst["tile_t"],
      cst["bd"], expb, expa)


def _outproj_kernel(x_ref, oa_ref, ob_ref, oc_ref, gt_ref, g_ref, sc_ref, sh_ref,
                    wo_ref, wqh_ref, wql_ref, x1_ref, h2_ref, q_ref):
    mo = _dot_b(oa_ref[...], wo_ref[0:A_W, :])
    mo = mo + _dot_b(ob_ref[...], wo_ref[A_W:A_W + B_W, :])
    mo = mo + _dot_b(oc_ref[...], wo_ref[A_W + B_W:, :])
    x1 = x_ref[...] + gt_ref[...] * mo
    x1_ref[...] = x1
    h2 = _rmsnorm(x1, g_ref[...]) * sc_ref[...] + sh_ref[...]
    h2_ref[...] = h2.astype(BF16)
    q_ref[...] = _dot_w(h2, wqh_ref[...], wql_ref[...])


def _outproj_call(x, oa, ob, oc, gt, g, sc, sh, wo, wqh, wql, tiles_per_group):
    t = x.shape[0]
    tm = TOKEN_TILE
    row = lambda w: pl.BlockSpec((tm, w), lambda i: (i, 0))
    return pl.pallas_call(
        _outproj_kernel,
        grid=(t // tm,),
        in_specs=[row(D_MODEL), row(A_W), row(B_W), row(C_W), _mod_spec(gt, tiles_per_group),
                  _const_spec((1, D_MODEL)), _mod_spec(sc, tiles_per_group), _mod_spec(sh, tiles_per_group),
                  _const_spec(wo.shape), _const_spec(wqh.shape), _const_spec(wql.shape)],
        out_specs=[row(D_MODEL), row(D_MODEL), row(D_MODEL)],
        out_shape=[jax.ShapeDtypeStruct((t, D_MODEL), F32), jax.ShapeDtypeStruct((t, D_MODEL), BF16),
                   jax.ShapeDtypeStruct((t, D_MODEL), F32)],
        compiler_params=_params(1),
        name="outproj",
    )(x, oa, ob, oc, gt, g, sc, sh, wo, wqh, wql)


def _top16(s):
    rank = jnp.full(s.shape, float(PEER_TOPK), F32)
    vals = []
    for r in range(PEER_TOPK):
        m = jnp.max(s, axis=0, keepdims=True)
        hit = s == m
        rank = jnp.where(hit, float(r), rank)
        s = jnp.where(hit, -jnp.inf, s)
        vals.append(m)
    return jnp.concatenate(vals, axis=0), rank


def _peer_kernel(h2_ref, q_ref, x1_ref, gt_ref, fg_ref, k1_ref, k2_ref, u_ref, vt_ref, o_ref,
                 cnt_s, e1_s, rank2_s, e2_s, acc_s, *, final):
    j = pl.program_id(1)
    tt = PEER_TOKEN_TILE
    nk = PEER_NKEYS
    a_per_step = PEER_EXPERT_BLOCK // nk

    @pl.when(j == 0)
    def _():
        acc_s[...] = jnp.zeros_like(acc_s)

        def head(h, carry):
            qh = q_ref[:, pl.ds(pl.multiple_of(h * nk, nk), nk)]
            s1 = _dot(k1_ref[...], qh, _NT, precision=HIGHEST)
            s2 = _dot(k2_ref[...], qh, _NT, precision=HIGHEST)
            v1, rank1 = _top16(s1)
            v2, rank2 = _top16(s2)
            c3 = v1[:, None, :] + v2[None, :, :]
            jrow = lax.broadcasted_iota(jnp.int32, (8, 1), 0)
            parts = [v1[0:1, :] + v2, v1[1:2, :] + v2[0:8, :]]
            for i in range(2, 8):
                parts.append(jnp.where(jrow < PEER_TOPK // (i + 1), v1[i:i + 1, :] + v2[0:8, :], -jnp.inf))
            parts.append(v1[8:16, :] + v2[0:1, :])
            cand = jnp.concatenate(parts, axis=0)
            m = None
            for _ in range(PEER_TOPK):
                m = jnp.max(cand, axis=0, keepdims=True)
                cand = jnp.where(cand == m, -jnp.inf, cand)
            tau = m
            sel = c3 >= tau[None, :, :]
            m0 = v1[0:1, :] + v2[0:1, :]
            zsum = jnp.sum(jnp.where(sel, jnp.exp(c3 - m0[None, :, :]), 0.0).reshape(-1, tt), axis=0, keepdims=True)
            n_i = jnp.sum(jnp.where(sel, 1.0, 0.0), axis=1)
            rank1 = rank1.astype(BF16)
            cnt = jnp.zeros((nk, tt), BF16)
            for i in range(PEER_TOPK):
                cnt = cnt + jnp.where(rank1 == i, n_i[i:i + 1, :].astype(BF16), jnp.zeros((), BF16))
            cnt_s[h] = cnt.astype(F32)
            e1_s[h] = jnp.exp(s1 - v1[0:1, :])
            rank2_s[h] = rank2.astype(BF16)
            e2_s[h] = (jnp.exp(s2 - v2[0:1, :]) / zsum).astype(BF16)
            return carry

        lax.fori_loop(0, PEER_HEADS, head, 0)

    h2 = h2_ref[...]
    n_sub = PEER_EXPERT_BLOCK // PEER_SUB_BLOCK
    a_per_sub = PEER_SUB_BLOCK // nk
    hidden = lambda s: _dot(u_ref[s * PEER_SUB_BLOCK:(s + 1) * PEER_SUB_BLOCK, :], h2, _NT)
    hid_next = hidden(0)
    for s in range(n_sub):
        e0 = s * PEER_SUB_BLOCK
        hid = hid_next
        if s + 1 < n_sub:
            hid_next = hidden(s + 1)
        coefs = []
        for al in range(a_per_sub):
            a = j * a_per_step + s * a_per_sub + al
            wsum = jnp.zeros((nk, tt), BF16)
            for h in range(PEER_HEADS):
                cnt_row = cnt_s[h, pl.ds(a, 1), :].astype(BF16)
                e1_row = e1_s[h, pl.ds(a, 1), :].astype(BF16)
                wsum = wsum + jnp.where(rank2_s[h] < cnt_row, e2_s[h] * e1_row, jnp.zeros((), BF16))
            coefs.append(_gelu(hid[al * nk:(al + 1) * nk, :].astype(BF16)) * wsum)
        coef = jnp.concatenate(coefs, axis=0)
        acc_s[...] += _dot(vt_ref[:, e0:e0 + PEER_SUB_BLOCK], coef)

    @pl.when(j == pl.num_programs(1) - 1)
    def _():
        x2 = x1_ref[...] + gt_ref[...] * acc_s[...].T
        if final:
            x2 = _rmsnorm(x2, fg_ref[...])
        o_ref[...] = x2


def _peer_call(h2b, q, x1, gt, fg, k1p, k2p, u_b, vt_b, tiles_per_group, final):
    t = h2b.shape[0]
    tt = PEER_TOKEN_TILE
    nb = PEER_EXPERT_BLOCK
    row = lambda w: pl.BlockSpec((tt, w), lambda i, j: (i, 0))
    _, r, w = gt.shape
    tab = lambda dt: pltpu.VMEM((PEER_HEADS, PEER_NKEYS, tt), dt)
    return pl.pallas_call(
        functools.partial(_peer_kernel, final=final),
        grid=(t // tt, PEER_N // nb),
        in_specs=[row(D_MODEL), row(D_MODEL), row(D_MODEL),
                  pl.BlockSpec((None, r, w), lambda i, j: (i // tiles_per_group, 0, 0)),
                  pl.BlockSpec((1, D_MODEL), lambda i, j: (0, 0)),
                  pl.BlockSpec((PEER_NKEYS, PEER_NKEYS), lambda i, j: (0, 0)),
                  pl.BlockSpec((PEER_NKEYS, PEER_NKEYS), lambda i, j: (0, 0)),
                  pl.BlockSpec((nb, D_MODEL), lambda i, j: (j, 0)),
                  pl.BlockSpec((D_MODEL, nb), lambda i, j: (0, j))],
        out_specs=row(D_MODEL),
        out_shape=jax.ShapeDtypeStruct((t, D_MODEL), F32),
        scratch_shapes=[tab(F32), tab(F32), tab(BF16), tab(BF16), pltpu.VMEM((D_MODEL, tt), F32)],
        compiler_params=_params(2),
        name="peer",
    )(h2b, q, x1, gt, fg, k1p, k2p, u_b, vt_b)


def _block_diag(wblk):
    n, d, e = wblk.shape
    eye = jnp.eye(n, dtype=wblk.dtype)
    return (eye[:, None, :, None] * wblk[:, :, None, :]).reshape(n * d, n * e)


def _hilo(w):
    hi = w.astype(BF16)
    return hi, (w - hi.astype(F32)).astype(BF16)


def _pad_sample(z, hist=None):
    bsz = z.shape[0] // 4
    w = z.shape[1]
    z3 = z.reshape(bsz, 4, w)
    h3 = jnp.zeros((bsz, SAMPLE_HIST, w), F32)
    if hist is not None:
        h3 = h3.at[:, :, :hist.shape[-1]].set(hist)
    return jnp.concatenate([h3, z3, jnp.zeros((bsz, 1, w), F32)], axis=1).reshape(bsz * SAMPLE_SEQ_ROWS, w)


def _unpad_sample(o):
    bsz = o.shape[0] // SAMPLE_SEQ_ROWS
    return o.reshape(bsz, SAMPLE_SEQ_ROWS, -1)[:, SAMPLE_HIST:SAMPLE_HIST + 4].reshape(bsz * 4, -1)


def _layer_weights(l, w_in, w_out, peer_wq, peer_k1, peer_k2, peer_u, peer_v, lru_wa, lru_wx):
    w_in_p = jnp.pad(w_in[l], ((0, 0), (0, N_IN_PAD - N_IN)))
    wih = w_in_p.astype(BF16)
    wo = w_out[l].astype(BF16)
    wqh, wql = _hilo(peer_wq[l])
    wg = jnp.concatenate([_block_diag(lru_wa[l]), _block_diag(lru_wx[l])], axis=1).astype(BF16)
    half = PEER_NKEYS // 2
    k1p = jnp.pad(peer_k1[l], ((0, 0), (0, half)))
    k2p = jnp.pad(peer_k2[l], ((0, 0), (half, 0)))
    u_b = peer_u[l].astype(BF16)
    vt_b = peer_v[l].T.astype(BF16)
    return dict(wih=wih, wo=wo, wqh=wqh, wql=wql, wg=wg,
                k1p=k1p, k2p=k2p, u_b=u_b, vt_b=vt_b)


def _trunk(x, mods, states, is_prompt, p, lw, l, final_g):
    t = x.shape[0]
    if is_prompt:
        sh1, sc1, gt1, sh2, sc2, gt2 = mods
    else:
        sh1, sc1, gt1, sh2, sc2 = (m.reshape(-1, TOKEN_TILE, D_MODEL) for m in mods[:5])
        gt2 = mods[5].reshape(-1, PEER_TOKEN_TILE, D_MODEL)
    if is_prompt:
        cfg = PROMPT_CFG
        tiles_tok = 2048 // TOKEN_TILE
        tiles_peer = 2048 // PEER_TOKEN_TILE
        n_seq = t // 2048
    else:
        cfg = SAMPLE_CFG
        tiles_tok = 1
        tiles_peer = 1
        n_seq = t // 4
    row1 = lambda v: v.reshape(1, -1)
    z_a, z_b, z_c = _inproj_call(x, row1(p['norm1_g']), sc1, sh1, lw['wih'], tiles_tok)
    if is_prompt:
        s_a = jnp.zeros((n_seq, A_W, HEAD_DIM), F32)
        s_c = jnp.zeros((n_seq, C_W, HEAD_DIM), F32)
        za_m, zb_m, zc_m = z_a, z_b, z_c
        inj = None
    else:
        st_hgrn, st_lru_h, st_lru_conv, st_dn, st_dn_conv = states
        s_a = jnp.swapaxes(st_hgrn, -1, -2).reshape(n_seq, A_W, HEAD_DIM)
        s_c = st_dn.reshape(n_seq, C_W, HEAD_DIM)
        za_m = _pad_sample(z_a)
        zb_m = _pad_sample(z_b, st_lru_conv)
        zc_m = _pad_sample(z_c, st_dn_conv)
        inj = jnp.zeros((n_seq, SAMPLE_SEQ_ROWS, B_W), F32).at[:, 0].set(st_lru_h).reshape(-1, B_W)
    rep = lambda v, h: row1(jnp.tile(v, h))
    oa, s_a_new = _hgrn_call(za_m, p['lb_param'], rep(p['a_norm_g'], A_HEADS), s_a, cfg, l)
    ob, hs = _lru_call(zb_m, inj, p['lru_conv_w'], row1(p['lru_conv_b']), lw['wg'],
                       row1(p['lru_ba']), row1(p['lru_bx']), row1(p['lru_L']), cfg)
    oc, s_c_new = _dn_call(zc_m, p['dn_conv_w'], row1(jnp.repeat(p['dn_A_log'], HEAD_DIM)),
                           row1(jnp.repeat(p['dn_dt_bias'], HEAD_DIM)), rep(p['dn_norm_g'], C_HEADS), s_c, cfg)
    if is_prompt:
        seq = 2048
        h_t = hs.reshape(n_seq, seq, B_W)[:, -1]
        buf_b = z_b.reshape(n_seq, seq, ZB_W)[:, -SAMPLE_HIST:, :B_W]
        buf_c = z_c.reshape(n_seq, seq, ZC_W)[:, -SAMPLE_HIST:, :3 * C_W]
    else:
        oa, ob, oc = _unpad_sample(oa), _unpad_sample(ob), _unpad_sample(oc)
        h_t = hs.reshape(n_seq, SAMPLE_SEQ_ROWS, B_W)[:, -1]
        buf_b = z_b.reshape(n_seq, 4, ZB_W)[:, 1:, :B_W]
        buf_c = z_c.reshape(n_seq, 4, ZC_W)[:, 1:, :3 * C_W]
    new_states = (jnp.swapaxes(s_a_new.reshape(n_seq, A_HEADS, HEAD_DIM, HEAD_DIM), -1, -2), h_t, buf_b,
                  s_c_new.reshape(n_seq, C_HEADS, HEAD_DIM, HEAD_DIM), buf_c)
    x1, h2b, q = _outproj_call(x, oa, ob, oc, gt1, row1(p['norm2_g']), sc2, sh2,
                               lw['wo'], lw['wqh'], lw['wql'], tiles_tok)
    x2 = _peer_call(h2b, q, x1, gt2, row1(final_g), lw['k1p'], lw['k2p'], lw['u_b'], lw['vt_b'],
                    tiles_peer, final=(l == DEPTH - 1))
    return x2, new_states


def kernel(x_prompt, x_sample, state_hgrn, state_lru_h, state_lru_conv, state_dn, state_dn_conv,
           c_prompt, c_sample, w_ada, b_ada, norm1_g, norm2_g, w_in, lb_param, a_norm_g,
           lru_conv_w, lru_conv_b, lru_wa, lru_ba, lru_wx, lru_bx, lru_L,
           dn_conv_w, dn_A_log, dn_dt_bias, dn_norm_g, w_out,
           peer_wq, peer_k1, peer_k2, peer_u, peer_v, final_norm_g):
    n_p, seq, _ = x_prompt.shape
    n_s, dec_seq, _ = x_sample.shape
    mod = _ada_call(jnp.concatenate([c_prompt, c_sample], axis=0), w_ada, b_ada)
    xp = x_prompt.reshape(n_p * seq, D_MODEL)
    xs = x_sample.reshape(n_s * dec_seq, D_MODEL)
    sample_states = (state_hgrn, state_lru_h, state_lru_conv, state_dn, state_dn_conv)
    p_new, s_new = [], []
    for l in range(DEPTH):
        p = dict(norm1_g=norm1_g[l], norm2_g=norm2_g[l], lb_param=lb_param, a_norm_g=a_norm_g[l],
                 lru_conv_w=lru_conv_w[l], lru_conv_b=lru_conv_b[l], lru_ba=lru_ba[l], lru_bx=lru_bx[l],
                 lru_L=lru_L[l], dn_conv_w=dn_conv_w[l], dn_A_log=dn_A_log[l], dn_dt_bias=dn_dt_bias[l],
                 dn_norm_g=dn_norm_g[l])
        lw = _layer_weights(l, w_in, w_out, peer_wq, peer_k1, peer_k2, peer_u, peer_v, lru_wa, lru_wx)
        parts = jnp.split(mod[l], 6, axis=-1)
        parts = [m + 1.0 if i in (1, 4) else m for i, m in enumerate(parts)]
        mods_p = [m[:n_p].reshape(n_p, 1, D_MODEL) for m in parts]
        mods_s = [jnp.repeat(m[n_p:], dec_seq, axis=0) for m in parts]
        xp, st_p = _trunk(xp, mods_p, None, True, p, lw, l, final_norm_g)
        xs, st_s = _trunk(xs, mods_s, tuple(s[l] for s in sample_states), False, p, lw, l, final_norm_g)
        p_new.append(st_p)
        s_new.append(st_s)
    stack = lambda sts: [jnp.stack([s[i] for s in sts]) for i in range(5)]
    p_st = stack(p_new)
    s_st = stack(s_new)
    return (xp.reshape(n_p, seq, D_MODEL), xs.reshape(n_s, dec_seq, D_MODEL), *p_st, *s_st)
```

```python
import functools

import jax
import jax.numpy as jnp
import numpy as np
from jax import lax
from jax.experimental import pallas as pl
from jax.experimental.pallas import tpu as pltpu

F32 = jnp.float32
BF16 = jnp.bfloat16
HIGHEST = lax.Precision.HIGHEST

D_MODEL = 1024
DEPTH = 2
HEAD_DIM = 64
A_HEADS = 4
A_W = 256
B_W = 384
C_HEADS = 6
C_W = 384
LRU_C = 8.0
CONV_W = 4
N_IN = 3340
N_IN_PAD = 3456
ZA_W = 1024
ZB_W = 768
ZC_W = 1664
PEER_HEADS = 8
PEER_NKEYS = 128
PEER_TOPK = 16
PEER_N = PEER_NKEYS * PEER_NKEYS
EPS = 1e-6

LANES = 128
SUPER_BLOCK = 256
SAMPLE_SEQ_ROWS = 8
SAMPLE_HIST = CONV_W - 1
TOKEN_TILE = 256
PEER_TOKEN_TILE = 512
PEER_EXPERT_BLOCK = 2048
PEER_SUB_BLOCK = 512
PEER_ROUTE_TILE = 256
VMEM_LIMIT_BYTES = 56 * 1024 * 1024

_NN = (((1,), (0,)), ((), ()))
_NT = (((1,), (1,)), ((), ()))
_TN = (((0,), (0,)), ((), ()))


def _dot(a, b, dims=_NN, precision=None):
    return lax.dot_general(a, b, dims, precision=precision, preferred_element_type=F32)


def _split2(x):
    hi = x.astype(BF16)
    lo = (x - hi.astype(F32)).astype(BF16)
    return hi, lo


def _split3(x):
    hi = x.astype(BF16)
    r = x - hi.astype(F32)
    mid = r.astype(BF16)
    lo = (r - mid.astype(F32)).astype(BF16)
    return hi, mid, lo


def _dot_x3(a, b, dims=_NN):
    ah, al = _split2(a)
    bh, bl = _split2(b)
    return _dot(ah, bh, dims) + _dot(ah, bl, dims) + _dot(al, bh, dims)


def _dot_w(a, wh, wl, dims=_NN):
    ah, al = _split2(a)
    return _dot(ah, wh, dims) + _dot(al, wh, dims) + _dot(ah, wl, dims)


def _dot_b(a, b, dims=_NN):
    return _dot(a.astype(BF16), b.astype(BF16), dims)


def _dot_sel(a, sel, dims=_NN, passes=3):
    parts = _split3(a)[:passes]
    out = _dot(parts[0], sel, dims)
    for p in parts[1:]:
        out = out + _dot(p, sel, dims)
    return out


def _sel_dot(sel, b, passes=3):
    parts = _split3(b)[:passes]
    out = _dot(sel, parts[0])
    for p in parts[1:]:
        out = out + _dot(sel, p)
    return out


def _softplus(x):
    return jnp.maximum(x, 0.0) + jnp.log(1.0 + jnp.exp(-jnp.abs(x)))


def _silu(x):
    return x * jax.nn.sigmoid(x)


def _gelu(x):
    return 0.5 * x * (1.0 + jnp.tanh(float(np.sqrt(2.0 / np.pi)) * (x + 0.044715 * (x * x * x))))


def _rmsnorm(x, g):
    return x * lax.rsqrt(jnp.mean(x * x, axis=-1, keepdims=True) + EPS) * g


def _const_spec(shape):
    nd = len(shape)
    return pl.BlockSpec(shape, lambda *_: (0,) * nd)


def _params(n_grid):
    return pltpu.CompilerParams(dimension_semantics=("arbitrary",) * n_grid,
                                vmem_limit_bytes=VMEM_LIMIT_BYTES)


def _ada_kernel(c_ref, w_ref, b_ref, o_ref):
    c = c_ref[...]
    o_ref[0] = _dot(_silu(c), w_ref[0], precision=HIGHEST) + b_ref[0]


def _ada_call(c_all, w_ada, b_ada):
    nb = c_all.shape[0]
    nt = 6 * D_MODEL // 1024
    return pl.pallas_call(
        _ada_kernel,
        grid=(DEPTH, nt),
        in_specs=[
            _const_spec((nb, D_MODEL)),
            pl.BlockSpec((1, D_MODEL, 1024), lambda l, j: (l, 0, j)),
            pl.BlockSpec((1, 1, 1024), lambda l, j: (l, 0, j)),
        ],
        out_specs=pl.BlockSpec((1, nb, 1024), lambda l, j: (l, 0, j)),
        out_shape=jax.ShapeDtypeStruct((DEPTH, nb, 6 * D_MODEL), F32),
        compiler_params=_params(2),
        name="ada_mod",
    )(c_all, w_ada, b_ada.reshape(DEPTH, 1, 6 * D_MODEL))


def _inproj_kernel(x_ref, g_ref, sc_ref, sh_ref, w_ref, za_ref, zb_ref, zc_ref):
    h = (_rmsnorm(x_ref[...], g_ref[...]) * sc_ref[...] + sh_ref[...]).astype(BF16)
    for o_ref, lo, hi in ((za_ref, 0, ZA_W), (zb_ref, ZA_W, ZA_W + ZB_W), (zc_ref, ZA_W + ZB_W, N_IN_PAD)):
        o_ref[...] = _dot(h, w_ref[:, lo:hi])


def _mod_spec(mod, tiles_per_group):
    _, r, w = mod.shape
    return pl.BlockSpec((None, r, w), lambda i: (i // tiles_per_group, 0, 0))


def _inproj_call(x, g, sc, sh, wh, tiles_per_group):
    t = x.shape[0]
    tm = TOKEN_TILE
    row = lambda w: pl.BlockSpec((tm, w), lambda i: (i, 0))
    return pl.pallas_call(
        _inproj_kernel,
        grid=(t // tm,),
        in_specs=[row(D_MODEL), _const_spec((1, D_MODEL)), _mod_spec(sc, tiles_per_group),
                  _mod_spec(sh, tiles_per_group), _const_spec(wh.shape)],
        out_specs=[row(ZA_W), row(ZB_W), row(ZC_W)],
        out_shape=[jax.ShapeDtypeStruct((t, ZA_W), F32), jax.ShapeDtypeStruct((t, ZB_W), F32),
                   jax.ShapeDtypeStruct((t, ZC_W), F32)],
        compiler_params=_params(1),
        name="inproj",
    )(x, g, sc, sh, wh)


class _SeqCfg:
    def __init__(self, seq_rows, hist, real, chunk):
        self.seq_rows = seq_rows
        self.hist = hist
        self.real = real
        self.chunk = chunk
        self.masked = hist > 0 or hist + real < seq_rows
        self.n_chunks = SUPER_BLOCK // chunk
        self.long_seq = seq_rows > SUPER_BLOCK
        self.sb_per_seq = max(seq_rows // SUPER_BLOCK, 1)
        self.seq_per_sb = max(SUPER_BLOCK // seq_rows, 1)
        assert self.long_seq or chunk == seq_rows


PROMPT_CFG = _SeqCfg(seq_rows=2048, hist=0, real=2048, chunk=16)
SAMPLE_CFG = _SeqCfg(seq_rows=SAMPLE_SEQ_ROWS, hist=SAMPLE_HIST, real=4, chunk=SAMPLE_SEQ_ROWS)


def _row_in_seq(cfg, n_rows):
    r = lax.broadcasted_iota(jnp.int32, (n_rows, 1), 0)
    return r % min(cfg.seq_rows, SUPER_BLOCK)


def _real_mask(cfg, n_rows):
    r = _row_in_seq(cfg, n_rows)
    return (r >= cfg.hist) & (r < cfg.hist + cfg.real)


def _causal_conv(x, xp_ref, w_ref):
    n = x.shape[0]
    xp_ref[8:8 + n, :] = x
    y = x * w_ref[CONV_W - 1:CONV_W, :]
    for k in range(CONV_W - 1):
        s = CONV_W - 1 - k
        y = y + xp_ref[8 - s:8 - s + n, :] * w_ref[k:k + 1, :]
    xp_ref[0:8, :] = x[n - 8:n, :]
    return y


def _expand_state(s_cat, tile_ref, bd_ref):
    return _dot_sel(s_cat, tile_ref[...]) * bd_ref[...]


def _compress_state(s_bd, tile_t_ref):
    return _dot_sel(s_bd, tile_t_ref[...])


def _head_norm_gate(o, z, g, ones_ref):
    ms = _dot_sel(o * o, ones_ref[...], passes=2) * (1.0 / HEAD_DIM)
    return o * lax.rsqrt(ms + EPS) * g * _silu(z)


def _run_chunks(cfg, sb, step, st_ref, s0_ref, sout_ref, tile_ref, tile_t_ref, bd_ref):
    if cfg.long_seq:
        @pl.when(sb % cfg.sb_per_seq == 0)
        def _():
            st_ref[...] = _expand_state(s0_ref[0], tile_ref, bd_ref)

        def body(k, carry):
            st_ref[...] = step(k, st_ref[...])
            return carry

        lax.fori_loop(0, cfg.n_chunks, body, 0, unroll=2)

        @pl.when(sb % cfg.sb_per_seq == cfg.sb_per_seq - 1)
        def _():
            sout_ref[0] = _compress_state(st_ref[...], tile_t_ref)
    else:
        def body(k, carry):
            st_new = step(k, _expand_state(s0_ref[k], tile_ref, bd_ref))
            sout_ref[k] = _compress_state(st_new, tile_t_ref)
            return carry

        lax.fori_loop(0, cfg.n_chunks, body, 0)


def _hgrn_kernel(z_ref, lbp_ref, g_ref, s0_ref, ltri_ref, ones_ref, tile_ref, tile_t_ref, bd_ref,
                 o_ref, sout_ref,
                 st_ref, q_s, k_s, g_s, qe_s, kt_s, egl_s, o_s, *, cfg, layer):
    sb = pl.program_id(0)
    c = cfg.chunk
    n = SUPER_BLOCK
    w = A_W
    lbp = lbp_ref[...]
    e = jnp.exp(lbp - jnp.max(lbp, axis=0, keepdims=True))
    lbs = e / jnp.sum(e, axis=0, keepdims=True)
    lb = jnp.sum(lbs[0:layer + 1, :], axis=0, keepdims=True) - lbs[0:1, :]

    aq = z_ref[:, 0:w]
    af = z_ref[:, w:2 * w]
    f = lb + (1.0 - lb) * jax.nn.sigmoid(af)
    lf = jnp.log(f)
    kk = 1.0 - f
    if cfg.masked:
        real = _real_mask(cfg, n)
        lf = jnp.where(real, lf, 0.0)
        kk = jnp.where(real, kk, 0.0)
    q = _silu(aq) * (HEAD_DIM ** -0.5)
    gcum = _sel_dot(ltri_ref[...], lf)
    g3 = gcum.reshape(cfg.n_chunks, c, w)
    gl3 = g3[:, c - 1:c, :]
    q_s[...] = q
    k_s[...] = kk
    g_s[...] = gcum
    qe_s[...] = q * jnp.exp(gcum)
    kt_s[...] = (kk.reshape(cfg.n_chunks, c, w) * jnp.exp(gl3 - g3)).reshape(n, w)
    egl_s[...] = jnp.exp(gl3).reshape(cfg.n_chunks, w)

    jio = lax.broadcasted_iota(jnp.int32, (c, 1), 0)
    ones = ones_ref[...]
    bd = bd_ref[...]

    def chunk(k, st):
        r0 = pl.multiple_of(k * c, c)
        rows = pl.ds(r0, c)
        q_c = q_s[rows, :]
        k_c = k_s[rows, :]
        g_c = g_s[rows, :]
        v_c = z_ref[rows, 2 * w:3 * w]
        o_inter = _dot(qe_s[rows, :].astype(BF16), st.astype(BF16), _NT)
        d_rows = []
        for i in range(c):
            dec = jnp.exp(jnp.minimum(g_c[i:i + 1, :] - g_c, 0.0))
            d_rows.append(jnp.where(jio <= i, k_c * dec * q_c[i:i + 1, :], 0.0))
        d = jnp.concatenate(d_rows, axis=0)
        a_b = _dot_sel(d, ones, passes=1)
        o_diag = jnp.sum(a_b.reshape(c, c, w) * v_c[None, :, :], axis=1)
        o_s[rows, :] = o_inter + o_diag
        upd = _dot_b(v_c, kt_s[rows, :], _TN)
        return st * egl_s[pl.ds(k, 1), :] + bd * upd

    _run_chunks(cfg, sb, chunk, st_ref, s0_ref, sout_ref, tile_ref, tile_t_ref, bd_ref)
    o_ref[...] = _head_norm_gate(o_s[...], z_ref[:, 3 * w:4 * w], g_ref[...], ones_ref)


def _mixer_consts(heads, cfg):
    w = heads * HEAD_DIM
    lane_head = np.arange(w) // HEAD_DIM
    ones = (lane_head[:, None] == lane_head[None, :]).astype(np.float32)
    tile = (np.arange(HEAD_DIM)[:, None] == (np.arange(w) % HEAD_DIM)[None, :]).astype(np.float32)
    r = np.arange(SUPER_BLOCK)
    ltri = ((r[:, None] // cfg.chunk == r[None, :] // cfg.chunk) & (r[None, :] <= r[:, None])).astype(np.float32)
    return dict(ltri=jnp.asarray(ltri, BF16), ones=jnp.asarray(ones, BF16), tile=jnp.asarray(tile, BF16),
                tile_t=jnp.asarray(tile.T, BF16), bd=jnp.asarray(ones, F32))


def _state_specs(cfg, w):
    blk = (1 if cfg.long_seq else cfg.seq_per_sb, w, HEAD_DIM)
    if cfg.long_seq:
        imap = lambda i: (i // cfg.sb_per_seq, 0, 0)
    else:
        imap = lambda i: (i, 0, 0)
    return pl.BlockSpec(blk, imap)


def _hgrn_call(z_a, lb_param, g_exp, s0_t, cfg, layer):
    rows = z_a.shape[0]
    w = A_W
    cst = _mixer_consts(A_HEADS, cfg)
    n_seq = s0_t.shape[0]
    vm = lambda shape: pltpu.VMEM(shape, F32)
    kern = functools.partial(_hgrn_kernel, cfg=cfg, layer=layer)
    return pl.pallas_call(
        kern,
        grid=(rows // SUPER_BLOCK,),
        in_specs=[pl.BlockSpec((SUPER_BLOCK, ZA_W), lambda i: (i, 0)), _const_spec((DEPTH, w)),
                  _const_spec((1, w)), _state_specs(cfg, w), _const_spec(cst["ltri"].shape),
                  _const_spec(cst["ones"].shape), _const_spec(cst["tile"].shape),
                  _const_spec(cst["tile_t"].shape), _const_spec(cst["bd"].shape)],
        out_specs=[pl.BlockSpec((SUPER_BLOCK, w), lambda i: (i, 0)), _state_specs(cfg, w)],
        out_shape=[jax.ShapeDtypeStruct((rows, w), F32), jax.ShapeDtypeStruct((n_seq, w, HEAD_DIM), F32)],
        scratch_shapes=[vm((w, w))] + [vm((SUPER_BLOCK, w))] * 5 + [vm((cfg.n_chunks, w)), vm((SUPER_BLOCK, w))],
        compiler_params=_params(1),
        name="hgrn2",
    )(z_a, lb_param, g_exp, s0_t, cst["ltri"], cst["ones"], cst["tile"], cst["tile_t"], cst["bd"])


def _lru_kernel(*refs, cfg):
    if cfg.long_seq:
        (z_ref, cw_ref, cb_ref, wg_ref, ba_ref, bx_ref, lp_ref,
         ob_ref, hs_ref, prev_ref, hc_ref) = refs
        inj_ref = None
    else:
        (z_ref, inj_ref, cw_ref, cb_ref, wg_ref, ba_ref, bx_ref, lp_ref,
         ob_ref, hs_ref, prev_ref, hc_ref) = refs
    sb = pl.program_id(0)
    n = SUPER_BLOCK
    w = B_W

    @pl.when(sb % cfg.sb_per_seq == 0)
    def _():
        prev_ref[0:8, :] = jnp.zeros((8, prev_ref.shape[1]), F32)
        hc_ref[...] = jnp.zeros_like(hc_ref)

    xc = _causal_conv(z_ref[:, 0:w], prev_ref, cw_ref) + cb_ref[...]
    gates = _dot_b(xc, wg_ref[...])
    r = jax.nn.sigmoid(gates[:, 0:w] + ba_ref[...])
    ig = jax.nn.sigmoid(gates[:, w:2 * w] + bx_ref[...])
    log_a = -LRU_C * r * _softplus(-lp_ref[...])
    a = jnp.exp(log_a)
    b = jnp.sqrt(1.0 - jnp.exp(2.0 * log_a)) * ig * xc
    if cfg.masked:
        real = _real_mask(cfg, n)
        a = jnp.where(real, a, 1.0)
        b = jnp.where(real, b, 0.0)
    if inj_ref is not None:
        b = b + inj_ref[...]
    ris = _row_in_seq(cfg, n)
    d = 1
    while d < min(n, cfg.seq_rows):
        has = ris >= d
        a_sh = jnp.where(has, pltpu.roll(a, d, 0), 1.0)
        b_sh = jnp.where(has, pltpu.roll(b, d, 0), 0.0)
        b = b + a * b_sh
        a = a * a_sh
        d *= 2
    hs = b + a * hc_ref[...]
    if cfg.long_seq:
        hc_ref[...] = hs[n - 1:n, :]
    hs_ref[...] = hs
    ob_ref[...] = hs * _gelu(z_ref[:, w:2 * w])


def _lru_call(z_b, inj, cw, cb, wg, ba, bx, lp, cfg):
    rows = z_b.shape[0]
    w = B_W
    row = lambda width: pl.BlockSpec((SUPER_BLOCK, width), lambda i: (i, 0))
    ins = [z_b] + ([] if cfg.long_seq else [inj]) + [cw, cb, wg, ba, bx, lp]
    specs = [row(ZB_W)] + ([] if cfg.long_seq else [row(w)]) + [_const_spec(a.shape) for a in ins[-6:]]
    return pl.pallas_call(
        functools.partial(_lru_kernel, cfg=cfg),
        grid=(rows // SUPER_BLOCK,),
        in_specs=specs,
        out_specs=[row(w), row(w)],
        out_shape=[jax.ShapeDtypeStruct((rows, w), F32)] * 2,
        scratch_shapes=[pltpu.VMEM((8 + SUPER_BLOCK, w), F32), pltpu.VMEM((1, w), F32)],
        compiler_params=_params(1),
        name="rglru",
    )(*ins)


def _dn_kernel(z_ref, cw_ref, alog_ref, dtb_ref, g_ref, s0_ref, ltri_ref, ones_ref, tile_ref, tile_t_ref,
               bd_ref, expb_ref, expa_ref,
               o_ref, sout_ref,
               st_ref, prev_ref, q_s, k_s, v_s, g_s, eg_s, beta_s, kt_s, egl_s, o_s, *, cfg):
    sb = pl.program_id(0)
    c = cfg.chunk
    n = SUPER_BLOCK
    w = C_W

    @pl.when(sb % cfg.sb_per_seq == 0)
    def _():
        prev_ref[0:8, :] = jnp.zeros((8, prev_ref.shape[1]), F32)

    qkv = _silu(_causal_conv(z_ref[:, 0:3 * w], prev_ref, cw_ref))
    q = qkv[:, 0:w]
    kx = qkv[:, w:2 * w]
    ones = ones_ref[...]
    q = q * lax.rsqrt(_dot_sel(q * q, ones, passes=2) + EPS) * (HEAD_DIM ** -0.5)
    kx = kx * lax.rsqrt(_dot_sel(kx * kx, ones, passes=2) + EPS)
    pc = z_ref[:, 4 * w:4 * w + LANES]
    beta = _dot_sel(jax.nn.sigmoid(pc), expb_ref[...], passes=2)
    gdec = _dot_sel(-jnp.exp(alog_ref[...]) * _softplus(pc + dtb_ref[...]), expa_ref[...], passes=2)
    if cfg.masked:
        real = _real_mask(cfg, n)
        beta = jnp.where(real, beta, 0.0)
        gdec = jnp.where(real, gdec, 0.0)
    gcum = _sel_dot(ltri_ref[...], gdec)
    g3 = gcum.reshape(cfg.n_chunks, c, w)
    gl3 = g3[:, c - 1:c, :]
    q_s[...] = q
    k_s[...] = kx
    v_s[...] = qkv[:, 2 * w:3 * w]
    g_s[...] = gcum
    eg_s[...] = jnp.exp(gcum)
    beta_s[...] = beta
    kt_s[...] = (kx.reshape(cfg.n_chunks, c, w) * jnp.exp(gl3 - g3)).reshape(n, w)
    egl_s[...] = jnp.exp(gl3).reshape(cfg.n_chunks, w)

    iio = lax.broadcasted_iota(jnp.int32, (c, 1), 0)
    bd = bd_ref[...]

    def chunk(k, st):
        r0 = pl.multiple_of(k * c, c)
        rows = pl.ds(r0, c)
        q_c = q_s[rows, :]
        k_c = k_s[rows, :]
        g_c = g_s[rows, :]
        eg_c = eg_s[rows, :]
        beta_c = beta_s[rows, :]
        qk_s = _dot_b(jnp.concatenate([q_c, k_c], axis=0), st)
        q_st = qk_s[0:c, :]
        k_st = qk_s[c:2 * c, :]
        d_rows = [k_c * k_c[j:j + 1, :] for j in range(c)] + [q_c * k_c[j:j + 1, :] for j in range(c)]
        dots = _dot_sel(jnp.concatenate(d_rows, axis=0), ones, passes=1)
        e = beta_c * (v_s[rows, :] - eg_c * k_st)
        o = eg_c * q_st
        for j in range(c):
            dec = jnp.exp(jnp.minimum(g_c - g_c[j:j + 1, :], 0.0))
            m_col = jnp.where(iio > j, beta_c * dots[j * c:(j + 1) * c, :] * dec, 0.0)
            e_j = e[j:j + 1, :]
            e = e - m_col * e_j
            qk_col = jnp.where(iio >= j, dots[(c + j) * c:(c + j + 1) * c, :] * dec, 0.0)
            o = o + qk_col * e_j
        o_s[rows, :] = o
        upd = _dot_b(kt_s[rows, :], e, _TN)
        return st * egl_s[pl.ds(k, 1), :] + bd * upd

    _run_chunks(cfg, sb, chunk, st_ref, s0_ref, sout_ref, tile_ref, tile_t_ref, bd_ref)
    o_ref[...] = _head_norm_gate(o_s[...], z_ref[:, 3 * w:4 * w], g_ref[...], ones_ref)


def _dn_call(z_c, cw, alog_exp, dtb_exp, g_exp, s0, cfg):
    rows = z_c.shape[0]
    w = C_W
    cst = _mixer_consts(C_HEADS, cfg)
    lane_head = np.arange(w) // HEAD_DIM
    expb = (np.arange(128)[:, None] == lane_head[None, :]).astype(np.float32)
    expa = (np.arange(128)[:, None] == (lane_head[None, :] + C_HEADS)).astype(np.float32)
    expb = jnp.asarray(expb, BF16)
    expa = jnp.asarray(expa, BF16)
    n_seq = s0.shape[0]
    vm = lambda shape: pltpu.VMEM(shape, F32)
    return pl.pallas_call(
        functools.partial(_dn_kernel, cfg=cfg),
        grid=(rows // SUPER_BLOCK,),
        in_specs=[pl.BlockSpec((SUPER_BLOCK, ZC_W), lambda i: (i, 0)), _const_spec(cw.shape),
                  _const_spec((1, LANES)), _const_spec((1, LANES)), _const_spec((1, w)), _state_specs(cfg, w),
                  _const_spec(cst["ltri"].shape), _const_spec(cst["ones"].shape),
                  _const_spec(cst["tile"].shape), _const_spec(cst["tile_t"].shape),
                  _const_spec(cst["bd"].shape), _const_spec(expb.shape), _const_spec(expa.shape)],
        out_specs=[pl.BlockSpec((SUPER_BLOCK, w), lambda i: (i, 0)), _state_specs(cfg, w)],
        out_shape=[jax.ShapeDtypeStruct((rows, w), F32), jax.ShapeDtypeStruct((n_seq, w, HEAD_DIM), F32)],
        scratch_shapes=[vm((w, w)), vm((8 + SUPER_BLOCK, 3 * w))] + [vm((SUPER_BLOCK, w))] * 7
        + [vm((cfg.n_chunks, w)), vm((SUPER_BLOCK, w))],
        compiler_params=_params(1),
        name="deltanet",
    )(z_c, cw, alog_exp, dtb_exp, g_exp, s0, cst["ltri"], cst["ones"], cst["tile"], cst["tile_t"],
      cst["bd"], expb, expa)


def _outproj_kernel(x_ref, oa_ref, ob_ref, oc_ref, gt_ref, g_ref, sc_ref, sh_ref,
                    wo_ref, wqh_ref, wql_ref, x1_ref, h2_ref, q_ref):
    mo = _dot_b(oa_ref[...], wo_ref[0:A_W, :])
    mo = mo + _dot_b(ob_ref[...], wo_ref[A_W:A_W + B_W, :])
    mo = mo + _dot_b(oc_ref[...], wo_ref[A_W + B_W:, :])
    x1 = x_ref[...] + gt_ref[...] * mo
    x1_ref[...] = x1
    h2 = _rmsnorm(x1, g_ref[...]) * sc_ref[...] + sh_ref[...]
    h2_ref[...] = h2.astype(BF16)
    q_ref[...] = _dot_w(h2, wqh_ref[...], wql_ref[...])


def _outproj_call(x, oa, ob, oc, gt, g, sc, sh, wo, wqh, wql, tiles_per_group):
    t = x.shape[0]
    tm = TOKEN_TILE
    row = lambda w: pl.BlockSpec((tm, w), lambda i: (i, 0))
    return pl.pallas_call(
        _outproj_kernel,
        grid=(t // tm,),
        in_specs=[row(D_MODEL), row(A_W), row(B_W), row(C_W), _mod_spec(gt, tiles_per_group),
                  _const_spec((1, D_MODEL)), _mod_spec(sc, tiles_per_group), _mod_spec(sh, tiles_per_group),
                  _const_spec(wo.shape), _const_spec(wqh.shape), _const_spec(wql.shape)],
        out_specs=[row(D_MODEL), row(D_MODEL), row(D_MODEL)],
        out_shape=[jax.ShapeDtypeStruct((t, D_MODEL), F32), jax.ShapeDtypeStruct((t, D_MODEL), BF16),
                   jax.ShapeDtypeStruct((t, D_MODEL), F32)],
        compiler_params=_params(1),
        name="outproj",
    )(x, oa, ob, oc, gt, g, sc, sh, wo, wqh, wql)


def _top16(s):
    vals = []
    for r in range(PEER_TOPK):
        m = jnp.max(s, axis=0, keepdims=True)
        s = jnp.where(s == m, -(32.0 + r) * 2.0 ** 95, s)
        vals.append(m)
    rank = jnp.where(s <= -(2.0 ** 99), s * -(2.0 ** -95) - 32.0, float(PEER_TOPK))
    return jnp.concatenate(vals, axis=0), rank


def _peer_kernel(h2_ref, q_ref, x1_ref, gt_ref, fg_ref, k1_ref, k2_ref, u_ref, vt_ref, o_ref,
                 cnt_s, e1_s, rank2_s, e2_s, acc_s, s1_s, s2_s, *, final):
    j = pl.program_id(1)
    tt = PEER_TOKEN_TILE
    nk = PEER_NKEYS
    a_per_step = PEER_EXPERT_BLOCK // nk

    @pl.when(j == 0)
    def _():
        acc_s[...] = jnp.zeros_like(acc_s)

        def head(h, carry):
            qh = q_ref[:, pl.ds(pl.multiple_of(h * nk, nk), nk)]
            s1_s[...] = _dot_x3(k1_ref[...], qh, _NT)
            s2_s[...] = _dot_x3(k2_ref[...], qh, _NT)

            def lane_tile(c, carry2):
                cols = pl.ds(pl.multiple_of(c * PEER_ROUTE_TILE, PEER_ROUTE_TILE), PEER_ROUTE_TILE)
                s1 = s1_s[:, cols]
                s2 = s2_s[:, cols]
                v1, rank1 = _top16(s1)
                v2, rank2 = _top16(s2)
                c3 = v1[:, None, :] + v2[None, :, :]
                jrow = lax.broadcasted_iota(jnp.int32, (8, 1), 0)
                parts = [v1[0:1, :] + v2, v1[1:2, :] + v2[0:8, :]]
                for i in range(2, 8):
                    parts.append(jnp.where(jrow < PEER_TOPK // (i + 1), v1[i:i + 1, :] + v2[0:8, :], -jnp.inf))
                parts.append(v1[8:16, :] + v2[0:1, :])
                cand = jnp.concatenate(parts, axis=0)
                m = None
                for _ in range(PEER_TOPK):
                    m = jnp.max(cand, axis=0, keepdims=True)
                    cand = jnp.where(cand == m, -jnp.inf, cand)
                tau = m
                sel = c3 >= tau[None, :, :]
                m0 = v1[0:1, :] + v2[0:1, :]
                zsum = jnp.sum(jnp.where(sel, jnp.exp(c3 - m0[None, :, :]), 0.0).reshape(-1, PEER_ROUTE_TILE),
                               axis=0, keepdims=True)
                n_i = jnp.sum(jnp.where(sel, 1.0, 0.0), axis=1)
                rank1 = rank1.astype(BF16)
                cnt = jnp.zeros((nk, PEER_ROUTE_TILE), BF16)
                for i in range(PEER_TOPK):
                    cnt = cnt + jnp.where(rank1 == i, n_i[i:i + 1, :].astype(BF16), jnp.zeros((), BF16))
                cnt_s[h, :, cols] = cnt.astype(F32)
                e1_s[h, :, cols] = jnp.exp(s1 - v1[0:1, :])
                rank2_s[h, :, cols] = rank2.astype(BF16)
                e2_s[h, :, cols] = (jnp.exp(s2 - v2[0:1, :]) / zsum).astype(BF16)
                return carry2

            lax.fori_loop(0, tt // PEER_ROUTE_TILE, lane_tile, 0)
            return carry

        lax.fori_loop(0, PEER_HEADS, head, 0)

    h2 = h2_ref[...]
    n_sub = PEER_EXPERT_BLOCK // PEER_SUB_BLOCK
    a_per_sub = PEER_SUB_BLOCK // nk
    hidden = lambda s: _dot(u_ref[s * PEER_SUB_BLOCK:(s + 1) * PEER_SUB_BLOCK, :], h2, _NT)
    hid_next = hidden(0)
    for s in range(n_sub):
        e0 = s * PEER_SUB_BLOCK
        hid = hid_next
        if s + 1 < n_sub:
            hid_next = hidden(s + 1)
        coefs = []
        for al in range(a_per_sub):
            a = j * a_per_step + s * a_per_sub + al
            wsum = jnp.zeros((nk, tt), BF16)
            for h in range(PEER_HEADS):
                cnt_row = cnt_s[h, pl.ds(a, 1), :].astype(BF16)
                e1_row = e1_s[h, pl.ds(a, 1), :].astype(BF16)
                wsum = wsum + jnp.where(rank2_s[h] < cnt_row, e2_s[h] * e1_row, jnp.zeros((), BF16))
            coefs.append(_gelu(hid[al * nk:(al + 1) * nk, :].astype(BF16)) * wsum)
        coef = jnp.concatenate(coefs, axis=0)
        acc_s[...] += _dot(vt_ref[:, e0:e0 + PEER_SUB_BLOCK], coef)

    @pl.when(j == pl.num_programs(1) - 1)
    def _():
        x2 = x1_ref[...] + gt_ref[...] * acc_s[...].T
        if final:
            x2 = _rmsnorm(x2, fg_ref[...])
        o_ref[...] = x2


def _peer_call(h2b, q, x1, gt, fg, k1p, k2p, u_b, vt_b, tiles_per_group, final):
    t = h2b.shape[0]
    tt = PEER_TOKEN_TILE
    nb = PEER_EXPERT_BLOCK
    row = lambda w: pl.BlockSpec((tt, w), lambda i, j: (i, 0))
    _, r, w = gt.shape
    tab = lambda dt: pltpu.VMEM((PEER_HEADS, PEER_NKEYS, tt), dt)
    return pl.pallas_call(
        functools.partial(_peer_kernel, final=final),
        grid=(t // tt, PEER_N // nb),
        in_specs=[row(D_MODEL), row(D_MODEL), row(D_MODEL),
                  pl.BlockSpec((None, r, w), lambda i, j: (i // tiles_per_group, 0, 0)),
                  pl.BlockSpec((1, D_MODEL), lambda i, j: (0, 0)),
                  pl.BlockSpec((PEER_NKEYS, PEER_NKEYS), lambda i, j: (0, 0)),
                  pl.BlockSpec((PEER_NKEYS, PEER_NKEYS), lambda i, j: (0, 0)),
                  pl.BlockSpec((nb, D_MODEL), lambda i, j: (j, 0)),
                  pl.BlockSpec((D_MODEL, nb), lambda i, j: (0, j))],
        out_specs=row(D_MODEL),
        out_shape=jax.ShapeDtypeStruct((t, D_MODEL), F32),
        scratch_shapes=[tab(F32), tab(F32), tab(BF16), tab(BF16), pltpu.VMEM((D_MODEL, tt), F32),
                        pltpu.VMEM((PEER_NKEYS, tt), F32), pltpu.VMEM((PEER_NKEYS, tt), F32)],
        compiler_params=_params(2),
        name="peer",
    )(h2b, q, x1, gt, fg, k1p, k2p, u_b, vt_b)


def _block_diag(wblk):
    n, d, e = wblk.shape
    eye = jnp.eye(n, dtype=wblk.dtype)
    return (eye[:, None, :, None] * wblk[:, :, None, :]).reshape(n * d, n * e)


def _hilo(w):
    hi = w.astype(BF16)
    return hi, (w - hi.astype(F32)).astype(BF16)


def _pad_sample(z, hist=None):
    bsz = z.shape[0] // 4
    w = z.shape[1]
    z3 = z.reshape(bsz, 4, w)
    h3 = jnp.zeros((bsz, SAMPLE_HIST, w), F32)
    if hist is not None:
        h3 = h3.at[:, :, :hist.shape[-1]].set(hist)
    return jnp.concatenate([h3, z3, jnp.zeros((bsz, 1, w), F32)], axis=1).reshape(bsz * SAMPLE_SEQ_ROWS, w)


def _unpad_sample(o):
    bsz = o.shape[0] // SAMPLE_SEQ_ROWS
    return o.reshape(bsz, SAMPLE_SEQ_ROWS, -1)[:, SAMPLE_HIST:SAMPLE_HIST + 4].reshape(bsz * 4, -1)


def _layer_weights(l, w_in, w_out, peer_wq, peer_k1, peer_k2, peer_u, peer_v, lru_wa, lru_wx):
    w_in_p = jnp.pad(w_in[l], ((0, 0), (0, N_IN_PAD - N_IN)))
    wih = w_in_p.astype(BF16)
    wo = w_out[l].astype(BF16)
    wqh, wql = _hilo(peer_wq[l])
    wg = jnp.concatenate([_block_diag(lru_wa[l]), _block_diag(lru_wx[l])], axis=1).astype(BF16)
    half = PEER_NKEYS // 2
    k1p = jnp.pad(peer_k1[l], ((0, 0), (0, half)))
    k2p = jnp.pad(peer_k2[l], ((0, 0), (half, 0)))
    u_b = peer_u[l].astype(BF16)
    vt_b = peer_v[l].T.astype(BF16)
    return dict(wih=wih, wo=wo, wqh=wqh, wql=wql, wg=wg,
                k1p=k1p, k2p=k2p, u_b=u_b, vt_b=vt_b)


def _trunk(x, mods, states, is_prompt, p, lw, l, final_g):
    t = x.shape[0]
    if is_prompt:
        sh1, sc1, gt1, sh2, sc2, gt2 = mods
    else:
        sh1, sc1, gt1, sh2, sc2 = (m.reshape(-1, TOKEN_TILE, D_MODEL) for m in mods[:5])
        gt2 = mods[5].reshape(-1, PEER_TOKEN_TILE, D_MODEL)
    if is_prompt:
        cfg = PROMPT_CFG
        tiles_tok = 2048 // TOKEN_TILE
        tiles_peer = 2048 // PEER_TOKEN_TILE
        n_seq = t // 2048
    else:
        cfg = SAMPLE_CFG
        tiles_tok = 1
        tiles_peer = 1
        n_seq = t // 4
    row1 = lambda v: v.reshape(1, -1)
    z_a, z_b, z_c = _inproj_call(x, row1(p['norm1_g']), sc1, sh1, lw['wih'], tiles_tok)
    if is_prompt:
        s_a = jnp.zeros((n_seq, A_W, HEAD_DIM), F32)
        s_c = jnp.zeros((n_seq, C_W, HEAD_DIM), F32)
        za_m, zb_m, zc_m = z_a, z_b, z_c
        inj = None
    else:
        st_hgrn, st_lru_h, st_lru_conv, st_dn, st_dn_conv = states
        s_a = jnp.swapaxes(st_hgrn, -1, -2).reshape(n_seq, A_W, HEAD_DIM)
        s_c = st_dn.reshape(n_seq, C_W, HEAD_DIM)
        za_m = _pad_sample(z_a)
        zb_m = _pad_sample(z_b, st_lru_conv)
        zc_m = _pad_sample(z_c, st_dn_conv)
        inj = jnp.zeros((n_seq, SAMPLE_SEQ_ROWS, B_W), F32).at[:, 0].set(st_lru_h).reshape(-1, B_W)
    rep = lambda v, h: row1(jnp.tile(v, h))
    oa, s_a_new = _hgrn_call(za_m, p['lb_param'], rep(p['a_norm_g'], A_HEADS), s_a, cfg, l)
    ob, hs = _lru_call(zb_m, inj, p['lru_conv_w'], row1(p['lru_conv_b']), lw['wg'],
                       row1(p['lru_ba']), row1(p['lru_bx']), row1(p['lru_L']), cfg)
    decay_cols = lambda v: jnp.zeros((1, LANES), F32).at[0, C_HEADS:2 * C_HEADS].set(v)
    oc, s_c_new = _dn_call(zc_m, p['dn_conv_w'], decay_cols(p['dn_A_log']), decay_cols(p['dn_dt_bias']),
                           rep(p['dn_norm_g'], C_HEADS), s_c, cfg)
    if is_prompt:
        seq = 2048
        h_t = hs.reshape(n_seq, seq, B_W)[:, -1]
        buf_b = z_b.reshape(n_seq, seq, ZB_W)[:, -SAMPLE_HIST:, :B_W]
        buf_c = z_c.reshape(n_seq, seq, ZC_W)[:, -SAMPLE_HIST:, :3 * C_W]
    else:
        oa, ob, oc = _unpad_sample(oa), _unpad_sample(ob), _unpad_sample(oc)
        h_t = hs.reshape(n_seq, SAMPLE_SEQ_ROWS, B_W)[:, -1]
        buf_b = z_b.reshape(n_seq, 4, ZB_W)[:, 1:, :B_W]
        buf_c = z_c.reshape(n_seq, 4, ZC_W)[:, 1:, :3 * C_W]
    new_states = (jnp.swapaxes(s_a_new.reshape(n_seq, A_HEADS, HEAD_DIM, HEAD_DIM), -1, -2), h_t, buf_b,
                  s_c_new.reshape(n_seq, C_HEADS, HEAD_DIM, HEAD_DIM), buf_c)
    x1, h2b, q = _outproj_call(x, oa, ob, oc, gt1, row1(p['norm2_g']), sc2, sh2,
                               lw['wo'], lw['wqh'], lw['wql'], tiles_tok)
    x2 = _peer_call(h2b, q, x1, gt2, row1(final_g), lw['k1p'], lw['k2p'], lw['u_b'], lw['vt_b'],
                    tiles_peer, final=(l == DEPTH - 1))
    return x2, new_states


def kernel(x_prompt, x_sample, state_hgrn, state_lru_h, state_lru_conv, state_dn, state_dn_conv,
           c_prompt, c_sample, w_ada, b_ada, norm1_g, norm2_g, w_in, lb_param, a_norm_g,
           lru_conv_w, lru_conv_b, lru_wa, lru_ba, lru_wx, lru_bx, lru_L,
           dn_conv_w, dn_A_log, dn_dt_bias, dn_norm_g, w_out,
           peer_wq, peer_k1, peer_k2, peer_u, peer_v, final_norm_g):
    n_p, seq, _ = x_prompt.shape
    n_s, dec_seq, _ = x_sample.shape
    mod = _ada_call(jnp.concatenate([c_prompt, c_sample], axis=0), w_ada, b_ada)
    xp = x_prompt.reshape(n_p * seq, D_MODEL)
    xs = x_sample.reshape(n_s * dec_seq, D_MODEL)
    sample_states = (state_hgrn, state_lru_h, state_lru_conv, state_dn, state_dn_conv)
    p_new, s_new = [], []
    for l in range(DEPTH):
        p = dict(norm1_g=norm1_g[l], norm2_g=norm2_g[l], lb_param=lb_param, a_norm_g=a_norm_g[l],
                 lru_conv_w=lru_conv_w[l], lru_conv_b=lru_conv_b[l], lru_ba=lru_ba[l], lru_bx=lru_bx[l],
                 lru_L=lru_L[l], dn_conv_w=dn_conv_w[l], dn_A_log=dn_A_log[l], dn_dt_bias=dn_dt_bias[l],
                 dn_norm_g=dn_norm_g[l])
        lw = _layer_weights(l, w_in, w_out, peer_wq, peer_k1, peer_k2, peer_u, peer_v, lru_wa, lru_wx)
        parts = jnp.split(mod[l], 6, axis=-1)
        parts = [m + 1.0 if i in (1, 4) else m for i, m in enumerate(parts)]
        mods_p = [m[:n_p].reshape(n_p, 1, D_MODEL) for m in parts]
        mods_s = [jnp.repeat(m[n_p:], dec_seq, axis=0) for m in parts]
        xp, st_p = _trunk(xp, mods_p, None, True, p, lw, l, final_norm_g)
        xs, st_s = _trunk(xs, mods_s, tuple(s[l] for s in sample_states), False, p, lw, l, final_norm_g)
        p_new.append(st_p)
        s_new.append(st_s)
    stack = lambda sts: [jnp.stack([s[i] for s in sts]) for i in range(5)]
    p_st = stack(p_new)
    s_st = stack(s_new)
    return (xp.reshape(n_p, seq, D_MODEL), xs.reshape(n_s, dec_seq, D_MODEL), *p_st, *s_st)
```

```python
import functools

import jax
import jax.numpy as jnp
import numpy as np
from jax import lax
from jax.experimental import pallas as pl
from jax.experimental.pallas import tpu as pltpu

F32 = jnp.float32
BF16 = jnp.bfloat16
HIGHEST = lax.Precision.HIGHEST

D_MODEL = 1024
DEPTH = 2
HEAD_DIM = 64
A_HEADS = 4
A_W = 256
B_W = 384
C_HEADS = 6
C_W = 384
LRU_C = 8.0
CONV_W = 4
N_IN = 3340
N_IN_PAD = 3456
ZA_W = 1024
ZB_W = 768
ZC_W = 1664
PEER_HEADS = 8
PEER_NKEYS = 128
PEER_TOPK = 16
PEER_N = PEER_NKEYS * PEER_NKEYS
EPS = 1e-6

LANES = 128
SUPER_BLOCK = 256
SAMPLE_SEQ_ROWS = 8
SAMPLE_HIST = CONV_W - 1
TOKEN_TILE = 256
PEER_TOKEN_TILE = 512
PEER_EXPERT_BLOCK = 2048
PEER_SUB_BLOCK = 512
PEER_ROUTE_TILE = 256
VMEM_LIMIT_BYTES = 56 * 1024 * 1024

_NN = (((1,), (0,)), ((), ()))
_NT = (((1,), (1,)), ((), ()))
_TN = (((0,), (0,)), ((), ()))


def _dot(a, b, dims=_NN, precision=None):
    return lax.dot_general(a, b, dims, precision=precision, preferred_element_type=F32)


def _split2(x):
    hi = x.astype(BF16)
    lo = (x - hi.astype(F32)).astype(BF16)
    return hi, lo


def _split3(x):
    hi = x.astype(BF16)
    r = x - hi.astype(F32)
    mid = r.astype(BF16)
    lo = (r - mid.astype(F32)).astype(BF16)
    return hi, mid, lo


def _dot_x3(a, b, dims=_NN):
    ah, al = _split2(a)
    bh, bl = _split2(b)
    return _dot(ah, bh, dims) + _dot(ah, bl, dims) + _dot(al, bh, dims)


def _dot_w(a, wh, wl, dims=_NN):
    ah, al = _split2(a)
    return _dot(ah, wh, dims) + _dot(al, wh, dims) + _dot(ah, wl, dims)


def _dot_b(a, b, dims=_NN):
    return _dot(a.astype(BF16), b.astype(BF16), dims)


def _dot_sel(a, sel, dims=_NN, passes=3):
    parts = _split3(a)[:passes]
    out = _dot(parts[0], sel, dims)
    for p in parts[1:]:
        out = out + _dot(p, sel, dims)
    return out


def _sel_dot(sel, b, passes=3):
    parts = _split3(b)[:passes]
    out = _dot(sel, parts[0])
    for p in parts[1:]:
        out = out + _dot(sel, p)
    return out


def _softplus(x):
    return jnp.maximum(x, 0.0) + jnp.log(1.0 + jnp.exp(-jnp.abs(x)))


def _silu(x):
    return x * jax.nn.sigmoid(x)


def _gelu(x):
    return 0.5 * x * (1.0 + jnp.tanh(float(np.sqrt(2.0 / np.pi)) * (x + 0.044715 * (x * x * x))))


def _rmsnorm(x, g):
    return x * lax.rsqrt(jnp.mean(x * x, axis=-1, keepdims=True) + EPS) * g


def _const_spec(shape):
    nd = len(shape)
    return pl.BlockSpec(shape, lambda *_: (0,) * nd)


def _params(n_grid):
    return pltpu.CompilerParams(dimension_semantics=("arbitrary",) * n_grid,
                                vmem_limit_bytes=VMEM_LIMIT_BYTES)


def _ada_kernel(c_ref, w_ref, b_ref, o_ref):
    c = c_ref[...]
    o_ref[0] = _dot(_silu(c), w_ref[0], precision=HIGHEST) + b_ref[0]


def _ada_call(c_all, w_ada, b_ada):
    nb = c_all.shape[0]
    nt = 6 * D_MODEL // 1024
    return pl.pallas_call(
        _ada_kernel,
        grid=(DEPTH, nt),
        in_specs=[
            _const_spec((nb, D_MODEL)),
            pl.BlockSpec((1, D_MODEL, 1024), lambda l, j: (l, 0, j)),
            pl.BlockSpec((1, 1, 1024), lambda l, j: (l, 0, j)),
        ],
        out_specs=pl.BlockSpec((1, nb, 1024), lambda l, j: (l, 0, j)),
        out_shape=jax.ShapeDtypeStruct((DEPTH, nb, 6 * D_MODEL), F32),
        compiler_params=_params(2),
        name="ada_mod",
    )(c_all, w_ada, b_ada.reshape(DEPTH, 1, 6 * D_MODEL))


def _inproj_kernel(x_ref, g_ref, sc_ref, sh_ref, w_ref, za_ref, zb_ref, zc_ref):
    h = (_rmsnorm(x_ref[...], g_ref[...]) * sc_ref[...] + sh_ref[...]).astype(BF16)
    for o_ref, lo, hi in ((za_ref, 0, ZA_W), (zb_ref, ZA_W, ZA_W + ZB_W), (zc_ref, ZA_W + ZB_W, N_IN_PAD)):
        o_ref[...] = _dot(h, w_ref[:, lo:hi])


def _mod_spec(mod, tiles_per_group):
    _, r, w = mod.shape
    return pl.BlockSpec((None, r, w), lambda i: (i // tiles_per_group, 0, 0))


def _inproj_call(x, g, sc, sh, wh, tiles_per_group):
    t = x.shape[0]
    tm = TOKEN_TILE
    row = lambda w: pl.BlockSpec((tm, w), lambda i: (i, 0))
    return pl.pallas_call(
        _inproj_kernel,
        grid=(t // tm,),
        in_specs=[row(D_MODEL), _const_spec((1, D_MODEL)), _mod_spec(sc, tiles_per_group),
                  _mod_spec(sh, tiles_per_group), _const_spec(wh.shape)],
        out_specs=[row(ZA_W), row(ZB_W), row(ZC_W)],
        out_shape=[jax.ShapeDtypeStruct((t, ZA_W), F32), jax.ShapeDtypeStruct((t, ZB_W), F32),
                   jax.ShapeDtypeStruct((t, ZC_W), F32)],
        compiler_params=_params(1),
        name="inproj",
    )(x, g, sc, sh, wh)


class _SeqCfg:
    def __init__(self, seq_rows, hist, real, chunk):
        self.seq_rows = seq_rows
        self.hist = hist
        self.real = real
        self.chunk = chunk
        self.masked = hist > 0 or hist + real < seq_rows
        self.n_chunks = SUPER_BLOCK // chunk
        self.long_seq = seq_rows > SUPER_BLOCK
        self.sb_per_seq = max(seq_rows // SUPER_BLOCK, 1)
        self.seq_per_sb = max(SUPER_BLOCK // seq_rows, 1)
        assert self.long_seq or chunk == seq_rows


PROMPT_CFG = _SeqCfg(seq_rows=2048, hist=0, real=2048, chunk=16)
SAMPLE_CFG = _SeqCfg(seq_rows=SAMPLE_SEQ_ROWS, hist=SAMPLE_HIST, real=4, chunk=SAMPLE_SEQ_ROWS)


def _row_in_seq(cfg, n_rows):
    r = lax.broadcasted_iota(jnp.int32, (n_rows, 1), 0)
    return r % min(cfg.seq_rows, SUPER_BLOCK)


def _real_mask(cfg, n_rows):
    r = _row_in_seq(cfg, n_rows)
    return (r >= cfg.hist) & (r < cfg.hist + cfg.real)


def _causal_conv(x, xp_ref, w_ref):
    n = x.shape[0]
    xp_ref[8:8 + n, :] = x
    y = x * w_ref[CONV_W - 1:CONV_W, :]
    for k in range(CONV_W - 1):
        s = CONV_W - 1 - k
        y = y + xp_ref[8 - s:8 - s + n, :] * w_ref[k:k + 1, :]
    xp_ref[0:8, :] = x[n - 8:n, :]
    return y


def _expand_state(s_cat, tile_ref, bd_ref):
    return _dot_sel(s_cat, tile_ref[...], passes=2) * bd_ref[...]


def _compress_state(s_bd, tile_t_ref):
    return _dot_sel(s_bd, tile_t_ref[...], passes=2)


def _head_norm_gate(o, z, g, ones_ref):
    ms = _dot_sel(o * o, ones_ref[...], passes=2) * (1.0 / HEAD_DIM)
    return o * lax.rsqrt(ms + EPS) * g * _silu(z)


def _run_chunks(cfg, sb, step, st_ref, s0_ref, sout_ref, tile_ref, tile_t_ref, bd_ref):
    if cfg.long_seq:
        @pl.when(sb % cfg.sb_per_seq == 0)
        def _():
            st_ref[...] = _expand_state(s0_ref[0], tile_ref, bd_ref)

        def body(k, carry):
            st_ref[...] = step(k, st_ref[...])
            return carry

        lax.fori_loop(0, cfg.n_chunks, body, 0, unroll=2)

        @pl.when(sb % cfg.sb_per_seq == cfg.sb_per_seq - 1)
        def _():
            sout_ref[0] = _compress_state(st_ref[...], tile_t_ref)
    else:
        def body(k, carry):
            st_new = step(k, _expand_state(s0_ref[k], tile_ref, bd_ref))
            sout_ref[k] = _compress_state(st_new, tile_t_ref)
            return carry

        lax.fori_loop(0, cfg.n_chunks, body, 0)


def _hgrn_kernel(z_ref, lbp_ref, g_ref, s0_ref, ltri_ref, ones_ref, tile_ref, tile_t_ref, bd_ref,
                 o_ref, sout_ref,
                 st_ref, q_s, k_s, g_s, qe_s, kt_s, egl_s, o_s, *, cfg, layer):
    sb = pl.program_id(0)
    c = cfg.chunk
    n = SUPER_BLOCK
    w = A_W
    lbp = lbp_ref[...]
    e = jnp.exp(lbp - jnp.max(lbp, axis=0, keepdims=True))
    lbs = e / jnp.sum(e, axis=0, keepdims=True)
    lb = jnp.sum(lbs[0:layer + 1, :], axis=0, keepdims=True) - lbs[0:1, :]

    aq = z_ref[:, 0:w]
    af = z_ref[:, w:2 * w]
    f = lb + (1.0 - lb) * jax.nn.sigmoid(af)
    lf = jnp.log(f)
    kk = 1.0 - f
    if cfg.masked:
        real = _real_mask(cfg, n)
        lf = jnp.where(real, lf, 0.0)
        kk = jnp.where(real, kk, 0.0)
    q = _silu(aq) * (HEAD_DIM ** -0.5)
    gcum = _sel_dot(ltri_ref[...], lf)
    g3 = gcum.reshape(cfg.n_chunks, c, w)
    gl3 = g3[:, c - 1:c, :]
    q_s[...] = q
    k_s[...] = kk
    g_s[...] = gcum
    qe_s[...] = q * jnp.exp(gcum)
    kt_s[...] = (kk.reshape(cfg.n_chunks, c, w) * jnp.exp(gl3 - g3)).reshape(n, w)
    egl_s[...] = jnp.exp(gl3).reshape(cfg.n_chunks, w)

    jio = lax.broadcasted_iota(jnp.int32, (c, 1), 0)
    ones = ones_ref[...]
    bd = bd_ref[...]

    def chunk(k, st):
        r0 = pl.multiple_of(k * c, c)
        rows = pl.ds(r0, c)
        q_c = q_s[rows, :]
        k_c = k_s[rows, :]
        g_c = g_s[rows, :]
        v_c = z_ref[rows, 2 * w:3 * w]
        o_inter = _dot(qe_s[rows, :].astype(BF16), st.astype(BF16), _NT)
        d_rows = []
        for i in range(c):
            dec = jnp.exp(jnp.minimum(g_c[i:i + 1, :] - g_c, 0.0))
            d_rows.append(jnp.where(jio <= i, k_c * dec * q_c[i:i + 1, :], 0.0))
        d = jnp.concatenate(d_rows, axis=0)
        a_b = _dot_sel(d, ones, passes=1)
        o_diag = jnp.sum(a_b.reshape(c, c, w) * v_c[None, :, :], axis=1)
        o_s[rows, :] = o_inter + o_diag
        upd = _dot_b(v_c, kt_s[rows, :], _TN)
        return st * egl_s[pl.ds(k, 1), :] + bd * upd

    _run_chunks(cfg, sb, chunk, st_ref, s0_ref, sout_ref, tile_ref, tile_t_ref, bd_ref)
    o_ref[...] = _head_norm_gate(o_s[...], z_ref[:, 3 * w:4 * w], g_ref[...], ones_ref)


def _mixer_consts(heads, cfg):
    w = heads * HEAD_DIM
    lane_head = np.arange(w) // HEAD_DIM
    ones = (lane_head[:, None] == lane_head[None, :]).astype(np.float32)
    tile = (np.arange(HEAD_DIM)[:, None] == (np.arange(w) % HEAD_DIM)[None, :]).astype(np.float32)
    r = np.arange(SUPER_BLOCK)
    ltri = ((r[:, None] // cfg.chunk == r[None, :] // cfg.chunk) & (r[None, :] <= r[:, None])).astype(np.float32)
    return dict(ltri=jnp.asarray(ltri, BF16), ones=jnp.asarray(ones, BF16), tile=jnp.asarray(tile, BF16),
                tile_t=jnp.asarray(tile.T, BF16), bd=jnp.asarray(ones, F32))


def _state_specs(cfg, w):
    blk = (1 if cfg.long_seq else cfg.seq_per_sb, w, HEAD_DIM)
    if cfg.long_seq:
        imap = lambda i: (i // cfg.sb_per_seq, 0, 0)
    else:
        imap = lambda i: (i, 0, 0)
    return pl.BlockSpec(blk, imap)


def _hgrn_call(z_a, lb_param, g_exp, s0_t, cfg, layer):
    rows = z_a.shape[0]
    w = A_W
    cst = _mixer_consts(A_HEADS, cfg)
    n_seq = s0_t.shape[0]
    vm = lambda shape: pltpu.VMEM(shape, F32)
    kern = functools.partial(_hgrn_kernel, cfg=cfg, layer=layer)
    return pl.pallas_call(
        kern,
        grid=(rows // SUPER_BLOCK,),
        in_specs=[pl.BlockSpec((SUPER_BLOCK, ZA_W), lambda i: (i, 0)), _const_spec((DEPTH, w)),
                  _const_spec((1, w)), _state_specs(cfg, w), _const_spec(cst["ltri"].shape),
                  _const_spec(cst["ones"].shape), _const_spec(cst["tile"].shape),
                  _const_spec(cst["tile_t"].shape), _const_spec(cst["bd"].shape)],
        out_specs=[pl.BlockSpec((SUPER_BLOCK, w), lambda i: (i, 0)), _state_specs(cfg, w)],
        out_shape=[jax.ShapeDtypeStruct((rows, w), F32), jax.ShapeDtypeStruct((n_seq, w, HEAD_DIM), F32)],
        scratch_shapes=[vm((w, w))] + [vm((SUPER_BLOCK, w))] * 5 + [vm((cfg.n_chunks, w)), vm((SUPER_BLOCK, w))],
        compiler_params=_params(1),
        name="hgrn2",
    )(z_a, lb_param, g_exp, s0_t, cst["ltri"], cst["ones"], cst["tile"], cst["tile_t"], cst["bd"])


def _lru_kernel(*refs, cfg):
    if cfg.long_seq:
        (z_ref, cw_ref, cb_ref, wg_ref, ba_ref, bx_ref, lp_ref,
         ob_ref, hs_ref, prev_ref, hc_ref) = refs
        inj_ref = None
    else:
        (z_ref, inj_ref, cw_ref, cb_ref, wg_ref, ba_ref, bx_ref, lp_ref,
         ob_ref, hs_ref, prev_ref, hc_ref) = refs
    sb = pl.program_id(0)
    n = SUPER_BLOCK
    w = B_W

    @pl.when(sb % cfg.sb_per_seq == 0)
    def _():
        prev_ref[0:8, :] = jnp.zeros((8, prev_ref.shape[1]), F32)
        hc_ref[...] = jnp.zeros_like(hc_ref)

    xc = _causal_conv(z_ref[:, 0:w], prev_ref, cw_ref) + cb_ref[...]
    gates = _dot_b(xc, wg_ref[...])
    r = jax.nn.sigmoid(gates[:, 0:w] + ba_ref[...])
    ig = jax.nn.sigmoid(gates[:, w:2 * w] + bx_ref[...])
    log_a = -LRU_C * r * _softplus(-lp_ref[...])
    a = jnp.exp(log_a)
    b = jnp.sqrt(1.0 - jnp.exp(2.0 * log_a)) * ig * xc
    if cfg.masked:
        real = _real_mask(cfg, n)
        a = jnp.where(real, a, 1.0)
        b = jnp.where(real, b, 0.0)
    if inj_ref is not None:
        b = b + inj_ref[...]
    ris = _row_in_seq(cfg, n)
    d = 1
    while d < min(n, cfg.seq_rows):
        has = ris >= d
        a_sh = jnp.where(has, pltpu.roll(a, d, 0), 1.0)
        b_sh = jnp.where(has, pltpu.roll(b, d, 0), 0.0)
        b = b + a * b_sh
        a = a * a_sh
        d *= 2
    hs = b + a * hc_ref[...]
    if cfg.long_seq:
        hc_ref[...] = hs[n - 1:n, :]
    hs_ref[...] = hs
    ob_ref[...] = hs * _gelu(z_ref[:, w:2 * w])


def _lru_call(z_b, inj, cw, cb, wg, ba, bx, lp, cfg):
    rows = z_b.shape[0]
    w = B_W
    row = lambda width: pl.BlockSpec((SUPER_BLOCK, width), lambda i: (i, 0))
    ins = [z_b] + ([] if cfg.long_seq else [inj]) + [cw, cb, wg, ba, bx, lp]
    specs = [row(ZB_W)] + ([] if cfg.long_seq else [row(w)]) + [_const_spec(a.shape) for a in ins[-6:]]
    return pl.pallas_call(
        functools.partial(_lru_kernel, cfg=cfg),
        grid=(rows // SUPER_BLOCK,),
        in_specs=specs,
        out_specs=[row(w), row(w)],
        out_shape=[jax.ShapeDtypeStruct((rows, w), F32)] * 2,
        scratch_shapes=[pltpu.VMEM((8 + SUPER_BLOCK, w), F32), pltpu.VMEM((1, w), F32)],
        compiler_params=_params(1),
        name="rglru",
    )(*ins)


def _dn_kernel(z_ref, cw_ref, alog_ref, dtb_ref, g_ref, s0_ref, ltri_ref, ones_ref, tile_ref, tile_t_ref,
               bd_ref, expb_ref, expa_ref,
               o_ref, sout_ref,
               st_ref, prev_ref, q_s, k_s, v_s, g_s, eg_s, beta_s, kt_s, egl_s, o_s, *, cfg):
    sb = pl.program_id(0)
    c = cfg.chunk
    n = SUPER_BLOCK
    w = C_W

    @pl.when(sb % cfg.sb_per_seq == 0)
    def _():
        prev_ref[0:8, :] = jnp.zeros((8, prev_ref.shape[1]), F32)

    qkv = _silu(_causal_conv(z_ref[:, 0:3 * w], prev_ref, cw_ref))
    q = qkv[:, 0:w]
    kx = qkv[:, w:2 * w]
    ones = ones_ref[...]
    q = q * lax.rsqrt(_dot_sel(q * q, ones, passes=2) + EPS) * (HEAD_DIM ** -0.5)
    kx = kx * lax.rsqrt(_dot_sel(kx * kx, ones, passes=2) + EPS)
    pc = z_ref[:, 4 * w:4 * w + LANES]
    beta = _dot_sel(jax.nn.sigmoid(pc), expb_ref[...], passes=2)
    gdec = _dot_sel(-jnp.exp(alog_ref[...]) * _softplus(pc + dtb_ref[...]), expa_ref[...], passes=2)
    if cfg.masked:
        real = _real_mask(cfg, n)
        beta = jnp.where(real, beta, 0.0)
        gdec = jnp.where(real, gdec, 0.0)
    gcum = _sel_dot(ltri_ref[...], gdec)
    g3 = gcum.reshape(cfg.n_chunks, c, w)
    gl3 = g3[:, c - 1:c, :]
    q_s[...] = q
    k_s[...] = kx
    v_s[...] = qkv[:, 2 * w:3 * w]
    g_s[...] = gcum
    eg_s[...] = jnp.exp(gcum)
    beta_s[...] = beta
    kt_s[...] = (kx.reshape(cfg.n_chunks, c, w) * jnp.exp(gl3 - g3)).reshape(n, w)
    egl_s[...] = jnp.exp(gl3).reshape(cfg.n_chunks, w)

    iio = lax.broadcasted_iota(jnp.int32, (c, 1), 0)
    bd = bd_ref[...]

    def chunk(k, st):
        r0 = pl.multiple_of(k * c, c)
        rows = pl.ds(r0, c)
        q_c = q_s[rows, :]
        k_c = k_s[rows, :]
        g_c = g_s[rows, :]
        eg_c = eg_s[rows, :]
        beta_c = beta_s[rows, :]
        qk_s = _dot_b(jnp.concatenate([q_c, k_c], axis=0), st)
        q_st = qk_s[0:c, :]
        k_st = qk_s[c:2 * c, :]
        d_rows = [k_c * k_c[j:j + 1, :] for j in range(c)] + [q_c * k_c[j:j + 1, :] for j in range(c)]
        dots = _dot_sel(jnp.concatenate(d_rows, axis=0), ones, passes=1)
        e = beta_c * (v_s[rows, :] - eg_c * k_st)
        o = eg_c * q_st
        for j in range(c):
            dec = jnp.exp(jnp.minimum(g_c - g_c[j:j + 1, :], 0.0))
            m_col = jnp.where(iio > j, beta_c * dots[j * c:(j + 1) * c, :] * dec, 0.0)
            e_j = e[j:j + 1, :]
            e = e - m_col * e_j
            qk_col = jnp.where(iio >= j, dots[(c + j) * c:(c + j + 1) * c, :] * dec, 0.0)
            o = o + qk_col * e_j
        o_s[rows, :] = o
        upd = _dot_b(kt_s[rows, :], e, _TN)
        return st * egl_s[pl.ds(k, 1), :] + bd * upd

    _run_chunks(cfg, sb, chunk, st_ref, s0_ref, sout_ref, tile_ref, tile_t_ref, bd_ref)
    o_ref[...] = _head_norm_gate(o_s[...], z_ref[:, 3 * w:4 * w], g_ref[...], ones_ref)


def _dn_call(z_c, cw, alog_exp, dtb_exp, g_exp, s0, cfg):
    rows = z_c.shape[0]
    w = C_W
    cst = _mixer_consts(C_HEADS, cfg)
    lane_head = np.arange(w) // HEAD_DIM
    expb = (np.arange(128)[:, None] == lane_head[None, :]).astype(np.float32)
    expa = (np.arange(128)[:, None] == (lane_head[None, :] + C_HEADS)).astype(np.float32)
    expb = jnp.asarray(expb, BF16)
    expa = jnp.asarray(expa, BF16)
    n_seq = s0.shape[0]
    vm = lambda shape: pltpu.VMEM(shape, F32)
    return pl.pallas_call(
        functools.partial(_dn_kernel, cfg=cfg),
        grid=(rows // SUPER_BLOCK,),
        in_specs=[pl.BlockSpec((SUPER_BLOCK, ZC_W), lambda i: (i, 0)), _const_spec(cw.shape),
                  _const_spec((1, LANES)), _const_spec((1, LANES)), _const_spec((1, w)), _state_specs(cfg, w),
                  _const_spec(cst["ltri"].shape), _const_spec(cst["ones"].shape),
                  _const_spec(cst["tile"].shape), _const_spec(cst["tile_t"].shape),
                  _const_spec(cst["bd"].shape), _const_spec(expb.shape), _const_spec(expa.shape)],
        out_specs=[pl.BlockSpec((SUPER_BLOCK, w), lambda i: (i, 0)), _state_specs(cfg, w)],
        out_shape=[jax.ShapeDtypeStruct((rows, w), F32), jax.ShapeDtypeStruct((n_seq, w, HEAD_DIM), F32)],
        scratch_shapes=[vm((w, w)), vm((8 + SUPER_BLOCK, 3 * w))] + [vm((SUPER_BLOCK, w))] * 7
        + [vm((cfg.n_chunks, w)), vm((SUPER_BLOCK, w))],
        compiler_params=_params(1),
        name="deltanet",
    )(z_c, cw, alog_exp, dtb_exp, g_exp, s0, cst["ltri"], cst["ones"], cst["tile"], cst["tile_t"],
      cst["bd"], expb, expa)


def _outproj_kernel(x_ref, oa_ref, ob_ref, oc_ref, gt_ref, g_ref, sc_ref, sh_ref,
                    wo_ref, wqh_ref, wql_ref, x1_ref, h2_ref, q_ref):
    mo = _dot_b(oa_ref[...], wo_ref[0:A_W, :])
    mo = mo + _dot_b(ob_ref[...], wo_ref[A_W:A_W + B_W, :])
    mo = mo + _dot_b(oc_ref[...], wo_ref[A_W + B_W:, :])
    x1 = x_ref[...] + gt_ref[...] * mo
    x1_ref[...] = x1
    h2 = _rmsnorm(x1, g_ref[...]) * sc_ref[...] + sh_ref[...]
    h2_ref[...] = h2.astype(BF16)
    q_ref[...] = _dot_w(h2, wqh_ref[...], wql_ref[...])


def _outproj_call(x, oa, ob, oc, gt, g, sc, sh, wo, wqh, wql, tiles_per_group):
    t = x.shape[0]
    tm = TOKEN_TILE
    row = lambda w: pl.BlockSpec((tm, w), lambda i: (i, 0))
    return pl.pallas_call(
        _outproj_kernel,
        grid=(t // tm,),
        in_specs=[row(D_MODEL), row(A_W), row(B_W), row(C_W), _mod_spec(gt, tiles_per_group),
                  _const_spec((1, D_MODEL)), _mod_spec(sc, tiles_per_group), _mod_spec(sh, tiles_per_group),
                  _const_spec(wo.shape), _const_spec(wqh.shape), _const_spec(wql.shape)],
        out_specs=[row(D_MODEL), row(D_MODEL), row(D_MODEL)],
        out_shape=[jax.ShapeDtypeStruct((t, D_MODEL), F32), jax.ShapeDtypeStruct((t, D_MODEL), BF16),
                   jax.ShapeDtypeStruct((t, D_MODEL), F32)],
        compiler_params=_params(1),
        name="outproj",
    )(x, oa, ob, oc, gt, g, sc, sh, wo, wqh, wql)


def _top16(s):
    vals = []
    for r in range(PEER_TOPK):
        m = jnp.max(s, axis=0, keepdims=True)
        s = jnp.where(s == m, -(32.0 + r) * 2.0 ** 95, s)
        vals.append(m)
    rank = jnp.where(s <= -(2.0 ** 99), s * -(2.0 ** -95) - 32.0, float(PEER_TOPK))
    return jnp.concatenate(vals, axis=0), rank


def _peer_kernel(h2_ref, q_ref, x1_ref, gt_ref, fg_ref, k1_ref, k2_ref, u_ref, vt_ref, o_ref,
                 cnt_s, e1_s, rank2_s, e2_s, acc_s, s1_s, s2_s, coef_s, *, final, n_steps):
    g = pl.program_id(0)
    tt = PEER_TOKEN_TILE
    nk = PEER_NKEYS
    nj = PEER_N // PEER_EXPERT_BLOCK
    a_per_step = PEER_EXPERT_BLOCK // nk
    j = jnp.minimum(g, n_steps - 2) % nj
    gb = jnp.maximum(g - 1, 0)

    @pl.when(g == 0)
    def _():
        coef_s[...] = jnp.zeros_like(coef_s)

    @pl.when(gb % nj == 0)
    def _():
        acc_s[...] = jnp.zeros_like(acc_s)

    @pl.when(jnp.logical_and(g % nj == 0, g < n_steps - 1))
    def _():

        def head(h, carry):
            qh = q_ref[:, pl.ds(pl.multiple_of(h * nk, nk), nk)]
            s1_s[...] = _dot_x3(k1_ref[...], qh, _NT)
            s2_s[...] = _dot_x3(k2_ref[...], qh, _NT)

            def lane_tile(c, carry2):
                cols = pl.ds(pl.multiple_of(c * PEER_ROUTE_TILE, PEER_ROUTE_TILE), PEER_ROUTE_TILE)
                s1 = s1_s[:, cols]
                s2 = s2_s[:, cols]
                v1, rank1 = _top16(s1)
                v2, rank2 = _top16(s2)
                c3 = v1[:, None, :] + v2[None, :, :]
                jrow = lax.broadcasted_iota(jnp.int32, (8, 1), 0)
                parts = [v1[0:1, :] + v2, v1[1:2, :] + v2[0:8, :]]
                for i in range(2, 8):
                    parts.append(jnp.where(jrow < PEER_TOPK // (i + 1), v1[i:i + 1, :] + v2[0:8, :], -jnp.inf))
                parts.append(v1[8:16, :] + v2[0:1, :])
                cand = jnp.concatenate(parts, axis=0)
                m = None
                for _ in range(PEER_TOPK):
                    m = jnp.max(cand, axis=0, keepdims=True)
                    cand = jnp.where(cand == m, -jnp.inf, cand)
                tau = m
                sel = c3 >= tau[None, :, :]
                m0 = v1[0:1, :] + v2[0:1, :]
                zsum = jnp.sum(jnp.where(sel, jnp.exp(c3 - m0[None, :, :]), 0.0).reshape(-1, PEER_ROUTE_TILE),
                               axis=0, keepdims=True)
                n_i = jnp.sum(jnp.where(sel, 1.0, 0.0), axis=1)
                rank1 = rank1.astype(BF16)
                cnt = jnp.zeros((nk, PEER_ROUTE_TILE), BF16)
                for i in range(PEER_TOPK):
                    cnt = cnt + jnp.where(rank1 == i, n_i[i:i + 1, :].astype(BF16), jnp.zeros((), BF16))
                cnt_s[h, :, cols] = cnt.astype(F32)
                e1_s[h, :, cols] = jnp.exp(s1 - v1[0:1, :])
                rank2_s[h, :, cols] = rank2.astype(BF16)
                e2_s[h, :, cols] = (jnp.exp(s2 - v2[0:1, :]) / zsum).astype(BF16)
                return carry2

            lax.fori_loop(0, tt // PEER_ROUTE_TILE, lane_tile, 0)
            return carry

        lax.fori_loop(0, PEER_HEADS, head, 0)

    h2 = h2_ref[...]
    n_sub = PEER_EXPERT_BLOCK // PEER_SUB_BLOCK
    a_per_sub = PEER_SUB_BLOCK // nk
    sub = lambda s: slice(s * PEER_SUB_BLOCK, (s + 1) * PEER_SUB_BLOCK)
    cur = coef_s.at[g % 2]
    prev = coef_s.at[(g + 1) % 2]
    hidden = lambda s: _dot(u_ref[sub(s), :], h2, _NT)
    hids = [hidden(0), hidden(1)]
    half = PEER_EXPERT_BLOCK // 2
    pack = 16
    for s in range(n_sub):
        if s % 2 == 0:
            k0 = (s // 2) * half
            acc_s[...] += _dot(vt_ref[:, k0:k0 + half], prev[k0:k0 + half, :])
            hids.extend(hidden(s2) for s2 in (s + 2, s + 3) if s2 < n_sub)
        for al in range(a_per_sub):
            a = j * a_per_step + s * a_per_sub + al
            wsum = jnp.zeros((nk // pack, pack, tt), BF16)
            for h in range(PEER_HEADS):
                cnt_row = jnp.broadcast_to(cnt_s[h, pl.ds(a, 1), :], (pack, tt)).astype(BF16)
                e1_row = jnp.broadcast_to(e1_s[h, pl.ds(a, 1), :], (pack, tt)).astype(BF16)
                rank2 = rank2_s[h].reshape(nk // pack, pack, tt)
                e2 = e2_s[h].reshape(nk // pack, pack, tt)
                wsum = wsum + jnp.where(rank2 < cnt_row[None], e2 * e1_row[None], jnp.zeros((), BF16))
            r0 = s * PEER_SUB_BLOCK + al * nk
            cur[r0:r0 + nk, :] = _gelu(hids[s][al * nk:(al + 1) * nk, :].astype(BF16)) * wsum.reshape(nk, tt)

    @pl.when(jnp.logical_and(g > 0, gb % nj == nj - 1))
    def _():
        x2 = x1_ref[...] + gt_ref[...] * acc_s[...].T
        if final:
            x2 = _rmsnorm(x2, fg_ref[...])
        o_ref[...] = x2


def _peer_call(h2b, q, x1, gt, fg, k1p, k2p, u_b, v_b, tiles_per_group, final):
    t = h2b.shape[0]
    tt = PEER_TOKEN_TILE
    nb = PEER_EXPERT_BLOCK
    nj = PEER_N // nb
    n_steps = (t // tt) * nj + 1
    front = lambda g: jnp.minimum(g, n_steps - 2)
    back = lambda g: jnp.maximum(g - 1, 0)
    row_f = lambda w: pl.BlockSpec((tt, w), lambda g: (front(g) // nj, 0))
    row_b = lambda w: pl.BlockSpec((tt, w), lambda g: (back(g) // nj, 0))
    _, r, w = gt.shape
    tab = lambda dt: pltpu.VMEM((PEER_HEADS, PEER_NKEYS, tt), dt)
    return pl.pallas_call(
        functools.partial(_peer_kernel, final=final, n_steps=n_steps),
        grid=(n_steps,),
        in_specs=[row_f(D_MODEL), row_f(D_MODEL), row_b(D_MODEL),
                  pl.BlockSpec((None, r, w), lambda g: (back(g) // nj // tiles_per_group, 0, 0)),
                  _const_spec((1, D_MODEL)),
                  _const_spec((PEER_NKEYS, PEER_NKEYS)),
                  _const_spec((PEER_NKEYS, PEER_NKEYS)),
                  pl.BlockSpec((nb, D_MODEL), lambda g: (front(g) % nj, 0)),
                  pl.BlockSpec((D_MODEL, nb), lambda g: (0, back(g) % nj))],
        out_specs=row_b(D_MODEL),
        out_shape=jax.ShapeDtypeStruct((t, D_MODEL), F32),
        scratch_shapes=[tab(F32), tab(F32), tab(BF16), tab(BF16), pltpu.VMEM((D_MODEL, tt), F32),
                        pltpu.VMEM((PEER_NKEYS, tt), F32), pltpu.VMEM((PEER_NKEYS, tt), F32),
                        pltpu.VMEM((2, nb, tt), BF16)],
        compiler_params=_params(1),
        name="peer",
    )(h2b, q, x1, gt, fg, k1p, k2p, u_b, v_b)


def _block_diag(wblk):
    n, d, e = wblk.shape
    eye = jnp.eye(n, dtype=wblk.dtype)
    return (eye[:, None, :, None] * wblk[:, :, None, :]).reshape(n * d, n * e)


def _hilo(w):
    hi = w.astype(BF16)
    return hi, (w - hi.astype(F32)).astype(BF16)


def _pad_sample(z, hist=None):
    bsz = z.shape[0] // 4
    w = z.shape[1]
    z3 = z.reshape(bsz, 4, w)
    h3 = jnp.zeros((bsz, SAMPLE_HIST, w), F32)
    if hist is not None:
        h3 = h3.at[:, :, :hist.shape[-1]].set(hist)
    return jnp.concatenate([h3, z3, jnp.zeros((bsz, 1, w), F32)], axis=1).reshape(bsz * SAMPLE_SEQ_ROWS, w)


def _unpad_sample(o):
    bsz = o.shape[0] // SAMPLE_SEQ_ROWS
    return o.reshape(bsz, SAMPLE_SEQ_ROWS, -1)[:, SAMPLE_HIST:SAMPLE_HIST + 4].reshape(bsz * 4, -1)


def _layer_weights(l, w_in, w_out, peer_wq, peer_k1, peer_k2, peer_u, peer_v, lru_wa, lru_wx):
    w_in_p = jnp.pad(w_in[l], ((0, 0), (0, N_IN_PAD - N_IN)))
    wih = w_in_p.astype(BF16)
    wo = w_out[l].astype(BF16)
    wqh, wql = _hilo(peer_wq[l])
    wg = jnp.concatenate([_block_diag(lru_wa[l]), _block_diag(lru_wx[l])], axis=1).astype(BF16)
    half = PEER_NKEYS // 2
    k1p = jnp.pad(peer_k1[l], ((0, 0), (0, half)))
    k2p = jnp.pad(peer_k2[l], ((0, 0), (half, 0)))
    u_b = peer_u[l].astype(BF16)
    v_b = peer_v[l].T.astype(BF16)
    return dict(wih=wih, wo=wo, wqh=wqh, wql=wql, wg=wg,
                k1p=k1p, k2p=k2p, u_b=u_b, v_b=v_b)


def _trunk(x, mods, states, is_prompt, p, lw, l, final_g):
    t = x.shape[0]
    if is_prompt:
        sh1, sc1, gt1, sh2, sc2, gt2 = mods
    else:
        sh1, sc1, gt1, sh2, sc2 = (m.reshape(-1, TOKEN_TILE, D_MODEL) for m in mods[:5])
        gt2 = mods[5].reshape(-1, PEER_TOKEN_TILE, D_MODEL)
    if is_prompt:
        cfg = PROMPT_CFG
        tiles_tok = 2048 // TOKEN_TILE
        tiles_peer = 2048 // PEER_TOKEN_TILE
        n_seq = t // 2048
    else:
        cfg = SAMPLE_CFG
        tiles_tok = 1
        tiles_peer = 1
        n_seq = t // 4
    row1 = lambda v: v.reshape(1, -1)
    z_a, z_b, z_c = _inproj_call(x, row1(p['norm1_g']), sc1, sh1, lw['wih'], tiles_tok)
    if is_prompt:
        s_a = jnp.zeros((n_seq, A_W, HEAD_DIM), F32)
        s_c = jnp.zeros((n_seq, C_W, HEAD_DIM), F32)
        za_m, zb_m, zc_m = z_a, z_b, z_c
        inj = None
    else:
        st_hgrn, st_lru_h, st_lru_conv, st_dn, st_dn_conv = states
        s_a = jnp.swapaxes(st_hgrn, -1, -2).reshape(n_seq, A_W, HEAD_DIM)
        s_c = st_dn.reshape(n_seq, C_W, HEAD_DIM)
        za_m = _pad_sample(z_a)
        zb_m = _pad_sample(z_b, st_lru_conv)
        zc_m = _pad_sample(z_c, st_dn_conv)
        inj = jnp.zeros((n_seq, SAMPLE_SEQ_ROWS, B_W), F32).at[:, 0].set(st_lru_h).reshape(-1, B_W)
    rep = lambda v, h: row1(jnp.tile(v, h))
    oa, s_a_new = _hgrn_call(za_m, p['lb_param'], rep(p['a_norm_g'], A_HEADS), s_a, cfg, l)
    ob, hs = _lru_call(zb_m, inj, p['lru_conv_w'], row1(p['lru_conv_b']), lw['wg'],
                       row1(p['lru_ba']), row1(p['lru_bx']), row1(p['lru_L']), cfg)
    decay_cols = lambda v: jnp.zeros((1, LANES), F32).at[0, C_HEADS:2 * C_HEADS].set(v)
    oc, s_c_new = _dn_call(zc_m, p['dn_conv_w'], decay_cols(p['dn_A_log']), decay_cols(p['dn_dt_bias']),
                           rep(p['dn_norm_g'], C_HEADS), s_c, cfg)
    if is_prompt:
        seq = 2048
        h_t = hs.reshape(n_seq, seq, B_W)[:, -1]
        buf_b = z_b.reshape(n_seq, seq, ZB_W)[:, -SAMPLE_HIST:, :B_W]
        buf_c = z_c.reshape(n_seq, seq, ZC_W)[:, -SAMPLE_HIST:, :3 * C_W]
    else:
        oa, ob, oc = _unpad_sample(oa), _unpad_sample(ob), _unpad_sample(oc)
        h_t = hs.reshape(n_seq, SAMPLE_SEQ_ROWS, B_W)[:, -1]
        buf_b = z_b.reshape(n_seq, 4, ZB_W)[:, 1:, :B_W]
        buf_c = z_c.reshape(n_seq, 4, ZC_W)[:, 1:, :3 * C_W]
    new_states = (jnp.swapaxes(s_a_new.reshape(n_seq, A_HEADS, HEAD_DIM, HEAD_DIM), -1, -2), h_t, buf_b,
                  s_c_new.reshape(n_seq, C_HEADS, HEAD_DIM, HEAD_DIM), buf_c)
    x1, h2b, q = _outproj_call(x, oa, ob, oc, gt1, row1(p['norm2_g']), sc2, sh2,
                               lw['wo'], lw['wqh'], lw['wql'], tiles_tok)
    x2 = _peer_call(h2b, q, x1, gt2, row1(final_g), lw['k1p'], lw['k2p'], lw['u_b'], lw['v_b'],
                    tiles_peer, final=(l == DEPTH - 1))
    return x2, new_states


def kernel(x_prompt, x_sample, state_hgrn, state_lru_h, state_lru_conv, state_dn, state_dn_conv,
           c_prompt, c_sample, w_ada, b_ada, norm1_g, norm2_g, w_in, lb_param, a_norm_g,
           lru_conv_w, lru_conv_b, lru_wa, lru_ba, lru_wx, lru_bx, lru_L,
           dn_conv_w, dn_A_log, dn_dt_bias, dn_norm_g, w_out,
           peer_wq, peer_k1, peer_k2, peer_u, peer_v, final_norm_g):
    n_p, seq, _ = x_prompt.shape
    n_s, dec_seq, _ = x_sample.shape
    mod = _ada_call(jnp.concatenate([c_prompt, c_sample], axis=0), w_ada, b_ada)
    xp = x_prompt.reshape(n_p * seq, D_MODEL)
    xs = x_sample.reshape(n_s * dec_seq, D_MODEL)
    sample_states = (state_hgrn, state_lru_h, state_lru_conv, state_dn, state_dn_conv)
    p_new, s_new = [], []
    for l in range(DEPTH):
        p = dict(norm1_g=norm1_g[l], norm2_g=norm2_g[l], lb_param=lb_param, a_norm_g=a_norm_g[l],
                 lru_conv_w=lru_conv_w[l], lru_conv_b=lru_conv_b[l], lru_ba=lru_ba[l], lru_bx=lru_bx[l],
                 lru_L=lru_L[l], dn_conv_w=dn_conv_w[l], dn_A_log=dn_A_log[l], dn_dt_bias=dn_dt_bias[l],
                 dn_norm_g=dn_norm_g[l])
        lw = _layer_weights(l, w_in, w_out, peer_wq, peer_k1, peer_k2, peer_u, peer_v, lru_wa, lru_wx)
        parts = jnp.split(mod[l], 6, axis=-1)
        parts = [m + 1.0 if i in (1, 4) else m for i, m in enumerate(parts)]
        mods_p = [m[:n_p].reshape(n_p, 1, D_MODEL) for m in parts]
        mods_s = [jnp.repeat(m[n_p:], dec_seq, axis=0) for m in parts]
        xp, st_p = _trunk(xp, mods_p, None, True, p, lw, l, final_norm_g)
        xs, st_s = _trunk(xs, mods_s, tuple(s[l] for s in sample_states), False, p, lw, l, final_norm_g)
        p_new.append(st_p)
        s_new.append(st_s)
    stack = lambda sts: [jnp.stack([s[i] for s in sts]) for i in range(5)]
    p_st = stack(p_new)
    s_st = stack(s_new)
    return (xp.reshape(n_p, seq, D_MODEL), xs.reshape(n_s, dec_seq, D_MODEL), *p_st, *s_st)
```

```python
import functools

import jax
import jax.numpy as jnp
import numpy as np
from jax import lax
from jax.experimental import pallas as pl
from jax.experimental.pallas import tpu as pltpu

F32 = jnp.float32
BF16 = jnp.bfloat16
HIGHEST = lax.Precision.HIGHEST

D_MODEL = 1024
DEPTH = 2
HEAD_DIM = 64
A_HEADS = 4
A_W = 256
B_W = 384
C_HEADS = 6
C_W = 384
LRU_C = 8.0
CONV_W = 4
N_IN = 3340
N_IN_PAD = 3456
ZA_W = 1024
ZB_W = 768
ZC_W = 1664
PEER_HEADS = 8
PEER_NKEYS = 128
PEER_TOPK = 16
PEER_N = PEER_NKEYS * PEER_NKEYS
EPS = 1e-6

LANES = 128
SUPER_BLOCK = 256
SAMPLE_SEQ_ROWS = 8
SAMPLE_HIST = CONV_W - 1
TOKEN_TILE = 256
PEER_TOKEN_TILE = 512
PEER_EXPERT_BLOCK = 2048
PEER_SUB_BLOCK = 512
PEER_ROUTE_TILE = 256
VMEM_LIMIT_BYTES = 56 * 1024 * 1024

_NN = (((1,), (0,)), ((), ()))
_NT = (((1,), (1,)), ((), ()))
_TN = (((0,), (0,)), ((), ()))


def _dot(a, b, dims=_NN, precision=None):
    return lax.dot_general(a, b, dims, precision=precision, preferred_element_type=F32)


def _split2(x):
    hi = x.astype(BF16)
    lo = (x - hi.astype(F32)).astype(BF16)
    return hi, lo


def _split3(x):
    hi = x.astype(BF16)
    r = x - hi.astype(F32)
    mid = r.astype(BF16)
    lo = (r - mid.astype(F32)).astype(BF16)
    return hi, mid, lo


def _dot_x3(a, b, dims=_NN):
    ah, al = _split2(a)
    bh, bl = _split2(b)
    return _dot(ah, bh, dims) + _dot(ah, bl, dims) + _dot(al, bh, dims)


def _dot_w(a, wh, wl, dims=_NN):
    ah, al = _split2(a)
    return _dot(ah, wh, dims) + _dot(al, wh, dims) + _dot(ah, wl, dims)


def _dot_b(a, b, dims=_NN):
    return _dot(a.astype(BF16), b.astype(BF16), dims)


def _dot_sel(a, sel, dims=_NN, passes=3):
    parts = _split3(a)[:passes]
    out = _dot(parts[0], sel, dims)
    for p in parts[1:]:
        out = out + _dot(p, sel, dims)
    return out


def _head_sums(x, ones_ref):
    ones = ones_ref[0:LANES, 0:LANES]
    xb = x.astype(BF16)
    return jnp.concatenate([_dot(xb[:, t:t + LANES], ones) for t in range(0, x.shape[1], LANES)], axis=1)


def _sel_dot(sel, b, passes=3):
    parts = _split3(b)[:passes]
    out = _dot(sel, parts[0])
    for p in parts[1:]:
        out = out + _dot(sel, p)
    return out


def _softplus(x):
    return jnp.maximum(x, 0.0) + jnp.log(1.0 + jnp.exp(-jnp.abs(x)))


def _silu(x):
    return x * jax.nn.sigmoid(x)


def _gelu_x2(x):
    c = float(np.sqrt(2.0 / np.pi))
    return x + x * jnp.tanh(x * (c + (0.044715 * c) * (x * x)))


def _gelu(x):
    return 0.5 * x * (1.0 + jnp.tanh(float(np.sqrt(2.0 / np.pi)) * (x + 0.044715 * (x * x * x))))


def _rmsnorm(x, g):
    return x * lax.rsqrt(jnp.mean(x * x, axis=-1, keepdims=True) + EPS) * g


def _const_spec(shape):
    nd = len(shape)
    return pl.BlockSpec(shape, lambda *_: (0,) * nd)


def _params(n_grid):
    return pltpu.CompilerParams(dimension_semantics=("arbitrary",) * n_grid,
                                vmem_limit_bytes=VMEM_LIMIT_BYTES)


def _ada_kernel(c_ref, w_ref, b_ref, o_ref):
    c = c_ref[...]
    o_ref[0] = _dot(_silu(c), w_ref[0], precision=HIGHEST) + b_ref[0]


def _ada_call(c_all, w_ada, b_ada):
    nb = c_all.shape[0]
    nt = 6 * D_MODEL // 1024
    return pl.pallas_call(
        _ada_kernel,
        grid=(DEPTH, nt),
        in_specs=[
            _const_spec((nb, D_MODEL)),
            pl.BlockSpec((1, D_MODEL, 1024), lambda l, j: (l, 0, j)),
            pl.BlockSpec((1, 1, 1024), lambda l, j: (l, 0, j)),
        ],
        out_specs=pl.BlockSpec((1, nb, 1024), lambda l, j: (l, 0, j)),
        out_shape=jax.ShapeDtypeStruct((DEPTH, nb, 6 * D_MODEL), F32),
        compiler_params=_params(2),
        name="ada_mod",
    )(c_all, w_ada, b_ada.reshape(DEPTH, 1, 6 * D_MODEL))


def _inproj_kernel(x_ref, g_ref, sc_ref, sh_ref, w_ref, za_ref, zb_ref, zc_ref):
    h = (_rmsnorm(x_ref[...], g_ref[...]) * sc_ref[...] + sh_ref[...]).astype(BF16)
    for o_ref, lo, hi in ((za_ref, 0, ZA_W), (zb_ref, ZA_W, ZA_W + ZB_W), (zc_ref, ZA_W + ZB_W, N_IN_PAD)):
        o_ref[...] = _dot(h, w_ref[:, lo:hi])


def _mod_spec(mod, tiles_per_group):
    _, r, w = mod.shape
    return pl.BlockSpec((None, r, w), lambda i: (i // tiles_per_group, 0, 0))


def _inproj_call(x, g, sc, sh, wh, tiles_per_group):
    t = x.shape[0]
    tm = TOKEN_TILE
    row = lambda w: pl.BlockSpec((tm, w), lambda i: (i, 0))
    return pl.pallas_call(
        _inproj_kernel,
        grid=(t // tm,),
        in_specs=[row(D_MODEL), _const_spec((1, D_MODEL)), _mod_spec(sc, tiles_per_group),
                  _mod_spec(sh, tiles_per_group), _const_spec(wh.shape)],
        out_specs=[row(ZA_W), row(ZB_W), row(ZC_W)],
        out_shape=[jax.ShapeDtypeStruct((t, ZA_W), F32), jax.ShapeDtypeStruct((t, ZB_W), F32),
                   jax.ShapeDtypeStruct((t, ZC_W), F32)],
        compiler_params=_params(1),
        name="inproj",
    )(x, g, sc, sh, wh)


class _SeqCfg:
    def __init__(self, seq_rows, hist, real, chunk):
        self.seq_rows = seq_rows
        self.hist = hist
        self.real = real
        self.chunk = chunk
        self.masked = hist > 0 or hist + real < seq_rows
        self.n_chunks = SUPER_BLOCK // chunk
        self.long_seq = seq_rows > SUPER_BLOCK
        self.sb_per_seq = max(seq_rows // SUPER_BLOCK, 1)
        self.seq_per_sb = max(SUPER_BLOCK // seq_rows, 1)
        assert self.long_seq or chunk == seq_rows


PROMPT_CFG = _SeqCfg(seq_rows=2048, hist=0, real=2048, chunk=16)
SAMPLE_CFG = _SeqCfg(seq_rows=SAMPLE_SEQ_ROWS, hist=SAMPLE_HIST, real=4, chunk=SAMPLE_SEQ_ROWS)


def _row_in_seq(cfg, n_rows):
    r = lax.broadcasted_iota(jnp.int32, (n_rows, 1), 0)
    return r % min(cfg.seq_rows, SUPER_BLOCK)


def _real_mask(cfg, n_rows):
    r = _row_in_seq(cfg, n_rows)
    return (r >= cfg.hist) & (r < cfg.hist + cfg.real)


def _causal_conv(x, xp_ref, w_ref):
    n = x.shape[0]
    xp_ref[8:8 + n, :] = x
    y = x * w_ref[CONV_W - 1:CONV_W, :]
    for k in range(CONV_W - 1):
        s = CONV_W - 1 - k
        y = y + xp_ref[8 - s:8 - s + n, :] * w_ref[k:k + 1, :]
    xp_ref[0:8, :] = x[n - 8:n, :]
    return y


def _expand_state(s_cat, tile_ref, bd_ref):
    return _dot_sel(s_cat, tile_ref[...], passes=2) * bd_ref[...]


def _compress_state(s_bd, tile_t_ref):
    return _dot_sel(s_bd, tile_t_ref[...], passes=2)


def _head_norm_gate(o, z, g, ones_ref):
    ms = _dot_sel(o * o, ones_ref[...], passes=2) * (1.0 / HEAD_DIM)
    return o * lax.rsqrt(ms + EPS) * g * _silu(z)


def _run_chunks(cfg, sb, step, st_ref, s0_ref, sout_ref, tile_ref, tile_t_ref, bd_ref):
    if cfg.long_seq:
        @pl.when(sb % cfg.sb_per_seq == 0)
        def _():
            st_ref[...] = _expand_state(s0_ref[0], tile_ref, bd_ref)

        def body(k, carry):
            st_ref[...] = step(k, st_ref[...])
            return carry

        lax.fori_loop(0, cfg.n_chunks, body, 0, unroll=2)

        @pl.when(sb % cfg.sb_per_seq == cfg.sb_per_seq - 1)
        def _():
            sout_ref[0] = _compress_state(st_ref[...], tile_t_ref)
    else:
        def body(k, carry):
            st_new = step(k, _expand_state(s0_ref[k], tile_ref, bd_ref))
            sout_ref[k] = _compress_state(st_new, tile_t_ref)
            return carry

        lax.fori_loop(0, cfg.n_chunks, body, 0)


def _hgrn_kernel(z_ref, lbp_ref, g_ref, s0_ref, ltri_ref, ones_ref, tile_ref, tile_t_ref, bd_ref,
                 o_ref, sout_ref,
                 st_ref, q_s, k_s, g_s, qe_s, kt_s, egl_s, o_s, *, cfg, layer):
    sb = pl.program_id(0)
    c = cfg.chunk
    n = SUPER_BLOCK
    w = A_W
    lbp = lbp_ref[...]
    e = jnp.exp(lbp - jnp.max(lbp, axis=0, keepdims=True))
    lbs = e / jnp.sum(e, axis=0, keepdims=True)
    lb = jnp.sum(lbs[0:layer + 1, :], axis=0, keepdims=True) - lbs[0:1, :]

    aq = z_ref[:, 0:w]
    af = z_ref[:, w:2 * w]
    f = lb + (1.0 - lb) * jax.nn.sigmoid(af)
    lf = jnp.log(f)
    kk = 1.0 - f
    if cfg.masked:
        real = _real_mask(cfg, n)
        lf = jnp.where(real, lf, 0.0)
        kk = jnp.where(real, kk, 0.0)
    q = _silu(aq) * (HEAD_DIM ** -0.5)
    gcum = _sel_dot(ltri_ref[...], lf)
    g3 = gcum.reshape(cfg.n_chunks, c, w)
    gl3 = g3[:, c - 1:c, :]
    q_s[...] = q
    k_s[...] = kk
    g_s[...] = gcum
    qe_s[...] = q * jnp.exp(gcum)
    kt_s[...] = (kk.reshape(cfg.n_chunks, c, w) * jnp.exp(gl3 - g3)).reshape(n, w)
    egl_s[...] = jnp.exp(gl3).reshape(cfg.n_chunks, w)

    jio = lax.broadcasted_iota(jnp.int32, (c, 1), 0)
    ones = ones_ref[...]
    bd = bd_ref[...]

    def chunk(k, st):
        r0 = pl.multiple_of(k * c, c)
        rows = pl.ds(r0, c)
        q_c = q_s[rows, :]
        k_c = k_s[rows, :]
        g_c = g_s[rows, :]
        v_c = z_ref[rows, 2 * w:3 * w]
        o_inter = _dot(qe_s[rows, :].astype(BF16), st.astype(BF16), _NT)
        lo = min(c, 8)
        d_rows = []
        for i in range(c):
            nj = lo if i < lo else c
            dec = jnp.exp(jnp.minimum(g_c[i:i + 1, :] - g_c[0:nj, :], 0.0))
            d_rows.append(jnp.where(jio[0:nj] <= i, k_c[0:nj, :] * dec * q_c[i:i + 1, :], 0.0))
        a_b = _head_sums(jnp.concatenate(d_rows, axis=0), ones_ref)
        o_diag = jnp.sum(a_b[0:lo * lo, :].reshape(lo, lo, w) * v_c[None, 0:lo, :], axis=1)
        if c > lo:
            o_hi = jnp.sum(a_b[lo * lo:, :].reshape(c - lo, c, w) * v_c[None, :, :], axis=1)
            o_diag = jnp.concatenate([o_diag, o_hi], axis=0)
        o_s[rows, :] = o_inter + o_diag
        upd = _dot_b(v_c, kt_s[rows, :], _TN)
        return st * egl_s[pl.ds(k, 1), :] + bd * upd

    _run_chunks(cfg, sb, chunk, st_ref, s0_ref, sout_ref, tile_ref, tile_t_ref, bd_ref)
    o_ref[...] = _head_norm_gate(o_s[...], z_ref[:, 3 * w:4 * w], g_ref[...], ones_ref)


def _mixer_consts(heads, cfg):
    w = heads * HEAD_DIM
    lane_head = np.arange(w) // HEAD_DIM
    ones = (lane_head[:, None] == lane_head[None, :]).astype(np.float32)
    tile = (np.arange(HEAD_DIM)[:, None] == (np.arange(w) % HEAD_DIM)[None, :]).astype(np.float32)
    r = np.arange(SUPER_BLOCK)
    ltri = ((r[:, None] // cfg.chunk == r[None, :] // cfg.chunk) & (r[None, :] <= r[:, None])).astype(np.float32)
    return dict(ltri=jnp.asarray(ltri, BF16), ones=jnp.asarray(ones, BF16), tile=jnp.asarray(tile, BF16),
                tile_t=jnp.asarray(tile.T, BF16), bd=jnp.asarray(ones, F32))


def _state_specs(cfg, w):
    blk = (1 if cfg.long_seq else cfg.seq_per_sb, w, HEAD_DIM)
    if cfg.long_seq:
        imap = lambda i: (i // cfg.sb_per_seq, 0, 0)
    else:
        imap = lambda i: (i, 0, 0)
    return pl.BlockSpec(blk, imap)


def _hgrn_call(z_a, lb_param, g_exp, s0_t, cfg, layer):
    rows = z_a.shape[0]
    w = A_W
    cst = _mixer_consts(A_HEADS, cfg)
    n_seq = s0_t.shape[0]
    vm = lambda shape: pltpu.VMEM(shape, F32)
    kern = functools.partial(_hgrn_kernel, cfg=cfg, layer=layer)
    return pl.pallas_call(
        kern,
        grid=(rows // SUPER_BLOCK,),
        in_specs=[pl.BlockSpec((SUPER_BLOCK, ZA_W), lambda i: (i, 0)), _const_spec((DEPTH, w)),
                  _const_spec((1, w)), _state_specs(cfg, w), _const_spec(cst["ltri"].shape),
                  _const_spec(cst["ones"].shape), _const_spec(cst["tile"].shape),
                  _const_spec(cst["tile_t"].shape), _const_spec(cst["bd"].shape)],
        out_specs=[pl.BlockSpec((SUPER_BLOCK, w), lambda i: (i, 0)), _state_specs(cfg, w)],
        out_shape=[jax.ShapeDtypeStruct((rows, w), F32), jax.ShapeDtypeStruct((n_seq, w, HEAD_DIM), F32)],
        scratch_shapes=[vm((w, w))] + [vm((SUPER_BLOCK, w))] * 5 + [vm((cfg.n_chunks, w)), vm((SUPER_BLOCK, w))],
        compiler_params=_params(1),
        name="hgrn2",
    )(z_a, lb_param, g_exp, s0_t, cst["ltri"], cst["ones"], cst["tile"], cst["tile_t"], cst["bd"])


def _lru_kernel(*refs, cfg):
    if cfg.long_seq:
        (z_ref, cw_ref, cb_ref, wg_ref, ba_ref, bx_ref, lp_ref,
         ob_ref, hs_ref, prev_ref, hc_ref) = refs
        inj_ref = None
    else:
        (z_ref, inj_ref, cw_ref, cb_ref, wg_ref, ba_ref, bx_ref, lp_ref,
         ob_ref, hs_ref, prev_ref, hc_ref) = refs
    sb = pl.program_id(0)
    n = SUPER_BLOCK
    w = B_W

    @pl.when(sb % cfg.sb_per_seq == 0)
    def _():
        prev_ref[0:8, :] = jnp.zeros((8, prev_ref.shape[1]), F32)
        hc_ref[...] = jnp.zeros_like(hc_ref)

    xc = _causal_conv(z_ref[:, 0:w], prev_ref, cw_ref) + cb_ref[...]
    gates = _dot_b(xc, wg_ref[...])
    r = jax.nn.sigmoid(gates[:, 0:w] + ba_ref[...])
    ig = jax.nn.sigmoid(gates[:, w:2 * w] + bx_ref[...])
    log_a = -LRU_C * r * _softplus(-lp_ref[...])
    a = jnp.exp(log_a)
    b = jnp.sqrt(1.0 - jnp.exp(2.0 * log_a)) * ig * xc
    if cfg.masked:
        real = _real_mask(cfg, n)
        a = jnp.where(real, a, 1.0)
        b = jnp.where(real, b, 0.0)
    if inj_ref is not None:
        b = b + inj_ref[...]
    ris = _row_in_seq(cfg, n)
    d = 1
    while d < min(n, cfg.seq_rows):
        has = ris >= d
        a_sh = jnp.where(has, pltpu.roll(a, d, 0), 1.0)
        b_sh = jnp.where(has, pltpu.roll(b, d, 0), 0.0)
        b = b + a * b_sh
        a = a * a_sh
        d *= 2
    hs = b + a * hc_ref[...]
    if cfg.long_seq:
        hc_ref[...] = hs[n - 1:n, :]
    hs_ref[...] = hs
    ob_ref[...] = hs * _gelu(z_ref[:, w:2 * w])


def _lru_call(z_b, inj, cw, cb, wg, ba, bx, lp, cfg):
    rows = z_b.shape[0]
    w = B_W
    row = lambda width: pl.BlockSpec((SUPER_BLOCK, width), lambda i: (i, 0))
    ins = [z_b] + ([] if cfg.long_seq else [inj]) + [cw, cb, wg, ba, bx, lp]
    specs = [row(ZB_W)] + ([] if cfg.long_seq else [row(w)]) + [_const_spec(a.shape) for a in ins[-6:]]
    return pl.pallas_call(
        functools.partial(_lru_kernel, cfg=cfg),
        grid=(rows // SUPER_BLOCK,),
        in_specs=specs,
        out_specs=[row(w), row(w)],
        out_shape=[jax.ShapeDtypeStruct((rows, w), F32)] * 2,
        scratch_shapes=[pltpu.VMEM((8 + SUPER_BLOCK, w), F32), pltpu.VMEM((1, w), F32)],
        compiler_params=_params(1),
        name="rglru",
    )(*ins)


def _dn_kernel(z_ref, cw_ref, alog_ref, dtb_ref, g_ref, s0_ref, ltri_ref, ones_ref, tile_ref, tile_t_ref,
               bd_ref, expb_ref, expa_ref,
               o_ref, sout_ref,
               st_ref, prev_ref, q_s, k_s, v_s, g_s, eg_s, beta_s, kt_s, egl_s, o_s, *, cfg):
    sb = pl.program_id(0)
    c = cfg.chunk
    n = SUPER_BLOCK
    w = C_W

    @pl.when(sb % cfg.sb_per_seq == 0)
    def _():
        prev_ref[0:8, :] = jnp.zeros((8, prev_ref.shape[1]), F32)

    qkv = _silu(_causal_conv(z_ref[:, 0:3 * w], prev_ref, cw_ref))
    q = qkv[:, 0:w]
    kx = qkv[:, w:2 * w]
    ones = ones_ref[...]
    q = q * lax.rsqrt(_dot_sel(q * q, ones, passes=2) + EPS) * (HEAD_DIM ** -0.5)
    kx = kx * lax.rsqrt(_dot_sel(kx * kx, ones, passes=2) + EPS)
    pc = z_ref[:, 4 * w:4 * w + LANES]
    beta = _dot_sel(jax.nn.sigmoid(pc), expb_ref[...], passes=2)
    gdec = _dot_sel(-jnp.exp(alog_ref[...]) * _softplus(pc + dtb_ref[...]), expa_ref[...], passes=2)
    if cfg.masked:
        real = _real_mask(cfg, n)
        beta = jnp.where(real, beta, 0.0)
        gdec = jnp.where(real, gdec, 0.0)
    gcum = _sel_dot(ltri_ref[...], gdec)
    g3 = gcum.reshape(cfg.n_chunks, c, w)
    gl3 = g3[:, c - 1:c, :]
    q_s[...] = q
    k_s[...] = kx
    v_s[...] = qkv[:, 2 * w:3 * w]
    g_s[...] = gcum
    eg_s[...] = jnp.exp(gcum)
    beta_s[...] = beta
    kt_s[...] = (kx.reshape(cfg.n_chunks, c, w) * jnp.exp(gl3 - g3)).reshape(n, w)
    egl_s[...] = jnp.exp(gl3).reshape(cfg.n_chunks, w)

    iio = lax.broadcasted_iota(jnp.int32, (c, 1), 0)
    bd = bd_ref[...]

    def chunk(k, st):
        r0 = pl.multiple_of(k * c, c)
        rows = pl.ds(r0, c)
        q_c = q_s[rows, :]
        k_c = k_s[rows, :]
        g_c = g_s[rows, :]
        eg_c = eg_s[rows, :]
        beta_c = beta_s[rows, :]
        qk_s = _dot_b(jnp.concatenate([q_c, k_c], axis=0), st)
        q_st = qk_s[0:c, :]
        k_st = qk_s[c:2 * c, :]
        lo = min(c, 8)
        tail = lambda x: x[lo:c, :]
        d_rows = ([k_c * k_c[j:j + 1, :] for j in range(lo)] + [q_c * k_c[j:j + 1, :] for j in range(lo)]
                  + [tail(k_c) * k_c[j:j + 1, :] for j in range(lo, c)]
                  + [tail(q_c) * k_c[j:j + 1, :] for j in range(lo, c)])
        dots = _head_sums(jnp.concatenate(d_rows, axis=0), ones_ref)
        e = beta_c * (v_s[rows, :] - eg_c * k_st)
        o = eg_c * q_st
        for j in range(lo):
            dec = jnp.exp(jnp.minimum(g_c - g_c[j:j + 1, :], 0.0))
            m_col = jnp.where(iio > j, beta_c * dots[j * c:(j + 1) * c, :] * dec, 0.0)
            e_j = e[j:j + 1, :]
            e = e - m_col * e_j
            qk_col = jnp.where(iio >= j, dots[(lo + j) * c:(lo + j + 1) * c, :] * dec, 0.0)
            o = o + qk_col * e_j
        if c > lo:
            nt = c - lo
            e_t, o_t, g_t, beta_t = tail(e), tail(o), tail(g_c), tail(beta_c)
            base = 2 * lo * c
            for j in range(lo, c):
                dec = jnp.exp(jnp.minimum(g_t - g_c[j:j + 1, :], 0.0))
                kk = dots[base + (j - lo) * nt:base + (j - lo + 1) * nt, :]
                qk = dots[base + (nt + j - lo) * nt:base + (nt + j - lo + 1) * nt, :]
                e_j = e_t[j - lo:j - lo + 1, :]
                e_t = e_t - jnp.where(iio[0:nt] > j - lo, beta_t * kk * dec, 0.0) * e_j
                o_t = o_t + jnp.where(iio[0:nt] >= j - lo, qk * dec, 0.0) * e_j
            e = jnp.concatenate([e[0:lo, :], e_t], axis=0)
            o = jnp.concatenate([o[0:lo, :], o_t], axis=0)
        o_s[rows, :] = o
        upd = _dot_b(kt_s[rows, :], e, _TN)
        return st * egl_s[pl.ds(k, 1), :] + bd * upd

    _run_chunks(cfg, sb, chunk, st_ref, s0_ref, sout_ref, tile_ref, tile_t_ref, bd_ref)
    o_ref[...] = _head_norm_gate(o_s[...], z_ref[:, 3 * w:4 * w], g_ref[...], ones_ref)


def _dn_call(z_c, cw, alog_exp, dtb_exp, g_exp, s0, cfg):
    rows = z_c.shape[0]
    w = C_W
    cst = _mixer_consts(C_HEADS, cfg)
    lane_head = np.arange(w) // HEAD_DIM
    expb = (np.arange(128)[:, None] == lane_head[None, :]).astype(np.float32)
    expa = (np.arange(128)[:, None] == (lane_head[None, :] + C_HEADS)).astype(np.float32)
    expb = jnp.asarray(expb, BF16)
    expa = jnp.asarray(expa, BF16)
    n_seq = s0.shape[0]
    vm = lambda shape: pltpu.VMEM(shape, F32)
    return pl.pallas_call(
        functools.partial(_dn_kernel, cfg=cfg),
        grid=(rows // SUPER_BLOCK,),
        in_specs=[pl.BlockSpec((SUPER_BLOCK, ZC_W), lambda i: (i, 0)), _const_spec(cw.shape),
                  _const_spec((1, LANES)), _const_spec((1, LANES)), _const_spec((1, w)), _state_specs(cfg, w),
                  _const_spec(cst["ltri"].shape), _const_spec(cst["ones"].shape),
                  _const_spec(cst["tile"].shape), _const_spec(cst["tile_t"].shape),
                  _const_spec(cst["bd"].shape), _const_spec(expb.shape), _const_spec(expa.shape)],
        out_specs=[pl.BlockSpec((SUPER_BLOCK, w), lambda i: (i, 0)), _state_specs(cfg, w)],
        out_shape=[jax.ShapeDtypeStruct((rows, w), F32), jax.ShapeDtypeStruct((n_seq, w, HEAD_DIM), F32)],
        scratch_shapes=[vm((w, w)), vm((8 + SUPER_BLOCK, 3 * w))] + [vm((SUPER_BLOCK, w))] * 7
        + [vm((cfg.n_chunks, w)), vm((SUPER_BLOCK, w))],
        compiler_params=_params(1),
        name="deltanet",
    )(z_c, cw, alog_exp, dtb_exp, g_exp, s0, cst["ltri"], cst["ones"], cst["tile"], cst["tile_t"],
      cst["bd"], expb, expa)


def _outproj_kernel(x_ref, oa_ref, ob_ref, oc_ref, gt_ref, g_ref, sc_ref, sh_ref,
                    wo_ref, wqh_ref, wql_ref, x1_ref, h2_ref, q_ref):
    mo = _dot_b(oa_ref[...], wo_ref[0:A_W, :])
    mo = mo + _dot_b(ob_ref[...], wo_ref[A_W:A_W + B_W, :])
    mo = mo + _dot_b(oc_ref[...], wo_ref[A_W + B_W:, :])
    x1 = x_ref[...] + gt_ref[...] * mo
    x1_ref[...] = x1
    h2 = _rmsnorm(x1, g_ref[...]) * sc_ref[...] + sh_ref[...]
    h2_ref[...] = h2.astype(BF16)
    q_ref[...] = _dot_w(h2, wqh_ref[...], wql_ref[...])


def _outproj_call(x, oa, ob, oc, gt, g, sc, sh, wo, wqh, wql, tiles_per_group):
    t = x.shape[0]
    tm = TOKEN_TILE
    row = lambda w: pl.BlockSpec((tm, w), lambda i: (i, 0))
    return pl.pallas_call(
        _outproj_kernel,
        grid=(t // tm,),
        in_specs=[row(D_MODEL), row(A_W), row(B_W), row(C_W), _mod_spec(gt, tiles_per_group),
                  _const_spec((1, D_MODEL)), _mod_spec(sc, tiles_per_group), _mod_spec(sh, tiles_per_group),
                  _const_spec(wo.shape), _const_spec(wqh.shape), _const_spec(wql.shape)],
        out_specs=[row(D_MODEL), row(D_MODEL), row(D_MODEL)],
        out_shape=[jax.ShapeDtypeStruct((t, D_MODEL), F32), jax.ShapeDtypeStruct((t, D_MODEL), BF16),
                   jax.ShapeDtypeStruct((t, D_MODEL), F32)],
        compiler_params=_params(1),
        name="outproj",
    )(x, oa, ob, oc, gt, g, sc, sh, wo, wqh, wql)


def _top16(s):
    vals = []
    for r in range(PEER_TOPK):
        m = jnp.max(s, axis=0, keepdims=True)
        s = jnp.where(s == m, -(32.0 + r) * 2.0 ** 95, s)
        vals.append(m)
    rank = jnp.where(s <= -(2.0 ** 99), s * -(2.0 ** -95) - 32.0, float(PEER_TOPK))
    return jnp.concatenate(vals, axis=0), rank


def _peer_kernel(h2_ref, q_ref, x1_ref, gt_ref, fg_ref, k1_ref, k2_ref, u_ref, vt_ref, o_ref,
                 cnt_s, e1_s, rank2_s, e2_s, acc_s, s1_s, s2_s, coef_s, *, final, n_steps):
    g = pl.program_id(0)
    tt = PEER_TOKEN_TILE
    nk = PEER_NKEYS
    nj = PEER_N // PEER_EXPERT_BLOCK
    a_per_step = PEER_EXPERT_BLOCK // nk
    j = jnp.minimum(g, n_steps - 2) % nj
    gb = jnp.maximum(g - 1, 0)

    @pl.when(g == 0)
    def _():
        coef_s[...] = jnp.zeros_like(coef_s)

    @pl.when(gb % nj == 0)
    def _():
        acc_s[...] = jnp.zeros_like(acc_s)

    @pl.when(jnp.logical_and(g % nj == 0, g < n_steps - 1))
    def _():

        def head(h, carry):
            qh = q_ref[:, pl.ds(pl.multiple_of(h * nk, nk), nk)]
            s1_s[...] = _dot_x3(k1_ref[...], qh, _NT)
            s2_s[...] = _dot_x3(k2_ref[...], qh, _NT)

            def lane_tile(c, carry2):
                cols = pl.ds(pl.multiple_of(c * PEER_ROUTE_TILE, PEER_ROUTE_TILE), PEER_ROUTE_TILE)
                s1 = s1_s[:, cols]
                s2 = s2_s[:, cols]
                v1, rank1 = _top16(s1)
                v2, rank2 = _top16(s2)
                c3 = v1[:, None, :] + v2[None, :, :]
                jrow = lax.broadcasted_iota(jnp.int32, (8, 1), 0)
                parts = [v1[0:1, :] + v2, v1[1:2, :] + v2[0:8, :]]
                for i in range(2, 8):
                    parts.append(jnp.where(jrow < PEER_TOPK // (i + 1), v1[i:i + 1, :] + v2[0:8, :], -jnp.inf))
                parts.append(v1[8:16, :] + v2[0:1, :])
                cand = jnp.concatenate(parts, axis=0)
                m = None
                for _ in range(PEER_TOPK):
                    m = jnp.max(cand, axis=0, keepdims=True)
                    cand = jnp.where(cand == m, -jnp.inf, cand)
                tau = m
                sel = c3 >= tau[None, :, :]
                m0 = v1[0:1, :] + v2[0:1, :]
                zsum = jnp.sum(jnp.where(sel, jnp.exp(c3 - m0[None, :, :]), 0.0).reshape(-1, PEER_ROUTE_TILE),
                               axis=0, keepdims=True)
                n_i = jnp.sum(jnp.where(sel, 1.0, 0.0), axis=1)
                rank1 = rank1.astype(BF16)
                cnt = jnp.zeros((nk, PEER_ROUTE_TILE), BF16)
                for i in range(PEER_TOPK):
                    cnt = cnt + jnp.where(rank1 == i, n_i[i:i + 1, :].astype(BF16), jnp.zeros((), BF16))
                cnt_s[h, :, cols] = cnt.astype(F32)
                e1_s[h, :, cols] = jnp.exp(s1 - v1[0:1, :])
                rank2_s[h, :, cols] = rank2.astype(BF16)
                e2_s[h, :, cols] = (jnp.exp(s2 - v2[0:1, :]) * (0.5 / zsum)).astype(BF16)
                return carry2

            lax.fori_loop(0, tt // PEER_ROUTE_TILE, lane_tile, 0)
            return carry

        lax.fori_loop(0, PEER_HEADS, head, 0)

    h2 = h2_ref[...]
    n_sub = PEER_EXPERT_BLOCK // PEER_SUB_BLOCK
    a_per_sub = PEER_SUB_BLOCK // nk
    sub = lambda s: slice(s * PEER_SUB_BLOCK, (s + 1) * PEER_SUB_BLOCK)
    cur = coef_s.at[g % 2]
    prev = coef_s.at[(g + 1) % 2]
    hidden = lambda s: _dot(u_ref[sub(s), :], h2, _NT)
    hids = [hidden(0), hidden(1)]
    half = PEER_EXPERT_BLOCK // 2
    pack = 16
    for s in range(n_sub):
        if s % 2 == 0:
            k0 = (s // 2) * half
            acc_s[...] += _dot(vt_ref[:, k0:k0 + half], prev[k0:k0 + half, :])
            hids.extend(hidden(s2) for s2 in (s + 2, s + 3) if s2 < n_sub)
        for al in range(a_per_sub):
            a = j * a_per_step + s * a_per_sub + al
            wsum = jnp.zeros((nk // pack, pack, tt), BF16)
            for h in range(PEER_HEADS):
                cnt_row = jnp.broadcast_to(cnt_s[h, pl.ds(a, 1), :], (pack, tt)).astype(BF16)
                e1_row = jnp.broadcast_to(e1_s[h, pl.ds(a, 1), :], (pack, tt)).astype(BF16)
                rank2 = rank2_s[h].reshape(nk // pack, pack, tt)
                e2 = e2_s[h].reshape(nk // pack, pack, tt)
                wsum = wsum + jnp.where(rank2 < cnt_row[None], e2 * e1_row[None], jnp.zeros((), BF16))
            r0 = s * PEER_SUB_BLOCK + al * nk
            cur[r0:r0 + nk, :] = _gelu_x2(hids[s][al * nk:(al + 1) * nk, :].astype(BF16)) * wsum.reshape(nk, tt)

    @pl.when(jnp.logical_and(g > 0, gb % nj == nj - 1))
    def _():
        x2 = x1_ref[...] + gt_ref[...] * acc_s[...].T
        if final:
            x2 = _rmsnorm(x2, fg_ref[...])
        o_ref[...] = x2


def _peer_call(h2b, q, x1, gt, fg, k1p, k2p, u_b, v_b, tiles_per_group, final):
    t = h2b.shape[0]
    tt = PEER_TOKEN_TILE
    nb = PEER_EXPERT_BLOCK
    nj = PEER_N // nb
    n_steps = (t // tt) * nj + 1
    front = lambda g: jnp.minimum(g, n_steps - 2)
    back = lambda g: jnp.maximum(g - 1, 0)
    row_f = lambda w: pl.BlockSpec((tt, w), lambda g: (front(g) // nj, 0))
    row_b = lambda w: pl.BlockSpec((tt, w), lambda g: (back(g) // nj, 0))
    _, r, w = gt.shape
    tab = lambda dt: pltpu.VMEM((PEER_HEADS, PEER_NKEYS, tt), dt)
    return pl.pallas_call(
        functools.partial(_peer_kernel, final=final, n_steps=n_steps),
        grid=(n_steps,),
        in_specs=[row_f(D_MODEL), row_f(D_MODEL), row_b(D_MODEL),
                  pl.BlockSpec((None, r, w), lambda g: (back(g) // nj // tiles_per_group, 0, 0)),
                  _const_spec((1, D_MODEL)),
                  _const_spec((PEER_NKEYS, PEER_NKEYS)),
                  _const_spec((PEER_NKEYS, PEER_NKEYS)),
                  pl.BlockSpec((nb, D_MODEL), lambda g: (front(g) % nj, 0)),
                  pl.BlockSpec((D_MODEL, nb), lambda g: (0, back(g) % nj))],
        out_specs=row_b(D_MODEL),
        out_shape=jax.ShapeDtypeStruct((t, D_MODEL), F32),
        scratch_shapes=[tab(F32), tab(F32), tab(BF16), tab(BF16), pltpu.VMEM((D_MODEL, tt), F32),
                        pltpu.VMEM((PEER_NKEYS, tt), F32), pltpu.VMEM((PEER_NKEYS, tt), F32),
                        pltpu.VMEM((2, nb, tt), BF16)],
        compiler_params=_params(1),
        name="peer",
    )(h2b, q, x1, gt, fg, k1p, k2p, u_b, v_b)


def _block_diag(wblk):
    n, d, e = wblk.shape
    eye = jnp.eye(n, dtype=wblk.dtype)
    return (eye[:, None, :, None] * wblk[:, :, None, :]).reshape(n * d, n * e)


def _hilo(w):
    hi = w.astype(BF16)
    return hi, (w - hi.astype(F32)).astype(BF16)


def _pad_sample(z, hist=None):
    bsz = z.shape[0] // 4
    w = z.shape[1]
    z3 = z.reshape(bsz, 4, w)
    h3 = jnp.zeros((bsz, SAMPLE_HIST, w), F32)
    if hist is not None:
        h3 = h3.at[:, :, :hist.shape[-1]].set(hist)
    return jnp.concatenate([h3, z3, jnp.zeros((bsz, 1, w), F32)], axis=1).reshape(bsz * SAMPLE_SEQ_ROWS, w)


def _unpad_sample(o):
    bsz = o.shape[0] // SAMPLE_SEQ_ROWS
    return o.reshape(bsz, SAMPLE_SEQ_ROWS, -1)[:, SAMPLE_HIST:SAMPLE_HIST + 4].reshape(bsz * 4, -1)


def _layer_weights(l, w_in, w_out, peer_wq, peer_k1, peer_k2, peer_u, peer_v, lru_wa, lru_wx):
    w_in_p = jnp.pad(w_in[l], ((0, 0), (0, N_IN_PAD - N_IN)))
    wih = w_in_p.astype(BF16)
    wo = w_out[l].astype(BF16)
    wqh, wql = _hilo(peer_wq[l])
    wg = jnp.concatenate([_block_diag(lru_wa[l]), _block_diag(lru_wx[l])], axis=1).astype(BF16)
    half = PEER_NKEYS // 2
    k1p = jnp.pad(peer_k1[l], ((0, 0), (0, half)))
    k2p = jnp.pad(peer_k2[l], ((0, 0), (half, 0)))
    u_b = peer_u[l].astype(BF16)
    v_b = peer_v[l].T.astype(BF16)
    return dict(wih=wih, wo=wo, wqh=wqh, wql=wql, wg=wg,
                k1p=k1p, k2p=k2p, u_b=u_b, v_b=v_b)


def _trunk(x, mods, states, is_prompt, p, lw, l, final_g):
    t = x.shape[0]
    if is_prompt:
        sh1, sc1, gt1, sh2, sc2, gt2 = mods
    else:
        sh1, sc1, gt1, sh2, sc2 = (m.reshape(-1, TOKEN_TILE, D_MODEL) for m in mods[:5])
        gt2 = mods[5].reshape(-1, PEER_TOKEN_TILE, D_MODEL)
    if is_prompt:
        cfg = PROMPT_CFG
        tiles_tok = 2048 // TOKEN_TILE
        tiles_peer = 2048 // PEER_TOKEN_TILE
        n_seq = t // 2048
    else:
        cfg = SAMPLE_CFG
        tiles_tok = 1
        tiles_peer = 1
        n_seq = t // 4
    row1 = lambda v: v.reshape(1, -1)
    z_a, z_b, z_c = _inproj_call(x, row1(p['norm1_g']), sc1, sh1, lw['wih'], tiles_tok)
    if is_prompt:
        s_a = jnp.zeros((n_seq, A_W, HEAD_DIM), F32)
        s_c = jnp.zeros((n_seq, C_W, HEAD_DIM), F32)
        za_m, zb_m, zc_m = z_a, z_b, z_c
        inj = None
    else:
        st_hgrn, st_lru_h, st_lru_conv, st_dn, st_dn_conv = states
        s_a = jnp.swapaxes(st_hgrn, -1, -2).reshape(n_seq, A_W, HEAD_DIM)
        s_c = st_dn.reshape(n_seq, C_W, HEAD_DIM)
        za_m = _pad_sample(z_a)
        zb_m = _pad_sample(z_b, st_lru_conv)
        zc_m = _pad_sample(z_c, st_dn_conv)
        inj = jnp.zeros((n_seq, SAMPLE_SEQ_ROWS, B_W), F32).at[:, 0].set(st_lru_h).reshape(-1, B_W)
    rep = lambda v, h: row1(jnp.tile(v, h))
    oa, s_a_new = _hgrn_call(za_m, p['lb_param'], rep(p['a_norm_g'], A_HEADS), s_a, cfg, l)
    ob, hs = _lru_call(zb_m, inj, p['lru_conv_w'], row1(p['lru_conv_b']), lw['wg'],
                       row1(p['lru_ba']), row1(p['lru_bx']), row1(p['lru_L']), cfg)
    decay_cols = lambda v: jnp.zeros((1, LANES), F32).at[0, C_HEADS:2 * C_HEADS].set(v)
    oc, s_c_new = _dn_call(zc_m, p['dn_conv_w'], decay_cols(p['dn_A_log']), decay_cols(p['dn_dt_bias']),
                           rep(p['dn_norm_g'], C_HEADS), s_c, cfg)
    if is_prompt:
        seq = 2048
        h_t = hs.reshape(n_seq, seq, B_W)[:, -1]
        buf_b = z_b.reshape(n_seq, seq, ZB_W)[:, -SAMPLE_HIST:, :B_W]
        buf_c = z_c.reshape(n_seq, seq, ZC_W)[:, -SAMPLE_HIST:, :3 * C_W]
    else:
        oa, ob, oc = _unpad_sample(oa), _unpad_sample(ob), _unpad_sample(oc)
        h_t = hs.reshape(n_seq, SAMPLE_SEQ_ROWS, B_W)[:, -1]
        buf_b = z_b.reshape(n_seq, 4, ZB_W)[:, 1:, :B_W]
        buf_c = z_c.reshape(n_seq, 4, ZC_W)[:, 1:, :3 * C_W]
    new_states = (jnp.swapaxes(s_a_new.reshape(n_seq, A_HEADS, HEAD_DIM, HEAD_DIM), -1, -2), h_t, buf_b,
                  s_c_new.reshape(n_seq, C_HEADS, HEAD_DIM, HEAD_DIM), buf_c)
    x1, h2b, q = _outproj_call(x, oa, ob, oc, gt1, row1(p['norm2_g']), sc2, sh2,
                               lw['wo'], lw['wqh'], lw['wql'], tiles_tok)
    x2 = _peer_call(h2b, q, x1, gt2, row1(final_g), lw['k1p'], lw['k2p'], lw['u_b'], lw['v_b'],
                    tiles_peer, final=(l == DEPTH - 1))
    return x2, new_states


def kernel(x_prompt, x_sample, state_hgrn, state_lru_h, state_lru_conv, state_dn, state_dn_conv,
           c_prompt, c_sample, w_ada, b_ada, norm1_g, norm2_g, w_in, lb_param, a_norm_g,
           lru_conv_w, lru_conv_b, lru_wa, lru_ba, lru_wx, lru_bx, lru_L,
           dn_conv_w, dn_A_log, dn_dt_bias, dn_norm_g, w_out,
           peer_wq, peer_k1, peer_k2, peer_u, peer_v, final_norm_g):
    n_p, seq, _ = x_prompt.shape
    n_s, dec_seq, _ = x_sample.shape
    mod = _ada_call(jnp.concatenate([c_prompt, c_sample], axis=0), w_ada, b_ada)
    xp = x_prompt.reshape(n_p * seq, D_MODEL)
    xs = x_sample.reshape(n_s * dec_seq, D_MODEL)
    sample_states = (state_hgrn, state_lru_h, state_lru_conv, state_dn, state_dn_conv)
    p_new, s_new = [], []
    for l in range(DEPTH):
        p = dict(norm1_g=norm1_g[l], norm2_g=norm2_g[l], lb_param=lb_param, a_norm_g=a_norm_g[l],
                 lru_conv_w=lru_conv_w[l], lru_conv_b=lru_conv_b[l], lru_ba=lru_ba[l], lru_bx=lru_bx[l],
                 lru_L=lru_L[l], dn_conv_w=dn_conv_w[l], dn_A_log=dn_A_log[l], dn_dt_bias=dn_dt_bias[l],
                 dn_norm_g=dn_norm_g[l])
        lw = _layer_weights(l, w_in, w_out, peer_wq, peer_k1, peer_k2, peer_u, peer_v, lru_wa, lru_wx)
        parts = jnp.split(mod[l], 6, axis=-1)
        parts = [m + 1.0 if i in (1, 4) else m for i, m in enumerate(parts)]
        mods_p = [m[:n_p].reshape(n_p, 1, D_MODEL) for m in parts]
        mods_s = [jnp.repeat(m[n_p:], dec_seq, axis=0) for m in parts]
        xp, st_p = _trunk(xp, mods_p, None, True, p, lw, l, final_norm_g)
        xs, st_s = _trunk(xs, mods_s, tuple(s[l] for s in sample_states), False, p, lw, l, final_norm_g)
        p_new.append(st_p)
        s_new.append(st_s)
    stack = lambda sts: [jnp.stack([s[i] for s in sts]) for i in range(5)]
    p_st = stack(p_new)
    s_st = stack(s_new)
    return (xp.reshape(n_p, seq, D_MODEL), xs.reshape(n_s, dec_seq, D_MODEL), *p_st, *s_st)
```

```python
import functools

import jax
import jax.numpy as jnp
import numpy as np
from jax import lax
from jax.experimental import pallas as pl
from jax.experimental.pallas import tpu as pltpu

F32 = jnp.float32
BF16 = jnp.bfloat16
HIGHEST = lax.Precision.HIGHEST

D_MODEL = 1024
DEPTH = 2
HEAD_DIM = 64
A_HEADS = 4
A_W = 256
B_W = 384
C_HEADS = 6
C_W = 384
LRU_C = 8.0
CONV_W = 4
N_IN = 3340
N_IN_PAD = 3456
ZA_W = 1024
ZB_W = 768
ZC_W = 1664
PEER_HEADS = 8
PEER_NKEYS = 128
PEER_TOPK = 16
PEER_N = PEER_NKEYS * PEER_NKEYS
EPS = 1e-6

LANES = 128
SUPER_BLOCK = 256
SAMPLE_SEQ_ROWS = 8
SAMPLE_HIST = CONV_W - 1
TOKEN_TILE = 256
PEER_TOKEN_TILE = 512
PEER_EXPERT_BLOCK = 2048
PEER_SUB_BLOCK = 512
PEER_ROUTE_TILE = 256
VMEM_LIMIT_BYTES = 56 * 1024 * 1024

_NN = (((1,), (0,)), ((), ()))
_NT = (((1,), (1,)), ((), ()))
_TN = (((0,), (0,)), ((), ()))


def _dot(a, b, dims=_NN, precision=None):
    return lax.dot_general(a, b, dims, precision=precision, preferred_element_type=F32)


def _split2(x):
    hi = x.astype(BF16)
    lo = (x - hi.astype(F32)).astype(BF16)
    return hi, lo


def _split3(x):
    hi = x.astype(BF16)
    r = x - hi.astype(F32)
    mid = r.astype(BF16)
    lo = (r - mid.astype(F32)).astype(BF16)
    return hi, mid, lo


def _dot_x3(a, b, dims=_NN):
    ah, al = _split2(a)
    bh, bl = _split2(b)
    return _dot(ah, bh, dims) + _dot(ah, bl, dims) + _dot(al, bh, dims)


def _dot_w(a, wh, wl, dims=_NN):
    ah, al = _split2(a)
    return _dot(ah, wh, dims) + _dot(al, wh, dims) + _dot(ah, wl, dims)


def _dot_b(a, b, dims=_NN):
    return _dot(a.astype(BF16), b.astype(BF16), dims)


def _dot_sel(a, sel, dims=_NN, passes=3):
    parts = _split3(a)[:passes]
    out = _dot(parts[0], sel, dims)
    for p in parts[1:]:
        out = out + _dot(p, sel, dims)
    return out


def _head_sums(x, ones_ref):
    ones = ones_ref[0:LANES, 0:LANES]
    xb = x.astype(BF16)
    return jnp.concatenate([_dot(xb[:, t:t + LANES], ones) for t in range(0, x.shape[1], LANES)], axis=1)


def _sel_dot(sel, b, passes=3):
    parts = _split3(b)[:passes]
    out = _dot(sel, parts[0])
    for p in parts[1:]:
        out = out + _dot(sel, p)
    return out


def _softplus(x):
    return jnp.maximum(x, 0.0) + jnp.log(1.0 + jnp.exp(-jnp.abs(x)))


def _silu(x):
    return x * jax.nn.sigmoid(x)


def _gelu_x2(x):
    c = float(np.sqrt(2.0 / np.pi))
    return x + x * jnp.tanh(x * (c + (0.044715 * c) * (x * x)))


def _gelu(x):
    return 0.5 * x * (1.0 + jnp.tanh(float(np.sqrt(2.0 / np.pi)) * (x + 0.044715 * (x * x * x))))


def _rmsnorm(x, g):
    return x * lax.rsqrt(jnp.mean(x * x, axis=-1, keepdims=True) + EPS) * g


def _const_spec(shape):
    nd = len(shape)
    return pl.BlockSpec(shape, lambda *_: (0,) * nd)


def _params(n_grid):
    return pltpu.CompilerParams(dimension_semantics=("arbitrary",) * n_grid,
                                vmem_limit_bytes=VMEM_LIMIT_BYTES)


def _ada_kernel(c_ref, w_ref, b_ref, o_ref):
    c = c_ref[...]
    o_ref[0] = _dot(_silu(c), w_ref[0], precision=HIGHEST) + b_ref[0]


def _ada_call(c_all, w_ada, b_ada):
    nb = c_all.shape[0]
    nt = 6 * D_MODEL // 1024
    return pl.pallas_call(
        _ada_kernel,
        grid=(DEPTH, nt),
        in_specs=[
            _const_spec((nb, D_MODEL)),
            pl.BlockSpec((1, D_MODEL, 1024), lambda l, j: (l, 0, j)),
            pl.BlockSpec((1, 1, 1024), lambda l, j: (l, 0, j)),
        ],
        out_specs=pl.BlockSpec((1, nb, 1024), lambda l, j: (l, 0, j)),
        out_shape=jax.ShapeDtypeStruct((DEPTH, nb, 6 * D_MODEL), F32),
        compiler_params=_params(2),
        name="ada_mod",
    )(c_all, w_ada, b_ada.reshape(DEPTH, 1, 6 * D_MODEL))


def _inproj_kernel(x_ref, g_ref, sc_ref, sh_ref, w_ref, za_ref, zb_ref, zc_ref):
    h = (_rmsnorm(x_ref[...], g_ref[...]) * sc_ref[...] + sh_ref[...]).astype(BF16)
    for o_ref, lo, hi in ((za_ref, 0, ZA_W), (zb_ref, ZA_W, ZA_W + ZB_W), (zc_ref, ZA_W + ZB_W, N_IN_PAD)):
        o_ref[...] = _dot(h, w_ref[:, lo:hi])


def _mod_spec(mod, tiles_per_group):
    _, r, w = mod.shape
    return pl.BlockSpec((None, r, w), lambda i: (i // tiles_per_group, 0, 0))


def _inproj_call(x, g, sc, sh, wh, tiles_per_group):
    t = x.shape[0]
    tm = TOKEN_TILE
    row = lambda w: pl.BlockSpec((tm, w), lambda i: (i, 0))
    return pl.pallas_call(
        _inproj_kernel,
        grid=(t // tm,),
        in_specs=[row(D_MODEL), _const_spec((1, D_MODEL)), _mod_spec(sc, tiles_per_group),
                  _mod_spec(sh, tiles_per_group), _const_spec(wh.shape)],
        out_specs=[row(ZA_W), row(ZB_W), row(ZC_W)],
        out_shape=[jax.ShapeDtypeStruct((t, ZA_W), F32), jax.ShapeDtypeStruct((t, ZB_W), F32),
                   jax.ShapeDtypeStruct((t, ZC_W), F32)],
        compiler_params=_params(1),
        name="inproj",
    )(x, g, sc, sh, wh)


class _SeqCfg:
    def __init__(self, seq_rows, hist, real, chunk, n_par):
        self.seq_rows = seq_rows
        self.n_par = n_par
        self.hist = hist
        self.real = real
        self.chunk = chunk
        self.masked = hist > 0 or hist + real < seq_rows
        self.n_chunks = SUPER_BLOCK // chunk
        self.long_seq = seq_rows > SUPER_BLOCK
        self.sb_per_seq = max(seq_rows // SUPER_BLOCK, 1)
        self.seq_per_sb = max(SUPER_BLOCK // seq_rows, 1)
        assert (self.long_seq or chunk == seq_rows) and (self.long_seq or n_par == 1)


PROMPT_CFG = _SeqCfg(seq_rows=2048, hist=0, real=2048, chunk=16, n_par=2)
SAMPLE_CFG = _SeqCfg(seq_rows=SAMPLE_SEQ_ROWS, hist=SAMPLE_HIST, real=4, chunk=SAMPLE_SEQ_ROWS, n_par=1)


def _row_in_seq(cfg, n_rows):
    r = lax.broadcasted_iota(jnp.int32, (n_rows, 1), 0)
    return r % min(cfg.seq_rows, SUPER_BLOCK)


def _real_mask(cfg, n_rows):
    r = _row_in_seq(cfg, n_rows)
    return (r >= cfg.hist) & (r < cfg.hist + cfg.real)


def _causal_conv(x, xp_ref, w_ref):
    n = x.shape[0]
    xp_ref[8:8 + n, :] = x
    y = x * w_ref[CONV_W - 1:CONV_W, :]
    for k in range(CONV_W - 1):
        s = CONV_W - 1 - k
        y = y + xp_ref[8 - s:8 - s + n, :] * w_ref[k:k + 1, :]
    xp_ref[0:8, :] = x[n - 8:n, :]
    return y


def _expand_state(s_cat, tile_ref, bd_ref):
    return _dot_sel(s_cat, tile_ref[...], passes=2) * bd_ref[...]


def _compress_state(s_bd, tile_t_ref):
    return _dot_sel(s_bd, tile_t_ref[...], passes=2)


def _head_norm_gate(o, z, g, ones_ref):
    ms = _dot_sel(o * o, ones_ref[...], passes=2) * (1.0 / HEAD_DIM)
    return o * lax.rsqrt(ms + EPS) * g * _silu(z)


def _run_chunks(cfg, sb, step, st_ref, s0_ref, sout_ref, tile_ref, tile_t_ref, bd_ref):
    if cfg.long_seq:
        @pl.when(sb % cfg.sb_per_seq == 0)
        def _():
            for p in range(cfg.n_par):
                st_ref[p] = _expand_state(s0_ref[p], tile_ref, bd_ref)

        def body(k, carry):
            for p in range(cfg.n_par):
                st_ref[p] = step(k, p, st_ref[p])
            return carry

        lax.fori_loop(0, cfg.n_chunks, body, 0)

        @pl.when(sb % cfg.sb_per_seq == cfg.sb_per_seq - 1)
        def _():
            for p in range(cfg.n_par):
                sout_ref[p] = _compress_state(st_ref[p], tile_t_ref)
    else:
        def body(k, carry):
            st_new = step(k, 0, _expand_state(s0_ref[k], tile_ref, bd_ref))
            sout_ref[k] = _compress_state(st_new, tile_t_ref)
            return carry

        lax.fori_loop(0, cfg.n_chunks, body, 0, unroll=2)


def _hgrn_kernel(z_ref, lbp_ref, g_ref, s0_ref, ltri_ref, ones_ref, tile_ref, tile_t_ref, bd_ref,
                 o_ref, sout_ref,
                 st_ref, q_s, k_s, g_s, qe_s, kt_s, egl_s, o_s, *, cfg, layer):
    sb = pl.program_id(0)
    c = cfg.chunk
    n = SUPER_BLOCK
    nch = cfg.n_chunks
    w = A_W
    lbp = lbp_ref[...]
    e = jnp.exp(lbp - jnp.max(lbp, axis=0, keepdims=True))
    lbs = e / jnp.sum(e, axis=0, keepdims=True)
    lb = jnp.sum(lbs[0:layer + 1, :], axis=0, keepdims=True) - lbs[0:1, :]

    for p in range(cfg.n_par):
        r = slice(p * n, (p + 1) * n)
        aq = z_ref[p, :, 0:w]
        af = z_ref[p, :, w:2 * w]
        f = lb + (1.0 - lb) * jax.nn.sigmoid(af)
        lf = jnp.log(f)
        kk = 1.0 - f
        if cfg.masked:
            real = _real_mask(cfg, n)
            lf = jnp.where(real, lf, 0.0)
            kk = jnp.where(real, kk, 0.0)
        q = _silu(aq) * (HEAD_DIM ** -0.5)
        gcum = _sel_dot(ltri_ref[...], lf)
        g3 = gcum.reshape(nch, c, w)
        gl3 = g3[:, c - 1:c, :]
        q_s[r, :] = q
        k_s[r, :] = kk
        g_s[r, :] = gcum
        qe_s[r, :] = q * jnp.exp(gcum)
        kt_s[r, :] = (kk.reshape(nch, c, w) * jnp.exp(gl3 - g3)).reshape(n, w)
        egl_s[p * nch:(p + 1) * nch, :] = jnp.exp(gl3).reshape(nch, w)

    jio = lax.broadcasted_iota(jnp.int32, (c, 1), 0)
    bd = bd_ref[...]

    def chunk(k, p, st):
        rows = pl.ds(pl.multiple_of(p * n + k * c, c), c)
        q_c = q_s[rows, :]
        k_c = k_s[rows, :]
        g_c = g_s[rows, :]
        v_c = z_ref[p, pl.ds(pl.multiple_of(k * c, c), c), 2 * w:3 * w]
        o_inter = _dot(qe_s[rows, :].astype(BF16), st.astype(BF16), _NT)
        lo = min(c, 8)
        d_rows = []
        for i in range(c):
            nj = lo if i < lo else c
            dec = jnp.exp(jnp.minimum(g_c[i:i + 1, :] - g_c[0:nj, :], 0.0))
            d_rows.append(jnp.where(jio[0:nj] <= i, k_c[0:nj, :] * dec * q_c[i:i + 1, :], 0.0))
        a_b = _head_sums(jnp.concatenate(d_rows, axis=0), ones_ref)
        o_diag = jnp.sum(a_b[0:lo * lo, :].reshape(lo, lo, w) * v_c[None, 0:lo, :], axis=1)
        if c > lo:
            o_hi = jnp.sum(a_b[lo * lo:, :].reshape(c - lo, c, w) * v_c[None, :, :], axis=1)
            o_diag = jnp.concatenate([o_diag, o_hi], axis=0)
        o_s[rows, :] = o_inter + o_diag
        upd = _dot_b(v_c, kt_s[rows, :], _TN)
        return st * egl_s[pl.ds(p * nch + k, 1), :] + bd * upd

    _run_chunks(cfg, sb, chunk, st_ref, s0_ref, sout_ref, tile_ref, tile_t_ref, bd_ref)
    for p in range(cfg.n_par):
        o_ref[p] = _head_norm_gate(o_s[p * n:(p + 1) * n, :], z_ref[p, :, 3 * w:4 * w], g_ref[...], ones_ref)


def _mixer_consts(heads, cfg):
    w = heads * HEAD_DIM
    lane_head = np.arange(w) // HEAD_DIM
    ones = (lane_head[:, None] == lane_head[None, :]).astype(np.float32)
    tile = (np.arange(HEAD_DIM)[:, None] == (np.arange(w) % HEAD_DIM)[None, :]).astype(np.float32)
    r = np.arange(SUPER_BLOCK)
    ltri = ((r[:, None] // cfg.chunk == r[None, :] // cfg.chunk) & (r[None, :] <= r[:, None])).astype(np.float32)
    return dict(ltri=jnp.asarray(ltri, BF16), ones=jnp.asarray(ones, BF16), tile=jnp.asarray(tile, BF16),
                tile_t=jnp.asarray(tile.T, BF16), bd=jnp.asarray(ones, F32))


def _state_specs(cfg, w):
    blk = (cfg.n_par if cfg.long_seq else cfg.seq_per_sb, w, HEAD_DIM)
    if cfg.long_seq:
        imap = lambda i: (i // cfg.sb_per_seq, 0, 0)
    else:
        imap = lambda i: (i, 0, 0)
    return pl.BlockSpec(blk, imap)


def _seq_view(x, cfg):
    return x.reshape(-1, cfg.seq_rows, x.shape[-1]) if cfg.long_seq else x[None]


def _seq_block_spec(cfg, width):
    if cfg.long_seq:
        return pl.BlockSpec((cfg.n_par, SUPER_BLOCK, width), lambda i: (i // cfg.sb_per_seq, i % cfg.sb_per_seq, 0))
    return pl.BlockSpec((1, SUPER_BLOCK, width), lambda i: (0, i, 0))


def _hgrn_call(z_a, lb_param, g_exp, s0_t, cfg, layer):
    rows = z_a.shape[0]
    w = A_W
    cst = _mixer_consts(A_HEADS, cfg)
    n_seq = s0_t.shape[0]
    par_rows = cfg.n_par * SUPER_BLOCK
    vm = lambda shape: pltpu.VMEM(shape, F32)
    kern = functools.partial(_hgrn_kernel, cfg=cfg, layer=layer)
    z3 = _seq_view(z_a, cfg)
    oa, s_new = pl.pallas_call(
        kern,
        grid=(rows // par_rows,),
        in_specs=[_seq_block_spec(cfg, ZA_W), _const_spec((DEPTH, w)),
                  _const_spec((1, w)), _state_specs(cfg, w), _const_spec(cst["ltri"].shape),
                  _const_spec(cst["ones"].shape), _const_spec(cst["tile"].shape),
                  _const_spec(cst["tile_t"].shape), _const_spec(cst["bd"].shape)],
        out_specs=[_seq_block_spec(cfg, w), _state_specs(cfg, w)],
        out_shape=[jax.ShapeDtypeStruct(z3.shape[:2] + (w,), F32), jax.ShapeDtypeStruct((n_seq, w, HEAD_DIM), F32)],
        scratch_shapes=[vm((cfg.n_par, w, w))] + [vm((par_rows, w))] * 5
        + [vm((cfg.n_par * cfg.n_chunks, w)), vm((par_rows, w))],
        compiler_params=_params(1),
        name="hgrn2",
    )(z3, lb_param, g_exp, s0_t, cst["ltri"], cst["ones"], cst["tile"], cst["tile_t"], cst["bd"])
    return oa.reshape(rows, w), s_new


def _lru_kernel(*refs, cfg):
    if cfg.long_seq:
        (z_ref, cw_ref, cb_ref, wg_ref, ba_ref, bx_ref, lp_ref,
         ob_ref, hs_ref, prev_ref, hc_ref) = refs
        inj_ref = None
    else:
        (z_ref, inj_ref, cw_ref, cb_ref, wg_ref, ba_ref, bx_ref, lp_ref,
         ob_ref, hs_ref, prev_ref, hc_ref) = refs
    sb = pl.program_id(0)
    n = SUPER_BLOCK
    w = B_W

    @pl.when(sb % cfg.sb_per_seq == 0)
    def _():
        prev_ref[0:8, :] = jnp.zeros((8, prev_ref.shape[1]), F32)
        hc_ref[...] = jnp.zeros_like(hc_ref)

    xc = _causal_conv(z_ref[:, 0:w], prev_ref, cw_ref) + cb_ref[...]
    gates = _dot_b(xc, wg_ref[...])
    r = jax.nn.sigmoid(gates[:, 0:w] + ba_ref[...])
    ig = jax.nn.sigmoid(gates[:, w:2 * w] + bx_ref[...])
    log_a = -LRU_C * r * _softplus(-lp_ref[...])
    a = jnp.exp(log_a)
    b = jnp.sqrt(1.0 - jnp.exp(2.0 * log_a)) * ig * xc
    if cfg.masked:
        real = _real_mask(cfg, n)
        a = jnp.where(real, a, 1.0)
        b = jnp.where(real, b, 0.0)
    if inj_ref is not None:
        b = b + inj_ref[...]
    ris = _row_in_seq(cfg, n)
    d = 1
    while d < min(n, cfg.seq_rows):
        has = ris >= d
        a_sh = jnp.where(has, pltpu.roll(a, d, 0), 1.0)
        b_sh = jnp.where(has, pltpu.roll(b, d, 0), 0.0)
        b = b + a * b_sh
        a = a * a_sh
        d *= 2
    hs = b + a * hc_ref[...]
    if cfg.long_seq:
        hc_ref[...] = hs[n - 1:n, :]
    hs_ref[...] = hs
    ob_ref[...] = hs * _gelu(z_ref[:, w:2 * w])


def _lru_call(z_b, inj, cw, cb, wg, ba, bx, lp, cfg):
    rows = z_b.shape[0]
    w = B_W
    row = lambda width: pl.BlockSpec((SUPER_BLOCK, width), lambda i: (i, 0))
    ins = [z_b] + ([] if cfg.long_seq else [inj]) + [cw, cb, wg, ba, bx, lp]
    specs = [row(ZB_W)] + ([] if cfg.long_seq else [row(w)]) + [_const_spec(a.shape) for a in ins[-6:]]
    return pl.pallas_call(
        functools.partial(_lru_kernel, cfg=cfg),
        grid=(rows // SUPER_BLOCK,),
        in_specs=specs,
        out_specs=[row(w), row(w)],
        out_shape=[jax.ShapeDtypeStruct((rows, w), F32)] * 2,
        scratch_shapes=[pltpu.VMEM((8 + SUPER_BLOCK, w), F32), pltpu.VMEM((1, w), F32)],
        compiler_params=_params(1),
        name="rglru",
    )(*ins)


def _dn_kernel(z_ref, cw_ref, alog_ref, dtb_ref, g_ref, s0_ref, ltri_ref, ones_ref, tile_ref, tile_t_ref,
               bd_ref, expb_ref, expa_ref,
               o_ref, sout_ref,
               st_ref, prev_ref, q_s, k_s, v_s, g_s, eg_s, beta_s, kt_s, egl_s, o_s, *, cfg):
    sb = pl.program_id(0)
    c = cfg.chunk
    n = SUPER_BLOCK
    nch = cfg.n_chunks
    w = C_W

    @pl.when(sb % cfg.sb_per_seq == 0)
    def _():
        for p in range(cfg.n_par):
            prev_ref[p, 0:8, :] = jnp.zeros((8, prev_ref.shape[2]), F32)

    ones = ones_ref[...]
    for p in range(cfg.n_par):
        r = slice(p * n, (p + 1) * n)
        qkv = _silu(_causal_conv(z_ref[p, :, 0:3 * w], prev_ref.at[p], cw_ref))
        q = qkv[:, 0:w]
        kx = qkv[:, w:2 * w]
        q = q * lax.rsqrt(_dot_sel(q * q, ones, passes=2) + EPS) * (HEAD_DIM ** -0.5)
        kx = kx * lax.rsqrt(_dot_sel(kx * kx, ones, passes=2) + EPS)
        pc = z_ref[p, :, 4 * w:4 * w + LANES]
        beta = _dot_sel(jax.nn.sigmoid(pc), expb_ref[...], passes=2)
        gdec = _dot_sel(-jnp.exp(alog_ref[...]) * _softplus(pc + dtb_ref[...]), expa_ref[...], passes=2)
        if cfg.masked:
            real = _real_mask(cfg, n)
            beta = jnp.where(real, beta, 0.0)
            gdec = jnp.where(real, gdec, 0.0)
        gcum = _sel_dot(ltri_ref[...], gdec)
        g3 = gcum.reshape(nch, c, w)
        gl3 = g3[:, c - 1:c, :]
        q_s[r, :] = q
        k_s[r, :] = kx
        v_s[r, :] = qkv[:, 2 * w:3 * w]
        g_s[r, :] = gcum
        eg_s[r, :] = jnp.exp(gcum)
        beta_s[r, :] = beta
        kt_s[r, :] = (kx.reshape(nch, c, w) * jnp.exp(gl3 - g3)).reshape(n, w)
        egl_s[p * nch:(p + 1) * nch, :] = jnp.exp(gl3).reshape(nch, w)

    iio = lax.broadcasted_iota(jnp.int32, (c, 1), 0)
    bd = bd_ref[...]

    def chunk(k, p, st):
        rows = pl.ds(pl.multiple_of(p * n + k * c, c), c)
        q_c = q_s[rows, :]
        k_c = k_s[rows, :]
        g_c = g_s[rows, :]
        eg_c = eg_s[rows, :]
        beta_c = beta_s[rows, :]
        qk_s = _dot_b(jnp.concatenate([q_c, k_c], axis=0), st)
        q_st = qk_s[0:c, :]
        k_st = qk_s[c:2 * c, :]
        lo = min(c, 8)
        tail = lambda x: x[lo:c, :]
        d_rows = ([k_c * k_c[j:j + 1, :] for j in range(lo)] + [q_c * k_c[j:j + 1, :] for j in range(lo)]
                  + [tail(k_c) * k_c[j:j + 1, :] for j in range(lo, c)]
                  + [tail(q_c) * k_c[j:j + 1, :] for j in range(lo, c)])
        dots = _head_sums(jnp.concatenate(d_rows, axis=0), ones_ref)
        e = beta_c * (v_s[rows, :] - eg_c * k_st)
        o = eg_c * q_st
        for j in range(lo):
            dec = jnp.exp(jnp.minimum(g_c - g_c[j:j + 1, :], 0.0))
            m_col = jnp.where(iio > j, beta_c * dots[j * c:(j + 1) * c, :] * dec, 0.0)
            e_j = e[j:j + 1, :]
            e = e - m_col * e_j
            qk_col = jnp.where(iio >= j, dots[(lo + j) * c:(lo + j + 1) * c, :] * dec, 0.0)
            o = o + qk_col * e_j
        if c > lo:
            nt = c - lo
            e_t, o_t, g_t, beta_t = tail(e), tail(o), tail(g_c), tail(beta_c)
            base = 2 * lo * c
            for j in range(lo, c):
                dec = jnp.exp(jnp.minimum(g_t - g_c[j:j + 1, :], 0.0))
                kk = dots[base + (j - lo) * nt:base + (j - lo + 1) * nt, :]
                qk = dots[base + (nt + j - lo) * nt:base + (nt + j - lo + 1) * nt, :]
                e_j = e_t[j - lo:j - lo + 1, :]
                e_t = e_t - jnp.where(iio[0:nt] > j - lo, beta_t * kk * dec, 0.0) * e_j
                o_t = o_t + jnp.where(iio[0:nt] >= j - lo, qk * dec, 0.0) * e_j
            e = jnp.concatenate([e[0:lo, :], e_t], axis=0)
            o = jnp.concatenate([o[0:lo, :], o_t], axis=0)
        o_s[rows, :] = o
        upd = _dot_b(kt_s[rows, :], e, _TN)
        return st * egl_s[pl.ds(p * nch + k, 1), :] + bd * upd

    _run_chunks(cfg, sb, chunk, st_ref, s0_ref, sout_ref, tile_ref, tile_t_ref, bd_ref)
    for p in range(cfg.n_par):
        o_ref[p] = _head_norm_gate(o_s[p * n:(p + 1) * n, :], z_ref[p, :, 3 * w:4 * w], g_ref[...], ones_ref)


def _dn_call(z_c, cw, alog_exp, dtb_exp, g_exp, s0, cfg):
    rows = z_c.shape[0]
    w = C_W
    cst = _mixer_consts(C_HEADS, cfg)
    lane_head = np.arange(w) // HEAD_DIM
    expb = (np.arange(128)[:, None] == lane_head[None, :]).astype(np.float32)
    expa = (np.arange(128)[:, None] == (lane_head[None, :] + C_HEADS)).astype(np.float32)
    expb = jnp.asarray(expb, BF16)
    expa = jnp.asarray(expa, BF16)
    n_seq = s0.shape[0]
    par_rows = cfg.n_par * SUPER_BLOCK
    vm = lambda shape: pltpu.VMEM(shape, F32)
    z3 = _seq_view(z_c, cfg)
    oc, s_new = pl.pallas_call(
        functools.partial(_dn_kernel, cfg=cfg),
        grid=(rows // par_rows,),
        in_specs=[_seq_block_spec(cfg, ZC_W), _const_spec(cw.shape),
                  _const_spec((1, LANES)), _const_spec((1, LANES)), _const_spec((1, w)), _state_specs(cfg, w),
                  _const_spec(cst["ltri"].shape), _const_spec(cst["ones"].shape),
                  _const_spec(cst["tile"].shape), _const_spec(cst["tile_t"].shape),
                  _const_spec(cst["bd"].shape), _const_spec(expb.shape), _const_spec(expa.shape)],
        out_specs=[_seq_block_spec(cfg, w), _state_specs(cfg, w)],
        out_shape=[jax.ShapeDtypeStruct(z3.shape[:2] + (w,), F32), jax.ShapeDtypeStruct((n_seq, w, HEAD_DIM), F32)],
        scratch_shapes=[vm((cfg.n_par, w, w)), vm((cfg.n_par, 8 + SUPER_BLOCK, 3 * w))] + [vm((par_rows, w))] * 7
        + [vm((cfg.n_par * cfg.n_chunks, w)), vm((par_rows, w))],
        compiler_params=_params(1),
        name="deltanet",
    )(z3, cw, alog_exp, dtb_exp, g_exp, s0, cst["ltri"], cst["ones"], cst["tile"], cst["tile_t"],
      cst["bd"], expb, expa)
    return oc.reshape(rows, w), s_new


def _outproj_kernel(x_ref, oa_ref, ob_ref, oc_ref, gt_ref, g_ref, sc_ref, sh_ref,
                    wo_ref, wqh_ref, wql_ref, x1_ref, h2_ref, q_ref):
    mo = _dot_b(oa_ref[...], wo_ref[0:A_W, :])
    mo = mo + _dot_b(ob_ref[...], wo_ref[A_W:A_W + B_W, :])
    mo = mo + _dot_b(oc_ref[...], wo_ref[A_W + B_W:, :])
    x1 = x_ref[...] + gt_ref[...] * mo
    x1_ref[...] = x1
    h2 = _rmsnorm(x1, g_ref[...]) * sc_ref[...] + sh_ref[...]
    h2_ref[...] = h2.astype(BF16)
    q_ref[...] = _dot_w(h2, wqh_ref[...], wql_ref[...])


def _outproj_call(x, oa, ob, oc, gt, g, sc, sh, wo, wqh, wql, tiles_per_group):
    t = x.shape[0]
    tm = TOKEN_TILE
    row = lambda w: pl.BlockSpec((tm, w), lambda i: (i, 0))
    return pl.pallas_call(
        _outproj_kernel,
        grid=(t // tm,),
        in_specs=[row(D_MODEL), row(A_W), row(B_W), row(C_W), _mod_spec(gt, tiles_per_group),
                  _const_spec((1, D_MODEL)), _mod_spec(sc, tiles_per_group), _mod_spec(sh, tiles_per_group),
                  _const_spec(wo.shape), _const_spec(wqh.shape), _const_spec(wql.shape)],
        out_specs=[row(D_MODEL), row(D_MODEL), row(D_MODEL)],
        out_shape=[jax.ShapeDtypeStruct((t, D_MODEL), F32), jax.ShapeDtypeStruct((t, D_MODEL), BF16),
                   jax.ShapeDtypeStruct((t, D_MODEL), F32)],
        compiler_params=_params(1),
        name="outproj",
    )(x, oa, ob, oc, gt, g, sc, sh, wo, wqh, wql)


def _top16(s):
    vals = []
    for r in range(PEER_TOPK):
        m = jnp.max(s, axis=0, keepdims=True)
        s = jnp.where(s == m, -(32.0 + r) * 2.0 ** 95, s)
        vals.append(m)
    rank = jnp.where(s <= -(2.0 ** 99), s * -(2.0 ** -95) - 32.0, float(PEER_TOPK))
    return jnp.concatenate(vals, axis=0), rank


def _peer_kernel(h2_ref, q_ref, x1_ref, gt_ref, fg_ref, k1_ref, k2_ref, u_ref, vt_ref, o_ref,
                 cnt_s, e1_s, rank2_s, e2_s, acc_s, s1_s, s2_s, coef_s, *, final, n_steps):
    g = pl.program_id(0)
    tt = PEER_TOKEN_TILE
    nk = PEER_NKEYS
    nj = PEER_N // PEER_EXPERT_BLOCK
    a_per_step = PEER_EXPERT_BLOCK // nk
    j = jnp.minimum(g, n_steps - 2) % nj
    gb = jnp.maximum(g - 1, 0)

    @pl.when(g == 0)
    def _():
        coef_s[...] = jnp.zeros_like(coef_s)

    @pl.when(gb % nj == 0)
    def _():
        acc_s[...] = jnp.zeros_like(acc_s)

    @pl.when(jnp.logical_and(g % nj == 0, g < n_steps - 1))
    def _():

        def head(h, carry):
            qh = q_ref[:, pl.ds(pl.multiple_of(h * nk, nk), nk)]
            s1_s[...] = _dot_x3(k1_ref[...], qh, _NT)
            s2_s[...] = _dot_x3(k2_ref[...], qh, _NT)

            def lane_tile(c, carry2):
                cols = pl.ds(pl.multiple_of(c * PEER_ROUTE_TILE, PEER_ROUTE_TILE), PEER_ROUTE_TILE)
                s1 = s1_s[:, cols]
                s2 = s2_s[:, cols]
                v1, rank1 = _top16(s1)
                v2, rank2 = _top16(s2)
                c3 = v1[:, None, :] + v2[None, :, :]
                jrow = lax.broadcasted_iota(jnp.int32, (8, 1), 0)
                parts = [v1[0:1, :] + v2, v1[1:2, :] + v2[0:8, :]]
                for i in range(2, 8):
                    parts.append(jnp.where(jrow < PEER_TOPK // (i + 1), v1[i:i + 1, :] + v2[0:8, :], -jnp.inf))
                parts.append(v1[8:16, :] + v2[0:1, :])
                cand = jnp.concatenate(parts, axis=0)
                m = None
                for _ in range(PEER_TOPK):
                    m = jnp.max(cand, axis=0, keepdims=True)
                    cand = jnp.where(cand == m, -jnp.inf, cand)
                tau = m
                sel = c3 >= tau[None, :, :]
                m0 = v1[0:1, :] + v2[0:1, :]
                zsum = jnp.sum(jnp.where(sel, jnp.exp(c3 - m0[None, :, :]), 0.0).reshape(-1, PEER_ROUTE_TILE),
                               axis=0, keepdims=True)
                n_i = jnp.sum(jnp.where(sel, 1.0, 0.0), axis=1)
                rank1 = rank1.astype(BF16)
                cnt = jnp.zeros((nk, PEER_ROUTE_TILE), BF16)
                for i in range(PEER_TOPK):
                    cnt = cnt + jnp.where(rank1 == i, n_i[i:i + 1, :].astype(BF16), jnp.zeros((), BF16))
                cnt_s[h, :, cols] = cnt.astype(F32)
                e1_s[h, :, cols] = jnp.exp(s1 - v1[0:1, :])
                rank2_s[h, :, cols] = rank2.astype(BF16)
                e2_s[h, :, cols] = (jnp.exp(s2 - v2[0:1, :]) * (0.5 / zsum)).astype(BF16)
                return carry2

            lax.fori_loop(0, tt // PEER_ROUTE_TILE, lane_tile, 0)
            return carry

        lax.fori_loop(0, PEER_HEADS, head, 0)

    h2 = h2_ref[...]
    n_sub = PEER_EXPERT_BLOCK // PEER_SUB_BLOCK
    a_per_sub = PEER_SUB_BLOCK // nk
    sub = lambda s: slice(s * PEER_SUB_BLOCK, (s + 1) * PEER_SUB_BLOCK)
    cur = coef_s.at[g % 2]
    prev = coef_s.at[(g + 1) % 2]
    hidden = lambda s: _dot(u_ref[sub(s), :], h2, _NT)
    hids = [hidden(0), hidden(1)]
    half = PEER_EXPERT_BLOCK // 2
    pack = 16
    for s in range(n_sub):
        if s % 2 == 0:
            k0 = (s // 2) * half
            acc_s[...] += _dot(vt_ref[:, k0:k0 + half], prev[k0:k0 + half, :])
            hids.extend(hidden(s2) for s2 in (s + 2, s + 3) if s2 < n_sub)
        for al in range(a_per_sub):
            a = j * a_per_step + s * a_per_sub + al
            wsum = jnp.zeros((nk // pack, pack, tt), BF16)
            for h in range(PEER_HEADS):
                cnt_row = jnp.broadcast_to(cnt_s[h, pl.ds(a, 1), :], (pack, tt)).astype(BF16)
                e1_row = jnp.broadcast_to(e1_s[h, pl.ds(a, 1), :], (pack, tt)).astype(BF16)
                rank2 = rank2_s[h].reshape(nk // pack, pack, tt)
                e2 = e2_s[h].reshape(nk // pack, pack, tt)
                wsum = jnp.where(rank2 < cnt_row[None], wsum + e2 * e1_row[None], wsum)
            r0 = s * PEER_SUB_BLOCK + al * nk
            cur[r0:r0 + nk, :] = _gelu_x2(hids[s][al * nk:(al + 1) * nk, :].astype(BF16)) * wsum.reshape(nk, tt)

    @pl.when(jnp.logical_and(g > 0, gb % nj == nj - 1))
    def _():
        x2 = x1_ref[...] + gt_ref[...] * acc_s[...].T
        if final:
            x2 = _rmsnorm(x2, fg_ref[...])
        o_ref[...] = x2


def _peer_call(h2b, q, x1, gt, fg, k1p, k2p, u_b, v_b, tiles_per_group, final):
    t = h2b.shape[0]
    tt = PEER_TOKEN_TILE
    nb = PEER_EXPERT_BLOCK
    nj = PEER_N // nb
    n_steps = (t // tt) * nj + 1
    front = lambda g: jnp.minimum(g, n_steps - 2)
    back = lambda g: jnp.maximum(g - 1, 0)
    row_f = lambda w: pl.BlockSpec((tt, w), lambda g: (front(g) // nj, 0))
    row_b = lambda w: pl.BlockSpec((tt, w), lambda g: (back(g) // nj, 0))
    _, r, w = gt.shape
    tab = lambda dt: pltpu.VMEM((PEER_HEADS, PEER_NKEYS, tt), dt)
    return pl.pallas_call(
        functools.partial(_peer_kernel, final=final, n_steps=n_steps),
        grid=(n_steps,),
        in_specs=[row_f(D_MODEL), row_f(D_MODEL), row_b(D_MODEL),
                  pl.BlockSpec((None, r, w), lambda g: (back(g) // nj // tiles_per_group, 0, 0)),
                  _const_spec((1, D_MODEL)),
                  _const_spec((PEER_NKEYS, PEER_NKEYS)),
                  _const_spec((PEER_NKEYS, PEER_NKEYS)),
                  pl.BlockSpec((nb, D_MODEL), lambda g: (front(g) % nj, 0)),
                  pl.BlockSpec((D_MODEL, nb), lambda g: (0, back(g) % nj))],
        out_specs=row_b(D_MODEL),
        out_shape=jax.ShapeDtypeStruct((t, D_MODEL), F32),
        scratch_shapes=[tab(F32), tab(F32), tab(BF16), tab(BF16), pltpu.VMEM((D_MODEL, tt), F32),
                        pltpu.VMEM((PEER_NKEYS, tt), F32), pltpu.VMEM((PEER_NKEYS, tt), F32),
                        pltpu.VMEM((2, nb, tt), BF16)],
        compiler_params=_params(1),
        name="peer",
    )(h2b, q, x1, gt, fg, k1p, k2p, u_b, v_b)


def _block_diag(wblk):
    n, d, e = wblk.shape
    eye = jnp.eye(n, dtype=wblk.dtype)
    return (eye[:, None, :, None] * wblk[:, :, None, :]).reshape(n * d, n * e)


def _hilo(w):
    hi = w.astype(BF16)
    return hi, (w - hi.astype(F32)).astype(BF16)


def _pad_sample(z, hist=None):
    bsz = z.shape[0] // 4
    w = z.shape[1]
    z3 = z.reshape(bsz, 4, w)
    h3 = jnp.zeros((bsz, SAMPLE_HIST, w), F32)
    if hist is not None:
        h3 = h3.at[:, :, :hist.shape[-1]].set(hist)
    return jnp.concatenate([h3, z3, jnp.zeros((bsz, 1, w), F32)], axis=1).reshape(bsz * SAMPLE_SEQ_ROWS, w)


def _unpad_sample(o):
    bsz = o.shape[0] // SAMPLE_SEQ_ROWS
    return o.reshape(bsz, SAMPLE_SEQ_ROWS, -1)[:, SAMPLE_HIST:SAMPLE_HIST + 4].reshape(bsz * 4, -1)


def _layer_weights(l, w_in, w_out, peer_wq, peer_k1, peer_k2, peer_u, peer_v, lru_wa, lru_wx):
    w_in_p = jnp.pad(w_in[l], ((0, 0), (0, N_IN_PAD - N_IN)))
    wih = w_in_p.astype(BF16)
    wo = w_out[l].astype(BF16)
    wqh, wql = _hilo(peer_wq[l])
    wg = jnp.concatenate([_block_diag(lru_wa[l]), _block_diag(lru_wx[l])], axis=1).astype(BF16)
    half = PEER_NKEYS // 2
    k1p = jnp.pad(peer_k1[l], ((0, 0), (0, half)))
    k2p = jnp.pad(peer_k2[l], ((0, 0), (half, 0)))
    u_b = peer_u[l].astype(BF16)
    v_b = peer_v[l].T.astype(BF16)
    return dict(wih=wih, wo=wo, wqh=wqh, wql=wql, wg=wg,
                k1p=k1p, k2p=k2p, u_b=u_b, v_b=v_b)


def _trunk(x, mods, states, is_prompt, p, lw, l, final_g):
    t = x.shape[0]
    if is_prompt:
        sh1, sc1, gt1, sh2, sc2, gt2 = mods
    else:
        sh1, sc1, gt1, sh2, sc2 = (m.reshape(-1, TOKEN_TILE, D_MODEL) for m in mods[:5])
        gt2 = mods[5].reshape(-1, PEER_TOKEN_TILE, D_MODEL)
    if is_prompt:
        cfg = PROMPT_CFG
        tiles_tok = 2048 // TOKEN_TILE
        tiles_peer = 2048 // PEER_TOKEN_TILE
        n_seq = t // 2048
    else:
        cfg = SAMPLE_CFG
        tiles_tok = 1
        tiles_peer = 1
        n_seq = t // 4
    row1 = lambda v: v.reshape(1, -1)
    z_a, z_b, z_c = _inproj_call(x, row1(p['norm1_g']), sc1, sh1, lw['wih'], tiles_tok)
    if is_prompt:
        s_a = jnp.zeros((n_seq, A_W, HEAD_DIM), F32)
        s_c = jnp.zeros((n_seq, C_W, HEAD_DIM), F32)
        za_m, zb_m, zc_m = z_a, z_b, z_c
        inj = None
    else:
        st_hgrn, st_lru_h, st_lru_conv, st_dn, st_dn_conv = states
        s_a = jnp.swapaxes(st_hgrn, -1, -2).reshape(n_seq, A_W, HEAD_DIM)
        s_c = st_dn.reshape(n_seq, C_W, HEAD_DIM)
        za_m = _pad_sample(z_a)
        zb_m = _pad_sample(z_b, st_lru_conv)
        zc_m = _pad_sample(z_c, st_dn_conv)
        inj = jnp.zeros((n_seq, SAMPLE_SEQ_ROWS, B_W), F32).at[:, 0].set(st_lru_h).reshape(-1, B_W)
    rep = lambda v, h: row1(jnp.tile(v, h))
    oa, s_a_new = _hgrn_call(za_m, p['lb_param'], rep(p['a_norm_g'], A_HEADS), s_a, cfg, l)
    ob, hs = _lru_call(zb_m, inj, p['lru_conv_w'], row1(p['lru_conv_b']), lw['wg'],
                       row1(p['lru_ba']), row1(p['lru_bx']), row1(p['lru_L']), cfg)
    decay_cols = lambda v: jnp.zeros((1, LANES), F32).at[0, C_HEADS:2 * C_HEADS].set(v)
    oc, s_c_new = _dn_call(zc_m, p['dn_conv_w'], decay_cols(p['dn_A_log']), decay_cols(p['dn_dt_bias']),
                           rep(p['dn_norm_g'], C_HEADS), s_c, cfg)
    if is_prompt:
        seq = 2048
        h_t = hs.reshape(n_seq, seq, B_W)[:, -1]
        buf_b = z_b.reshape(n_seq, seq, ZB_W)[:, -SAMPLE_HIST:, :B_W]
        buf_c = z_c.reshape(n_seq, seq, ZC_W)[:, -SAMPLE_HIST:, :3 * C_W]
    else:
        oa, ob, oc = _unpad_sample(oa), _unpad_sample(ob), _unpad_sample(oc)
        h_t = hs.reshape(n_seq, SAMPLE_SEQ_ROWS, B_W)[:, -1]
        buf_b = z_b.reshape(n_seq, 4, ZB_W)[:, 1:, :B_W]
        buf_c = z_c.reshape(n_seq, 4, ZC_W)[:, 1:, :3 * C_W]
    new_states = (jnp.swapaxes(s_a_new.reshape(n_seq, A_HEADS, HEAD_DIM, HEAD_DIM), -1, -2), h_t, buf_b,
                  s_c_new.reshape(n_seq, C_HEADS, HEAD_DIM, HEAD_DIM), buf_c)
    x1, h2b, q = _outproj_call(x, oa, ob, oc, gt1, row1(p['norm2_g']), sc2, sh2,
                               lw['wo'], lw['wqh'], lw['wql'], tiles_tok)
    x2 = _peer_call(h2b, q, x1, gt2, row1(final_g), lw['k1p'], lw['k2p'], lw['u_b'], lw['v_b'],
                    tiles_peer, final=(l == DEPTH - 1))
    return x2, new_states


def kernel(x_prompt, x_sample, state_hgrn, state_lru_h, state_lru_conv, state_dn, state_dn_conv,
           c_prompt, c_sample, w_ada, b_ada, norm1_g, norm2_g, w_in, lb_param, a_norm_g,
           lru_conv_w, lru_conv_b, lru_wa, lru_ba, lru_wx, lru_bx, lru_L,
           dn_conv_w, dn_A_log, dn_dt_bias, dn_norm_g, w_out,
           peer_wq, peer_k1, peer_k2, peer_u, peer_v, final_norm_g):
    n_p, seq, _ = x_prompt.shape
    n_s, dec_seq, _ = x_sample.shape
    mod = _ada_call(jnp.concatenate([c_prompt, c_sample], axis=0), w_ada, b_ada)
    xp = x_prompt.reshape(n_p * seq, D_MODEL)
    xs = x_sample.reshape(n_s * dec_seq, D_MODEL)
    sample_states = (state_hgrn, state_lru_h, state_lru_conv, state_dn, state_dn_conv)
    p_new, s_new = [], []
    for l in range(DEPTH):
        p = dict(norm1_g=norm1_g[l], norm2_g=norm2_g[l], lb_param=lb_param, a_norm_g=a_norm_g[l],
                 lru_conv_w=lru_conv_w[l], lru_conv_b=lru_conv_b[l], lru_ba=lru_ba[l], lru_bx=lru_bx[l],
                 lru_L=lru_L[l], dn_conv_w=dn_conv_w[l], dn_A_log=dn_A_log[l], dn_dt_bias=dn_dt_bias[l],
                 dn_norm_g=dn_norm_g[l])
        lw = _layer_weights(l, w_in, w_out, peer_wq, peer_k1, peer_k2, peer_u, peer_v, lru_wa, lru_wx)
        parts = jnp.split(mod[l], 6, axis=-1)
        parts = [m + 1.0 if i in (1, 4) else m for i, m in enumerate(parts)]
        mods_p = [m[:n_p].reshape(n_p, 1, D_MODEL) for m in parts]
        mods_s = [jnp.repeat(m[n_p:], dec_seq, axis=0) for m in parts]
        xp, st_p = _trunk(xp, mods_p, None, True, p, lw, l, final_norm_g)
        xs, st_s = _trunk(xs, mods_s, tuple(s[l] for s in sample_states), False, p, lw, l, final_norm_g)
        p_new.append(st_p)
        s_new.append(st_s)
    stack = lambda sts: [jnp.stack([s[i] for s in sts]) for i in range(5)]
    p_st = stack(p_new)
    s_st = stack(s_new)
    return (xp.reshape(n_p, seq, D_MODEL), xs.reshape(n_s, dec_seq, D_MODEL), *p_st, *s_st)
```

```python
import functools

import jax
import jax.numpy as jnp
import numpy as np
from jax import lax
from jax.experimental import pallas as pl
from jax.experimental.pallas import tpu as pltpu

F32 = jnp.float32
BF16 = jnp.bfloat16
HIGHEST = lax.Precision.HIGHEST

D_MODEL = 1024
DEPTH = 2
HEAD_DIM = 64
A_HEADS = 4
A_W = 256
B_W = 384
C_HEADS = 6
C_W = 384
LRU_C = 8.0
CONV_W = 4
N_IN = 3340
N_IN_PAD = 3456
ZA_W = 1024
ZB_W = 768
ZC_W = 1664
PEER_HEADS = 8
PEER_NKEYS = 128
PEER_TOPK = 16
PEER_N = PEER_NKEYS * PEER_NKEYS
EPS = 1e-6

LANES = 128
SUPER_BLOCK = 256
SAMPLE_SEQ_ROWS = 8
SAMPLE_HIST = CONV_W - 1
TOKEN_TILE = 256
PEER_TOKEN_TILE = 512
PEER_EXPERT_BLOCK = 2048
PEER_SUB_BLOCK = 512
PEER_ROUTE_TILE = 256
VMEM_LIMIT_BYTES = 56 * 1024 * 1024

_NN = (((1,), (0,)), ((), ()))
_NT = (((1,), (1,)), ((), ()))
_TN = (((0,), (0,)), ((), ()))


def _dot(a, b, dims=_NN, precision=None):
    return lax.dot_general(a, b, dims, precision=precision, preferred_element_type=F32)


def _split2(x):
    hi = x.astype(BF16)
    lo = (x - hi.astype(F32)).astype(BF16)
    return hi, lo


def _split3(x):
    hi = x.astype(BF16)
    r = x - hi.astype(F32)
    mid = r.astype(BF16)
    lo = (r - mid.astype(F32)).astype(BF16)
    return hi, mid, lo


def _dot_x3(a, b, dims=_NN):
    ah, al = _split2(a)
    bh, bl = _split2(b)
    return _dot(ah, bh, dims) + _dot(ah, bl, dims) + _dot(al, bh, dims)


def _dot_w(a, wh, wl, dims=_NN):
    ah, al = _split2(a)
    return _dot(ah, wh, dims) + _dot(al, wh, dims) + _dot(ah, wl, dims)


def _dot_b(a, b, dims=_NN):
    return _dot(a.astype(BF16), b.astype(BF16), dims)


def _dot_sel(a, sel, dims=_NN, passes=3):
    parts = _split3(a)[:passes]
    out = _dot(parts[0], sel, dims)
    for p in parts[1:]:
        out = out + _dot(p, sel, dims)
    return out


def _head_sums(x, ones_ref):
    ones = ones_ref[0:LANES, 0:LANES]
    xb = x.astype(BF16)
    return jnp.concatenate([_dot(xb[:, t:t + LANES], ones) for t in range(0, x.shape[1], LANES)], axis=1)


def _sel_dot(sel, b, passes=3):
    parts = _split3(b)[:passes]
    out = _dot(sel, parts[0])
    for p in parts[1:]:
        out = out + _dot(sel, p)
    return out


def _softplus(x):
    return jnp.maximum(x, 0.0) + jnp.log(1.0 + jnp.exp(-jnp.abs(x)))


def _silu(x):
    return x * jax.nn.sigmoid(x)


def _gelu_x2(x):
    c = float(np.sqrt(2.0 / np.pi))
    return x + x * jnp.tanh(x * (c + (0.044715 * c) * (x * x)))


def _gelu(x):
    return 0.5 * x * (1.0 + jnp.tanh(float(np.sqrt(2.0 / np.pi)) * (x + 0.044715 * (x * x * x))))


def _rmsnorm(x, g):
    return x * lax.rsqrt(jnp.mean(x * x, axis=-1, keepdims=True) + EPS) * g


def _const_spec(shape):
    nd = len(shape)
    return pl.BlockSpec(shape, lambda *_: (0,) * nd)


def _params(n_grid):
    return pltpu.CompilerParams(dimension_semantics=("arbitrary",) * n_grid,
                                vmem_limit_bytes=VMEM_LIMIT_BYTES)


def _ada_kernel(c_ref, w_ref, b_ref, o_ref):
    c = c_ref[...]
    o_ref[0] = _dot(_silu(c), w_ref[0], precision=HIGHEST) + b_ref[0]


def _ada_call(c_all, w_ada, b_ada):
    nb = c_all.shape[0]
    nt = 6 * D_MODEL // 1024
    return pl.pallas_call(
        _ada_kernel,
        grid=(DEPTH, nt),
        in_specs=[
            _const_spec((nb, D_MODEL)),
            pl.BlockSpec((1, D_MODEL, 1024), lambda l, j: (l, 0, j)),
            pl.BlockSpec((1, 1, 1024), lambda l, j: (l, 0, j)),
        ],
        out_specs=pl.BlockSpec((1, nb, 1024), lambda l, j: (l, 0, j)),
        out_shape=jax.ShapeDtypeStruct((DEPTH, nb, 6 * D_MODEL), F32),
        compiler_params=_params(2),
        name="ada_mod",
    )(c_all, w_ada, b_ada.reshape(DEPTH, 1, 6 * D_MODEL))


def _inproj_kernel(x_ref, g_ref, sc_ref, sh_ref, w_ref, za_ref, zb_ref, zc_ref):
    h = (_rmsnorm(x_ref[...], g_ref[...]) * sc_ref[...] + sh_ref[...]).astype(BF16)
    for o_ref, lo, hi in ((za_ref, 0, ZA_W), (zb_ref, ZA_W, ZA_W + ZB_W), (zc_ref, ZA_W + ZB_W, N_IN_PAD)):
        o_ref[...] = _dot(h, w_ref[:, lo:hi])


def _mod_spec(mod, tiles_per_group):
    _, r, w = mod.shape
    return pl.BlockSpec((None, r, w), lambda i: (i // tiles_per_group, 0, 0))


def _inproj_call(x, g, sc, sh, wh, tiles_per_group):
    t = x.shape[0]
    tm = TOKEN_TILE
    row = lambda w: pl.BlockSpec((tm, w), lambda i: (i, 0))
    return pl.pallas_call(
        _inproj_kernel,
        grid=(t // tm,),
        in_specs=[row(D_MODEL), _const_spec((1, D_MODEL)), _mod_spec(sc, tiles_per_group),
                  _mod_spec(sh, tiles_per_group), _const_spec(wh.shape)],
        out_specs=[row(ZA_W), row(ZB_W), row(ZC_W)],
        out_shape=[jax.ShapeDtypeStruct((t, ZA_W), F32), jax.ShapeDtypeStruct((t, ZB_W), F32),
                   jax.ShapeDtypeStruct((t, ZC_W), F32)],
        compiler_params=_params(1),
        name="inproj",
    )(x, g, sc, sh, wh)


class _SeqCfg:
    def __init__(self, seq_rows, hist, real, chunk, n_par):
        self.seq_rows = seq_rows
        self.n_par = n_par
        self.hist = hist
        self.real = real
        self.chunk = chunk
        self.masked = hist > 0 or hist + real < seq_rows
        self.n_chunks = SUPER_BLOCK // chunk
        self.long_seq = seq_rows > SUPER_BLOCK
        self.sb_per_seq = max(seq_rows // SUPER_BLOCK, 1)
        self.seq_per_sb = max(SUPER_BLOCK // seq_rows, 1)
        assert (self.long_seq or chunk == seq_rows) and (self.long_seq or n_par == 1)


PROMPT_CFG = _SeqCfg(seq_rows=2048, hist=0, real=2048, chunk=16, n_par=2)
SAMPLE_CFG = _SeqCfg(seq_rows=SAMPLE_SEQ_ROWS, hist=SAMPLE_HIST, real=4, chunk=SAMPLE_SEQ_ROWS, n_par=1)


def _row_in_seq(cfg, n_rows):
    r = lax.broadcasted_iota(jnp.int32, (n_rows, 1), 0)
    return r % min(cfg.seq_rows, SUPER_BLOCK)


def _real_mask(cfg, n_rows):
    r = _row_in_seq(cfg, n_rows)
    return (r >= cfg.hist) & (r < cfg.hist + cfg.real)


def _causal_conv(x, xp_ref, w_ref):
    n = x.shape[0]
    xp_ref[8:8 + n, :] = x
    y = x * w_ref[CONV_W - 1:CONV_W, :]
    for k in range(CONV_W - 1):
        s = CONV_W - 1 - k
        y = y + xp_ref[8 - s:8 - s + n, :] * w_ref[k:k + 1, :]
    xp_ref[0:8, :] = x[n - 8:n, :]
    return y


def _lane_tiles(x):
    return [x[:, t:t + LANES] for t in range(0, x.shape[1], LANES)]


def _expand_state(s_cat, tile_ref, bd_ref):
    tile = tile_ref[:, 0:LANES]
    bd = bd_ref[0:LANES, 0:LANES]
    return jnp.stack([_dot_sel(s_cat[r:r + LANES, :], tile, passes=2) * bd
                      for r in range(0, s_cat.shape[0], LANES)])


def _compress_state(s_tiles, tile_t_ref):
    tile_t = tile_t_ref[0:LANES, :]
    return jnp.concatenate([_dot_sel(s_tiles[t], tile_t, passes=2) for t in range(s_tiles.shape[0])], axis=0)


def _state_matmul(x, s_tiles, dims):
    return jnp.concatenate([_dot_b(xt, s_tiles[t], dims) for t, xt in enumerate(_lane_tiles(x))], axis=1)


def _state_update(s_tiles, decay_row, a, b, bd_ref):
    bd = bd_ref[0:LANES, 0:LANES]
    return jnp.stack([s_tiles[t] * dt + bd * _dot_b(at, bt, _TN)
                      for t, (dt, at, bt) in enumerate(zip(_lane_tiles(decay_row), _lane_tiles(a), _lane_tiles(b)))])


def _head_norm_gate(o, z, g, ones_ref):
    ms = _dot_sel(o * o, ones_ref[...], passes=2) * (1.0 / HEAD_DIM)
    return o * lax.rsqrt(ms + EPS) * g * _silu(z)


def _run_chunks(cfg, sb, step, st_ref, s0_ref, sout_ref, tile_ref, tile_t_ref, bd_ref):
    if cfg.long_seq:
        @pl.when(sb % cfg.sb_per_seq == 0)
        def _():
            for p in range(cfg.n_par):
                st_ref[p] = _expand_state(s0_ref[p], tile_ref, bd_ref)

        def body(k, carry):
            for p in range(cfg.n_par):
                st_ref[p] = step(k, p, st_ref[p])
            return carry

        lax.fori_loop(0, cfg.n_chunks, body, 0)

        @pl.when(sb % cfg.sb_per_seq == cfg.sb_per_seq - 1)
        def _():
            for p in range(cfg.n_par):
                sout_ref[p] = _compress_state(st_ref[p], tile_t_ref)
    else:
        def body(k, carry):
            st_new = step(k, 0, _expand_state(s0_ref[k], tile_ref, bd_ref))
            sout_ref[k] = _compress_state(st_new, tile_t_ref)
            return carry

        lax.fori_loop(0, cfg.n_chunks, body, 0, unroll=2)


def _hgrn_kernel(z_ref, lbp_ref, g_ref, s0_ref, ltri_ref, ones_ref, tile_ref, tile_t_ref, bd_ref,
                 o_ref, sout_ref,
                 st_ref, q_s, k_s, g_s, qe_s, kt_s, egl_s, o_s, *, cfg, layer):
    sb = pl.program_id(0)
    c = cfg.chunk
    n = SUPER_BLOCK
    nch = cfg.n_chunks
    w = A_W
    lbp = lbp_ref[...]
    e = jnp.exp(lbp - jnp.max(lbp, axis=0, keepdims=True))
    lbs = e / jnp.sum(e, axis=0, keepdims=True)
    lb = jnp.sum(lbs[0:layer + 1, :], axis=0, keepdims=True) - lbs[0:1, :]

    for p in range(cfg.n_par):
        r = slice(p * n, (p + 1) * n)
        aq = z_ref[p, :, 0:w]
        af = z_ref[p, :, w:2 * w]
        f = lb + (1.0 - lb) * jax.nn.sigmoid(af)
        lf = jnp.log(f)
        kk = 1.0 - f
        if cfg.masked:
            real = _real_mask(cfg, n)
            lf = jnp.where(real, lf, 0.0)
            kk = jnp.where(real, kk, 0.0)
        q = _silu(aq) * (HEAD_DIM ** -0.5)
        gcum = _sel_dot(ltri_ref[...], lf)
        g3 = gcum.reshape(nch, c, w)
        gl3 = g3[:, c - 1:c, :]
        q_s[r, :] = q
        k_s[r, :] = kk
        g_s[r, :] = gcum
        qe_s[r, :] = q * jnp.exp(gcum)
        kt_s[r, :] = (kk.reshape(nch, c, w) * jnp.exp(gl3 - g3)).reshape(n, w)
        egl_s[p * nch:(p + 1) * nch, :] = jnp.exp(gl3).reshape(nch, w)

    jio = lax.broadcasted_iota(jnp.int32, (c, 1), 0)
    bd = bd_ref[...]

    def chunk(k, p, st):
        rows = pl.ds(pl.multiple_of(p * n + k * c, c), c)
        q_c = q_s[rows, :]
        k_c = k_s[rows, :]
        g_c = g_s[rows, :]
        v_c = z_ref[p, pl.ds(pl.multiple_of(k * c, c), c), 2 * w:3 * w]
        o_inter = _state_matmul(qe_s[rows, :], st, _NT)
        lo = min(c, 8)
        d_rows = []
        for i in range(c):
            nj = lo if i < lo else c
            dec = jnp.exp(jnp.minimum(g_c[i:i + 1, :] - g_c[0:nj, :], 0.0))
            d_rows.append(jnp.where(jio[0:nj] <= i, k_c[0:nj, :] * dec * q_c[i:i + 1, :], 0.0))
        a_b = _head_sums(jnp.concatenate(d_rows, axis=0), ones_ref)
        o_diag = jnp.sum(a_b[0:lo * lo, :].reshape(lo, lo, w) * v_c[None, 0:lo, :], axis=1)
        if c > lo:
            o_hi = jnp.sum(a_b[lo * lo:, :].reshape(c - lo, c, w) * v_c[None, :, :], axis=1)
            o_diag = jnp.concatenate([o_diag, o_hi], axis=0)
        o_s[rows, :] = o_inter + o_diag
        return _state_update(st, egl_s[pl.ds(p * nch + k, 1), :], v_c, kt_s[rows, :], bd_ref)

    _run_chunks(cfg, sb, chunk, st_ref, s0_ref, sout_ref, tile_ref, tile_t_ref, bd_ref)
    for p in range(cfg.n_par):
        o_ref[p] = _head_norm_gate(o_s[p * n:(p + 1) * n, :], z_ref[p, :, 3 * w:4 * w], g_ref[...], ones_ref)


def _mixer_consts(heads, cfg):
    w = heads * HEAD_DIM
    lane_head = np.arange(w) // HEAD_DIM
    ones = (lane_head[:, None] == lane_head[None, :]).astype(np.float32)
    tile = (np.arange(HEAD_DIM)[:, None] == (np.arange(w) % HEAD_DIM)[None, :]).astype(np.float32)
    r = np.arange(SUPER_BLOCK)
    ltri = ((r[:, None] // cfg.chunk == r[None, :] // cfg.chunk) & (r[None, :] <= r[:, None])).astype(np.float32)
    return dict(ltri=jnp.asarray(ltri, BF16), ones=jnp.asarray(ones, BF16), tile=jnp.asarray(tile, BF16),
                tile_t=jnp.asarray(tile.T, BF16), bd=jnp.asarray(ones, F32))


def _state_specs(cfg, w):
    blk = (cfg.n_par if cfg.long_seq else cfg.seq_per_sb, w, HEAD_DIM)
    if cfg.long_seq:
        imap = lambda i: (i // cfg.sb_per_seq, 0, 0)
    else:
        imap = lambda i: (i, 0, 0)
    return pl.BlockSpec(blk, imap)


def _seq_view(x, cfg):
    return x.reshape(-1, cfg.seq_rows, x.shape[-1]) if cfg.long_seq else x[None]


def _seq_block_spec(cfg, width):
    if cfg.long_seq:
        return pl.BlockSpec((cfg.n_par, SUPER_BLOCK, width), lambda i: (i // cfg.sb_per_seq, i % cfg.sb_per_seq, 0))
    return pl.BlockSpec((1, SUPER_BLOCK, width), lambda i: (0, i, 0))


def _hgrn_call(z_a, lb_param, g_exp, s0_t, cfg, layer):
    rows = z_a.shape[0]
    w = A_W
    cst = _mixer_consts(A_HEADS, cfg)
    n_seq = s0_t.shape[0]
    par_rows = cfg.n_par * SUPER_BLOCK
    vm = lambda shape: pltpu.VMEM(shape, F32)
    kern = functools.partial(_hgrn_kernel, cfg=cfg, layer=layer)
    z3 = _seq_view(z_a, cfg)
    oa, s_new = pl.pallas_call(
        kern,
        grid=(rows // par_rows,),
        in_specs=[_seq_block_spec(cfg, ZA_W), _const_spec((DEPTH, w)),
                  _const_spec((1, w)), _state_specs(cfg, w), _const_spec(cst["ltri"].shape),
                  _const_spec(cst["ones"].shape), _const_spec(cst["tile"].shape),
                  _const_spec(cst["tile_t"].shape), _const_spec(cst["bd"].shape)],
        out_specs=[_seq_block_spec(cfg, w), _state_specs(cfg, w)],
        out_shape=[jax.ShapeDtypeStruct(z3.shape[:2] + (w,), F32), jax.ShapeDtypeStruct((n_seq, w, HEAD_DIM), F32)],
        scratch_shapes=[vm((cfg.n_par, w // LANES, LANES, LANES))] + [vm((par_rows, w))] * 5
        + [vm((cfg.n_par * cfg.n_chunks, w)), vm((par_rows, w))],
        compiler_params=_params(1),
        name="hgrn2",
    )(z3, lb_param, g_exp, s0_t, cst["ltri"], cst["ones"], cst["tile"], cst["tile_t"], cst["bd"])
    return oa.reshape(rows, w), s_new


def _lru_kernel(*refs, cfg):
    if cfg.long_seq:
        (z_ref, cw_ref, cb_ref, wg_ref, ba_ref, bx_ref, lp_ref,
         ob_ref, hs_ref, prev_ref, hc_ref) = refs
        inj_ref = None
    else:
        (z_ref, inj_ref, cw_ref, cb_ref, wg_ref, ba_ref, bx_ref, lp_ref,
         ob_ref, hs_ref, prev_ref, hc_ref) = refs
    sb = pl.program_id(0)
    n = SUPER_BLOCK
    w = B_W

    @pl.when(sb % cfg.sb_per_seq == 0)
    def _():
        prev_ref[0:8, :] = jnp.zeros((8, prev_ref.shape[1]), F32)
        hc_ref[...] = jnp.zeros_like(hc_ref)

    xc = _causal_conv(z_ref[:, 0:w], prev_ref, cw_ref) + cb_ref[...]
    gates = _dot_b(xc, wg_ref[...])
    r = jax.nn.sigmoid(gates[:, 0:w] + ba_ref[...])
    ig = jax.nn.sigmoid(gates[:, w:2 * w] + bx_ref[...])
    log_a = -LRU_C * r * _softplus(-lp_ref[...])
    a = jnp.exp(log_a)
    b = jnp.sqrt(1.0 - jnp.exp(2.0 * log_a)) * ig * xc
    if cfg.masked:
        real = _real_mask(cfg, n)
        a = jnp.where(real, a, 1.0)
        b = jnp.where(real, b, 0.0)
    if inj_ref is not None:
        b = b + inj_ref[...]
    ris = _row_in_seq(cfg, n)
    d = 1
    while d < min(n, cfg.seq_rows):
        has = ris >= d
        a_sh = jnp.where(has, pltpu.roll(a, d, 0), 1.0)
        b_sh = jnp.where(has, pltpu.roll(b, d, 0), 0.0)
        b = b + a * b_sh
        a = a * a_sh
        d *= 2
    hs = b + a * hc_ref[...]
    if cfg.long_seq:
        hc_ref[...] = hs[n - 1:n, :]
    hs_ref[...] = hs
    ob_ref[...] = hs * _gelu(z_ref[:, w:2 * w])


def _lru_call(z_b, inj, cw, cb, wg, ba, bx, lp, cfg):
    rows = z_b.shape[0]
    w = B_W
    row = lambda width: pl.BlockSpec((SUPER_BLOCK, width), lambda i: (i, 0))
    ins = [z_b] + ([] if cfg.long_seq else [inj]) + [cw, cb, wg, ba, bx, lp]
    specs = [row(ZB_W)] + ([] if cfg.long_seq else [row(w)]) + [_const_spec(a.shape) for a in ins[-6:]]
    return pl.pallas_call(
        functools.partial(_lru_kernel, cfg=cfg),
        grid=(rows // SUPER_BLOCK,),
        in_specs=specs,
        out_specs=[row(w), row(w)],
        out_shape=[jax.ShapeDtypeStruct((rows, w), F32)] * 2,
        scratch_shapes=[pltpu.VMEM((8 + SUPER_BLOCK, w), F32), pltpu.VMEM((1, w), F32)],
        compiler_params=_params(1),
        name="rglru",
    )(*ins)


def _dn_kernel(z_ref, cw_ref, alog_ref, dtb_ref, g_ref, s0_ref, ltri_ref, ones_ref, tile_ref, tile_t_ref,
               bd_ref, expb_ref, expa_ref,
               o_ref, sout_ref,
               st_ref, prev_ref, q_s, k_s, v_s, g_s, eg_s, beta_s, kt_s, egl_s, o_s, *, cfg):
    sb = pl.program_id(0)
    c = cfg.chunk
    n = SUPER_BLOCK
    nch = cfg.n_chunks
    w = C_W

    @pl.when(sb % cfg.sb_per_seq == 0)
    def _():
        for p in range(cfg.n_par):
            prev_ref[p, 0:8, :] = jnp.zeros((8, prev_ref.shape[2]), F32)

    ones = ones_ref[...]
    for p in range(cfg.n_par):
        r = slice(p * n, (p + 1) * n)
        qkv = _silu(_causal_conv(z_ref[p, :, 0:3 * w], prev_ref.at[p], cw_ref))
        q = qkv[:, 0:w]
        kx = qkv[:, w:2 * w]
        q = q * lax.rsqrt(_dot_sel(q * q, ones, passes=2) + EPS) * (HEAD_DIM ** -0.5)
        kx = kx * lax.rsqrt(_dot_sel(kx * kx, ones, passes=2) + EPS)
        pc = z_ref[p, :, 4 * w:4 * w + LANES]
        beta = _dot_sel(jax.nn.sigmoid(pc), expb_ref[...], passes=2)
        gdec = _dot_sel(-jnp.exp(alog_ref[...]) * _softplus(pc + dtb_ref[...]), expa_ref[...], passes=2)
        if cfg.masked:
            real = _real_mask(cfg, n)
            beta = jnp.where(real, beta, 0.0)
            gdec = jnp.where(real, gdec, 0.0)
        gcum = _sel_dot(ltri_ref[...], gdec)
        g3 = gcum.reshape(nch, c, w)
        gl3 = g3[:, c - 1:c, :]
        q_s[r, :] = q
        k_s[r, :] = kx
        v_s[r, :] = qkv[:, 2 * w:3 * w]
        g_s[r, :] = gcum
        eg_s[r, :] = jnp.exp(gcum)
        beta_s[r, :] = beta
        kt_s[r, :] = (kx.reshape(nch, c, w) * jnp.exp(gl3 - g3)).reshape(n, w)
        egl_s[p * nch:(p + 1) * nch, :] = jnp.exp(gl3).reshape(nch, w)

    iio = lax.broadcasted_iota(jnp.int32, (c, 1), 0)
    bd = bd_ref[...]

    def chunk(k, p, st):
        rows = pl.ds(pl.multiple_of(p * n + k * c, c), c)
        q_c = q_s[rows, :]
        k_c = k_s[rows, :]
        g_c = g_s[rows, :]
        eg_c = eg_s[rows, :]
        beta_c = beta_s[rows, :]
        qk_s = _state_matmul(jnp.concatenate([q_c, k_c], axis=0), st, _NN)
        q_st = qk_s[0:c, :]
        k_st = qk_s[c:2 * c, :]
        lo = min(c, 8)
        tail = lambda x: x[lo:c, :]
        d_rows = ([k_c * k_c[j:j + 1, :] for j in range(lo)] + [q_c * k_c[j:j + 1, :] for j in range(lo)]
                  + [tail(k_c) * k_c[j:j + 1, :] for j in range(lo, c)]
                  + [tail(q_c) * k_c[j:j + 1, :] for j in range(lo, c)])
        dots = _head_sums(jnp.concatenate(d_rows, axis=0), ones_ref)
        e = beta_c * (v_s[rows, :] - eg_c * k_st)
        o = eg_c * q_st
        for j in range(lo):
            dec = jnp.exp(jnp.minimum(g_c - g_c[j:j + 1, :], 0.0))
            m_col = jnp.where(iio > j, beta_c * dots[j * c:(j + 1) * c, :] * dec, 0.0)
            e_j = e[j:j + 1, :]
            e = e - m_col * e_j
            qk_col = jnp.where(iio >= j, dots[(lo + j) * c:(lo + j + 1) * c, :] * dec, 0.0)
            o = o + qk_col * e_j
        if c > lo:
            nt = c - lo
            e_t, o_t, g_t, beta_t = tail(e), tail(o), tail(g_c), tail(beta_c)
            base = 2 * lo * c
            for j in range(lo, c):
                dec = jnp.exp(jnp.minimum(g_t - g_c[j:j + 1, :], 0.0))
                kk = dots[base + (j - lo) * nt:base + (j - lo + 1) * nt, :]
                qk = dots[base + (nt + j - lo) * nt:base + (nt + j - lo + 1) * nt, :]
                e_j = e_t[j - lo:j - lo + 1, :]
                e_t = e_t - jnp.where(iio[0:nt] > j - lo, beta_t * kk * dec, 0.0) * e_j
                o_t = o_t + jnp.where(iio[0:nt] >= j - lo, qk * dec, 0.0) * e_j
            e = jnp.concatenate([e[0:lo, :], e_t], axis=0)
            o = jnp.concatenate([o[0:lo, :], o_t], axis=0)
        o_s[rows, :] = o
        return _state_update(st, egl_s[pl.ds(p * nch + k, 1), :], kt_s[rows, :], e, bd_ref)

    _run_chunks(cfg, sb, chunk, st_ref, s0_ref, sout_ref, tile_ref, tile_t_ref, bd_ref)
    for p in range(cfg.n_par):
        o_ref[p] = _head_norm_gate(o_s[p * n:(p + 1) * n, :], z_ref[p, :, 3 * w:4 * w], g_ref[...], ones_ref)


def _dn_call(z_c, cw, alog_exp, dtb_exp, g_exp, s0, cfg):
    rows = z_c.shape[0]
    w = C_W
    cst = _mixer_consts(C_HEADS, cfg)
    lane_head = np.arange(w) // HEAD_DIM
    expb = (np.arange(128)[:, None] == lane_head[None, :]).astype(np.float32)
    expa = (np.arange(128)[:, None] == (lane_head[None, :] + C_HEADS)).astype(np.float32)
    expb = jnp.asarray(expb, BF16)
    expa = jnp.asarray(expa, BF16)
    n_seq = s0.shape[0]
    par_rows = cfg.n_par * SUPER_BLOCK
    vm = lambda shape: pltpu.VMEM(shape, F32)
    z3 = _seq_view(z_c, cfg)
    oc, s_new = pl.pallas_call(
        functools.partial(_dn_kernel, cfg=cfg),
        grid=(rows // par_rows,),
        in_specs=[_seq_block_spec(cfg, ZC_W), _const_spec(cw.shape),
                  _const_spec((1, LANES)), _const_spec((1, LANES)), _const_spec((1, w)), _state_specs(cfg, w),
                  _const_spec(cst["ltri"].shape), _const_spec(cst["ones"].shape),
                  _const_spec(cst["tile"].shape), _const_spec(cst["tile_t"].shape),
                  _const_spec(cst["bd"].shape), _const_spec(expb.shape), _const_spec(expa.shape)],
        out_specs=[_seq_block_spec(cfg, w), _state_specs(cfg, w)],
        out_shape=[jax.ShapeDtypeStruct(z3.shape[:2] + (w,), F32), jax.ShapeDtypeStruct((n_seq, w, HEAD_DIM), F32)],
        scratch_shapes=[vm((cfg.n_par, w // LANES, LANES, LANES)), vm((cfg.n_par, 8 + SUPER_BLOCK, 3 * w))]
        + [vm((par_rows, w))] * 7
        + [vm((cfg.n_par * cfg.n_chunks, w)), vm((par_rows, w))],
        compiler_params=_params(1),
        name="deltanet",
    )(z3, cw, alog_exp, dtb_exp, g_exp, s0, cst["ltri"], cst["ones"], cst["tile"], cst["tile_t"],
      cst["bd"], expb, expa)
    return oc.reshape(rows, w), s_new


def _outproj_kernel(x_ref, oa_ref, ob_ref, oc_ref, gt_ref, g_ref, sc_ref, sh_ref,
                    wo_ref, wqh_ref, wql_ref, x1_ref, h2_ref, q_ref):
    mo = _dot_b(oa_ref[...], wo_ref[0:A_W, :])
    mo = mo + _dot_b(ob_ref[...], wo_ref[A_W:A_W + B_W, :])
    mo = mo + _dot_b(oc_ref[...], wo_ref[A_W + B_W:, :])
    x1 = x_ref[...] + gt_ref[...] * mo
    x1_ref[...] = x1
    h2 = _rmsnorm(x1, g_ref[...]) * sc_ref[...] + sh_ref[...]
    h2_ref[...] = h2.astype(BF16)
    q_ref[...] = _dot_w(h2, wqh_ref[...], wql_ref[...])


def _outproj_call(x, oa, ob, oc, gt, g, sc, sh, wo, wqh, wql, tiles_per_group):
    t = x.shape[0]
    tm = TOKEN_TILE
    row = lambda w: pl.BlockSpec((tm, w), lambda i: (i, 0))
    return pl.pallas_call(
        _outproj_kernel,
        grid=(t // tm,),
        in_specs=[row(D_MODEL), row(A_W), row(B_W), row(C_W), _mod_spec(gt, tiles_per_group),
                  _const_spec((1, D_MODEL)), _mod_spec(sc, tiles_per_group), _mod_spec(sh, tiles_per_group),
                  _const_spec(wo.shape), _const_spec(wqh.shape), _const_spec(wql.shape)],
        out_specs=[row(D_MODEL), row(D_MODEL), row(D_MODEL)],
        out_shape=[jax.ShapeDtypeStruct((t, D_MODEL), F32), jax.ShapeDtypeStruct((t, D_MODEL), BF16),
                   jax.ShapeDtypeStruct((t, D_MODEL), F32)],
        compiler_params=_params(1),
        name="outproj",
    )(x, oa, ob, oc, gt, g, sc, sh, wo, wqh, wql)


def _top16(s):
    vals = []
    for r in range(PEER_TOPK):
        m = jnp.max(s, axis=0, keepdims=True)
        s = jnp.where(s == m, -(32.0 + r) * 2.0 ** 95, s)
        vals.append(m)
    rank = jnp.where(s <= -(2.0 ** 99), s * -(2.0 ** -95) - 32.0, float(PEER_TOPK))
    return jnp.concatenate(vals, axis=0), rank


def _peer_kernel(h2_ref, q_ref, x1_ref, gt_ref, fg_ref, k1_ref, k2_ref, u_ref, vt_ref, o_ref,
                 cnt_s, e1_s, rank2_s, e2_s, acc_s, s1_s, s2_s, coef_s, *, final, n_steps):
    g = pl.program_id(0)
    tt = PEER_TOKEN_TILE
    nk = PEER_NKEYS
    nj = PEER_N // PEER_EXPERT_BLOCK
    a_per_step = PEER_EXPERT_BLOCK // nk
    j = jnp.minimum(g, n_steps - 2) % nj
    gb = jnp.maximum(g - 1, 0)

    @pl.when(g == 0)
    def _():
        coef_s[...] = jnp.zeros_like(coef_s)

    @pl.when(gb % nj == 0)
    def _():
        acc_s[...] = jnp.zeros_like(acc_s)

    @pl.when(jnp.logical_and(g % nj == 0, g < n_steps - 1))
    def _():

        def head(h, carry):
            qh = q_ref[:, pl.ds(pl.multiple_of(h * nk, nk), nk)]
            s1_s[...] = _dot_x3(k1_ref[...], qh, _NT)
            s2_s[...] = _dot_x3(k2_ref[...], qh, _NT)

            def lane_tile(c, carry2):
                cols = pl.ds(pl.multiple_of(c * PEER_ROUTE_TILE, PEER_ROUTE_TILE), PEER_ROUTE_TILE)
                s1 = s1_s[:, cols]
                s2 = s2_s[:, cols]
                v1, rank1 = _top16(s1)
                v2, rank2 = _top16(s2)
                c3 = v1[:, None, :] + v2[None, :, :]
                jrow = lax.broadcasted_iota(jnp.int32, (8, 1), 0)
                parts = [v1[0:1, :] + v2, v1[1:2, :] + v2[0:8, :]]
                for i in range(2, 8):
                    parts.append(jnp.where(jrow < PEER_TOPK // (i + 1), v1[i:i + 1, :] + v2[0:8, :], -jnp.inf))
                parts.append(v1[8:16, :] + v2[0:1, :])
                cand = jnp.concatenate(parts, axis=0)
                m = None
                for _ in range(PEER_TOPK):
                    m = jnp.max(cand, axis=0, keepdims=True)
                    cand = jnp.where(cand == m, -jnp.inf, cand)
                tau = m
                sel = c3 >= tau[None, :, :]
                m0 = v1[0:1, :] + v2[0:1, :]
                zsum = jnp.sum(jnp.where(sel, jnp.exp(c3 - m0[None, :, :]), 0.0).reshape(-1, PEER_ROUTE_TILE),
                               axis=0, keepdims=True)
                n_i = jnp.sum(jnp.where(sel, 1.0, 0.0), axis=1)
                rank1 = rank1.astype(BF16)
                cnt = jnp.zeros((nk, PEER_ROUTE_TILE), BF16)
                for i in range(PEER_TOPK):
                    cnt = cnt + jnp.where(rank1 == i, n_i[i:i + 1, :].astype(BF16), jnp.zeros((), BF16))
                cnt_s[h, :, cols] = cnt.astype(F32)
                e1_s[h, :, cols] = jnp.exp(s1 - v1[0:1, :])
                rank2_s[h, :, cols] = rank2.astype(BF16)
                e2_s[h, :, cols] = (jnp.exp(s2 - v2[0:1, :]) * (0.5 / zsum)).astype(BF16)
                return carry2

            lax.fori_loop(0, tt // PEER_ROUTE_TILE, lane_tile, 0)
            return carry

        lax.fori_loop(0, PEER_HEADS, head, 0)

    h2 = h2_ref[...]
    n_sub = PEER_EXPERT_BLOCK // PEER_SUB_BLOCK
    a_per_sub = PEER_SUB_BLOCK // nk
    sub = lambda s: slice(s * PEER_SUB_BLOCK, (s + 1) * PEER_SUB_BLOCK)
    cur = coef_s.at[g % 2]
    prev = coef_s.at[(g + 1) % 2]
    hidden = lambda s: _dot(u_ref[sub(s), :], h2, _NT)
    hids = [hidden(0), hidden(1)]
    half = PEER_EXPERT_BLOCK // 2
    pack = 16
    for s in range(n_sub):
        if s % 2 == 0:
            k0 = (s // 2) * half
            acc_s[...] += _dot(vt_ref[:, k0:k0 + half], prev[k0:k0 + half, :])
            hids.extend(hidden(s2) for s2 in (s + 2, s + 3) if s2 < n_sub)
        for al in range(a_per_sub):
            a = j * a_per_step + s * a_per_sub + al
            wsum = jnp.zeros((nk // pack, pack, tt), BF16)
            for h in range(PEER_HEADS):
                cnt_row = jnp.broadcast_to(cnt_s[h, pl.ds(a, 1), :], (pack, tt)).astype(BF16)
                e1_row = jnp.broadcast_to(e1_s[h, pl.ds(a, 1), :], (pack, tt)).astype(BF16)
                rank2 = rank2_s[h].reshape(nk // pack, pack, tt)
                e2 = e2_s[h].reshape(nk // pack, pack, tt)
                wsum = jnp.where(rank2 < cnt_row[None], wsum + e2 * e1_row[None], wsum)
            r0 = s * PEER_SUB_BLOCK + al * nk
            cur[r0:r0 + nk, :] = _gelu_x2(hids[s][al * nk:(al + 1) * nk, :].astype(BF16)) * wsum.reshape(nk, tt)

    @pl.when(jnp.logical_and(g > 0, gb % nj == nj - 1))
    def _():
        x2 = x1_ref[...] + gt_ref[...] * acc_s[...].T
        if final:
            x2 = _rmsnorm(x2, fg_ref[...])
        o_ref[...] = x2


def _peer_call(h2b, q, x1, gt, fg, k1p, k2p, u_b, v_b, layer, tiles_per_group, final):
    t = h2b.shape[0]
    tt = PEER_TOKEN_TILE
    nb = PEER_EXPERT_BLOCK
    nj = PEER_N // nb
    n_steps = (t // tt) * nj + 1
    front = lambda g: jnp.minimum(g, n_steps - 2)
    back = lambda g: jnp.maximum(g - 1, 0)
    row_f = lambda w: pl.BlockSpec((tt, w), lambda g: (front(g) // nj, 0))
    row_b = lambda w: pl.BlockSpec((tt, w), lambda g: (back(g) // nj, 0))
    _, r, w = gt.shape
    tab = lambda dt: pltpu.VMEM((PEER_HEADS, PEER_NKEYS, tt), dt)
    return pl.pallas_call(
        functools.partial(_peer_kernel, final=final, n_steps=n_steps),
        grid=(n_steps,),
        in_specs=[row_f(D_MODEL), row_f(D_MODEL), row_b(D_MODEL),
                  pl.BlockSpec((None, r, w), lambda g: (back(g) // nj // tiles_per_group, 0, 0)),
                  _const_spec((1, D_MODEL)),
                  _const_spec((PEER_NKEYS, PEER_NKEYS)),
                  _const_spec((PEER_NKEYS, PEER_NKEYS)),
                  pl.BlockSpec((None, nb, D_MODEL), lambda g: (layer, front(g) % nj, 0)),
                  pl.BlockSpec((None, D_MODEL, nb), lambda g: (layer, 0, back(g) % nj))],
        out_specs=row_b(D_MODEL),
        out_shape=jax.ShapeDtypeStruct((t, D_MODEL), F32),
        scratch_shapes=[tab(F32), tab(F32), tab(BF16), tab(BF16), pltpu.VMEM((D_MODEL, tt), F32),
                        pltpu.VMEM((PEER_NKEYS, tt), F32), pltpu.VMEM((PEER_NKEYS, tt), F32),
                        pltpu.VMEM((2, nb, tt), BF16)],
        compiler_params=_params(1),
        name="peer",
    )(h2b, q, x1, gt, fg, k1p, k2p, u_b, v_b)


def _block_diag(wblk):
    n, d, e = wblk.shape
    eye = jnp.eye(n, dtype=wblk.dtype)
    return (eye[:, None, :, None] * wblk[:, :, None, :]).reshape(n * d, n * e)


def _hilo(w):
    hi = w.astype(BF16)
    return hi, (w - hi.astype(F32)).astype(BF16)


def _pad_sample(z, hist=None):
    bsz = z.shape[0] // 4
    w = z.shape[1]
    z3 = z.reshape(bsz, 4, w)
    h3 = jnp.zeros((bsz, SAMPLE_HIST, w), F32)
    if hist is not None:
        h3 = h3.at[:, :, :hist.shape[-1]].set(hist)
    return jnp.concatenate([h3, z3, jnp.zeros((bsz, 1, w), F32)], axis=1).reshape(bsz * SAMPLE_SEQ_ROWS, w)


def _unpad_sample(o):
    bsz = o.shape[0] // SAMPLE_SEQ_ROWS
    return o.reshape(bsz, SAMPLE_SEQ_ROWS, -1)[:, SAMPLE_HIST:SAMPLE_HIST + 4].reshape(bsz * 4, -1)


def _layer_weights(l, w_in, w_out, peer_wq, peer_k1, peer_k2, peer_u, peer_v, lru_wa, lru_wx):
    w_in_p = jnp.pad(w_in[l], ((0, 0), (0, N_IN_PAD - N_IN)))
    wih = w_in_p.astype(BF16)
    wo = w_out[l].astype(BF16)
    wqh, wql = _hilo(peer_wq[l])
    wg = jnp.concatenate([_block_diag(lru_wa[l]), _block_diag(lru_wx[l])], axis=1).astype(BF16)
    half = PEER_NKEYS // 2
    k1p = jnp.pad(peer_k1[l], ((0, 0), (0, half)))
    k2p = jnp.pad(peer_k2[l], ((0, 0), (half, 0)))
    return dict(wih=wih, wo=wo, wqh=wqh, wql=wql, wg=wg, k1p=k1p, k2p=k2p)


def _trunk(x, mods, states, is_prompt, p, lw, l, final_g):
    t = x.shape[0]
    if is_prompt:
        sh1, sc1, gt1, sh2, sc2, gt2 = mods
    else:
        sh1, sc1, gt1, sh2, sc2 = (m.reshape(-1, TOKEN_TILE, D_MODEL) for m in mods[:5])
        gt2 = mods[5].reshape(-1, PEER_TOKEN_TILE, D_MODEL)
    if is_prompt:
        cfg = PROMPT_CFG
        tiles_tok = 2048 // TOKEN_TILE
        tiles_peer = 2048 // PEER_TOKEN_TILE
        n_seq = t // 2048
    else:
        cfg = SAMPLE_CFG
        tiles_tok = 1
        tiles_peer = 1
        n_seq = t // 4
    row1 = lambda v: v.reshape(1, -1)
    z_a, z_b, z_c = _inproj_call(x, row1(p['norm1_g']), sc1, sh1, lw['wih'], tiles_tok)
    if is_prompt:
        s_a = jnp.zeros((n_seq, A_W, HEAD_DIM), F32)
        s_c = jnp.zeros((n_seq, C_W, HEAD_DIM), F32)
        za_m, zb_m, zc_m = z_a, z_b, z_c
        inj = None
    else:
        st_hgrn, st_lru_h, st_lru_conv, st_dn, st_dn_conv = states
        s_a = jnp.swapaxes(st_hgrn, -1, -2).reshape(n_seq, A_W, HEAD_DIM)
        s_c = st_dn.reshape(n_seq, C_W, HEAD_DIM)
        za_m = _pad_sample(z_a)
        zb_m = _pad_sample(z_b, st_lru_conv)
        zc_m = _pad_sample(z_c, st_dn_conv)
        inj = jnp.zeros((n_seq, SAMPLE_SEQ_ROWS, B_W), F32).at[:, 0].set(st_lru_h).reshape(-1, B_W)
    rep = lambda v, h: row1(jnp.tile(v, h))
    oa, s_a_new = _hgrn_call(za_m, p['lb_param'], rep(p['a_norm_g'], A_HEADS), s_a, cfg, l)
    ob, hs = _lru_call(zb_m, inj, p['lru_conv_w'], row1(p['lru_conv_b']), lw['wg'],
                       row1(p['lru_ba']), row1(p['lru_bx']), row1(p['lru_L']), cfg)
    decay_cols = lambda v: jnp.zeros((1, LANES), F32).at[0, C_HEADS:2 * C_HEADS].set(v)
    oc, s_c_new = _dn_call(zc_m, p['dn_conv_w'], decay_cols(p['dn_A_log']), decay_cols(p['dn_dt_bias']),
                           rep(p['dn_norm_g'], C_HEADS), s_c, cfg)
    if is_prompt:
        seq = 2048
        h_t = hs.reshape(n_seq, seq, B_W)[:, -1]
        buf_b = z_b.reshape(n_seq, seq, ZB_W)[:, -SAMPLE_HIST:, :B_W]
        buf_c = z_c.reshape(n_seq, seq, ZC_W)[:, -SAMPLE_HIST:, :3 * C_W]
    else:
        oa, ob, oc = _unpad_sample(oa), _unpad_sample(ob), _unpad_sample(oc)
        h_t = hs.reshape(n_seq, SAMPLE_SEQ_ROWS, B_W)[:, -1]
        buf_b = z_b.reshape(n_seq, 4, ZB_W)[:, 1:, :B_W]
        buf_c = z_c.reshape(n_seq, 4, ZC_W)[:, 1:, :3 * C_W]
    new_states = (jnp.swapaxes(s_a_new.reshape(n_seq, A_HEADS, HEAD_DIM, HEAD_DIM), -1, -2), h_t, buf_b,
                  s_c_new.reshape(n_seq, C_HEADS, HEAD_DIM, HEAD_DIM), buf_c)
    x1, h2b, q = _outproj_call(x, oa, ob, oc, gt1, row1(p['norm2_g']), sc2, sh2,
                               lw['wo'], lw['wqh'], lw['wql'], tiles_tok)
    x2 = _peer_call(h2b, q, x1, gt2, row1(final_g), lw['k1p'], lw['k2p'], lw['u_b'], lw['v_b'], l,
                    tiles_peer, final=(l == DEPTH - 1))
    return x2, new_states


def kernel(x_prompt, x_sample, state_hgrn, state_lru_h, state_lru_conv, state_dn, state_dn_conv,
           c_prompt, c_sample, w_ada, b_ada, norm1_g, norm2_g, w_in, lb_param, a_norm_g,
           lru_conv_w, lru_conv_b, lru_wa, lru_ba, lru_wx, lru_bx, lru_L,
           dn_conv_w, dn_A_log, dn_dt_bias, dn_norm_g, w_out,
           peer_wq, peer_k1, peer_k2, peer_u, peer_v, final_norm_g):
    n_p, seq, _ = x_prompt.shape
    n_s, dec_seq, _ = x_sample.shape
    mod = _ada_call(jnp.concatenate([c_prompt, c_sample], axis=0), w_ada, b_ada)
    xp = x_prompt.reshape(n_p * seq, D_MODEL)
    xs = x_sample.reshape(n_s * dec_seq, D_MODEL)
    sample_states = (state_hgrn, state_lru_h, state_lru_conv, state_dn, state_dn_conv)
    u_all = peer_u.astype(BF16)
    vt_all = jnp.swapaxes(peer_v, 1, 2).astype(BF16)
    p_new, s_new = [], []
    for l in range(DEPTH):
        p = dict(norm1_g=norm1_g[l], norm2_g=norm2_g[l], lb_param=lb_param, a_norm_g=a_norm_g[l],
                 lru_conv_w=lru_conv_w[l], lru_conv_b=lru_conv_b[l], lru_ba=lru_ba[l], lru_bx=lru_bx[l],
                 lru_L=lru_L[l], dn_conv_w=dn_conv_w[l], dn_A_log=dn_A_log[l], dn_dt_bias=dn_dt_bias[l],
                 dn_norm_g=dn_norm_g[l])
        lw = _layer_weights(l, w_in, w_out, peer_wq, peer_k1, peer_k2, peer_u, peer_v, lru_wa, lru_wx)
        lw['u_b'], lw['v_b'] = u_all, vt_all
        parts = jnp.split(mod[l], 6, axis=-1)
        parts = [m + 1.0 if i in (1, 4) else m for i, m in enumerate(parts)]
        mods_p = [m[:n_p].reshape(n_p, 1, D_MODEL) for m in parts]
        mods_s = [jnp.repeat(m[n_p:], dec_seq, axis=0) for m in parts]
        xp, st_p = _trunk(xp, mods_p, None, True, p, lw, l, final_norm_g)
        xs, st_s = _trunk(xs, mods_s, tuple(s[l] for s in sample_states), False, p, lw, l, final_norm_g)
        p_new.append(st_p)
        s_new.append(st_s)
    stack = lambda sts: [jnp.stack([s[i] for s in sts]) for i in range(5)]
    p_st = stack(p_new)
    s_st = stack(s_new)
    return (xp.reshape(n_p, seq, D_MODEL), xs.reshape(n_s, dec_seq, D_MODEL), *p_st, *s_st)
```

```python
import functools

import jax
import jax.numpy as jnp
import numpy as np
from jax import lax
from jax.experimental import pallas as pl
from jax.experimental.pallas import tpu as pltpu

F32 = jnp.float32
BF16 = jnp.bfloat16
HIGHEST = lax.Precision.HIGHEST

D_MODEL = 1024
DEPTH = 2
HEAD_DIM = 64
A_HEADS = 4
A_W = 256
B_W = 384
C_HEADS = 6
C_W = 384
LRU_C = 8.0
CONV_W = 4
N_IN = 3340
N_IN_PAD = 3456
ZA_W = 1024
ZB_W = 768
ZC_W = 1664
PEER_HEADS = 8
PEER_NKEYS = 128
PEER_TOPK = 16
PEER_N = PEER_NKEYS * PEER_NKEYS
EPS = 1e-6

LANES = 128
SUPER_BLOCK = 256
SAMPLE_SEQ_ROWS = 8
SAMPLE_HIST = CONV_W - 1
TOKEN_TILE = 256
PEER_TOKEN_TILE = 512
PEER_EXPERT_BLOCK = 2048
PEER_SUB_BLOCK = 512
PEER_ROUTE_TILE = 256
VMEM_LIMIT_BYTES = 56 * 1024 * 1024

_NN = (((1,), (0,)), ((), ()))
_NT = (((1,), (1,)), ((), ()))
_TN = (((0,), (0,)), ((), ()))


def _dot(a, b, dims=_NN, precision=None):
    return lax.dot_general(a, b, dims, precision=precision, preferred_element_type=F32)


def _split2(x):
    hi = x.astype(BF16)
    lo = (x - hi.astype(F32)).astype(BF16)
    return hi, lo


def _split3(x):
    hi = x.astype(BF16)
    r = x - hi.astype(F32)
    mid = r.astype(BF16)
    lo = (r - mid.astype(F32)).astype(BF16)
    return hi, mid, lo


def _dot_x3(a, b, dims=_NN):
    ah, al = _split2(a)
    bh, bl = _split2(b)
    return _dot(ah, bh, dims) + _dot(ah, bl, dims) + _dot(al, bh, dims)


def _dot_w(a, wh, wl, dims=_NN):
    ah, al = _split2(a)
    return _dot(ah, wh, dims) + _dot(al, wh, dims) + _dot(ah, wl, dims)


def _dot_b(a, b, dims=_NN):
    return _dot(a.astype(BF16), b.astype(BF16), dims)


def _dot_sel(a, sel, dims=_NN, passes=3):
    parts = _split3(a)[:passes]
    out = _dot(parts[0], sel, dims)
    for p in parts[1:]:
        out = out + _dot(p, sel, dims)
    return out


def _head_sums(x, ones_ref):
    ones = ones_ref[0:LANES, 0:LANES]
    xb = x.astype(BF16)
    return jnp.concatenate([_dot(xb[:, t:t + LANES], ones) for t in range(0, x.shape[1], LANES)], axis=1)


def _sel_dot(sel, b, passes=3):
    parts = _split3(b)[:passes]
    out = _dot(sel, parts[0])
    for p in parts[1:]:
        out = out + _dot(sel, p)
    return out


def _softplus(x):
    return jnp.maximum(x, 0.0) + jnp.log(1.0 + jnp.exp(-jnp.abs(x)))


def _silu(x):
    return x * jax.nn.sigmoid(x)


def _gelu_x2(x):
    c = float(np.sqrt(2.0 / np.pi))
    return x + x * jnp.tanh(x * (c + (0.044715 * c) * (x * x)))


def _gelu(x):
    return 0.5 * x * (1.0 + jnp.tanh(float(np.sqrt(2.0 / np.pi)) * (x + 0.044715 * (x * x * x))))


def _rmsnorm(x, g):
    return x * lax.rsqrt(jnp.mean(x * x, axis=-1, keepdims=True) + EPS) * g


def _const_spec(shape):
    nd = len(shape)
    return pl.BlockSpec(shape, lambda *_: (0,) * nd)


def _params(n_grid):
    return pltpu.CompilerParams(dimension_semantics=("arbitrary",) * n_grid,
                                vmem_limit_bytes=VMEM_LIMIT_BYTES)


def _ada_kernel(c_ref, w_ref, b_ref, o_ref):
    c = c_ref[...]
    o_ref[0] = _dot(_silu(c), w_ref[0], precision=HIGHEST) + b_ref[0]


def _ada_call(c_all, w_ada, b_ada):
    nb = c_all.shape[0]
    nt = 6 * D_MODEL // 1024
    return pl.pallas_call(
        _ada_kernel,
        grid=(DEPTH, nt),
        in_specs=[
            _const_spec((nb, D_MODEL)),
            pl.BlockSpec((1, D_MODEL, 1024), lambda l, j: (l, 0, j)),
            pl.BlockSpec((1, 1, 1024), lambda l, j: (l, 0, j)),
        ],
        out_specs=pl.BlockSpec((1, nb, 1024), lambda l, j: (l, 0, j)),
        out_shape=jax.ShapeDtypeStruct((DEPTH, nb, 6 * D_MODEL), F32),
        compiler_params=_params(2),
        name="ada_mod",
    )(c_all, w_ada, b_ada.reshape(DEPTH, 1, 6 * D_MODEL))


def _inproj_kernel(x_ref, g_ref, sc_ref, sh_ref, w_ref, za_ref, zb_ref, zc_ref):
    h = (_rmsnorm(x_ref[...], g_ref[...]) * sc_ref[...] + sh_ref[...]).astype(BF16)
    for o_ref, lo, hi in ((za_ref, 0, ZA_W), (zb_ref, ZA_W, ZA_W + ZB_W), (zc_ref, ZA_W + ZB_W, N_IN_PAD)):
        o_ref[...] = _dot(h, w_ref[:, lo:hi])


def _mod_spec(mod, tiles_per_group):
    _, r, w = mod.shape
    return pl.BlockSpec((None, r, w), lambda i: (i // tiles_per_group, 0, 0))


def _inproj_call(x, g, sc, sh, wh, tiles_per_group):
    t = x.shape[0]
    tm = TOKEN_TILE
    row = lambda w: pl.BlockSpec((tm, w), lambda i: (i, 0))
    return pl.pallas_call(
        _inproj_kernel,
        grid=(t // tm,),
        in_specs=[row(D_MODEL), _const_spec((1, D_MODEL)), _mod_spec(sc, tiles_per_group),
                  _mod_spec(sh, tiles_per_group), _const_spec(wh.shape)],
        out_specs=[row(ZA_W), row(ZB_W), row(ZC_W)],
        out_shape=[jax.ShapeDtypeStruct((t, ZA_W), F32), jax.ShapeDtypeStruct((t, ZB_W), F32),
                   jax.ShapeDtypeStruct((t, ZC_W), F32)],
        compiler_params=_params(1),
        name="inproj",
    )(x, g, sc, sh, wh)


class _SeqCfg:
    def __init__(self, seq_rows, hist, real, chunk, n_par):
        self.seq_rows = seq_rows
        self.n_par = n_par
        self.hist = hist
        self.real = real
        self.chunk = chunk
        self.masked = hist > 0 or hist + real < seq_rows
        self.n_chunks = SUPER_BLOCK // chunk
        self.long_seq = seq_rows > SUPER_BLOCK
        self.sb_per_seq = max(seq_rows // SUPER_BLOCK, 1)
        self.seq_per_sb = max(SUPER_BLOCK // seq_rows, 1)
        assert (self.long_seq or chunk == seq_rows) and (self.long_seq or n_par == 1)


PROMPT_CFG = _SeqCfg(seq_rows=2048, hist=0, real=2048, chunk=16, n_par=2)
SAMPLE_CFG = _SeqCfg(seq_rows=SAMPLE_SEQ_ROWS, hist=SAMPLE_HIST, real=4, chunk=SAMPLE_SEQ_ROWS, n_par=1)


def _row_in_seq(cfg, n_rows):
    r = lax.broadcasted_iota(jnp.int32, (n_rows, 1), 0)
    return r % min(cfg.seq_rows, SUPER_BLOCK)


def _real_mask(cfg, n_rows):
    r = _row_in_seq(cfg, n_rows)
    return (r >= cfg.hist) & (r < cfg.hist + cfg.real)


def _causal_conv(x, xp_ref, w_ref):
    n = x.shape[0]
    xp_ref[8:8 + n, :] = x
    y = x * w_ref[CONV_W - 1:CONV_W, :]
    for k in range(CONV_W - 1):
        s = CONV_W - 1 - k
        y = y + xp_ref[8 - s:8 - s + n, :] * w_ref[k:k + 1, :]
    xp_ref[0:8, :] = x[n - 8:n, :]
    return y


def _lane_tiles(x):
    return [x[:, t:t + LANES] for t in range(0, x.shape[1], LANES)]


def _expand_state(s_cat, tile_ref, bd_ref):
    tile = tile_ref[:, 0:LANES]
    bd = bd_ref[0:LANES, 0:LANES]
    return jnp.stack([_dot_sel(s_cat[r:r + LANES, :], tile, passes=2) * bd
                      for r in range(0, s_cat.shape[0], LANES)])


def _compress_state(s_tiles, tile_t_ref):
    tile_t = tile_t_ref[0:LANES, :]
    return jnp.concatenate([_dot_sel(s_tiles[t], tile_t, passes=2) for t in range(s_tiles.shape[0])], axis=0)


def _state_matmul(x, s_tiles, dims):
    return jnp.concatenate([_dot_b(xt, s_tiles[t], dims) for t, xt in enumerate(_lane_tiles(x))], axis=1)


def _state_update(s_tiles, decay_row, a, b, bd_ref):
    bd = bd_ref[0:LANES, 0:LANES]
    return jnp.stack([s_tiles[t] * dt + bd * _dot_b(at, bt, _TN)
                      for t, (dt, at, bt) in enumerate(zip(_lane_tiles(decay_row), _lane_tiles(a), _lane_tiles(b)))])


def _head_norm_gate(o, z, g, ones_ref):
    ms = _dot_sel(o * o, ones_ref[...], passes=2) * (1.0 / HEAD_DIM)
    return o * lax.rsqrt(ms + EPS) * g * _silu(z)


def _run_chunks(cfg, sb, step, st_ref, s0_ref, sout_ref, tile_ref, tile_t_ref, bd_ref):
    if cfg.long_seq:
        @pl.when(sb % cfg.sb_per_seq == 0)
        def _():
            for p in range(cfg.n_par):
                st_ref[p] = _expand_state(s0_ref[p], tile_ref, bd_ref)

        def body(k, carry):
            for p in range(cfg.n_par):
                st_ref[p] = step(k, p, st_ref[p])
            return carry

        lax.fori_loop(0, cfg.n_chunks, body, 0)

        @pl.when(sb % cfg.sb_per_seq == cfg.sb_per_seq - 1)
        def _():
            for p in range(cfg.n_par):
                sout_ref[p] = _compress_state(st_ref[p], tile_t_ref)
    else:
        def body(k, carry):
            st_new = step(k, 0, _expand_state(s0_ref[k], tile_ref, bd_ref))
            sout_ref[k] = _compress_state(st_new, tile_t_ref)
            return carry

        lax.fori_loop(0, cfg.n_chunks, body, 0, unroll=2)


def _hgrn_kernel(z_ref, lbp_ref, g_ref, s0_ref, ltri_ref, ones_ref, tile_ref, tile_t_ref, bd_ref,
                 o_ref, sout_ref,
                 st_ref, q_s, k_s, g_s, qe_s, kt_s, egl_s, o_s, *, cfg, layer):
    sb = pl.program_id(0)
    c = cfg.chunk
    n = SUPER_BLOCK
    nch = cfg.n_chunks
    w = A_W
    lbp = lbp_ref[...]
    e = jnp.exp(lbp - jnp.max(lbp, axis=0, keepdims=True))
    lbs = e / jnp.sum(e, axis=0, keepdims=True)
    lb = jnp.sum(lbs[0:layer + 1, :], axis=0, keepdims=True) - lbs[0:1, :]

    for p in range(cfg.n_par):
        r = slice(p * n, (p + 1) * n)
        aq = z_ref[p, :, 0:w]
        af = z_ref[p, :, w:2 * w]
        f = lb + (1.0 - lb) * jax.nn.sigmoid(af)
        lf = jnp.log(f)
        kk = 1.0 - f
        if cfg.masked:
            real = _real_mask(cfg, n)
            lf = jnp.where(real, lf, 0.0)
            kk = jnp.where(real, kk, 0.0)
        q = _silu(aq) * (HEAD_DIM ** -0.5)
        gcum = _sel_dot(ltri_ref[...], lf)
        g3 = gcum.reshape(nch, c, w)
        gl3 = g3[:, c - 1:c, :]
        q_s[r, :] = q
        k_s[r, :] = kk
        g_s[r, :] = gcum
        qe_s[r, :] = q * jnp.exp(gcum)
        kt_s[r, :] = (kk.reshape(nch, c, w) * jnp.exp(gl3 - g3)).reshape(n, w)
        egl_s[p * nch:(p + 1) * nch, :] = jnp.exp(gl3).reshape(nch, w)

    jio = lax.broadcasted_iota(jnp.int32, (c, 1), 0)
    bd = bd_ref[...]

    def chunk(k, p, st):
        rows = pl.ds(pl.multiple_of(p * n + k * c, c), c)
        q_c = q_s[rows, :]
        k_c = k_s[rows, :]
        g_c = g_s[rows, :]
        v_c = z_ref[p, pl.ds(pl.multiple_of(k * c, c), c), 2 * w:3 * w]
        o_inter = _state_matmul(qe_s[rows, :], st, _NT)
        lo = min(c, 8)
        d_rows = []
        for i in range(c):
            nj = lo if i < lo else c
            dec = jnp.exp(jnp.minimum(g_c[i:i + 1, :] - g_c[0:nj, :], 0.0))
            d_rows.append(jnp.where(jio[0:nj] <= i, k_c[0:nj, :] * dec * q_c[i:i + 1, :], 0.0))
        a_b = _head_sums(jnp.concatenate(d_rows, axis=0), ones_ref)
        o_diag = jnp.sum(a_b[0:lo * lo, :].reshape(lo, lo, w) * v_c[None, 0:lo, :], axis=1)
        if c > lo:
            o_hi = jnp.sum(a_b[lo * lo:, :].reshape(c - lo, c, w) * v_c[None, :, :], axis=1)
            o_diag = jnp.concatenate([o_diag, o_hi], axis=0)
        o_s[rows, :] = o_inter + o_diag
        return _state_update(st, egl_s[pl.ds(p * nch + k, 1), :], v_c, kt_s[rows, :], bd_ref)

    _run_chunks(cfg, sb, chunk, st_ref, s0_ref, sout_ref, tile_ref, tile_t_ref, bd_ref)
    for p in range(cfg.n_par):
        o_ref[p] = _head_norm_gate(o_s[p * n:(p + 1) * n, :], z_ref[p, :, 3 * w:4 * w], g_ref[...], ones_ref)


def _mixer_consts(heads, cfg):
    w = heads * HEAD_DIM
    lane_head = np.arange(w) // HEAD_DIM
    ones = (lane_head[:, None] == lane_head[None, :]).astype(np.float32)
    tile = (np.arange(HEAD_DIM)[:, None] == (np.arange(w) % HEAD_DIM)[None, :]).astype(np.float32)
    r = np.arange(SUPER_BLOCK)
    ltri = ((r[:, None] // cfg.chunk == r[None, :] // cfg.chunk) & (r[None, :] <= r[:, None])).astype(np.float32)
    return dict(ltri=jnp.asarray(ltri, BF16), ones=jnp.asarray(ones, BF16), tile=jnp.asarray(tile, BF16),
                tile_t=jnp.asarray(tile.T, BF16), bd=jnp.asarray(ones, F32))


def _state_specs(cfg, w):
    blk = (cfg.n_par if cfg.long_seq else cfg.seq_per_sb, w, HEAD_DIM)
    if cfg.long_seq:
        imap = lambda i: (i // cfg.sb_per_seq, 0, 0)
    else:
        imap = lambda i: (i, 0, 0)
    return pl.BlockSpec(blk, imap)


def _seq_view(x, cfg):
    if not cfg.long_seq:
        return x[None]
    assert x.shape[0] % (cfg.n_par * cfg.seq_rows) == 0, "sequence count must be a multiple of n_par"
    return x.reshape(-1, cfg.seq_rows, x.shape[-1])


def _seq_block_spec(cfg, width):
    if cfg.long_seq:
        return pl.BlockSpec((cfg.n_par, SUPER_BLOCK, width), lambda i: (i // cfg.sb_per_seq, i % cfg.sb_per_seq, 0))
    return pl.BlockSpec((1, SUPER_BLOCK, width), lambda i: (0, i, 0))


def _hgrn_call(z_a, lb_param, g_exp, s0_t, cfg, layer):
    rows = z_a.shape[0]
    w = A_W
    cst = _mixer_consts(A_HEADS, cfg)
    n_seq = s0_t.shape[0]
    par_rows = cfg.n_par * SUPER_BLOCK
    vm = lambda shape: pltpu.VMEM(shape, F32)
    kern = functools.partial(_hgrn_kernel, cfg=cfg, layer=layer)
    z3 = _seq_view(z_a, cfg)
    oa, s_new = pl.pallas_call(
        kern,
        grid=(rows // par_rows,),
        in_specs=[_seq_block_spec(cfg, ZA_W), _const_spec((DEPTH, w)),
                  _const_spec((1, w)), _state_specs(cfg, w), _const_spec(cst["ltri"].shape),
                  _const_spec(cst["ones"].shape), _const_spec(cst["tile"].shape),
                  _const_spec(cst["tile_t"].shape), _const_spec(cst["bd"].shape)],
        out_specs=[_seq_block_spec(cfg, w), _state_specs(cfg, w)],
        out_shape=[jax.ShapeDtypeStruct(z3.shape[:2] + (w,), F32), jax.ShapeDtypeStruct((n_seq, w, HEAD_DIM), F32)],
        scratch_shapes=[vm((cfg.n_par, w // LANES, LANES, LANES))] + [vm((par_rows, w))] * 5
        + [vm((cfg.n_par * cfg.n_chunks, w)), vm((par_rows, w))],
        compiler_params=_params(1),
        name="hgrn2",
    )(z3, lb_param, g_exp, s0_t, cst["ltri"], cst["ones"], cst["tile"], cst["tile_t"], cst["bd"])
    return oa.reshape(rows, w), s_new


def _lru_kernel(*refs, cfg):
    if cfg.long_seq:
        (z_ref, cw_ref, cb_ref, wg_ref, ba_ref, bx_ref, lp_ref,
         ob_ref, hs_ref, prev_ref, hc_ref) = refs
        inj_ref = None
    else:
        (z_ref, inj_ref, cw_ref, cb_ref, wg_ref, ba_ref, bx_ref, lp_ref,
         ob_ref, hs_ref, prev_ref, hc_ref) = refs
    sb = pl.program_id(0)
    n = SUPER_BLOCK
    w = B_W

    @pl.when(sb % cfg.sb_per_seq == 0)
    def _():
        prev_ref[0:8, :] = jnp.zeros((8, prev_ref.shape[1]), F32)
        hc_ref[...] = jnp.zeros_like(hc_ref)

    xc = _causal_conv(z_ref[:, 0:w], prev_ref, cw_ref) + cb_ref[...]
    gates = _dot_b(xc, wg_ref[...])
    r = jax.nn.sigmoid(gates[:, 0:w] + ba_ref[...])
    ig = jax.nn.sigmoid(gates[:, w:2 * w] + bx_ref[...])
    log_a = -LRU_C * r * _softplus(-lp_ref[...])
    a = jnp.exp(log_a)
    b = jnp.sqrt(1.0 - jnp.exp(2.0 * log_a)) * ig * xc
    if cfg.masked:
        real = _real_mask(cfg, n)
        a = jnp.where(real, a, 1.0)
        b = jnp.where(real, b, 0.0)
    if inj_ref is not None:
        b = b + inj_ref[...]
    ris = _row_in_seq(cfg, n)
    d = 1
    while d < min(n, cfg.seq_rows):
        has = ris >= d
        a_sh = jnp.where(has, pltpu.roll(a, d, 0), 1.0)
        b_sh = jnp.where(has, pltpu.roll(b, d, 0), 0.0)
        b = b + a * b_sh
        a = a * a_sh
        d *= 2
    hs = b + a * hc_ref[...]
    if cfg.long_seq:
        hc_ref[...] = hs[n - 1:n, :]
    hs_ref[...] = hs
    ob_ref[...] = hs * _gelu(z_ref[:, w:2 * w])


def _lru_call(z_b, inj, cw, cb, wg, ba, bx, lp, cfg):
    rows = z_b.shape[0]
    w = B_W
    row = lambda width: pl.BlockSpec((SUPER_BLOCK, width), lambda i: (i, 0))
    ins = [z_b] + ([] if cfg.long_seq else [inj]) + [cw, cb, wg, ba, bx, lp]
    specs = [row(ZB_W)] + ([] if cfg.long_seq else [row(w)]) + [_const_spec(a.shape) for a in ins[-6:]]
    return pl.pallas_call(
        functools.partial(_lru_kernel, cfg=cfg),
        grid=(rows // SUPER_BLOCK,),
        in_specs=specs,
        out_specs=[row(w), row(w)],
        out_shape=[jax.ShapeDtypeStruct((rows, w), F32)] * 2,
        scratch_shapes=[pltpu.VMEM((8 + SUPER_BLOCK, w), F32), pltpu.VMEM((1, w), F32)],
        compiler_params=_params(1),
        name="rglru",
    )(*ins)


def _dn_kernel(z_ref, cw_ref, alog_ref, dtb_ref, g_ref, s0_ref, ltri_ref, ones_ref, tile_ref, tile_t_ref,
               bd_ref, expb_ref, expa_ref,
               o_ref, sout_ref,
               st_ref, prev_ref, q_s, k_s, v_s, g_s, eg_s, beta_s, kt_s, egl_s, o_s, *, cfg):
    sb = pl.program_id(0)
    c = cfg.chunk
    n = SUPER_BLOCK
    nch = cfg.n_chunks
    w = C_W

    @pl.when(sb % cfg.sb_per_seq == 0)
    def _():
        for p in range(cfg.n_par):
            prev_ref[p, 0:8, :] = jnp.zeros((8, prev_ref.shape[2]), F32)

    ones = ones_ref[...]
    for p in range(cfg.n_par):
        r = slice(p * n, (p + 1) * n)
        qkv = _silu(_causal_conv(z_ref[p, :, 0:3 * w], prev_ref.at[p], cw_ref))
        q = qkv[:, 0:w]
        kx = qkv[:, w:2 * w]
        q = q * lax.rsqrt(_dot_sel(q * q, ones, passes=2) + EPS) * (HEAD_DIM ** -0.5)
        kx = kx * lax.rsqrt(_dot_sel(kx * kx, ones, passes=2) + EPS)
        pc = z_ref[p, :, 4 * w:4 * w + LANES]
        beta = _dot_sel(jax.nn.sigmoid(pc), expb_ref[...], passes=2)
        gdec = _dot_sel(-jnp.exp(alog_ref[...]) * _softplus(pc + dtb_ref[...]), expa_ref[...], passes=2)
        if cfg.masked:
            real = _real_mask(cfg, n)
            beta = jnp.where(real, beta, 0.0)
            gdec = jnp.where(real, gdec, 0.0)
        gcum = _sel_dot(ltri_ref[...], gdec)
        g3 = gcum.reshape(nch, c, w)
        gl3 = g3[:, c - 1:c, :]
        q_s[r, :] = q
        k_s[r, :] = kx
        v_s[r, :] = qkv[:, 2 * w:3 * w]
        g_s[r, :] = gcum
        eg_s[r, :] = jnp.exp(gcum)
        beta_s[r, :] = beta
        kt_s[r, :] = (kx.reshape(nch, c, w) * jnp.exp(gl3 - g3)).reshape(n, w)
        egl_s[p * nch:(p + 1) * nch, :] = jnp.exp(gl3).reshape(nch, w)

    iio = lax.broadcasted_iota(jnp.int32, (c, 1), 0)
    bd = bd_ref[...]

    def chunk(k, p, st):
        rows = pl.ds(pl.multiple_of(p * n + k * c, c), c)
        q_c = q_s[rows, :]
        k_c = k_s[rows, :]
        g_c = g_s[rows, :]
        eg_c = eg_s[rows, :]
        beta_c = beta_s[rows, :]
        qk_s = _state_matmul(jnp.concatenate([q_c, k_c], axis=0), st, _NN)
        q_st = qk_s[0:c, :]
        k_st = qk_s[c:2 * c, :]
        lo = min(c, 8)
        tail = lambda x: x[lo:c, :]
        d_rows = ([k_c * k_c[j:j + 1, :] for j in range(lo)] + [q_c * k_c[j:j + 1, :] for j in range(lo)]
                  + [tail(k_c) * k_c[j:j + 1, :] for j in range(lo, c)]
                  + [tail(q_c) * k_c[j:j + 1, :] for j in range(lo, c)])
        dots = _head_sums(jnp.concatenate(d_rows, axis=0), ones_ref)
        e = beta_c * (v_s[rows, :] - eg_c * k_st)
        o = eg_c * q_st
        for j in range(lo):
            dec = jnp.exp(jnp.minimum(g_c - g_c[j:j + 1, :], 0.0))
            m_col = jnp.where(iio > j, beta_c * dots[j * c:(j + 1) * c, :] * dec, 0.0)
            e_j = e[j:j + 1, :]
            e = e - m_col * e_j
            qk_col = jnp.where(iio >= j, dots[(lo + j) * c:(lo + j + 1) * c, :] * dec, 0.0)
            o = o + qk_col * e_j
        if c > lo:
            nt = c - lo
            e_t, o_t, g_t, beta_t = tail(e), tail(o), tail(g_c), tail(beta_c)
            base = 2 * lo * c
            for j in range(lo, c):
                dec = jnp.exp(jnp.minimum(g_t - g_c[j:j + 1, :], 0.0))
                kk = dots[base + (j - lo) * nt:base + (j - lo + 1) * nt, :]
                qk = dots[base + (nt + j - lo) * nt:base + (nt + j - lo + 1) * nt, :]
                e_j = e_t[j - lo:j - lo + 1, :]
                e_t = e_t - jnp.where(iio[0:nt] > j - lo, beta_t * kk * dec, 0.0) * e_j
                o_t = o_t + jnp.where(iio[0:nt] >= j - lo, qk * dec, 0.0) * e_j
            e = jnp.concatenate([e[0:lo, :], e_t], axis=0)
            o = jnp.concatenate([o[0:lo, :], o_t], axis=0)
        o_s[rows, :] = o
        return _state_update(st, egl_s[pl.ds(p * nch + k, 1), :], kt_s[rows, :], e, bd_ref)

    _run_chunks(cfg, sb, chunk, st_ref, s0_ref, sout_ref, tile_ref, tile_t_ref, bd_ref)
    for p in range(cfg.n_par):
        o_ref[p] = _head_norm_gate(o_s[p * n:(p + 1) * n, :], z_ref[p, :, 3 * w:4 * w], g_ref[...], ones_ref)


def _dn_call(z_c, cw, alog_exp, dtb_exp, g_exp, s0, cfg):
    rows = z_c.shape[0]
    w = C_W
    cst = _mixer_consts(C_HEADS, cfg)
    lane_head = np.arange(w) // HEAD_DIM
    expb = (np.arange(128)[:, None] == lane_head[None, :]).astype(np.float32)
    expa = (np.arange(128)[:, None] == (lane_head[None, :] + C_HEADS)).astype(np.float32)
    expb = jnp.asarray(expb, BF16)
    expa = jnp.asarray(expa, BF16)
    n_seq = s0.shape[0]
    par_rows = cfg.n_par * SUPER_BLOCK
    vm = lambda shape: pltpu.VMEM(shape, F32)
    z3 = _seq_view(z_c, cfg)
    oc, s_new = pl.pallas_call(
        functools.partial(_dn_kernel, cfg=cfg),
        grid=(rows // par_rows,),
        in_specs=[_seq_block_spec(cfg, ZC_W), _const_spec(cw.shape),
                  _const_spec((1, LANES)), _const_spec((1, LANES)), _const_spec((1, w)), _state_specs(cfg, w),
                  _const_spec(cst["ltri"].shape), _const_spec(cst["ones"].shape),
                  _const_spec(cst["tile"].shape), _const_spec(cst["tile_t"].shape),
                  _const_spec(cst["bd"].shape), _const_spec(expb.shape), _const_spec(expa.shape)],
        out_specs=[_seq_block_spec(cfg, w), _state_specs(cfg, w)],
        out_shape=[jax.ShapeDtypeStruct(z3.shape[:2] + (w,), F32), jax.ShapeDtypeStruct((n_seq, w, HEAD_DIM), F32)],
        scratch_shapes=[vm((cfg.n_par, w // LANES, LANES, LANES)), vm((cfg.n_par, 8 + SUPER_BLOCK, 3 * w))]
        + [vm((par_rows, w))] * 7
        + [vm((cfg.n_par * cfg.n_chunks, w)), vm((par_rows, w))],
        compiler_params=_params(1),
        name="deltanet",
    )(z3, cw, alog_exp, dtb_exp, g_exp, s0, cst["ltri"], cst["ones"], cst["tile"], cst["tile_t"],
      cst["bd"], expb, expa)
    return oc.reshape(rows, w), s_new


def _outproj_kernel(x_ref, oa_ref, ob_ref, oc_ref, gt_ref, g_ref, sc_ref, sh_ref,
                    wo_ref, wq_ref, x1_ref, h2_ref, q_ref):
    mo = _dot_b(oa_ref[...], wo_ref[0:A_W, :])
    mo = mo + _dot_b(ob_ref[...], wo_ref[A_W:A_W + B_W, :])
    mo = mo + _dot_b(oc_ref[...], wo_ref[A_W + B_W:, :])
    x1 = x_ref[...] + gt_ref[...] * mo
    x1_ref[...] = x1
    h2 = _rmsnorm(x1, g_ref[...]) * sc_ref[...] + sh_ref[...]
    h2_ref[...] = h2.astype(BF16)
    q_ref[...] = _dot_b(h2, wq_ref[...])


def _outproj_call(x, oa, ob, oc, gt, g, sc, sh, wo, wq, tiles_per_group):
    t = x.shape[0]
    tm = TOKEN_TILE
    row = lambda w: pl.BlockSpec((tm, w), lambda i: (i, 0))
    return pl.pallas_call(
        _outproj_kernel,
        grid=(t // tm,),
        in_specs=[row(D_MODEL), row(A_W), row(B_W), row(C_W), _mod_spec(gt, tiles_per_group),
                  _const_spec((1, D_MODEL)), _mod_spec(sc, tiles_per_group), _mod_spec(sh, tiles_per_group),
                  _const_spec(wo.shape), _const_spec(wq.shape)],
        out_specs=[row(D_MODEL), row(D_MODEL), row(D_MODEL)],
        out_shape=[jax.ShapeDtypeStruct((t, D_MODEL), F32), jax.ShapeDtypeStruct((t, D_MODEL), BF16),
                   jax.ShapeDtypeStruct((t, D_MODEL), F32)],
        compiler_params=_params(1),
        name="outproj",
    )(x, oa, ob, oc, gt, g, sc, sh, wo, wq)


def _top16(s):
    vals = []
    for r in range(PEER_TOPK):
        m = jnp.max(s, axis=0, keepdims=True)
        s = jnp.where(s == m, -(32.0 + r) * 2.0 ** 95, s)
        vals.append(m)
    rank = jnp.where(s <= -(2.0 ** 99), s * -(2.0 ** -95) - 32.0, float(PEER_TOPK))
    return jnp.concatenate(vals, axis=0), rank


def _peer_kernel(h2_ref, q_ref, x1_ref, gt_ref, fg_ref, k1_ref, k2_ref, u_ref, vt_ref, o_ref,
                 cnt_s, e1_s, rank2_s, e2_s, acc_s, s1_s, s2_s, coef_s, *, final, n_steps):
    g = pl.program_id(0)
    tt = PEER_TOKEN_TILE
    nk = PEER_NKEYS
    nj = PEER_N // PEER_EXPERT_BLOCK
    a_per_step = PEER_EXPERT_BLOCK // nk
    j = jnp.minimum(g, n_steps - 2) % nj
    gb = jnp.maximum(g - 1, 0)

    @pl.when(g == 0)
    def _():
        coef_s[...] = jnp.zeros_like(coef_s)

    @pl.when(gb % nj == 0)
    def _():
        acc_s[...] = jnp.zeros_like(acc_s)

    @pl.when(jnp.logical_and(g % nj == 0, g < n_steps - 1))
    def _():

        def head(h, carry):
            qh = q_ref[:, pl.ds(pl.multiple_of(h * nk, nk), nk)]
            s1_s[...] = _dot_x3(k1_ref[...], qh, _NT)
            s2_s[...] = _dot_x3(k2_ref[...], qh, _NT)

            def lane_tile(c, carry2):
                cols = pl.ds(pl.multiple_of(c * PEER_ROUTE_TILE, PEER_ROUTE_TILE), PEER_ROUTE_TILE)
                s1 = s1_s[:, cols]
                s2 = s2_s[:, cols]
                v1, rank1 = _top16(s1)
                v2, rank2 = _top16(s2)
                c3 = v1[:, None, :] + v2[None, :, :]
                jrow = lax.broadcasted_iota(jnp.int32, (8, 1), 0)
                parts = [v1[0:1, :] + v2, v1[1:2, :] + v2[0:8, :]]
                for i in range(2, 8):
                    parts.append(jnp.where(jrow < PEER_TOPK // (i + 1), v1[i:i + 1, :] + v2[0:8, :], -jnp.inf))
                parts.append(v1[8:16, :] + v2[0:1, :])
                cand = jnp.concatenate(parts, axis=0)
                m = None
                for _ in range(PEER_TOPK):
                    m = jnp.max(cand, axis=0, keepdims=True)
                    cand = jnp.where(cand == m, -jnp.inf, cand)
                tau = m
                sel = c3 >= tau[None, :, :]
                m0 = v1[0:1, :] + v2[0:1, :]
                zsum = jnp.sum(jnp.where(sel, jnp.exp(c3 - m0[None, :, :]), 0.0).reshape(-1, PEER_ROUTE_TILE),
                               axis=0, keepdims=True)
                n_i = jnp.sum(jnp.where(sel, 1.0, 0.0), axis=1)
                rank1 = rank1.astype(BF16)
                cnt = jnp.zeros((nk, PEER_ROUTE_TILE), BF16)
                for i in range(PEER_TOPK):
                    cnt = cnt + jnp.where(rank1 == i, n_i[i:i + 1, :].astype(BF16), jnp.zeros((), BF16))
                cnt_s[h, :, cols] = cnt.astype(F32)
                e1_s[h, :, cols] = jnp.exp(s1 - v1[0:1, :])
                rank2_s[h, :, cols] = rank2.astype(BF16)
                e2_s[h, :, cols] = (jnp.exp(s2 - v2[0:1, :]) * (0.5 / zsum)).astype(BF16)
                return carry2

            lax.fori_loop(0, tt // PEER_ROUTE_TILE, lane_tile, 0)
            return carry

        lax.fori_loop(0, PEER_HEADS, head, 0)

    h2 = h2_ref[...]
    n_sub = PEER_EXPERT_BLOCK // PEER_SUB_BLOCK
    a_per_sub = PEER_SUB_BLOCK // nk
    sub = lambda s: slice(s * PEER_SUB_BLOCK, (s + 1) * PEER_SUB_BLOCK)
    cur = coef_s.at[g % 2]
    prev = coef_s.at[(g + 1) % 2]
    hids = [_dot(u_ref[sub(s), :], h2, _NT).astype(BF16) for s in range(n_sub)]
    acc_s[...] += _dot(vt_ref[...], prev[...])
    pack = 16
    for s in range(n_sub):
        for al in range(a_per_sub):
            a = j * a_per_step + s * a_per_sub + al
            wsum = jnp.zeros((nk // pack, pack, tt), BF16)
            for h in range(PEER_HEADS):
                cnt_row = jnp.broadcast_to(cnt_s[h, pl.ds(a, 1), :], (pack, tt)).astype(BF16)
                e1_row = jnp.broadcast_to(e1_s[h, pl.ds(a, 1), :], (pack, tt)).astype(BF16)
                rank2 = rank2_s[h].reshape(nk // pack, pack, tt)
                e2 = e2_s[h].reshape(nk // pack, pack, tt)
                wsum = jnp.where(rank2 < cnt_row[None], wsum + e2 * e1_row[None], wsum)
            r0 = s * PEER_SUB_BLOCK + al * nk
            cur[r0:r0 + nk, :] = _gelu_x2(hids[s][al * nk:(al + 1) * nk, :]) * wsum.reshape(nk, tt)

    @pl.when(jnp.logical_and(g > 0, gb % nj == nj - 1))
    def _():
        x2 = x1_ref[...] + gt_ref[...] * acc_s[...].T
        if final:
            x2 = _rmsnorm(x2, fg_ref[...])
        o_ref[...] = x2


def _peer_call(h2b, q, x1, gt, fg, k1p, k2p, u_b, v_b, layer, tiles_per_group, final):
    t = h2b.shape[0]
    tt = PEER_TOKEN_TILE
    nb = PEER_EXPERT_BLOCK
    nj = PEER_N // nb
    n_steps = (t // tt) * nj + 1
    front = lambda g: jnp.minimum(g, n_steps - 2)
    back = lambda g: jnp.maximum(g - 1, 0)
    row_f = lambda w: pl.BlockSpec((tt, w), lambda g: (front(g) // nj, 0))
    row_b = lambda w: pl.BlockSpec((tt, w), lambda g: (back(g) // nj, 0))
    _, r, w = gt.shape
    tab = lambda dt: pltpu.VMEM((PEER_HEADS, PEER_NKEYS, tt), dt)
    return pl.pallas_call(
        functools.partial(_peer_kernel, final=final, n_steps=n_steps),
        grid=(n_steps,),
        in_specs=[row_f(D_MODEL), row_f(D_MODEL), row_b(D_MODEL),
                  pl.BlockSpec((None, r, w), lambda g: (back(g) // nj // tiles_per_group, 0, 0)),
                  _const_spec((1, D_MODEL)),
                  _const_spec((PEER_NKEYS, PEER_NKEYS)),
                  _const_spec((PEER_NKEYS, PEER_NKEYS)),
                  pl.BlockSpec((None, nb, D_MODEL), lambda g: (layer, front(g) % nj, 0)),
                  pl.BlockSpec((None, D_MODEL, nb), lambda g: (layer, 0, back(g) % nj))],
        out_specs=row_b(D_MODEL),
        out_shape=jax.ShapeDtypeStruct((t, D_MODEL), F32),
        scratch_shapes=[tab(F32), tab(F32), tab(BF16), tab(BF16), pltpu.VMEM((D_MODEL, tt), F32),
                        pltpu.VMEM((PEER_NKEYS, tt), F32), pltpu.VMEM((PEER_NKEYS, tt), F32),
                        pltpu.VMEM((2, nb, tt), BF16)],
        compiler_params=_params(1),
        name="peer",
    )(h2b, q, x1, gt, fg, k1p, k2p, u_b, v_b)


def _block_diag(wblk):
    n, d, e = wblk.shape
    eye = jnp.eye(n, dtype=wblk.dtype)
    return (eye[:, None, :, None] * wblk[:, :, None, :]).reshape(n * d, n * e)


def _hilo(w):
    hi = w.astype(BF16)
    return hi, (w - hi.astype(F32)).astype(BF16)


def _pad_sample(z, hist=None):
    bsz = z.shape[0] // 4
    w = z.shape[1]
    z3 = z.reshape(bsz, 4, w)
    h3 = jnp.zeros((bsz, SAMPLE_HIST, w), F32)
    if hist is not None:
        h3 = h3.at[:, :, :hist.shape[-1]].set(hist)
    return jnp.concatenate([h3, z3, jnp.zeros((bsz, 1, w), F32)], axis=1).reshape(bsz * SAMPLE_SEQ_ROWS, w)


def _unpad_sample(o):
    bsz = o.shape[0] // SAMPLE_SEQ_ROWS
    return o.reshape(bsz, SAMPLE_SEQ_ROWS, -1)[:, SAMPLE_HIST:SAMPLE_HIST + 4].reshape(bsz * 4, -1)


def _layer_weights(l, w_in, w_out, peer_wq, peer_k1, peer_k2, peer_u, peer_v, lru_wa, lru_wx):
    w_in_p = jnp.pad(w_in[l], ((0, 0), (0, N_IN_PAD - N_IN)))
    wih = w_in_p.astype(BF16)
    wo = w_out[l].astype(BF16)
    wq = peer_wq[l].astype(BF16)
    wg = jnp.concatenate([_block_diag(lru_wa[l]), _block_diag(lru_wx[l])], axis=1).astype(BF16)
    half = PEER_NKEYS // 2
    k1p = jnp.pad(peer_k1[l], ((0, 0), (0, half)))
    k2p = jnp.pad(peer_k2[l], ((0, 0), (half, 0)))
    return dict(wih=wih, wo=wo, wq=wq, wg=wg, k1p=k1p, k2p=k2p)


def _trunk(x, mods, states, is_prompt, p, lw, l, final_g):
    t = x.shape[0]
    if is_prompt:
        sh1, sc1, gt1, sh2, sc2, gt2 = mods
    else:
        sh1, sc1, gt1, sh2, sc2 = (m.reshape(-1, TOKEN_TILE, D_MODEL) for m in mods[:5])
        gt2 = mods[5].reshape(-1, PEER_TOKEN_TILE, D_MODEL)
    if is_prompt:
        cfg = PROMPT_CFG
        tiles_tok = 2048 // TOKEN_TILE
        tiles_peer = 2048 // PEER_TOKEN_TILE
        n_seq = t // 2048
    else:
        cfg = SAMPLE_CFG
        tiles_tok = 1
        tiles_peer = 1
        n_seq = t // 4
    row1 = lambda v: v.reshape(1, -1)
    z_a, z_b, z_c = _inproj_call(x, row1(p['norm1_g']), sc1, sh1, lw['wih'], tiles_tok)
    if is_prompt:
        s_a = jnp.zeros((n_seq, A_W, HEAD_DIM), F32)
        s_c = jnp.zeros((n_seq, C_W, HEAD_DIM), F32)
        za_m, zb_m, zc_m = z_a, z_b, z_c
        inj = None
    else:
        st_hgrn, st_lru_h, st_lru_conv, st_dn, st_dn_conv = states
        s_a = jnp.swapaxes(st_hgrn, -1, -2).reshape(n_seq, A_W, HEAD_DIM)
        s_c = st_dn.reshape(n_seq, C_W, HEAD_DIM)
        za_m = _pad_sample(z_a)
        zb_m = _pad_sample(z_b, st_lru_conv)
        zc_m = _pad_sample(z_c, st_dn_conv)
        inj = jnp.zeros((n_seq, SAMPLE_SEQ_ROWS, B_W), F32).at[:, 0].set(st_lru_h).reshape(-1, B_W)
    rep = lambda v, h: row1(jnp.tile(v, h))
    oa, s_a_new = _hgrn_call(za_m, p['lb_param'], rep(p['a_norm_g'], A_HEADS), s_a, cfg, l)
    ob, hs = _lru_call(zb_m, inj, p['lru_conv_w'], row1(p['lru_conv_b']), lw['wg'],
                       row1(p['lru_ba']), row1(p['lru_bx']), row1(p['lru_L']), cfg)
    decay_cols = lambda v: jnp.zeros((1, LANES), F32).at[0, C_HEADS:2 * C_HEADS].set(v)
    oc, s_c_new = _dn_call(zc_m, p['dn_conv_w'], decay_cols(p['dn_A_log']), decay_cols(p['dn_dt_bias']),
                           rep(p['dn_norm_g'], C_HEADS), s_c, cfg)
    if is_prompt:
        seq = 2048
        h_t = hs.reshape(n_seq, seq, B_W)[:, -1]
        buf_b = z_b.reshape(n_seq, seq, ZB_W)[:, -SAMPLE_HIST:, :B_W]
        buf_c = z_c.reshape(n_seq, seq, ZC_W)[:, -SAMPLE_HIST:, :3 * C_W]
    else:
        oa, ob, oc = _unpad_sample(oa), _unpad_sample(ob), _unpad_sample(oc)
        h_t = hs.reshape(n_seq, SAMPLE_SEQ_ROWS, B_W)[:, -1]
        buf_b = z_b.reshape(n_seq, 4, ZB_W)[:, 1:, :B_W]
        buf_c = z_c.reshape(n_seq, 4, ZC_W)[:, 1:, :3 * C_W]
    new_states = (jnp.swapaxes(s_a_new.reshape(n_seq, A_HEADS, HEAD_DIM, HEAD_DIM), -1, -2), h_t, buf_b,
                  s_c_new.reshape(n_seq, C_HEADS, HEAD_DIM, HEAD_DIM), buf_c)
    x1, h2b, q = _outproj_call(x, oa, ob, oc, gt1, row1(p['norm2_g']), sc2, sh2,
                               lw['wo'], lw['wq'], tiles_tok)
    x2 = _peer_call(h2b, q, x1, gt2, row1(final_g), lw['k1p'], lw['k2p'], lw['u_b'], lw['v_b'], l,
                    tiles_peer, final=(l == DEPTH - 1))
    return x2, new_states


def kernel(x_prompt, x_sample, state_hgrn, state_lru_h, state_lru_conv, state_dn, state_dn_conv,
           c_prompt, c_sample, w_ada, b_ada, norm1_g, norm2_g, w_in, lb_param, a_norm_g,
           lru_conv_w, lru_conv_b, lru_wa, lru_ba, lru_wx, lru_bx, lru_L,
           dn_conv_w, dn_A_log, dn_dt_bias, dn_norm_g, w_out,
           peer_wq, peer_k1, peer_k2, peer_u, peer_v, final_norm_g):
    n_p, seq, _ = x_prompt.shape
    n_s, dec_seq, _ = x_sample.shape
    mod = _ada_call(jnp.concatenate([c_prompt, c_sample], axis=0), w_ada, b_ada)
    xp = x_prompt.reshape(n_p * seq, D_MODEL)
    xs = x_sample.reshape(n_s * dec_seq, D_MODEL)
    sample_states = (state_hgrn, state_lru_h, state_lru_conv, state_dn, state_dn_conv)
    u_all = peer_u.astype(BF16)
    vt_all = jnp.swapaxes(peer_v, 1, 2).astype(BF16)
    p_new, s_new = [], []
    for l in range(DEPTH):
        p = dict(norm1_g=norm1_g[l], norm2_g=norm2_g[l], lb_param=lb_param, a_norm_g=a_norm_g[l],
                 lru_conv_w=lru_conv_w[l], lru_conv_b=lru_conv_b[l], lru_ba=lru_ba[l], lru_bx=lru_bx[l],
                 lru_L=lru_L[l], dn_conv_w=dn_conv_w[l], dn_A_log=dn_A_log[l], dn_dt_bias=dn_dt_bias[l],
                 dn_norm_g=dn_norm_g[l])
        lw = _layer_weights(l, w_in, w_out, peer_wq, peer_k1, peer_k2, peer_u, peer_v, lru_wa, lru_wx)
        lw['u_b'], lw['v_b'] = u_all, vt_all
        parts = jnp.split(mod[l], 6, axis=-1)
        parts = [m + 1.0 if i in (1, 4) else m for i, m in enumerate(parts)]
        mods_p = [m[:n_p].reshape(n_p, 1, D_MODEL) for m in parts]
        mods_s = [jnp.repeat(m[n_p:], dec_seq, axis=0) for m in parts]
        xp, st_p = _trunk(xp, mods_p, None, True, p, lw, l, final_norm_g)
        xs, st_s = _trunk(xs, mods_s, tuple(s[l] for s in sample_states), False, p, lw, l, final_norm_g)
        p_new.append(st_p)
        s_new.append(st_s)
    stack = lambda sts: [jnp.stack([s[i] for s in sts]) for i in range(5)]
    p_st = stack(p_new)
    s_st = stack(s_new)
    return (xp.reshape(n_p, seq, D_MODEL), xs.reshape(n_s, dec_seq, D_MODEL), *p_st, *s_st)
```

```python
import functools

import jax
import jax.numpy as jnp
import numpy as np
from jax import lax
from jax.experimental import pallas as pl
from jax.experimental.pallas import tpu as pltpu

F32 = jnp.float32
BF16 = jnp.bfloat16
HIGHEST = lax.Precision.HIGHEST

D_MODEL = 1024
DEPTH = 2
HEAD_DIM = 64
A_HEADS = 4
A_W = 256
B_W = 384
C_HEADS = 6
C_W = 384
LRU_C = 8.0
CONV_W = 4
N_IN = 3340
N_IN_PAD = 3456
ZA_W = 1024
ZB_W = 768
ZC_W = 1664
PEER_HEADS = 8
PEER_NKEYS = 128
PEER_TOPK = 16
PEER_N = PEER_NKEYS * PEER_NKEYS
EPS = 1e-6

LANES = 128
SUPER_BLOCK = 256
SAMPLE_SEQ_ROWS = 8
SAMPLE_HIST = CONV_W - 1
TOKEN_TILE = 512
PEER_TOKEN_TILE = 512
PEER_EXPERT_BLOCK = 2048
PEER_SUB_BLOCK = 512
PEER_ROUTE_TILE = 256
VMEM_LIMIT_BYTES = 56 * 1024 * 1024

_NN = (((1,), (0,)), ((), ()))
_NT = (((1,), (1,)), ((), ()))
_TN = (((0,), (0,)), ((), ()))


def _dot(a, b, dims=_NN, precision=None):
    return lax.dot_general(a, b, dims, precision=precision, preferred_element_type=F32)


def _split2(x):
    hi = x.astype(BF16)
    lo = (x - hi.astype(F32)).astype(BF16)
    return hi, lo


def _split3(x):
    hi = x.astype(BF16)
    r = x - hi.astype(F32)
    mid = r.astype(BF16)
    lo = (r - mid.astype(F32)).astype(BF16)
    return hi, mid, lo


def _dot_x3(a, b, dims=_NN):
    ah, al = _split2(a)
    bh, bl = _split2(b)
    return _dot(ah, bh, dims) + _dot(ah, bl, dims) + _dot(al, bh, dims)


def _dot_w(a, wh, wl, dims=_NN):
    ah, al = _split2(a)
    return _dot(ah, wh, dims) + _dot(al, wh, dims) + _dot(ah, wl, dims)


def _dot_b(a, b, dims=_NN):
    return _dot(a.astype(BF16), b.astype(BF16), dims)


def _dot_sel(a, sel, dims=_NN, passes=3):
    parts = _split3(a)[:passes]
    out = _dot(parts[0], sel, dims)
    for p in parts[1:]:
        out = out + _dot(p, sel, dims)
    return out


def _head_sums(x, ones_ref):
    ones = ones_ref[0:LANES, 0:LANES]
    xb = x.astype(BF16)
    return jnp.concatenate([_dot(xb[:, t:t + LANES], ones) for t in range(0, x.shape[1], LANES)], axis=1)


def _sel_dot(sel, b, passes=3):
    parts = _split3(b)[:passes]
    out = _dot(sel, parts[0])
    for p in parts[1:]:
        out = out + _dot(sel, p)
    return out


def _softplus(x):
    return jnp.maximum(x, 0.0) + jnp.log(1.0 + jnp.exp(-jnp.abs(x)))


def _silu(x):
    return x * jax.nn.sigmoid(x)


def _gelu_x2(x):
    c = float(np.sqrt(2.0 / np.pi))
    return x + x * jnp.tanh(x * (c + (0.044715 * c) * (x * x)))


def _gelu(x):
    return 0.5 * x * (1.0 + jnp.tanh(float(np.sqrt(2.0 / np.pi)) * (x + 0.044715 * (x * x * x))))


def _rmsnorm(x, g):
    return x * lax.rsqrt(jnp.mean(x * x, axis=-1, keepdims=True) + EPS) * g


def _const_spec(shape):
    nd = len(shape)
    return pl.BlockSpec(shape, lambda *_: (0,) * nd)


def _params(n_grid):
    return pltpu.CompilerParams(dimension_semantics=("arbitrary",) * n_grid,
                                vmem_limit_bytes=VMEM_LIMIT_BYTES)


def _ada_kernel(c_ref, w_ref, b_ref, o_ref):
    c = c_ref[...]
    o_ref[0] = _dot_x3(_silu(c), w_ref[0]) + b_ref[0]


def _ada_call(c_all, w_ada, b_ada):
    nb = c_all.shape[0]
    nt = 6 * D_MODEL // 1024
    return pl.pallas_call(
        _ada_kernel,
        grid=(DEPTH, nt),
        in_specs=[
            _const_spec((nb, D_MODEL)),
            pl.BlockSpec((1, D_MODEL, 1024), lambda l, j: (l, 0, j)),
            pl.BlockSpec((1, 1, 1024), lambda l, j: (l, 0, j)),
        ],
        out_specs=pl.BlockSpec((1, nb, 1024), lambda l, j: (l, 0, j)),
        out_shape=jax.ShapeDtypeStruct((DEPTH, nb, 6 * D_MODEL), F32),
        compiler_params=_params(2),
        name="ada_mod",
    )(c_all, w_ada, b_ada.reshape(DEPTH, 1, 6 * D_MODEL))


def _inproj_kernel(x_ref, g_ref, sc_ref, sh_ref, w_ref, za_ref, zb_ref, zc_ref):
    h = (_rmsnorm(x_ref[...], g_ref[...]) * sc_ref[...] + sh_ref[...]).astype(BF16)
    for o_ref, lo, hi in ((za_ref, 0, ZA_W), (zb_ref, ZA_W, ZA_W + ZB_W), (zc_ref, ZA_W + ZB_W, N_IN_PAD)):
        o_ref[...] = _dot(h, w_ref[:, lo:hi])


def _mod_spec(mod, tiles_per_group):
    _, r, w = mod.shape
    return pl.BlockSpec((None, r, w), lambda i: (i // tiles_per_group, 0, 0))


def _inproj_call(x, g, sc, sh, wh, tiles_per_group):
    t = x.shape[0]
    tm = TOKEN_TILE
    row = lambda w: pl.BlockSpec((tm, w), lambda i: (i, 0))
    return pl.pallas_call(
        _inproj_kernel,
        grid=(t // tm,),
        in_specs=[row(D_MODEL), _const_spec((1, D_MODEL)), _mod_spec(sc, tiles_per_group),
                  _mod_spec(sh, tiles_per_group), _const_spec(wh.shape)],
        out_specs=[row(ZA_W), row(ZB_W), row(ZC_W)],
        out_shape=[jax.ShapeDtypeStruct((t, ZA_W), F32), jax.ShapeDtypeStruct((t, ZB_W), F32),
                   jax.ShapeDtypeStruct((t, ZC_W), F32)],
        compiler_params=_params(1),
        name="inproj",
    )(x, g, sc, sh, wh)


class _SeqCfg:
    def __init__(self, seq_rows, hist, real, chunk, n_par):
        self.seq_rows = seq_rows
        self.n_par = n_par
        self.hist = hist
        self.real = real
        self.chunk = chunk
        self.masked = hist > 0 or hist + real < seq_rows
        self.n_chunks = SUPER_BLOCK // chunk
        self.long_seq = seq_rows > SUPER_BLOCK
        self.sb_per_seq = max(seq_rows // SUPER_BLOCK, 1)
        self.seq_per_sb = max(SUPER_BLOCK // seq_rows, 1)
        assert (self.long_seq or chunk == seq_rows) and (self.long_seq or n_par == 1)


PROMPT_CFG = _SeqCfg(seq_rows=2048, hist=0, real=2048, chunk=16, n_par=2)
SAMPLE_CFG = _SeqCfg(seq_rows=SAMPLE_SEQ_ROWS, hist=SAMPLE_HIST, real=4, chunk=SAMPLE_SEQ_ROWS, n_par=1)


def _row_in_seq(cfg, n_rows):
    r = lax.broadcasted_iota(jnp.int32, (n_rows, 1), 0)
    return r % min(cfg.seq_rows, SUPER_BLOCK)


def _real_mask(cfg, n_rows):
    r = _row_in_seq(cfg, n_rows)
    return (r >= cfg.hist) & (r < cfg.hist + cfg.real)


def _causal_conv(x, xp_ref, w_ref):
    n = x.shape[0]
    xp_ref[8:8 + n, :] = x
    y = x * w_ref[CONV_W - 1:CONV_W, :]
    for k in range(CONV_W - 1):
        s = CONV_W - 1 - k
        y = y + xp_ref[8 - s:8 - s + n, :] * w_ref[k:k + 1, :]
    xp_ref[0:8, :] = x[n - 8:n, :]
    return y


def _lane_tiles(x):
    return [x[:, t:t + LANES] for t in range(0, x.shape[1], LANES)]


def _expand_state(s_cat, tile_ref, bd_ref):
    tile = tile_ref[:, 0:LANES]
    bd = bd_ref[0:LANES, 0:LANES]
    return jnp.stack([_dot_sel(s_cat[r:r + LANES, :], tile, passes=2) * bd
                      for r in range(0, s_cat.shape[0], LANES)])


def _compress_state(s_tiles, tile_t_ref):
    tile_t = tile_t_ref[0:LANES, :]
    return jnp.concatenate([_dot_sel(s_tiles[t], tile_t, passes=2) for t in range(s_tiles.shape[0])], axis=0)


def _state_matmul(x, s_tiles, dims):
    return jnp.concatenate([_dot_b(xt, s_tiles[t], dims) for t, xt in enumerate(_lane_tiles(x))], axis=1)


def _state_update(s_tiles, decay_row, a, b, bd_ref):
    bd = bd_ref[0:LANES, 0:LANES]
    return jnp.stack([s_tiles[t] * dt + bd * _dot_b(at, bt, _TN)
                      for t, (dt, at, bt) in enumerate(zip(_lane_tiles(decay_row), _lane_tiles(a), _lane_tiles(b)))])


def _head_norm_gate(o, z, g, ones_ref):
    ms = _dot_sel(o * o, ones_ref[...], passes=2) * (1.0 / HEAD_DIM)
    return o * lax.rsqrt(ms + EPS) * g * _silu(z)


def _run_chunks(cfg, sb, step, st_ref, s0_ref, sout_ref, tile_ref, tile_t_ref, bd_ref):
    if cfg.long_seq:
        @pl.when(sb % cfg.sb_per_seq == 0)
        def _():
            for p in range(cfg.n_par):
                st_ref[p] = _expand_state(s0_ref[p], tile_ref, bd_ref)

        def body(k, carry):
            for p in range(cfg.n_par):
                st_ref[p] = step(k, p, st_ref[p])
            return carry

        lax.fori_loop(0, cfg.n_chunks, body, 0)

        @pl.when(sb % cfg.sb_per_seq == cfg.sb_per_seq - 1)
        def _():
            for p in range(cfg.n_par):
                sout_ref[p] = _compress_state(st_ref[p], tile_t_ref)
    else:
        def body(k, carry):
            st_new = step(k, 0, _expand_state(s0_ref[k], tile_ref, bd_ref))
            sout_ref[k] = _compress_state(st_new, tile_t_ref)
            return carry

        lax.fori_loop(0, cfg.n_chunks, body, 0, unroll=2)


def _hgrn_kernel(z_ref, lbp_ref, g_ref, s0_ref, ltri_ref, ones_ref, tile_ref, tile_t_ref, bd_ref,
                 o_ref, sout_ref,
                 st_ref, q_s, k_s, g_s, qe_s, kt_s, egl_s, o_s, *, cfg, layer):
    sb = pl.program_id(0)
    c = cfg.chunk
    n = SUPER_BLOCK
    nch = cfg.n_chunks
    w = A_W
    lbp = lbp_ref[...]
    e = jnp.exp(lbp - jnp.max(lbp, axis=0, keepdims=True))
    lbs = e / jnp.sum(e, axis=0, keepdims=True)
    lb = jnp.sum(lbs[0:layer + 1, :], axis=0, keepdims=True) - lbs[0:1, :]

    for p in range(cfg.n_par):
        r = slice(p * n, (p + 1) * n)
        aq = z_ref[p, :, 0:w]
        af = z_ref[p, :, w:2 * w]
        f = lb + (1.0 - lb) * jax.nn.sigmoid(af)
        lf = jnp.log(f)
        kk = 1.0 - f
        if cfg.masked:
            real = _real_mask(cfg, n)
            lf = jnp.where(real, lf, 0.0)
            kk = jnp.where(real, kk, 0.0)
        q = _silu(aq) * (HEAD_DIM ** -0.5)
        gcum = _sel_dot(ltri_ref[...], lf)
        g3 = gcum.reshape(nch, c, w)
        gl3 = g3[:, c - 1:c, :]
        q_s[r, :] = q
        k_s[r, :] = kk
        g_s[r, :] = gcum
        qe_s[r, :] = q * jnp.exp(gcum)
        kt_s[r, :] = (kk.reshape(nch, c, w) * jnp.exp(gl3 - g3)).reshape(n, w)
        egl_s[p * nch:(p + 1) * nch, :] = jnp.exp(gl3).reshape(nch, w)

    jio = lax.broadcasted_iota(jnp.int32, (c, 1), 0)
    bd = bd_ref[...]

    def chunk(k, p, st):
        rows = pl.ds(pl.multiple_of(p * n + k * c, c), c)
        q_c = q_s[rows, :]
        k_c = k_s[rows, :]
        g_c = g_s[rows, :]
        v_c = z_ref[p, pl.ds(pl.multiple_of(k * c, c), c), 2 * w:3 * w]
        o_inter = _state_matmul(qe_s[rows, :], st, _NT)
        lo = min(c, 8)
        d_rows = []
        for i in range(c):
            nj = lo if i < lo else c
            dec = jnp.exp(jnp.minimum(g_c[i:i + 1, :] - g_c[0:nj, :], 0.0))
            d_rows.append(jnp.where(jio[0:nj] <= i, k_c[0:nj, :] * dec * q_c[i:i + 1, :], 0.0))
        a_b = _head_sums(jnp.concatenate(d_rows, axis=0), ones_ref)
        o_diag = jnp.sum(a_b[0:lo * lo, :].reshape(lo, lo, w) * v_c[None, 0:lo, :], axis=1)
        if c > lo:
            o_hi = jnp.sum(a_b[lo * lo:, :].reshape(c - lo, c, w) * v_c[None, :, :], axis=1)
            o_diag = jnp.concatenate([o_diag, o_hi], axis=0)
        o_s[rows, :] = o_inter + o_diag
        return _state_update(st, egl_s[pl.ds(p * nch + k, 1), :], v_c, kt_s[rows, :], bd_ref)

    _run_chunks(cfg, sb, chunk, st_ref, s0_ref, sout_ref, tile_ref, tile_t_ref, bd_ref)
    for p in range(cfg.n_par):
        o_ref[p] = _head_norm_gate(o_s[p * n:(p + 1) * n, :], z_ref[p, :, 3 * w:4 * w], g_ref[...], ones_ref)


def _mixer_consts(heads, cfg):
    w = heads * HEAD_DIM
    lane_head = np.arange(w) // HEAD_DIM
    ones = (lane_head[:, None] == lane_head[None, :]).astype(np.float32)
    tile = (np.arange(HEAD_DIM)[:, None] == (np.arange(w) % HEAD_DIM)[None, :]).astype(np.float32)
    r = np.arange(SUPER_BLOCK)
    ltri = ((r[:, None] // cfg.chunk == r[None, :] // cfg.chunk) & (r[None, :] <= r[:, None])).astype(np.float32)
    return dict(ltri=jnp.asarray(ltri, BF16), ones=jnp.asarray(ones, BF16), tile=jnp.asarray(tile, BF16),
                tile_t=jnp.asarray(tile.T, BF16), bd=jnp.asarray(ones, F32))


def _state_specs(cfg, w):
    blk = (cfg.n_par if cfg.long_seq else cfg.seq_per_sb, w, HEAD_DIM)
    if cfg.long_seq:
        imap = lambda i: (i // cfg.sb_per_seq, 0, 0)
    else:
        imap = lambda i: (i, 0, 0)
    return pl.BlockSpec(blk, imap)


def _seq_view(x, cfg):
    if not cfg.long_seq:
        return x[None]
    assert x.shape[0] % (cfg.n_par * cfg.seq_rows) == 0, "sequence count must be a multiple of n_par"
    return x.reshape(-1, cfg.seq_rows, x.shape[-1])


def _seq_block_spec(cfg, width):
    if cfg.long_seq:
        return pl.BlockSpec((cfg.n_par, SUPER_BLOCK, width), lambda i: (i // cfg.sb_per_seq, i % cfg.sb_per_seq, 0))
    return pl.BlockSpec((1, SUPER_BLOCK, width), lambda i: (0, i, 0))


def _hgrn_call(z_a, lb_param, g_exp, s0_t, cfg, layer):
    rows = z_a.shape[0]
    w = A_W
    cst = _mixer_consts(A_HEADS, cfg)
    n_seq = s0_t.shape[0]
    par_rows = cfg.n_par * SUPER_BLOCK
    vm = lambda shape: pltpu.VMEM(shape, F32)
    kern = functools.partial(_hgrn_kernel, cfg=cfg, layer=layer)
    z3 = _seq_view(z_a, cfg)
    oa, s_new = pl.pallas_call(
        kern,
        grid=(rows // par_rows,),
        in_specs=[_seq_block_spec(cfg, ZA_W), _const_spec((DEPTH, w)),
                  _const_spec((1, w)), _state_specs(cfg, w), _const_spec(cst["ltri"].shape),
                  _const_spec(cst["ones"].shape), _const_spec(cst["tile"].shape),
                  _const_spec(cst["tile_t"].shape), _const_spec(cst["bd"].shape)],
        out_specs=[_seq_block_spec(cfg, w), _state_specs(cfg, w)],
        out_shape=[jax.ShapeDtypeStruct(z3.shape[:2] + (w,), F32), jax.ShapeDtypeStruct((n_seq, w, HEAD_DIM), F32)],
        scratch_shapes=[vm((cfg.n_par, w // LANES, LANES, LANES))] + [vm((par_rows, w))] * 5
        + [vm((cfg.n_par * cfg.n_chunks, w)), vm((par_rows, w))],
        compiler_params=_params(1),
        name="hgrn2",
    )(z3, lb_param, g_exp, s0_t, cst["ltri"], cst["ones"], cst["tile"], cst["tile_t"], cst["bd"])
    return oa.reshape(rows, w), s_new


def _lru_kernel(*refs, cfg):
    if cfg.long_seq:
        (z_ref, cw_ref, cb_ref, wg_ref, ba_ref, bx_ref, lp_ref,
         ob_ref, hs_ref, prev_ref, hc_ref) = refs
        inj_ref = None
    else:
        (z_ref, inj_ref, cw_ref, cb_ref, wg_ref, ba_ref, bx_ref, lp_ref,
         ob_ref, hs_ref, prev_ref, hc_ref) = refs
    sb = pl.program_id(0)
    n = SUPER_BLOCK
    w = B_W

    @pl.when(sb % cfg.sb_per_seq == 0)
    def _():
        prev_ref[0:8, :] = jnp.zeros((8, prev_ref.shape[1]), F32)
        hc_ref[...] = jnp.zeros_like(hc_ref)

    xc = _causal_conv(z_ref[:, 0:w], prev_ref, cw_ref) + cb_ref[...]
    gates = _dot_b(xc, wg_ref[...])
    r = jax.nn.sigmoid(gates[:, 0:w] + ba_ref[...])
    ig = jax.nn.sigmoid(gates[:, w:2 * w] + bx_ref[...])
    log_a = -LRU_C * r * _softplus(-lp_ref[...])
    a = jnp.exp(log_a)
    b = jnp.sqrt(1.0 - jnp.exp(2.0 * log_a)) * ig * xc
    if cfg.masked:
        real = _real_mask(cfg, n)
        a = jnp.where(real, a, 1.0)
        b = jnp.where(real, b, 0.0)
    if inj_ref is not None:
        b = b + inj_ref[...]
    group = 8
    assert cfg.seq_rows % group == 0
    rig = lax.broadcasted_iota(jnp.int32, (n, 1), 0) % group
    d = 1
    while d < group:
        has = rig >= d
        a_sh = jnp.where(has, pltpu.roll(a, d, 0), 1.0)
        b_sh = jnp.where(has, pltpu.roll(b, d, 0), 0.0)
        b = b + a * b_sh
        a = a * a_sh
        d *= 2
    if cfg.long_seq:
        a3 = a.reshape(n // group, group, w)
        b3 = b.reshape(n // group, group, w)
        carry = hc_ref[...]
        rows = []
        for i in range(n // group):
            h_i = b3[i] + a3[i] * carry
            rows.append(h_i)
            carry = h_i[group - 1:group, :]
        hs = jnp.concatenate(rows, axis=0)
        hc_ref[...] = carry
    else:
        hs = b
    hs_ref[...] = hs
    ob_ref[...] = hs * _gelu(z_ref[:, w:2 * w])


def _lru_call(z_b, inj, cw, cb, wg, ba, bx, lp, cfg):
    rows = z_b.shape[0]
    w = B_W
    row = lambda width: pl.BlockSpec((SUPER_BLOCK, width), lambda i: (i, 0))
    ins = [z_b] + ([] if cfg.long_seq else [inj]) + [cw, cb, wg, ba, bx, lp]
    specs = [row(ZB_W)] + ([] if cfg.long_seq else [row(w)]) + [_const_spec(a.shape) for a in ins[-6:]]
    return pl.pallas_call(
        functools.partial(_lru_kernel, cfg=cfg),
        grid=(rows // SUPER_BLOCK,),
        in_specs=specs,
        out_specs=[row(w), row(w)],
        out_shape=[jax.ShapeDtypeStruct((rows, w), F32)] * 2,
        scratch_shapes=[pltpu.VMEM((8 + SUPER_BLOCK, w), F32), pltpu.VMEM((1, w), F32)],
        compiler_params=_params(1),
        name="rglru",
    )(*ins)


def _dn_kernel(z_ref, cw_ref, alog_ref, dtb_ref, g_ref, s0_ref, ltri_ref, ones_ref, tile_ref, tile_t_ref,
               bd_ref, expb_ref, expa_ref,
               o_ref, sout_ref,
               st_ref, prev_ref, q_s, k_s, v_s, g_s, eg_s, beta_s, kt_s, egl_s, o_s, *, cfg):
    sb = pl.program_id(0)
    c = cfg.chunk
    n = SUPER_BLOCK
    nch = cfg.n_chunks
    w = C_W

    @pl.when(sb % cfg.sb_per_seq == 0)
    def _():
        for p in range(cfg.n_par):
            prev_ref[p, 0:8, :] = jnp.zeros((8, prev_ref.shape[2]), F32)

    ones = ones_ref[...]
    for p in range(cfg.n_par):
        r = slice(p * n, (p + 1) * n)
        qkv = _silu(_causal_conv(z_ref[p, :, 0:3 * w], prev_ref.at[p], cw_ref))
        q = qkv[:, 0:w]
        kx = qkv[:, w:2 * w]
        q = q * lax.rsqrt(_dot_sel(q * q, ones, passes=2) + EPS) * (HEAD_DIM ** -0.5)
        kx = kx * lax.rsqrt(_dot_sel(kx * kx, ones, passes=2) + EPS)
        pc = z_ref[p, :, 4 * w:4 * w + LANES]
        beta = _dot_sel(jax.nn.sigmoid(pc), expb_ref[...], passes=2)
        gdec = _dot_sel(-jnp.exp(alog_ref[...]) * _softplus(pc + dtb_ref[...]), expa_ref[...], passes=2)
        if cfg.masked:
            real = _real_mask(cfg, n)
            beta = jnp.where(real, beta, 0.0)
            gdec = jnp.where(real, gdec, 0.0)
        gcum = _sel_dot(ltri_ref[...], gdec)
        g3 = gcum.reshape(nch, c, w)
        gl3 = g3[:, c - 1:c, :]
        q_s[r, :] = q
        k_s[r, :] = kx
        v_s[r, :] = qkv[:, 2 * w:3 * w]
        g_s[r, :] = gcum
        eg_s[r, :] = jnp.exp(gcum)
        beta_s[r, :] = beta
        kt_s[r, :] = (kx.reshape(nch, c, w) * jnp.exp(gl3 - g3)).reshape(n, w)
        egl_s[p * nch:(p + 1) * nch, :] = jnp.exp(gl3).reshape(nch, w)

    iio = lax.broadcasted_iota(jnp.int32, (c, 1), 0)
    bd = bd_ref[...]

    def chunk(k, p, st):
        rows = pl.ds(pl.multiple_of(p * n + k * c, c), c)
        q_c = q_s[rows, :]
        k_c = k_s[rows, :]
        g_c = g_s[rows, :]
        eg_c = eg_s[rows, :]
        beta_c = beta_s[rows, :]
        qk_s = _state_matmul(jnp.concatenate([q_c, k_c], axis=0), st, _NN)
        q_st = qk_s[0:c, :]
        k_st = qk_s[c:2 * c, :]
        lo = min(c, 8)
        tail = lambda x: x[lo:c, :]
        d_rows = ([k_c * k_c[j:j + 1, :] for j in range(lo)] + [q_c * k_c[j:j + 1, :] for j in range(lo)]
                  + [tail(k_c) * k_c[j:j + 1, :] for j in range(lo, c)]
                  + [tail(q_c) * k_c[j:j + 1, :] for j in range(lo, c)])
        dots = _head_sums(jnp.concatenate(d_rows, axis=0), ones_ref)
        e = beta_c * (v_s[rows, :] - eg_c * k_st)
        o = eg_c * q_st
        for j in range(lo):
            dec = jnp.exp(jnp.minimum(g_c - g_c[j:j + 1, :], 0.0))
            m_col = jnp.where(iio > j, beta_c * dots[j * c:(j + 1) * c, :] * dec, 0.0)
            e_j = e[j:j + 1, :]
            e = e - m_col * e_j
            qk_col = jnp.where(iio >= j, dots[(lo + j) * c:(lo + j + 1) * c, :] * dec, 0.0)
            o = o + qk_col * e_j
        if c > lo:
            nt = c - lo
            e_t, o_t, g_t, beta_t = tail(e), tail(o), tail(g_c), tail(beta_c)
            base = 2 * lo * c
            for j in range(lo, c):
                dec = jnp.exp(jnp.minimum(g_t - g_c[j:j + 1, :], 0.0))
                kk = dots[base + (j - lo) * nt:base + (j - lo + 1) * nt, :]
                qk = dots[base + (nt + j - lo) * nt:base + (nt + j - lo + 1) * nt, :]
                e_j = e_t[j - lo:j - lo + 1, :]
                e_t = e_t - jnp.where(iio[0:nt] > j - lo, beta_t * kk * dec, 0.0) * e_j
                o_t = o_t + jnp.where(iio[0:nt] >= j - lo, qk * dec, 0.0) * e_j
            e = jnp.concatenate([e[0:lo, :], e_t], axis=0)
            o = jnp.concatenate([o[0:lo, :], o_t], axis=0)
        o_s[rows, :] = o
        return _state_update(st, egl_s[pl.ds(p * nch + k, 1), :], kt_s[rows, :], e, bd_ref)

    _run_chunks(cfg, sb, chunk, st_ref, s0_ref, sout_ref, tile_ref, tile_t_ref, bd_ref)
    for p in range(cfg.n_par):
        o_ref[p] = _head_norm_gate(o_s[p * n:(p + 1) * n, :], z_ref[p, :, 3 * w:4 * w], g_ref[...], ones_ref)


def _dn_call(z_c, cw, alog_exp, dtb_exp, g_exp, s0, cfg):
    rows = z_c.shape[0]
    w = C_W
    cst = _mixer_consts(C_HEADS, cfg)
    lane_head = np.arange(w) // HEAD_DIM
    expb = (np.arange(128)[:, None] == lane_head[None, :]).astype(np.float32)
    expa = (np.arange(128)[:, None] == (lane_head[None, :] + C_HEADS)).astype(np.float32)
    expb = jnp.asarray(expb, BF16)
    expa = jnp.asarray(expa, BF16)
    n_seq = s0.shape[0]
    par_rows = cfg.n_par * SUPER_BLOCK
    vm = lambda shape: pltpu.VMEM(shape, F32)
    z3 = _seq_view(z_c, cfg)
    oc, s_new = pl.pallas_call(
        functools.partial(_dn_kernel, cfg=cfg),
        grid=(rows // par_rows,),
        in_specs=[_seq_block_spec(cfg, ZC_W), _const_spec(cw.shape),
                  _const_spec((1, LANES)), _const_spec((1, LANES)), _const_spec((1, w)), _state_specs(cfg, w),
                  _const_spec(cst["ltri"].shape), _const_spec(cst["ones"].shape),
                  _const_spec(cst["tile"].shape), _const_spec(cst["tile_t"].shape),
                  _const_spec(cst["bd"].shape), _const_spec(expb.shape), _const_spec(expa.shape)],
        out_specs=[_seq_block_spec(cfg, w), _state_specs(cfg, w)],
        out_shape=[jax.ShapeDtypeStruct(z3.shape[:2] + (w,), F32), jax.ShapeDtypeStruct((n_seq, w, HEAD_DIM), F32)],
        scratch_shapes=[vm((cfg.n_par, w // LANES, LANES, LANES)), vm((cfg.n_par, 8 + SUPER_BLOCK, 3 * w))]
        + [vm((par_rows, w))] * 7
        + [vm((cfg.n_par * cfg.n_chunks, w)), vm((par_rows, w))],
        compiler_params=_params(1),
        name="deltanet",
    )(z3, cw, alog_exp, dtb_exp, g_exp, s0, cst["ltri"], cst["ones"], cst["tile"], cst["tile_t"],
      cst["bd"], expb, expa)
    return oc.reshape(rows, w), s_new


def _outproj_kernel(x_ref, oa_ref, ob_ref, oc_ref, gt_ref, g_ref, sc_ref, sh_ref,
                    wo_ref, wq_ref, x1_ref, h2_ref, q_ref):
    mo = _dot_b(oa_ref[...], wo_ref[0:A_W, :])
    mo = mo + _dot_b(ob_ref[...], wo_ref[A_W:A_W + B_W, :])
    mo = mo + _dot_b(oc_ref[...], wo_ref[A_W + B_W:, :])
    x1 = x_ref[...] + gt_ref[...] * mo
    x1_ref[...] = x1
    h2 = _rmsnorm(x1, g_ref[...]) * sc_ref[...] + sh_ref[...]
    h2_ref[...] = h2.astype(BF16)
    q_ref[...] = _dot_b(h2, wq_ref[...])


def _outproj_call(x, oa, ob, oc, gt, g, sc, sh, wo, wq, tiles_per_group):
    t = x.shape[0]
    tm = TOKEN_TILE
    row = lambda w: pl.BlockSpec((tm, w), lambda i: (i, 0))
    return pl.pallas_call(
        _outproj_kernel,
        grid=(t // tm,),
        in_specs=[row(D_MODEL), row(A_W), row(B_W), row(C_W), _mod_spec(gt, tiles_per_group),
                  _const_spec((1, D_MODEL)), _mod_spec(sc, tiles_per_group), _mod_spec(sh, tiles_per_group),
                  _const_spec(wo.shape), _const_spec(wq.shape)],
        out_specs=[row(D_MODEL), row(D_MODEL), row(D_MODEL)],
        out_shape=[jax.ShapeDtypeStruct((t, D_MODEL), F32), jax.ShapeDtypeStruct((t, D_MODEL), BF16),
                   jax.ShapeDtypeStruct((t, D_MODEL), F32)],
        compiler_params=_params(1),
        name="outproj",
    )(x, oa, ob, oc, gt, g, sc, sh, wo, wq)


def _top16(s):
    vals = []
    for r in range(PEER_TOPK):
        m = jnp.max(s, axis=0, keepdims=True)
        s = jnp.where(s == m, -(32.0 + r) * 2.0 ** 95, s)
        vals.append(m)
    rank = jnp.where(s <= -(2.0 ** 99), s * -(2.0 ** -95) - 32.0, float(PEER_TOPK))
    return jnp.concatenate(vals, axis=0), rank


def _route_tile(s1, s2):
    nk, t = s1.shape
    v1, rank1 = _top16(s1)
    v2, rank2 = _top16(s2)
    c3 = v1[:, None, :] + v2[None, :, :]
    jrow = lax.broadcasted_iota(jnp.int32, (8, 1), 0)
    parts = [v1[0:1, :] + v2, v1[1:2, :] + v2[0:8, :]]
    for i in range(2, 8):
        parts.append(jnp.where(jrow < PEER_TOPK // (i + 1), v1[i:i + 1, :] + v2[0:8, :], -jnp.inf))
    parts.append(v1[8:16, :] + v2[0:1, :])
    cand = jnp.concatenate(parts, axis=0)
    m = None
    for _ in range(PEER_TOPK):
        m = jnp.max(cand, axis=0, keepdims=True)
        cand = jnp.where(cand == m, -jnp.inf, cand)
    tau = m
    sel = c3 >= tau[None, :, :]
    m0 = v1[0:1, :] + v2[0:1, :]
    zsum = jnp.sum(jnp.where(sel, jnp.exp(c3 - m0[None, :, :]), 0.0).reshape(-1, t), axis=0, keepdims=True)
    n_i = jnp.sum(jnp.where(sel, 1.0, 0.0), axis=1)
    rank1 = rank1.astype(BF16)
    cnt = jnp.zeros((nk, t), BF16)
    for i in range(PEER_TOPK):
        cnt = cnt + jnp.where(rank1 == i, n_i[i:i + 1, :].astype(BF16), jnp.zeros((), BF16))
    return (cnt.astype(F32), jnp.exp(s1 - v1[0:1, :]), rank2.astype(BF16),
            (jnp.exp(s2 - v2[0:1, :]) * (0.5 / zsum)).astype(BF16))


def _peer_kernel(h2_ref, q_ref, x1_ref, gt_ref, fg_ref, k1_ref, k2_ref, u_ref, vt_ref, o_ref,
                 cnt_s, e1_s, rank2_s, e2_s, acc_s, s1_s, s2_s, coef_s, *, final, n_steps):
    g = pl.program_id(0)
    tt = PEER_TOKEN_TILE
    rt = PEER_ROUTE_TILE
    nk = PEER_NKEYS
    nj = PEER_N // PEER_EXPERT_BLOCK
    a_per_step = PEER_EXPERT_BLOCK // nk
    j = jnp.minimum(g, n_steps - 2) % nj
    gb = jnp.maximum(g - 1, 0)

    @pl.when(g == 0)
    def _():
        coef_s[...] = jnp.zeros_like(coef_s)

    @pl.when(gb % nj == 0)
    def _():
        acc_s[...] = jnp.zeros_like(acc_s)

    @pl.when(jnp.logical_and(g % nj == 0, g < n_steps - 1))
    def _():

        def head(h, carry):
            qh = q_ref[:, pl.ds(pl.multiple_of(h * nk, nk), nk)]
            s1_s[...] = _dot_x3(k1_ref[...], qh, _NT)
            s2_s[...] = _dot_x3(k2_ref[...], qh, _NT)

            def lane_tile(c, carry2):
                cols = pl.ds(pl.multiple_of(c * rt, rt), rt)
                (cnt_s[h, :, cols], e1_s[h, :, cols], rank2_s[h, :, cols],
                 e2_s[h, :, cols]) = _route_tile(s1_s[:, cols], s2_s[:, cols])
                return carry2

            lax.fori_loop(0, tt // rt, lane_tile, 0)
            return carry

        lax.fori_loop(0, PEER_HEADS, head, 0)

    h2 = h2_ref[...]
    n_sub = PEER_EXPERT_BLOCK // PEER_SUB_BLOCK
    a_per_sub = PEER_SUB_BLOCK // nk
    sub = lambda s: slice(s * PEER_SUB_BLOCK, (s + 1) * PEER_SUB_BLOCK)
    cur = coef_s.at[g % 2]
    prev = coef_s.at[(g + 1) % 2]
    hids = [_dot(u_ref[sub(s), :], h2, _NT).astype(BF16) for s in range(n_sub)]
    acc_s[...] += _dot(vt_ref[...], prev[...])
    pack = 16
    for s in range(n_sub):
        for al in range(a_per_sub):
            a = j * a_per_step + s * a_per_sub + al
            wsum = jnp.zeros((nk // pack, pack, tt), BF16)
            for h in range(PEER_HEADS):
                cnt_row = jnp.broadcast_to(cnt_s[h, pl.ds(a, 1), :], (pack, tt)).astype(BF16)
                e1_row = jnp.broadcast_to(e1_s[h, pl.ds(a, 1), :], (pack, tt)).astype(BF16)
                rank2 = rank2_s[h].reshape(nk // pack, pack, tt)
                e2 = e2_s[h].reshape(nk // pack, pack, tt)
                wsum = jnp.where(rank2 < cnt_row[None], wsum + e2 * e1_row[None], wsum)
            r0 = s * PEER_SUB_BLOCK + al * nk
            cur[r0:r0 + nk, :] = _gelu_x2(hids[s][al * nk:(al + 1) * nk, :]) * wsum.reshape(nk, tt)

    @pl.when(jnp.logical_and(g > 0, gb % nj == nj - 1))
    def _():
        x2 = x1_ref[...] + gt_ref[...] * acc_s[...].T
        if final:
            x2 = _rmsnorm(x2, fg_ref[...])
        o_ref[...] = x2


def _peer_call(h2b, q, x1, gt, fg, k1p, k2p, u_b, v_b, layer, tiles_per_group, final):
    t = h2b.shape[0]
    tt = PEER_TOKEN_TILE
    nb = PEER_EXPERT_BLOCK
    nj = PEER_N // nb
    n_tiles = t // tt
    n_steps = n_tiles * nj + 1
    front = lambda g: jnp.minimum(g, n_steps - 2)
    back = lambda g: jnp.maximum(g - 1, 0)
    row_f = lambda w: pl.BlockSpec((tt, w), lambda g: (front(g) // nj, 0))
    row_b = lambda w: pl.BlockSpec((tt, w), lambda g: (back(g) // nj, 0))
    _, r, w = gt.shape
    tab = lambda dt: pltpu.VMEM((PEER_HEADS, PEER_NKEYS, tt), dt)
    return pl.pallas_call(
        functools.partial(_peer_kernel, final=final, n_steps=n_steps),
        grid=(n_steps,),
        in_specs=[row_f(D_MODEL), row_f(D_MODEL), row_b(D_MODEL),
                  pl.BlockSpec((None, r, w), lambda g: (back(g) // nj // tiles_per_group, 0, 0)),
                  _const_spec((1, D_MODEL)),
                  _const_spec((PEER_NKEYS, PEER_NKEYS)),
                  _const_spec((PEER_NKEYS, PEER_NKEYS)),
                  pl.BlockSpec((None, nb, D_MODEL), lambda g: (layer, front(g) % nj, 0)),
                  pl.BlockSpec((None, D_MODEL, nb), lambda g: (layer, 0, back(g) % nj))],
        out_specs=row_b(D_MODEL),
        out_shape=jax.ShapeDtypeStruct((t, D_MODEL), F32),
        scratch_shapes=[tab(F32), tab(F32), tab(BF16), tab(BF16), pltpu.VMEM((D_MODEL, tt), F32),
                        pltpu.VMEM((PEER_NKEYS, tt), F32), pltpu.VMEM((PEER_NKEYS, tt), F32),
                        pltpu.VMEM((2, nb, tt), BF16)],
        compiler_params=_params(1),
        name="peer",
    )(h2b, q, x1, gt, fg, k1p, k2p, u_b, v_b)


def _block_diag(wblk):
    n, d, e = wblk.shape
    eye = jnp.eye(n, dtype=wblk.dtype)
    return (eye[:, None, :, None] * wblk[:, :, None, :]).reshape(n * d, n * e)


def _hilo(w):
    hi = w.astype(BF16)
    return hi, (w - hi.astype(F32)).astype(BF16)


def _pad_sample(z, hist=None):
    bsz = z.shape[0] // 4
    w = z.shape[1]
    z3 = z.reshape(bsz, 4, w)
    h3 = jnp.zeros((bsz, SAMPLE_HIST, w), F32)
    if hist is not None:
        h3 = h3.at[:, :, :hist.shape[-1]].set(hist)
    return jnp.concatenate([h3, z3, jnp.zeros((bsz, 1, w), F32)], axis=1).reshape(bsz * SAMPLE_SEQ_ROWS, w)


def _unpad_sample(o):
    bsz = o.shape[0] // SAMPLE_SEQ_ROWS
    return o.reshape(bsz, SAMPLE_SEQ_ROWS, -1)[:, SAMPLE_HIST:SAMPLE_HIST + 4].reshape(bsz * 4, -1)


def _layer_weights(l, w_in, w_out, peer_wq, peer_k1, peer_k2, peer_u, peer_v, lru_wa, lru_wx):
    w_in_p = jnp.pad(w_in[l], ((0, 0), (0, N_IN_PAD - N_IN)))
    wih = w_in_p.astype(BF16)
    wo = w_out[l].astype(BF16)
    wq = peer_wq[l].astype(BF16)
    wg = jnp.concatenate([_block_diag(lru_wa[l]), _block_diag(lru_wx[l])], axis=1).astype(BF16)
    half = PEER_NKEYS // 2
    k1p = jnp.pad(peer_k1[l], ((0, 0), (0, half)))
    k2p = jnp.pad(peer_k2[l], ((0, 0), (half, 0)))
    return dict(wih=wih, wo=wo, wq=wq, wg=wg, k1p=k1p, k2p=k2p)


def _trunk(x, mods, states, is_prompt, p, lw, l, final_g):
    t = x.shape[0]
    if is_prompt:
        sh1, sc1, gt1, sh2, sc2, gt2 = mods
    else:
        sh1, sc1, gt1, sh2, sc2 = (m.reshape(-1, TOKEN_TILE, D_MODEL) for m in mods[:5])
        gt2 = mods[5].reshape(-1, PEER_TOKEN_TILE, D_MODEL)
    if is_prompt:
        cfg = PROMPT_CFG
        tiles_tok = 2048 // TOKEN_TILE
        tiles_peer = 2048 // PEER_TOKEN_TILE
        n_seq = t // 2048
    else:
        cfg = SAMPLE_CFG
        tiles_tok = 1
        tiles_peer = 1
        n_seq = t // 4
    row1 = lambda v: v.reshape(1, -1)
    z_a, z_b, z_c = _inproj_call(x, row1(p['norm1_g']), sc1, sh1, lw['wih'], tiles_tok)
    if is_prompt:
        s_a = jnp.zeros((n_seq, A_W, HEAD_DIM), F32)
        s_c = jnp.zeros((n_seq, C_W, HEAD_DIM), F32)
        za_m, zb_m, zc_m = z_a, z_b, z_c
        inj = None
    else:
        st_hgrn, st_lru_h, st_lru_conv, st_dn, st_dn_conv = states
        s_a = jnp.swapaxes(st_hgrn, -1, -2).reshape(n_seq, A_W, HEAD_DIM)
        s_c = st_dn.reshape(n_seq, C_W, HEAD_DIM)
        za_m = _pad_sample(z_a)
        zb_m = _pad_sample(z_b, st_lru_conv)
        zc_m = _pad_sample(z_c, st_dn_conv)
        inj = jnp.zeros((n_seq, SAMPLE_SEQ_ROWS, B_W), F32).at[:, 0].set(st_lru_h).reshape(-1, B_W)
    rep = lambda v, h: row1(jnp.tile(v, h))
    oa, s_a_new = _hgrn_call(za_m, p['lb_param'], rep(p['a_norm_g'], A_HEADS), s_a, cfg, l)
    ob, hs = _lru_call(zb_m, inj, p['lru_conv_w'], row1(p['lru_conv_b']), lw['wg'],
                       row1(p['lru_ba']), row1(p['lru_bx']), row1(p['lru_L']), cfg)
    decay_cols = lambda v: jnp.zeros((1, LANES), F32).at[0, C_HEADS:2 * C_HEADS].set(v)
    oc, s_c_new = _dn_call(zc_m, p['dn_conv_w'], decay_cols(p['dn_A_log']), decay_cols(p['dn_dt_bias']),
                           rep(p['dn_norm_g'], C_HEADS), s_c, cfg)
    if is_prompt:
        seq = 2048
        h_t = hs.reshape(n_seq, seq, B_W)[:, -1]
        buf_b = z_b.reshape(n_seq, seq, ZB_W)[:, -SAMPLE_HIST:, :B_W]
        buf_c = z_c.reshape(n_seq, seq, ZC_W)[:, -SAMPLE_HIST:, :3 * C_W]
    else:
        oa, ob, oc = _unpad_sample(oa), _unpad_sample(ob), _unpad_sample(oc)
        h_t = hs.reshape(n_seq, SAMPLE_SEQ_ROWS, B_W)[:, -1]
        buf_b = z_b.reshape(n_seq, 4, ZB_W)[:, 1:, :B_W]
        buf_c = z_c.reshape(n_seq, 4, ZC_W)[:, 1:, :3 * C_W]
    new_states = (jnp.swapaxes(s_a_new.reshape(n_seq, A_HEADS, HEAD_DIM, HEAD_DIM), -1, -2), h_t, buf_b,
                  s_c_new.reshape(n_seq, C_HEADS, HEAD_DIM, HEAD_DIM), buf_c)
    x1, h2b, q = _outproj_call(x, oa, ob, oc, gt1, row1(p['norm2_g']), sc2, sh2,
                               lw['wo'], lw['wq'], tiles_tok)
    x2 = _peer_call(h2b, q, x1, gt2, row1(final_g), lw['k1p'], lw['k2p'], lw['u_b'], lw['v_b'], l,
                    tiles_peer, final=(l == DEPTH - 1))
    return x2, new_states


def kernel(x_prompt, x_sample, state_hgrn, state_lru_h, state_lru_conv, state_dn, state_dn_conv,
           c_prompt, c_sample, w_ada, b_ada, norm1_g, norm2_g, w_in, lb_param, a_norm_g,
           lru_conv_w, lru_conv_b, lru_wa, lru_ba, lru_wx, lru_bx, lru_L,
           dn_conv_w, dn_A_log, dn_dt_bias, dn_norm_g, w_out,
           peer_wq, peer_k1, peer_k2, peer_u, peer_v, final_norm_g):
    n_p, seq, _ = x_prompt.shape
    n_s, dec_seq, _ = x_sample.shape
    mod = _ada_call(jnp.concatenate([c_prompt, c_sample], axis=0), w_ada, b_ada)
    xp = x_prompt.reshape(n_p * seq, D_MODEL)
    xs = x_sample.reshape(n_s * dec_seq, D_MODEL)
    sample_states = (state_hgrn, state_lru_h, state_lru_conv, state_dn, state_dn_conv)
    u_all = peer_u.astype(BF16)
    vt_all = jnp.swapaxes(peer_v, 1, 2).astype(BF16)
    p_new, s_new = [], []
    for l in range(DEPTH):
        p = dict(norm1_g=norm1_g[l], norm2_g=norm2_g[l], lb_param=lb_param, a_norm_g=a_norm_g[l],
                 lru_conv_w=lru_conv_w[l], lru_conv_b=lru_conv_b[l], lru_ba=lru_ba[l], lru_bx=lru_bx[l],
                 lru_L=lru_L[l], dn_conv_w=dn_conv_w[l], dn_A_log=dn_A_log[l], dn_dt_bias=dn_dt_bias[l],
                 dn_norm_g=dn_norm_g[l])
        lw = _layer_weights(l, w_in, w_out, peer_wq, peer_k1, peer_k2, peer_u, peer_v, lru_wa, lru_wx)
        lw['u_b'], lw['v_b'] = u_all, vt_all
        parts = jnp.split(mod[l], 6, axis=-1)
        parts = [m + 1.0 if i in (1, 4) else m for i, m in enumerate(parts)]
        mods_p = [m[:n_p].reshape(n_p, 1, D_MODEL) for m in parts]
        mods_s = [jnp.repeat(m[n_p:], dec_seq, axis=0) for m in parts]
        xp, st_p = _trunk(xp, mods_p, None, True, p, lw, l, final_norm_g)
        xs, st_s = _trunk(xs, mods_s, tuple(s[l] for s in sample_states), False, p, lw, l, final_norm_g)
        p_new.append(st_p)
        s_new.append(st_s)
    stack = lambda sts: [jnp.stack([s[i] for s in sts]) for i in range(5)]
    p_st = stack(p_new)
    s_st = stack(s_new)
    return (xp.reshape(n_p, seq, D_MODEL), xs.reshape(n_s, dec_seq, D_MODEL), *p_st, *s_st)
```

```python
import functools

import jax
import jax.numpy as jnp
import numpy as np
from jax import lax
from jax.experimental import pallas as pl
from jax.experimental.pallas import tpu as pltpu

F32 = jnp.float32
BF16 = jnp.bfloat16
HIGHEST = lax.Precision.HIGHEST

D_MODEL = 1024
DEPTH = 2
HEAD_DIM = 64
A_HEADS = 4
A_W = 256
B_W = 384
C_HEADS = 6
C_W = 384
LRU_C = 8.0
CONV_W = 4
N_IN = 3340
N_IN_PAD = 3456
ZA_W = 1024
ZB_W = 768
ZC_W = 1664
PEER_HEADS = 8
PEER_NKEYS = 128
PEER_TOPK = 16
PEER_N = PEER_NKEYS * PEER_NKEYS
EPS = 1e-6

LANES = 128
SUPER_BLOCK = 256
SAMPLE_SEQ_ROWS = 8
SAMPLE_HIST = CONV_W - 1
TOKEN_TILE = 512
PEER_TOKEN_TILE = 512
PEER_EXPERT_BLOCK = 2048
PEER_SUB_BLOCK = 512
PEER_ROUTE_TILE = 256
VMEM_LIMIT_BYTES = 56 * 1024 * 1024

_NN = (((1,), (0,)), ((), ()))
_NT = (((1,), (1,)), ((), ()))
_TN = (((0,), (0,)), ((), ()))


def _dot(a, b, dims=_NN, precision=None):
    return lax.dot_general(a, b, dims, precision=precision, preferred_element_type=F32)


def _split2(x):
    hi = x.astype(BF16)
    lo = (x - hi.astype(F32)).astype(BF16)
    return hi, lo


def _split3(x):
    hi = x.astype(BF16)
    r = x - hi.astype(F32)
    mid = r.astype(BF16)
    lo = (r - mid.astype(F32)).astype(BF16)
    return hi, mid, lo


def _dot_x3(a, b, dims=_NN):
    ah, al = _split2(a)
    bh, bl = _split2(b)
    return _dot(ah, bh, dims) + _dot(ah, bl, dims) + _dot(al, bh, dims)


def _dot_w(a, wh, wl, dims=_NN):
    ah, al = _split2(a)
    return _dot(ah, wh, dims) + _dot(al, wh, dims) + _dot(ah, wl, dims)


def _dot_b(a, b, dims=_NN):
    return _dot(a.astype(BF16), b.astype(BF16), dims)


def _dot_sel(a, sel, dims=_NN, passes=3):
    parts = _split3(a)[:passes]
    out = _dot(parts[0], sel, dims)
    for p in parts[1:]:
        out = out + _dot(p, sel, dims)
    return out


def _head_sums(x, ones_ref):
    ones = ones_ref[0:LANES, 0:LANES]
    xb = x.astype(BF16)
    return jnp.concatenate([_dot(xb[:, t:t + LANES], ones) for t in range(0, x.shape[1], LANES)], axis=1)


def _sel_dot(sel, b, passes=3):
    parts = _split3(b)[:passes]
    out = _dot(sel, parts[0])
    for p in parts[1:]:
        out = out + _dot(sel, p)
    return out


def _softplus(x):
    return jnp.maximum(x, 0.0) + jnp.log(1.0 + jnp.exp(-jnp.abs(x)))


def _silu(x):
    return x * jax.nn.sigmoid(x)


def _gelu_x2(x):
    c = float(np.sqrt(2.0 / np.pi))
    return x + x * jnp.tanh(x * (c + (0.044715 * c) * (x * x)))


def _gelu(x):
    return 0.5 * x * (1.0 + jnp.tanh(float(np.sqrt(2.0 / np.pi)) * (x + 0.044715 * (x * x * x))))


def _rmsnorm(x, g):
    return x * lax.rsqrt(jnp.mean(x * x, axis=-1, keepdims=True) + EPS) * g


def _const_spec(shape):
    nd = len(shape)
    return pl.BlockSpec(shape, lambda *_: (0,) * nd)


def _params(n_grid):
    return pltpu.CompilerParams(dimension_semantics=("arbitrary",) * n_grid,
                                vmem_limit_bytes=VMEM_LIMIT_BYTES)


def _ada_kernel(c_ref, w_ref, b_ref, o_ref):
    c = c_ref[...]
    o_ref[0] = _dot_x3(_silu(c), w_ref[0]) + b_ref[0]


def _ada_call(c_all, w_ada, b_ada):
    nb = c_all.shape[0]
    nt = 6 * D_MODEL // 1024
    return pl.pallas_call(
        _ada_kernel,
        grid=(DEPTH, nt),
        in_specs=[
            _const_spec((nb, D_MODEL)),
            pl.BlockSpec((1, D_MODEL, 1024), lambda l, j: (l, 0, j)),
            pl.BlockSpec((1, 1, 1024), lambda l, j: (l, 0, j)),
        ],
        out_specs=pl.BlockSpec((1, nb, 1024), lambda l, j: (l, 0, j)),
        out_shape=jax.ShapeDtypeStruct((DEPTH, nb, 6 * D_MODEL), F32),
        compiler_params=_params(2),
        name="ada_mod",
    )(c_all, w_ada, b_ada.reshape(DEPTH, 1, 6 * D_MODEL))


def _inproj_kernel(x_ref, g_ref, sc_ref, sh_ref, w_ref, za_ref, zb_ref, zc_ref):
    h = (_rmsnorm(x_ref[...], g_ref[...]) * sc_ref[...] + sh_ref[...]).astype(BF16)
    for o_ref, lo, hi in ((za_ref, 0, ZA_W), (zb_ref, ZA_W, ZA_W + ZB_W), (zc_ref, ZA_W + ZB_W, N_IN_PAD)):
        o_ref[...] = _dot(h, w_ref[:, lo:hi])


def _mod_spec(mod, tiles_per_group):
    _, r, w = mod.shape
    return pl.BlockSpec((None, r, w), lambda i: (i // tiles_per_group, 0, 0))


def _inproj_call(x, g, sc, sh, wh, tiles_per_group):
    t = x.shape[0]
    tm = TOKEN_TILE
    row = lambda w: pl.BlockSpec((tm, w), lambda i: (i, 0))
    return pl.pallas_call(
        _inproj_kernel,
        grid=(t // tm,),
        in_specs=[row(D_MODEL), _const_spec((1, D_MODEL)), _mod_spec(sc, tiles_per_group),
                  _mod_spec(sh, tiles_per_group), _const_spec(wh.shape)],
        out_specs=[row(ZA_W), row(ZB_W), row(ZC_W)],
        out_shape=[jax.ShapeDtypeStruct((t, ZA_W), F32), jax.ShapeDtypeStruct((t, ZB_W), F32),
                   jax.ShapeDtypeStruct((t, ZC_W), F32)],
        compiler_params=_params(1),
        name="inproj",
    )(x, g, sc, sh, wh)


class _SeqCfg:
    def __init__(self, seq_rows, hist, real, chunk, n_par):
        self.seq_rows = seq_rows
        self.n_par = n_par
        self.hist = hist
        self.real = real
        self.chunk = chunk
        self.masked = hist > 0 or hist + real < seq_rows
        self.n_chunks = SUPER_BLOCK // chunk
        self.long_seq = seq_rows > SUPER_BLOCK
        self.sb_per_seq = max(seq_rows // SUPER_BLOCK, 1)
        self.seq_per_sb = max(SUPER_BLOCK // seq_rows, 1)
        assert (self.long_seq or chunk == seq_rows) and (self.long_seq or n_par == 1)


PROMPT_CFG = _SeqCfg(seq_rows=2048, hist=0, real=2048, chunk=16, n_par=4)
PROMPT_HGRN_CFG = _SeqCfg(seq_rows=2048, hist=0, real=2048, chunk=16, n_par=8)
SAMPLE_CFG = _SeqCfg(seq_rows=SAMPLE_SEQ_ROWS, hist=SAMPLE_HIST, real=4, chunk=SAMPLE_SEQ_ROWS, n_par=1)


def _row_in_seq(cfg, n_rows):
    r = lax.broadcasted_iota(jnp.int32, (n_rows, 1), 0)
    return r % min(cfg.seq_rows, SUPER_BLOCK)


def _real_mask(cfg, n_rows):
    r = _row_in_seq(cfg, n_rows)
    return (r >= cfg.hist) & (r < cfg.hist + cfg.real)


def _causal_conv(x, xp_ref, w_ref):
    n = x.shape[0]
    xp_ref[8:8 + n, :] = x
    y = x * w_ref[CONV_W - 1:CONV_W, :]
    for k in range(CONV_W - 1):
        s = CONV_W - 1 - k
        y = y + xp_ref[8 - s:8 - s + n, :] * w_ref[k:k + 1, :]
    xp_ref[0:8, :] = x[n - 8:n, :]
    return y


def _lane_tiles(x):
    return [x[:, t:t + LANES] for t in range(0, x.shape[1], LANES)]


def _expand_state(s_cat, tile_ref, bd_ref):
    tile = tile_ref[:, 0:LANES]
    bd = bd_ref[0:LANES, 0:LANES]
    return jnp.stack([_dot_sel(s_cat[r:r + LANES, :], tile, passes=2) * bd
                      for r in range(0, s_cat.shape[0], LANES)])


def _compress_state(s_tiles, tile_t_ref):
    tile_t = tile_t_ref[0:LANES, :]
    return jnp.concatenate([_dot_sel(s_tiles[t], tile_t, passes=2) for t in range(s_tiles.shape[0])], axis=0)


def _state_matmul(x, s_tiles, dims):
    return jnp.concatenate([_dot_b(xt, s_tiles[t], dims) for t, xt in enumerate(_lane_tiles(x))], axis=1)


def _state_update(s_tiles, decay_row, a, b, bd_ref):
    bd = bd_ref[0:LANES, 0:LANES]
    return jnp.stack([s_tiles[t] * dt + bd * _dot_b(at, bt, _TN)
                      for t, (dt, at, bt) in enumerate(zip(_lane_tiles(decay_row), _lane_tiles(a), _lane_tiles(b)))])


def _head_norm_gate(o, z, g, ones_ref):
    ms = _dot_sel(o * o, ones_ref[...], passes=2) * (1.0 / HEAD_DIM)
    return o * lax.rsqrt(ms + EPS) * g * _silu(z)


def _run_chunks(cfg, sb, step, st_ref, s0_ref, sout_ref, tile_ref, tile_t_ref, bd_ref):
    if cfg.long_seq:
        @pl.when(sb % cfg.sb_per_seq == 0)
        def _():
            for p in range(cfg.n_par):
                st_ref[p] = _expand_state(s0_ref[p], tile_ref, bd_ref)

        def body(k, carry):
            for p in range(cfg.n_par):
                st_ref[p] = step(k, p, st_ref[p])
            return carry

        lax.fori_loop(0, cfg.n_chunks, body, 0)

        @pl.when(sb % cfg.sb_per_seq == cfg.sb_per_seq - 1)
        def _():
            for p in range(cfg.n_par):
                sout_ref[p] = _compress_state(st_ref[p], tile_t_ref)
    else:
        def body(k, carry):
            st_new = step(k, 0, _expand_state(s0_ref[k], tile_ref, bd_ref))
            sout_ref[k] = _compress_state(st_new, tile_t_ref)
            return carry

        lax.fori_loop(0, cfg.n_chunks, body, 0, unroll=2)


def _hgrn_kernel(z_ref, lbp_ref, g_ref, s0_ref, ltri_ref, ones_ref, tile_ref, tile_t_ref, bd_ref,
                 o_ref, sout_ref,
                 st_ref, q_s, k_s, g_s, qe_s, kt_s, egl_s, o_s, *, cfg, layer):
    sb = pl.program_id(0)
    c = cfg.chunk
    n = SUPER_BLOCK
    nch = cfg.n_chunks
    w = A_W
    lbp = lbp_ref[...]
    e = jnp.exp(lbp - jnp.max(lbp, axis=0, keepdims=True))
    lbs = e / jnp.sum(e, axis=0, keepdims=True)
    lb = jnp.sum(lbs[0:layer + 1, :], axis=0, keepdims=True) - lbs[0:1, :]

    for p in range(cfg.n_par):
        r = slice(p * n, (p + 1) * n)
        aq = z_ref[p, :, 0:w]
        af = z_ref[p, :, w:2 * w]
        f = lb + (1.0 - lb) * jax.nn.sigmoid(af)
        lf = jnp.log(f)
        kk = 1.0 - f
        if cfg.masked:
            real = _real_mask(cfg, n)
            lf = jnp.where(real, lf, 0.0)
            kk = jnp.where(real, kk, 0.0)
        q = _silu(aq) * (HEAD_DIM ** -0.5)
        gcum = _sel_dot(ltri_ref[...], lf)
        g3 = gcum.reshape(nch, c, w)
        gl3 = g3[:, c - 1:c, :]
        q_s[r, :] = q
        k_s[r, :] = kk
        g_s[r, :] = gcum
        qe_s[r, :] = q * jnp.exp(gcum)
        kt_s[r, :] = (kk.reshape(nch, c, w) * jnp.exp(gl3 - g3)).reshape(n, w)
        egl_s[p * nch:(p + 1) * nch, :] = jnp.exp(gl3).reshape(nch, w)

    jio = lax.broadcasted_iota(jnp.int32, (c, 1), 0)
    bd = bd_ref[...]

    def chunk(k, p, st):
        rows = pl.ds(pl.multiple_of(p * n + k * c, c), c)
        q_c = q_s[rows, :]
        k_c = k_s[rows, :]
        g_c = g_s[rows, :]
        v_c = z_ref[p, pl.ds(pl.multiple_of(k * c, c), c), 2 * w:3 * w]
        o_inter = _state_matmul(qe_s[rows, :], st, _NT)
        lo = min(c, 8)
        d_rows = []
        for i in range(c):
            nj = lo if i < lo else c
            dec = jnp.exp(jnp.minimum(g_c[i:i + 1, :] - g_c[0:nj, :], 0.0))
            d_rows.append(jnp.where(jio[0:nj] <= i, k_c[0:nj, :] * dec * q_c[i:i + 1, :], 0.0))
        a_b = _head_sums(jnp.concatenate(d_rows, axis=0), ones_ref)
        o_diag = jnp.sum(a_b[0:lo * lo, :].reshape(lo, lo, w) * v_c[None, 0:lo, :], axis=1)
        if c > lo:
            o_hi = jnp.sum(a_b[lo * lo:, :].reshape(c - lo, c, w) * v_c[None, :, :], axis=1)
            o_diag = jnp.concatenate([o_diag, o_hi], axis=0)
        o_s[rows, :] = o_inter + o_diag
        return _state_update(st, egl_s[pl.ds(p * nch + k, 1), :], v_c, kt_s[rows, :], bd_ref)

    _run_chunks(cfg, sb, chunk, st_ref, s0_ref, sout_ref, tile_ref, tile_t_ref, bd_ref)
    for p in range(cfg.n_par):
        o_ref[p] = _head_norm_gate(o_s[p * n:(p + 1) * n, :], z_ref[p, :, 3 * w:4 * w], g_ref[...], ones_ref)


def _mixer_consts(heads, cfg):
    w = heads * HEAD_DIM
    lane_head = np.arange(w) // HEAD_DIM
    ones = (lane_head[:, None] == lane_head[None, :]).astype(np.float32)
    tile = (np.arange(HEAD_DIM)[:, None] == (np.arange(w) % HEAD_DIM)[None, :]).astype(np.float32)
    r = np.arange(SUPER_BLOCK)
    ltri = ((r[:, None] // cfg.chunk == r[None, :] // cfg.chunk) & (r[None, :] <= r[:, None])).astype(np.float32)
    return dict(ltri=jnp.asarray(ltri, BF16), ones=jnp.asarray(ones, BF16), tile=jnp.asarray(tile, BF16),
                tile_t=jnp.asarray(tile.T, BF16), bd=jnp.asarray(ones, F32))


def _state_specs(cfg, w):
    blk = (cfg.n_par if cfg.long_seq else cfg.seq_per_sb, w, HEAD_DIM)
    if cfg.long_seq:
        imap = lambda i: (i // cfg.sb_per_seq, 0, 0)
    else:
        imap = lambda i: (i, 0, 0)
    return pl.BlockSpec(blk, imap)


def _seq_view(x, cfg):
    if not cfg.long_seq:
        return x[None]
    assert x.shape[0] % (cfg.n_par * cfg.seq_rows) == 0, "sequence count must be a multiple of n_par"
    return x.reshape(-1, cfg.seq_rows, x.shape[-1])


def _seq_block_spec(cfg, width):
    if cfg.long_seq:
        return pl.BlockSpec((cfg.n_par, SUPER_BLOCK, width), lambda i: (i // cfg.sb_per_seq, i % cfg.sb_per_seq, 0))
    return pl.BlockSpec((1, SUPER_BLOCK, width), lambda i: (0, i, 0))


def _hgrn_call(z_a, lb_param, g_exp, s0_t, cfg, layer):
    rows = z_a.shape[0]
    w = A_W
    cst = _mixer_consts(A_HEADS, cfg)
    n_seq = s0_t.shape[0]
    par_rows = cfg.n_par * SUPER_BLOCK
    vm = lambda shape: pltpu.VMEM(shape, F32)
    kern = functools.partial(_hgrn_kernel, cfg=cfg, layer=layer)
    z3 = _seq_view(z_a, cfg)
    oa, s_new = pl.pallas_call(
        kern,
        grid=(rows // par_rows,),
        in_specs=[_seq_block_spec(cfg, ZA_W), _const_spec((DEPTH, w)),
                  _const_spec((1, w)), _state_specs(cfg, w), _const_spec(cst["ltri"].shape),
                  _const_spec(cst["ones"].shape), _const_spec(cst["tile"].shape),
                  _const_spec(cst["tile_t"].shape), _const_spec(cst["bd"].shape)],
        out_specs=[_seq_block_spec(cfg, w), _state_specs(cfg, w)],
        out_shape=[jax.ShapeDtypeStruct(z3.shape[:2] + (w,), F32), jax.ShapeDtypeStruct((n_seq, w, HEAD_DIM), F32)],
        scratch_shapes=[vm((cfg.n_par, w // LANES, LANES, LANES))] + [vm((par_rows, w))] * 5
        + [vm((cfg.n_par * cfg.n_chunks, w)), vm((par_rows, w))],
        compiler_params=_params(1),
        name="hgrn2",
    )(z3, lb_param, g_exp, s0_t, cst["ltri"], cst["ones"], cst["tile"], cst["tile_t"], cst["bd"])
    return oa.reshape(rows, w), s_new


def _lru_kernel(*refs, cfg):
    if cfg.long_seq:
        (z_ref, cw_ref, cb_ref, wg_ref, ba_ref, bx_ref, lp_ref,
         ob_ref, hs_ref, prev_ref, hc_ref) = refs
        inj_ref = None
    else:
        (z_ref, inj_ref, cw_ref, cb_ref, wg_ref, ba_ref, bx_ref, lp_ref,
         ob_ref, hs_ref, prev_ref, hc_ref) = refs
    sb = pl.program_id(0)
    n = SUPER_BLOCK
    w = B_W

    @pl.when(sb % cfg.sb_per_seq == 0)
    def _():
        prev_ref[0:8, :] = jnp.zeros((8, prev_ref.shape[1]), F32)
        hc_ref[...] = jnp.zeros_like(hc_ref)

    xc = _causal_conv(z_ref[:, 0:w], prev_ref, cw_ref) + cb_ref[...]
    gates = _dot_b(xc, wg_ref[...])
    r = jax.nn.sigmoid(gates[:, 0:w] + ba_ref[...])
    ig = jax.nn.sigmoid(gates[:, w:2 * w] + bx_ref[...])
    log_a = -LRU_C * r * _softplus(-lp_ref[...])
    a = jnp.exp(log_a)
    b = jnp.sqrt(1.0 - jnp.exp(2.0 * log_a)) * ig * xc
    if cfg.masked:
        real = _real_mask(cfg, n)
        a = jnp.where(real, a, 1.0)
        b = jnp.where(real, b, 0.0)
    if inj_ref is not None:
        b = b + inj_ref[...]
    group = 8
    assert cfg.seq_rows % group == 0
    rig = lax.broadcasted_iota(jnp.int32, (n, 1), 0) % group
    d = 1
    while d < group:
        has = rig >= d
        a_sh = jnp.where(has, pltpu.roll(a, d, 0), 1.0)
        b_sh = jnp.where(has, pltpu.roll(b, d, 0), 0.0)
        b = b + a * b_sh
        a = a * a_sh
        d *= 2
    if cfg.long_seq:
        a3 = a.reshape(n // group, group, w)
        b3 = b.reshape(n // group, group, w)
        carry = hc_ref[...]
        rows = []
        for i in range(n // group):
            h_i = b3[i] + a3[i] * carry
            rows.append(h_i)
            carry = h_i[group - 1:group, :]
        hs = jnp.concatenate(rows, axis=0)
        hc_ref[...] = carry
    else:
        hs = b
    hs_ref[...] = hs
    ob_ref[...] = hs * _gelu(z_ref[:, w:2 * w])


def _lru_call(z_b, inj, cw, cb, wg, ba, bx, lp, cfg):
    rows = z_b.shape[0]
    w = B_W
    row = lambda width: pl.BlockSpec((SUPER_BLOCK, width), lambda i: (i, 0))
    ins = [z_b] + ([] if cfg.long_seq else [inj]) + [cw, cb, wg, ba, bx, lp]
    specs = [row(ZB_W)] + ([] if cfg.long_seq else [row(w)]) + [_const_spec(a.shape) for a in ins[-6:]]
    return pl.pallas_call(
        functools.partial(_lru_kernel, cfg=cfg),
        grid=(rows // SUPER_BLOCK,),
        in_specs=specs,
        out_specs=[row(w), row(w)],
        out_shape=[jax.ShapeDtypeStruct((rows, w), F32)] * 2,
        scratch_shapes=[pltpu.VMEM((8 + SUPER_BLOCK, w), F32), pltpu.VMEM((1, w), F32)],
        compiler_params=_params(1),
        name="rglru",
    )(*ins)


def _dn_kernel(z_ref, cw_ref, alog_ref, dtb_ref, g_ref, s0_ref, ltri_ref, ones_ref, tile_ref, tile_t_ref,
               bd_ref, expb_ref, expa_ref,
               o_ref, sout_ref,
               st_ref, prev_ref, q_s, k_s, v_s, g_s, eg_s, beta_s, kt_s, egl_s, o_s, *, cfg):
    sb = pl.program_id(0)
    c = cfg.chunk
    n = SUPER_BLOCK
    nch = cfg.n_chunks
    w = C_W

    @pl.when(sb % cfg.sb_per_seq == 0)
    def _():
        for p in range(cfg.n_par):
            prev_ref[p, 0:8, :] = jnp.zeros((8, prev_ref.shape[2]), F32)

    ones = ones_ref[...]
    for p in range(cfg.n_par):
        r = slice(p * n, (p + 1) * n)
        qkv = _silu(_causal_conv(z_ref[p, :, 0:3 * w], prev_ref.at[p], cw_ref))
        q = qkv[:, 0:w]
        kx = qkv[:, w:2 * w]
        q = q * lax.rsqrt(_dot_sel(q * q, ones, passes=2) + EPS) * (HEAD_DIM ** -0.5)
        kx = kx * lax.rsqrt(_dot_sel(kx * kx, ones, passes=2) + EPS)
        pc = z_ref[p, :, 4 * w:4 * w + LANES]
        beta = _dot_sel(jax.nn.sigmoid(pc), expb_ref[...], passes=2)
        gdec = _dot_sel(-jnp.exp(alog_ref[...]) * _softplus(pc + dtb_ref[...]), expa_ref[...], passes=2)
        if cfg.masked:
            real = _real_mask(cfg, n)
            beta = jnp.where(real, beta, 0.0)
            gdec = jnp.where(real, gdec, 0.0)
        gcum = _sel_dot(ltri_ref[...], gdec)
        g3 = gcum.reshape(nch, c, w)
        gl3 = g3[:, c - 1:c, :]
        q_s[r, :] = q
        k_s[r, :] = kx
        v_s[r, :] = qkv[:, 2 * w:3 * w]
        g_s[r, :] = gcum
        eg_s[r, :] = jnp.exp(gcum)
        beta_s[r, :] = beta
        kt_s[r, :] = (kx.reshape(nch, c, w) * jnp.exp(gl3 - g3)).reshape(n, w)
        egl_s[p * nch:(p + 1) * nch, :] = jnp.exp(gl3).reshape(nch, w)

    iio = lax.broadcasted_iota(jnp.int32, (c, 1), 0)
    bd = bd_ref[...]

    def chunk(k, p, st):
        rows = pl.ds(pl.multiple_of(p * n + k * c, c), c)
        q_c = q_s[rows, :]
        k_c = k_s[rows, :]
        g_c = g_s[rows, :]
        eg_c = eg_s[rows, :]
        beta_c = beta_s[rows, :]
        qk_s = _state_matmul(jnp.concatenate([q_c, k_c], axis=0), st, _NN)
        q_st = qk_s[0:c, :]
        k_st = qk_s[c:2 * c, :]
        lo = min(c, 8)
        tail = lambda x: x[lo:c, :]
        d_rows = ([k_c * k_c[j:j + 1, :] for j in range(lo)] + [q_c * k_c[j:j + 1, :] for j in range(lo)]
                  + [tail(k_c) * k_c[j:j + 1, :] for j in range(lo, c)]
                  + [tail(q_c) * k_c[j:j + 1, :] for j in range(lo, c)])
        dots = _head_sums(jnp.concatenate(d_rows, axis=0), ones_ref)
        e = beta_c * (v_s[rows, :] - eg_c * k_st)
        o = eg_c * q_st
        for j in range(lo):
            dec = jnp.exp(jnp.minimum(g_c - g_c[j:j + 1, :], 0.0))
            m_col = jnp.where(iio > j, beta_c * dots[j * c:(j + 1) * c, :] * dec, 0.0)
            e_j = e[j:j + 1, :]
            e = e - m_col * e_j
            qk_col = jnp.where(iio >= j, dots[(lo + j) * c:(lo + j + 1) * c, :] * dec, 0.0)
            o = o + qk_col * e_j
        if c > lo:
            nt = c - lo
            e_t, o_t, g_t, beta_t = tail(e), tail(o), tail(g_c), tail(beta_c)
            base = 2 * lo * c
            for j in range(lo, c):
                dec = jnp.exp(jnp.minimum(g_t - g_c[j:j + 1, :], 0.0))
                kk = dots[base + (j - lo) * nt:base + (j - lo + 1) * nt, :]
                qk = dots[base + (nt + j - lo) * nt:base + (nt + j - lo + 1) * nt, :]
                e_j = e_t[j - lo:j - lo + 1, :]
                e_t = e_t - jnp.where(iio[0:nt] > j - lo, beta_t * kk * dec, 0.0) * e_j
                o_t = o_t + jnp.where(iio[0:nt] >= j - lo, qk * dec, 0.0) * e_j
            e = jnp.concatenate([e[0:lo, :], e_t], axis=0)
            o = jnp.concatenate([o[0:lo, :], o_t], axis=0)
        o_s[rows, :] = o
        return _state_update(st, egl_s[pl.ds(p * nch + k, 1), :], kt_s[rows, :], e, bd_ref)

    _run_chunks(cfg, sb, chunk, st_ref, s0_ref, sout_ref, tile_ref, tile_t_ref, bd_ref)
    for p in range(cfg.n_par):
        o_ref[p] = _head_norm_gate(o_s[p * n:(p + 1) * n, :], z_ref[p, :, 3 * w:4 * w], g_ref[...], ones_ref)


def _dn_call(z_c, cw, alog_exp, dtb_exp, g_exp, s0, cfg):
    rows = z_c.shape[0]
    w = C_W
    cst = _mixer_consts(C_HEADS, cfg)
    lane_head = np.arange(w) // HEAD_DIM
    expb = (np.arange(128)[:, None] == lane_head[None, :]).astype(np.float32)
    expa = (np.arange(128)[:, None] == (lane_head[None, :] + C_HEADS)).astype(np.float32)
    expb = jnp.asarray(expb, BF16)
    expa = jnp.asarray(expa, BF16)
    n_seq = s0.shape[0]
    par_rows = cfg.n_par * SUPER_BLOCK
    vm = lambda shape: pltpu.VMEM(shape, F32)
    z3 = _seq_view(z_c, cfg)
    oc, s_new = pl.pallas_call(
        functools.partial(_dn_kernel, cfg=cfg),
        grid=(rows // par_rows,),
        in_specs=[_seq_block_spec(cfg, ZC_W), _const_spec(cw.shape),
                  _const_spec((1, LANES)), _const_spec((1, LANES)), _const_spec((1, w)), _state_specs(cfg, w),
                  _const_spec(cst["ltri"].shape), _const_spec(cst["ones"].shape),
                  _const_spec(cst["tile"].shape), _const_spec(cst["tile_t"].shape),
                  _const_spec(cst["bd"].shape), _const_spec(expb.shape), _const_spec(expa.shape)],
        out_specs=[_seq_block_spec(cfg, w), _state_specs(cfg, w)],
        out_shape=[jax.ShapeDtypeStruct(z3.shape[:2] + (w,), F32), jax.ShapeDtypeStruct((n_seq, w, HEAD_DIM), F32)],
        scratch_shapes=[vm((cfg.n_par, w // LANES, LANES, LANES)), vm((cfg.n_par, 8 + SUPER_BLOCK, 3 * w))]
        + [vm((par_rows, w))] * 7
        + [vm((cfg.n_par * cfg.n_chunks, w)), vm((par_rows, w))],
        compiler_params=_params(1),
        name="deltanet",
    )(z3, cw, alog_exp, dtb_exp, g_exp, s0, cst["ltri"], cst["ones"], cst["tile"], cst["tile_t"],
      cst["bd"], expb, expa)
    return oc.reshape(rows, w), s_new


def _outproj_kernel(x_ref, oa_ref, ob_ref, oc_ref, gt_ref, g_ref, sc_ref, sh_ref,
                    wo_ref, wq_ref, x1_ref, h2_ref, q_ref):
    mo = _dot_b(oa_ref[...], wo_ref[0:A_W, :])
    mo = mo + _dot_b(ob_ref[...], wo_ref[A_W:A_W + B_W, :])
    mo = mo + _dot_b(oc_ref[...], wo_ref[A_W + B_W:, :])
    x1 = x_ref[...] + gt_ref[...] * mo
    x1_ref[...] = x1
    h2 = _rmsnorm(x1, g_ref[...]) * sc_ref[...] + sh_ref[...]
    h2_ref[...] = h2.astype(BF16)
    q_ref[...] = _dot_b(h2, wq_ref[...])


def _outproj_call(x, oa, ob, oc, gt, g, sc, sh, wo, wq, tiles_per_group):
    t = x.shape[0]
    tm = TOKEN_TILE
    row = lambda w: pl.BlockSpec((tm, w), lambda i: (i, 0))
    return pl.pallas_call(
        _outproj_kernel,
        grid=(t // tm,),
        in_specs=[row(D_MODEL), row(A_W), row(B_W), row(C_W), _mod_spec(gt, tiles_per_group),
                  _const_spec((1, D_MODEL)), _mod_spec(sc, tiles_per_group), _mod_spec(sh, tiles_per_group),
                  _const_spec(wo.shape), _const_spec(wq.shape)],
        out_specs=[row(D_MODEL), row(D_MODEL), row(D_MODEL)],
        out_shape=[jax.ShapeDtypeStruct((t, D_MODEL), F32), jax.ShapeDtypeStruct((t, D_MODEL), BF16),
                   jax.ShapeDtypeStruct((t, D_MODEL), F32)],
        compiler_params=_params(1),
        name="outproj",
    )(x, oa, ob, oc, gt, g, sc, sh, wo, wq)


def _top16(s):
    vals = []
    for r in range(PEER_TOPK):
        m = jnp.max(s, axis=0, keepdims=True)
        s = jnp.where(s == m, -(32.0 + r) * 2.0 ** 95, s)
        vals.append(m)
    rank = jnp.where(s <= -(2.0 ** 99), s * -(2.0 ** -95) - 32.0, float(PEER_TOPK))
    return jnp.concatenate(vals, axis=0), rank


def _route_tile(s1, s2):
    nk, t = s1.shape
    v1, rank1 = _top16(s1)
    v2, rank2 = _top16(s2)
    c3 = v1[:, None, :] + v2[None, :, :]
    jrow = lax.broadcasted_iota(jnp.int32, (8, 1), 0)
    parts = [v1[0:1, :] + v2, v1[1:2, :] + v2[0:8, :]]
    for i in range(2, 8):
        parts.append(jnp.where(jrow < PEER_TOPK // (i + 1), v1[i:i + 1, :] + v2[0:8, :], -jnp.inf))
    parts.append(v1[8:16, :] + v2[0:1, :])
    cand = jnp.concatenate(parts, axis=0)
    m = None
    for _ in range(PEER_TOPK):
        m = jnp.max(cand, axis=0, keepdims=True)
        cand = jnp.where(cand == m, -jnp.inf, cand)
    tau = m
    sel = c3 >= tau[None, :, :]
    m0 = v1[0:1, :] + v2[0:1, :]
    zsum = jnp.sum(jnp.where(sel, jnp.exp(c3 - m0[None, :, :]), 0.0).reshape(-1, t), axis=0, keepdims=True)
    n_i = jnp.sum(jnp.where(sel, 1.0, 0.0), axis=1)
    rank1 = rank1.astype(BF16)
    cnt = jnp.zeros((nk, t), BF16)
    for i in range(PEER_TOPK):
        cnt = cnt + jnp.where(rank1 == i, n_i[i:i + 1, :].astype(BF16), jnp.zeros((), BF16))
    return (cnt.astype(F32), jnp.exp(s1 - v1[0:1, :]), rank2.astype(BF16),
            (jnp.exp(s2 - v2[0:1, :]) * (0.5 / zsum)).astype(BF16))


def _peer_kernel(h2_ref, q_ref, x1_ref, gt_ref, fg_ref, k1_ref, k2_ref, u_ref, vt_ref, o_ref,
                 cnt_s, e1_s, rank2_s, e2_s, acc_s, s1_s, s2_s, coef_s, *, final, n_steps):
    g = pl.program_id(0)
    tt = PEER_TOKEN_TILE
    rt = PEER_ROUTE_TILE
    nk = PEER_NKEYS
    nj = PEER_N // PEER_EXPERT_BLOCK
    a_per_step = PEER_EXPERT_BLOCK // nk
    j = jnp.minimum(g, n_steps - 2) % nj
    gb = jnp.maximum(g - 1, 0)

    @pl.when(g == 0)
    def _():
        coef_s[...] = jnp.zeros_like(coef_s)

    @pl.when(gb % nj == 0)
    def _():
        acc_s[...] = jnp.zeros_like(acc_s)

    @pl.when(jnp.logical_and(g % nj == 0, g < n_steps - 1))
    def _():

        def head(h, carry):
            qh = q_ref[:, pl.ds(pl.multiple_of(h * nk, nk), nk)]
            s1_s[...] = _dot_x3(k1_ref[...], qh, _NT)
            s2_s[...] = _dot_x3(k2_ref[...], qh, _NT)

            def lane_tile(c, carry2):
                cols = pl.ds(pl.multiple_of(c * rt, rt), rt)
                (cnt_s[h, :, cols], e1_s[h, :, cols], rank2_s[h, :, cols],
                 e2_s[h, :, cols]) = _route_tile(s1_s[:, cols], s2_s[:, cols])
                return carry2

            lax.fori_loop(0, tt // rt, lane_tile, 0)
            return carry

        lax.fori_loop(0, PEER_HEADS, head, 0)

    h2 = h2_ref[...]
    n_sub = PEER_EXPERT_BLOCK // PEER_SUB_BLOCK
    a_per_sub = PEER_SUB_BLOCK // nk
    sub = lambda s: slice(s * PEER_SUB_BLOCK, (s + 1) * PEER_SUB_BLOCK)
    cur = coef_s.at[g % 2]
    prev = coef_s.at[(g + 1) % 2]
    hids = [_dot(u_ref[sub(s), :], h2, _NT).astype(BF16) for s in range(n_sub)]
    acc_s[...] += _dot(vt_ref[...], prev[...])
    pack = 16
    for s in range(n_sub):
        for al in range(a_per_sub):
            a = j * a_per_step + s * a_per_sub + al
            wsum = jnp.zeros((nk // pack, pack, tt), BF16)
            for h in range(PEER_HEADS):
                cnt_row = jnp.broadcast_to(cnt_s[h, pl.ds(a, 1), :], (pack, tt)).astype(BF16)
                e1_row = jnp.broadcast_to(e1_s[h, pl.ds(a, 1), :], (pack, tt)).astype(BF16)
                rank2 = rank2_s[h].reshape(nk // pack, pack, tt)
                e2 = e2_s[h].reshape(nk // pack, pack, tt)
                wsum = jnp.where(rank2 < cnt_row[None], wsum + e2 * e1_row[None], wsum)
            r0 = s * PEER_SUB_BLOCK + al * nk
            cur[r0:r0 + nk, :] = _gelu_x2(hids[s][al * nk:(al + 1) * nk, :]) * wsum.reshape(nk, tt)

    @pl.when(jnp.logical_and(g > 0, gb % nj == nj - 1))
    def _():
        x2 = x1_ref[...] + gt_ref[...] * acc_s[...].T
        if final:
            x2 = _rmsnorm(x2, fg_ref[...])
        o_ref[...] = x2


def _peer_call(h2b, q, x1, gt, fg, k1p, k2p, u_b, v_b, layer, tiles_per_group, final):
    t = h2b.shape[0]
    tt = PEER_TOKEN_TILE
    nb = PEER_EXPERT_BLOCK
    nj = PEER_N // nb
    n_tiles = t // tt
    n_steps = n_tiles * nj + 1
    front = lambda g: jnp.minimum(g, n_steps - 2)
    back = lambda g: jnp.maximum(g - 1, 0)
    row_f = lambda w: pl.BlockSpec((tt, w), lambda g: (front(g) // nj, 0))
    row_b = lambda w: pl.BlockSpec((tt, w), lambda g: (back(g) // nj, 0))
    _, r, w = gt.shape
    tab = lambda dt: pltpu.VMEM((PEER_HEADS, PEER_NKEYS, tt), dt)
    return pl.pallas_call(
        functools.partial(_peer_kernel, final=final, n_steps=n_steps),
        grid=(n_steps,),
        in_specs=[row_f(D_MODEL), row_f(D_MODEL), row_b(D_MODEL),
                  pl.BlockSpec((None, r, w), lambda g: (back(g) // nj // tiles_per_group, 0, 0)),
                  _const_spec((1, D_MODEL)),
                  _const_spec((PEER_NKEYS, PEER_NKEYS)),
                  _const_spec((PEER_NKEYS, PEER_NKEYS)),
                  pl.BlockSpec((None, nb, D_MODEL), lambda g: (layer, front(g) % nj, 0)),
                  pl.BlockSpec((None, D_MODEL, nb), lambda g: (layer, 0, back(g) % nj))],
        out_specs=row_b(D_MODEL),
        out_shape=jax.ShapeDtypeStruct((t, D_MODEL), F32),
        scratch_shapes=[tab(F32), tab(F32), tab(BF16), tab(BF16), pltpu.VMEM((D_MODEL, tt), F32),
                        pltpu.VMEM((PEER_NKEYS, tt), F32), pltpu.VMEM((PEER_NKEYS, tt), F32),
                        pltpu.VMEM((2, nb, tt), BF16)],
        compiler_params=_params(1),
        name="peer",
    )(h2b, q, x1, gt, fg, k1p, k2p, u_b, v_b)


def _block_diag(wblk):
    n, d, e = wblk.shape
    eye = jnp.eye(n, dtype=wblk.dtype)
    return (eye[:, None, :, None] * wblk[:, :, None, :]).reshape(n * d, n * e)


def _hilo(w):
    hi = w.astype(BF16)
    return hi, (w - hi.astype(F32)).astype(BF16)


def _pad_sample(z, hist=None):
    bsz = z.shape[0] // 4
    w = z.shape[1]
    z3 = z.reshape(bsz, 4, w)
    h3 = jnp.zeros((bsz, SAMPLE_HIST, w), F32)
    if hist is not None:
        h3 = h3.at[:, :, :hist.shape[-1]].set(hist)
    return jnp.concatenate([h3, z3, jnp.zeros((bsz, 1, w), F32)], axis=1).reshape(bsz * SAMPLE_SEQ_ROWS, w)


def _unpad_sample(o):
    bsz = o.shape[0] // SAMPLE_SEQ_ROWS
    return o.reshape(bsz, SAMPLE_SEQ_ROWS, -1)[:, SAMPLE_HIST:SAMPLE_HIST + 4].reshape(bsz * 4, -1)


def _layer_weights(l, w_in, w_out, peer_wq, peer_k1, peer_k2, peer_u, peer_v, lru_wa, lru_wx):
    w_in_p = jnp.pad(w_in[l], ((0, 0), (0, N_IN_PAD - N_IN)))
    wih = w_in_p.astype(BF16)
    wo = w_out[l].astype(BF16)
    wq = peer_wq[l].astype(BF16)
    wg = jnp.concatenate([_block_diag(lru_wa[l]), _block_diag(lru_wx[l])], axis=1).astype(BF16)
    half = PEER_NKEYS // 2
    k1p = jnp.pad(peer_k1[l], ((0, 0), (0, half)))
    k2p = jnp.pad(peer_k2[l], ((0, 0), (half, 0)))
    return dict(wih=wih, wo=wo, wq=wq, wg=wg, k1p=k1p, k2p=k2p)


def _trunk(x, mods, states, is_prompt, p, lw, l, final_g):
    t = x.shape[0]
    if is_prompt:
        sh1, sc1, gt1, sh2, sc2, gt2 = mods
    else:
        sh1, sc1, gt1, sh2, sc2 = (m.reshape(-1, TOKEN_TILE, D_MODEL) for m in mods[:5])
        gt2 = mods[5].reshape(-1, PEER_TOKEN_TILE, D_MODEL)
    if is_prompt:
        cfg = PROMPT_CFG
        tiles_tok = 2048 // TOKEN_TILE
        tiles_peer = 2048 // PEER_TOKEN_TILE
        n_seq = t // 2048
    else:
        cfg = SAMPLE_CFG
        tiles_tok = 1
        tiles_peer = 1
        n_seq = t // 4
    row1 = lambda v: v.reshape(1, -1)
    z_a, z_b, z_c = _inproj_call(x, row1(p['norm1_g']), sc1, sh1, lw['wih'], tiles_tok)
    if is_prompt:
        s_a = jnp.zeros((n_seq, A_W, HEAD_DIM), F32)
        s_c = jnp.zeros((n_seq, C_W, HEAD_DIM), F32)
        za_m, zb_m, zc_m = z_a, z_b, z_c
        inj = None
    else:
        st_hgrn, st_lru_h, st_lru_conv, st_dn, st_dn_conv = states
        s_a = jnp.swapaxes(st_hgrn, -1, -2).reshape(n_seq, A_W, HEAD_DIM)
        s_c = st_dn.reshape(n_seq, C_W, HEAD_DIM)
        za_m = _pad_sample(z_a)
        zb_m = _pad_sample(z_b, st_lru_conv)
        zc_m = _pad_sample(z_c, st_dn_conv)
        inj = jnp.zeros((n_seq, SAMPLE_SEQ_ROWS, B_W), F32).at[:, 0].set(st_lru_h).reshape(-1, B_W)
    rep = lambda v, h: row1(jnp.tile(v, h))
    oa, s_a_new = _hgrn_call(za_m, p['lb_param'], rep(p['a_norm_g'], A_HEADS), s_a,
                             PROMPT_HGRN_CFG if is_prompt else cfg, l)
    ob, hs = _lru_call(zb_m, inj, p['lru_conv_w'], row1(p['lru_conv_b']), lw['wg'],
                       row1(p['lru_ba']), row1(p['lru_bx']), row1(p['lru_L']), cfg)
    decay_cols = lambda v: jnp.zeros((1, LANES), F32).at[0, C_HEADS:2 * C_HEADS].set(v)
    oc, s_c_new = _dn_call(zc_m, p['dn_conv_w'], decay_cols(p['dn_A_log']), decay_cols(p['dn_dt_bias']),
                           rep(p['dn_norm_g'], C_HEADS), s_c, cfg)
    if is_prompt:
        seq = 2048
        h_t = hs.reshape(n_seq, seq, B_W)[:, -1]
        buf_b = z_b.reshape(n_seq, seq, ZB_W)[:, -SAMPLE_HIST:, :B_W]
        buf_c = z_c.reshape(n_seq, seq, ZC_W)[:, -SAMPLE_HIST:, :3 * C_W]
    else:
        oa, ob, oc = _unpad_sample(oa), _unpad_sample(ob), _unpad_sample(oc)
        h_t = hs.reshape(n_seq, SAMPLE_SEQ_ROWS, B_W)[:, -1]
        buf_b = z_b.reshape(n_seq, 4, ZB_W)[:, 1:, :B_W]
        buf_c = z_c.reshape(n_seq, 4, ZC_W)[:, 1:, :3 * C_W]
    new_states = (jnp.swapaxes(s_a_new.reshape(n_seq, A_HEADS, HEAD_DIM, HEAD_DIM), -1, -2), h_t, buf_b,
                  s_c_new.reshape(n_seq, C_HEADS, HEAD_DIM, HEAD_DIM), buf_c)
    x1, h2b, q = _outproj_call(x, oa, ob, oc, gt1, row1(p['norm2_g']), sc2, sh2,
                               lw['wo'], lw['wq'], tiles_tok)
    x2 = _peer_call(h2b, q, x1, gt2, row1(final_g), lw['k1p'], lw['k2p'], lw['u_b'], lw['v_b'], l,
                    tiles_peer, final=(l == DEPTH - 1))
    return x2, new_states


def kernel(x_prompt, x_sample, state_hgrn, state_lru_h, state_lru_conv, state_dn, state_dn_conv,
           c_prompt, c_sample, w_ada, b_ada, norm1_g, norm2_g, w_in, lb_param, a_norm_g,
           lru_conv_w, lru_conv_b, lru_wa, lru_ba, lru_wx, lru_bx, lru_L,
           dn_conv_w, dn_A_log, dn_dt_bias, dn_norm_g, w_out,
           peer_wq, peer_k1, peer_k2, peer_u, peer_v, final_norm_g):
    n_p, seq, _ = x_prompt.shape
    n_s, dec_seq, _ = x_sample.shape
    mod = _ada_call(jnp.concatenate([c_prompt, c_sample], axis=0), w_ada, b_ada)
    xp = x_prompt.reshape(n_p * seq, D_MODEL)
    xs = x_sample.reshape(n_s * dec_seq, D_MODEL)
    sample_states = (state_hgrn, state_lru_h, state_lru_conv, state_dn, state_dn_conv)
    u_all = peer_u.astype(BF16)
    vt_all = jnp.swapaxes(peer_v, 1, 2).astype(BF16)
    p_new, s_new = [], []
    for l in range(DEPTH):
        p = dict(norm1_g=norm1_g[l], norm2_g=norm2_g[l], lb_param=lb_param, a_norm_g=a_norm_g[l],
                 lru_conv_w=lru_conv_w[l], lru_conv_b=lru_conv_b[l], lru_ba=lru_ba[l], lru_bx=lru_bx[l],
                 lru_L=lru_L[l], dn_conv_w=dn_conv_w[l], dn_A_log=dn_A_log[l], dn_dt_bias=dn_dt_bias[l],
                 dn_norm_g=dn_norm_g[l])
        lw = _layer_weights(l, w_in, w_out, peer_wq, peer_k1, peer_k2, peer_u, peer_v, lru_wa, lru_wx)
        lw['u_b'], lw['v_b'] = u_all, vt_all
        parts = jnp.split(mod[l], 6, axis=-1)
        parts = [m + 1.0 if i in (1, 4) else m for i, m in enumerate(parts)]
        mods_p = [m[:n_p].reshape(n_p, 1, D_MODEL) for m in parts]
        mods_s = [jnp.repeat(m[n_p:], dec_seq, axis=0) for m in parts]
        xp, st_p = _trunk(xp, mods_p, None, True, p, lw, l, final_norm_g)
        xs, st_s = _trunk(xs, mods_s, tuple(s[l] for s in sample_states), False, p, lw, l, final_norm_g)
        p_new.append(st_p)
        s_new.append(st_s)
    stack = lambda sts: [jnp.stack([s[i] for s in sts]) for i in range(5)]
    p_st = stack(p_new)
    s_st = stack(s_new)
    return (xp.reshape(n_p, seq, D_MODEL), xs.reshape(n_s, dec_seq, D_MODEL), *p_st, *s_st)
```

```python
import functools

import jax
import jax.numpy as jnp
import numpy as np
from jax import lax
from jax.experimental import pallas as pl
from jax.experimental.pallas import tpu as pltpu

F32 = jnp.float32
BF16 = jnp.bfloat16

D_MODEL = 1024
DEPTH = 2
HEAD_DIM = 64
A_HEADS = 4
A_W = 256
B_W = 384
C_HEADS = 6
C_W = 384
LRU_C = 8.0
CONV_W = 4
N_IN = 3340
N_IN_PAD = 3456
ZA_W = 1024
ZB_W = 768
ZC_W = 1664
PEER_HEADS = 8
PEER_NKEYS = 128
PEER_TOPK = 16
PEER_N = PEER_NKEYS * PEER_NKEYS
EPS = 1e-6

LANES = 128
SUPER_BLOCK = 256
SAMPLE_SEQ_ROWS = 8
SAMPLE_HIST = CONV_W - 1
TOKEN_TILE = 512
PEER_TOKEN_TILE = 512
PEER_EXPERT_BLOCK = 2048
PEER_SUB_BLOCK = 512
PEER_ROUTE_TILE = 256
VMEM_LIMIT_BYTES = 56 * 1024 * 1024

_NN = (((1,), (0,)), ((), ()))
_NT = (((1,), (1,)), ((), ()))
_TN = (((0,), (0,)), ((), ()))


def _dot(a, b, dims=_NN):
    return lax.dot_general(a, b, dims, preferred_element_type=F32)


def _split2(x):
    hi = x.astype(BF16)
    lo = (x - hi.astype(F32)).astype(BF16)
    return hi, lo


def _split3(x):
    hi = x.astype(BF16)
    r = x - hi.astype(F32)
    mid = r.astype(BF16)
    lo = (r - mid.astype(F32)).astype(BF16)
    return hi, mid, lo


def _dot_x3(a, b, dims=_NN):
    ah, al = _split2(a)
    bh, bl = _split2(b)
    return _dot(ah, bh, dims) + _dot(ah, bl, dims) + _dot(al, bh, dims)


def _dot_b(a, b, dims=_NN):
    return _dot(a.astype(BF16), b.astype(BF16), dims)


def _dot_sel(a, sel, dims=_NN, passes=3):
    parts = _split3(a)[:passes]
    out = _dot(parts[0], sel, dims)
    for p in parts[1:]:
        out = out + _dot(p, sel, dims)
    return out


def _head_sums(x, ones_ref):
    ones = ones_ref[0:LANES, 0:LANES]
    xb = x.astype(BF16)
    return jnp.concatenate([_dot(xb[:, t:t + LANES], ones) for t in range(0, x.shape[1], LANES)], axis=1)


def _sel_dot(sel, b, passes=3):
    parts = _split3(b)[:passes]
    out = _dot(sel, parts[0])
    for p in parts[1:]:
        out = out + _dot(sel, p)
    return out


def _softplus(x):
    return jnp.maximum(x, 0.0) + jnp.log(1.0 + jnp.exp(-jnp.abs(x)))


def _silu(x):
    return x * jax.nn.sigmoid(x)


def _gelu_x2(x):
    c = float(np.sqrt(2.0 / np.pi))
    return x + x * jnp.tanh(x * (c + (0.044715 * c) * (x * x)))


def _gelu(x):
    return 0.5 * x * (1.0 + jnp.tanh(float(np.sqrt(2.0 / np.pi)) * (x + 0.044715 * (x * x * x))))


def _rmsnorm(x, g):
    return x * lax.rsqrt(jnp.mean(x * x, axis=-1, keepdims=True) + EPS) * g


def _const_spec(shape):
    nd = len(shape)
    return pl.BlockSpec(shape, lambda *_: (0,) * nd)


def _params(n_grid):
    return pltpu.CompilerParams(dimension_semantics=("arbitrary",) * n_grid,
                                vmem_limit_bytes=VMEM_LIMIT_BYTES)


def _ada_kernel(c_ref, w_ref, b_ref, o_ref):
    c = c_ref[...]
    o_ref[0] = _dot_x3(_silu(c), w_ref[0]) + b_ref[0]


def _ada_call(c_all, w_ada, b_ada):
    nb = c_all.shape[0]
    nt = 6 * D_MODEL // 1024
    return pl.pallas_call(
        _ada_kernel,
        grid=(DEPTH, nt),
        in_specs=[
            _const_spec((nb, D_MODEL)),
            pl.BlockSpec((1, D_MODEL, 1024), lambda l, j: (l, 0, j)),
            pl.BlockSpec((1, 1, 1024), lambda l, j: (l, 0, j)),
        ],
        out_specs=pl.BlockSpec((1, nb, 1024), lambda l, j: (l, 0, j)),
        out_shape=jax.ShapeDtypeStruct((DEPTH, nb, 6 * D_MODEL), F32),
        compiler_params=_params(2),
        name="ada_mod",
    )(c_all, w_ada, b_ada.reshape(DEPTH, 1, 6 * D_MODEL))


def _inproj_kernel(x_ref, g_ref, sc_ref, sh_ref, w_ref, za_ref, zb_ref, zc_ref):
    h = (_rmsnorm(x_ref[...], g_ref[...]) * sc_ref[...] + sh_ref[...]).astype(BF16)
    for o_ref, lo, hi in ((za_ref, 0, ZA_W), (zb_ref, ZA_W, ZA_W + ZB_W), (zc_ref, ZA_W + ZB_W, N_IN_PAD)):
        o_ref[...] = _dot(h, w_ref[:, lo:hi])


def _mod_spec(mod, tiles_per_group):
    _, r, w = mod.shape
    return pl.BlockSpec((None, r, w), lambda i: (i // tiles_per_group, 0, 0))


def _inproj_call(x, g, sc, sh, wh, tiles_per_group):
    t = x.shape[0]
    tm = TOKEN_TILE
    row = lambda w: pl.BlockSpec((tm, w), lambda i: (i, 0))
    return pl.pallas_call(
        _inproj_kernel,
        grid=(t // tm,),
        in_specs=[row(D_MODEL), _const_spec((1, D_MODEL)), _mod_spec(sc, tiles_per_group),
                  _mod_spec(sh, tiles_per_group), _const_spec(wh.shape)],
        out_specs=[row(ZA_W), row(ZB_W), row(ZC_W)],
        out_shape=[jax.ShapeDtypeStruct((t, ZA_W), F32), jax.ShapeDtypeStruct((t, ZB_W), F32),
                   jax.ShapeDtypeStruct((t, ZC_W), F32)],
        compiler_params=_params(1),
        name="inproj",
    )(x, g, sc, sh, wh)


class _SeqCfg:
    def __init__(self, seq_rows, hist, real, chunk, n_par):
        self.seq_rows = seq_rows
        self.n_par = n_par
        self.hist = hist
        self.real = real
        self.chunk = chunk
        self.masked = hist > 0 or hist + real < seq_rows
        self.n_chunks = SUPER_BLOCK // chunk
        self.long_seq = seq_rows > SUPER_BLOCK
        self.sb_per_seq = max(seq_rows // SUPER_BLOCK, 1)
        self.seq_per_sb = max(SUPER_BLOCK // seq_rows, 1)
        assert (self.long_seq or chunk == seq_rows) and (self.long_seq or n_par == 1)


PROMPT_CFG = _SeqCfg(seq_rows=2048, hist=0, real=2048, chunk=16, n_par=4)
PROMPT_HGRN_CFG = _SeqCfg(seq_rows=2048, hist=0, real=2048, chunk=16, n_par=8)
SAMPLE_CFG = _SeqCfg(seq_rows=SAMPLE_SEQ_ROWS, hist=SAMPLE_HIST, real=4, chunk=SAMPLE_SEQ_ROWS, n_par=1)


def _row_in_seq(cfg, n_rows):
    r = lax.broadcasted_iota(jnp.int32, (n_rows, 1), 0)
    return r % min(cfg.seq_rows, SUPER_BLOCK)


def _real_mask(cfg, n_rows):
    r = _row_in_seq(cfg, n_rows)
    return (r >= cfg.hist) & (r < cfg.hist + cfg.real)


def _causal_conv(x, xp_ref, w_ref):
    n = x.shape[0]
    xp_ref[8:8 + n, :] = x
    y = x * w_ref[CONV_W - 1:CONV_W, :]
    for k in range(CONV_W - 1):
        s = CONV_W - 1 - k
        y = y + xp_ref[8 - s:8 - s + n, :] * w_ref[k:k + 1, :]
    xp_ref[0:8, :] = x[n - 8:n, :]
    return y


def _lane_tiles(x):
    return [x[:, t:t + LANES] for t in range(0, x.shape[1], LANES)]


def _expand_state(s_cat, tile_ref, bd_ref):
    tile = tile_ref[:, 0:LANES]
    bd = bd_ref[0:LANES, 0:LANES]
    return jnp.stack([_dot_sel(s_cat[r:r + LANES, :], tile, passes=2) * bd
                      for r in range(0, s_cat.shape[0], LANES)])


def _compress_state(s_tiles, tile_t_ref):
    tile_t = tile_t_ref[0:LANES, :]
    return jnp.concatenate([_dot_sel(s_tiles[t], tile_t, passes=2) for t in range(s_tiles.shape[0])], axis=0)


def _state_matmul(x, s_tiles, dims):
    return jnp.concatenate([_dot_b(xt, s_tiles[t], dims) for t, xt in enumerate(_lane_tiles(x))], axis=1)


def _state_update(s_tiles, decay_row, a, b, bd_ref):
    bd = bd_ref[0:LANES, 0:LANES]
    return jnp.stack([s_tiles[t] * dt + bd * _dot_b(at, bt, _TN)
                      for t, (dt, at, bt) in enumerate(zip(_lane_tiles(decay_row), _lane_tiles(a), _lane_tiles(b)))])


def _head_norm_gate(o, z, g, ones_ref):
    ms = _dot_sel(o * o, ones_ref[...], passes=2) * (1.0 / HEAD_DIM)
    return o * lax.rsqrt(ms + EPS) * g * _silu(z)


def _run_chunks(cfg, sb, step, st_ref, s0_ref, sout_ref, tile_ref, tile_t_ref, bd_ref):
    if cfg.long_seq:
        @pl.when(sb % cfg.sb_per_seq == 0)
        def _():
            for p in range(cfg.n_par):
                st_ref[p] = _expand_state(s0_ref[p], tile_ref, bd_ref)

        def body(k, carry):
            for p in range(cfg.n_par):
                st_ref[p] = step(k, p, st_ref[p])
            return carry

        lax.fori_loop(0, cfg.n_chunks, body, 0)

        @pl.when(sb % cfg.sb_per_seq == cfg.sb_per_seq - 1)
        def _():
            for p in range(cfg.n_par):
                sout_ref[p] = _compress_state(st_ref[p], tile_t_ref)
    else:
        def body(k, carry):
            st_new = step(k, 0, _expand_state(s0_ref[k], tile_ref, bd_ref))
            sout_ref[k] = _compress_state(st_new, tile_t_ref)
            return carry

        lax.fori_loop(0, cfg.n_chunks, body, 0, unroll=4)


def _hgrn_kernel(z_ref, lbp_ref, g_ref, s0_ref, ltri_ref, ones_ref, tile_ref, tile_t_ref, bd_ref,
                 o_ref, sout_ref,
                 st_ref, q_s, k_s, g_s, qe_s, kt_s, egl_s, o_s, *, cfg, layer):
    sb = pl.program_id(0)
    c = cfg.chunk
    n = SUPER_BLOCK
    nch = cfg.n_chunks
    w = A_W
    lbp = lbp_ref[...]
    e = jnp.exp(lbp - jnp.max(lbp, axis=0, keepdims=True))
    lbs = e / jnp.sum(e, axis=0, keepdims=True)
    lb = jnp.sum(lbs[0:layer + 1, :], axis=0, keepdims=True) - lbs[0:1, :]

    for p in range(cfg.n_par):
        r = slice(p * n, (p + 1) * n)
        aq = z_ref[p, :, 0:w]
        af = z_ref[p, :, w:2 * w]
        f = lb + (1.0 - lb) * jax.nn.sigmoid(af)
        lf = jnp.log(f)
        kk = 1.0 - f
        if cfg.masked:
            real = _real_mask(cfg, n)
            lf = jnp.where(real, lf, 0.0)
            kk = jnp.where(real, kk, 0.0)
        q = _silu(aq) * (HEAD_DIM ** -0.5)
        gcum = _sel_dot(ltri_ref[...], lf)
        g3 = gcum.reshape(nch, c, w)
        gl3 = g3[:, c - 1:c, :]
        q_s[r, :] = q
        k_s[r, :] = kk
        g_s[r, :] = gcum
        qe_s[r, :] = q * jnp.exp(gcum)
        kt_s[r, :] = (kk.reshape(nch, c, w) * jnp.exp(gl3 - g3)).reshape(n, w)
        egl_s[p * nch:(p + 1) * nch, :] = jnp.exp(gl3).reshape(nch, w)

    jio = lax.broadcasted_iota(jnp.int32, (c, 1), 0)
    bd = bd_ref[...]

    def chunk(k, p, st):
        rows = pl.ds(pl.multiple_of(p * n + k * c, c), c)
        q_c = q_s[rows, :]
        k_c = k_s[rows, :]
        g_c = g_s[rows, :]
        v_c = z_ref[p, pl.ds(pl.multiple_of(k * c, c), c), 2 * w:3 * w]
        o_inter = _state_matmul(qe_s[rows, :], st, _NT)
        lo = min(c, 8)
        d_rows = []
        for i in range(c):
            nj = lo if i < lo else c
            dec = jnp.exp(jnp.minimum(g_c[i:i + 1, :] - g_c[0:nj, :], 0.0))
            d_rows.append(jnp.where(jio[0:nj] <= i, k_c[0:nj, :] * dec * q_c[i:i + 1, :], 0.0))
        a_b = _head_sums(jnp.concatenate(d_rows, axis=0), ones_ref)
        o_diag = jnp.sum(a_b[0:lo * lo, :].reshape(lo, lo, w) * v_c[None, 0:lo, :], axis=1)
        if c > lo:
            o_hi = jnp.sum(a_b[lo * lo:, :].reshape(c - lo, c, w) * v_c[None, :, :], axis=1)
            o_diag = jnp.concatenate([o_diag, o_hi], axis=0)
        o_s[rows, :] = o_inter + o_diag
        return _state_update(st, egl_s[pl.ds(p * nch + k, 1), :], v_c, kt_s[rows, :], bd_ref)

    _run_chunks(cfg, sb, chunk, st_ref, s0_ref, sout_ref, tile_ref, tile_t_ref, bd_ref)
    for p in range(cfg.n_par):
        o_ref[p] = _head_norm_gate(o_s[p * n:(p + 1) * n, :], z_ref[p, :, 3 * w:4 * w], g_ref[...], ones_ref)


def _mixer_consts(heads, cfg):
    w = heads * HEAD_DIM
    lane_head = np.arange(w) // HEAD_DIM
    ones = (lane_head[:, None] == lane_head[None, :]).astype(np.float32)
    tile = (np.arange(HEAD_DIM)[:, None] == (np.arange(w) % HEAD_DIM)[None, :]).astype(np.float32)
    r = np.arange(SUPER_BLOCK)
    ltri = ((r[:, None] // cfg.chunk == r[None, :] // cfg.chunk) & (r[None, :] <= r[:, None])).astype(np.float32)
    return dict(ltri=jnp.asarray(ltri, BF16), ones=jnp.asarray(ones, BF16), tile=jnp.asarray(tile, BF16),
                tile_t=jnp.asarray(tile.T, BF16), bd=jnp.asarray(ones, F32))


def _state_specs(cfg, w):
    blk = (cfg.n_par if cfg.long_seq else cfg.seq_per_sb, w, HEAD_DIM)
    if cfg.long_seq:
        imap = lambda i: (i // cfg.sb_per_seq, 0, 0)
    else:
        imap = lambda i: (i, 0, 0)
    return pl.BlockSpec(blk, imap)


def _seq_view(x, cfg):
    if not cfg.long_seq:
        return x[None]
    assert x.shape[0] % (cfg.n_par * cfg.seq_rows) == 0, "sequence count must be a multiple of n_par"
    return x.reshape(-1, cfg.seq_rows, x.shape[-1])


def _seq_block_spec(cfg, width):
    if cfg.long_seq:
        return pl.BlockSpec((cfg.n_par, SUPER_BLOCK, width), lambda i: (i // cfg.sb_per_seq, i % cfg.sb_per_seq, 0))
    return pl.BlockSpec((1, SUPER_BLOCK, width), lambda i: (0, i, 0))


def _hgrn_call(z_a, lb_param, g_exp, s0_t, cfg, layer):
    rows = z_a.shape[0]
    w = A_W
    cst = _mixer_consts(A_HEADS, cfg)
    n_seq = s0_t.shape[0]
    par_rows = cfg.n_par * SUPER_BLOCK
    vm = lambda shape: pltpu.VMEM(shape, F32)
    kern = functools.partial(_hgrn_kernel, cfg=cfg, layer=layer)
    z3 = _seq_view(z_a, cfg)
    oa, s_new = pl.pallas_call(
        kern,
        grid=(rows // par_rows,),
        in_specs=[_seq_block_spec(cfg, ZA_W), _const_spec((DEPTH, w)),
                  _const_spec((1, w)), _state_specs(cfg, w), _const_spec(cst["ltri"].shape),
                  _const_spec(cst["ones"].shape), _const_spec(cst["tile"].shape),
                  _const_spec(cst["tile_t"].shape), _const_spec(cst["bd"].shape)],
        out_specs=[_seq_block_spec(cfg, w), _state_specs(cfg, w)],
        out_shape=[jax.ShapeDtypeStruct(z3.shape[:2] + (w,), F32), jax.ShapeDtypeStruct((n_seq, w, HEAD_DIM), F32)],
        scratch_shapes=[vm((cfg.n_par, w // LANES, LANES, LANES))] + [vm((par_rows, w))] * 5
        + [vm((cfg.n_par * cfg.n_chunks, w)), vm((par_rows, w))],
        compiler_params=_params(1),
        name="hgrn2",
    )(z3, lb_param, g_exp, s0_t, cst["ltri"], cst["ones"], cst["tile"], cst["tile_t"], cst["bd"])
    return oa.reshape(rows, w), s_new


def _lru_kernel(*refs, cfg):
    if cfg.long_seq:
        (z_ref, cw_ref, cb_ref, wg_ref, ba_ref, bx_ref, lp_ref,
         ob_ref, hs_ref, prev_ref, hc_ref) = refs
        inj_ref = None
    else:
        (z_ref, inj_ref, cw_ref, cb_ref, wg_ref, ba_ref, bx_ref, lp_ref,
         ob_ref, hs_ref, prev_ref, hc_ref) = refs
    sb = pl.program_id(0)
    n = SUPER_BLOCK
    w = B_W

    @pl.when(sb % cfg.sb_per_seq == 0)
    def _():
        prev_ref[0:8, :] = jnp.zeros((8, prev_ref.shape[1]), F32)
        hc_ref[...] = jnp.zeros_like(hc_ref)

    xc = _causal_conv(z_ref[:, 0:w], prev_ref, cw_ref) + cb_ref[...]
    gates = _dot_b(xc, wg_ref[...])
    r = jax.nn.sigmoid(gates[:, 0:w] + ba_ref[...])
    ig = jax.nn.sigmoid(gates[:, w:2 * w] + bx_ref[...])
    log_a = -LRU_C * r * _softplus(-lp_ref[...])
    a = jnp.exp(log_a)
    b = jnp.sqrt(1.0 - jnp.exp(2.0 * log_a)) * ig * xc
    if cfg.masked:
        real = _real_mask(cfg, n)
        a = jnp.where(real, a, 1.0)
        b = jnp.where(real, b, 0.0)
    if inj_ref is not None:
        b = b + inj_ref[...]
    group = 8
    assert cfg.seq_rows % group == 0
    rig = lax.broadcasted_iota(jnp.int32, (n, 1), 0) % group
    d = 1
    while d < group:
        has = rig >= d
        a_sh = jnp.where(has, pltpu.roll(a, d, 0), 1.0)
        b_sh = jnp.where(has, pltpu.roll(b, d, 0), 0.0)
        b = b + a * b_sh
        a = a * a_sh
        d *= 2
    if cfg.long_seq:
        a3 = a.reshape(n // group, group, w)
        b3 = b.reshape(n // group, group, w)
        carry = hc_ref[...]
        rows = []
        for i in range(n // group):
            h_i = b3[i] + a3[i] * carry
            rows.append(h_i)
            carry = h_i[group - 1:group, :]
        hs = jnp.concatenate(rows, axis=0)
        hc_ref[...] = carry
    else:
        hs = b
    hs_ref[...] = hs
    ob_ref[...] = hs * _gelu(z_ref[:, w:2 * w])


def _lru_call(z_b, inj, cw, cb, wg, ba, bx, lp, cfg):
    rows = z_b.shape[0]
    w = B_W
    row = lambda width: pl.BlockSpec((SUPER_BLOCK, width), lambda i: (i, 0))
    ins = [z_b] + ([] if cfg.long_seq else [inj]) + [cw, cb, wg, ba, bx, lp]
    specs = [row(ZB_W)] + ([] if cfg.long_seq else [row(w)]) + [_const_spec(a.shape) for a in ins[-6:]]
    return pl.pallas_call(
        functools.partial(_lru_kernel, cfg=cfg),
        grid=(rows // SUPER_BLOCK,),
        in_specs=specs,
        out_specs=[row(w), row(w)],
        out_shape=[jax.ShapeDtypeStruct((rows, w), F32)] * 2,
        scratch_shapes=[pltpu.VMEM((8 + SUPER_BLOCK, w), F32), pltpu.VMEM((1, w), F32)],
        compiler_params=_params(1),
        name="rglru",
    )(*ins)


def _dn_kernel(z_ref, cw_ref, alog_ref, dtb_ref, g_ref, s0_ref, ltri_ref, ones_ref, tile_ref, tile_t_ref,
               bd_ref, expb_ref, expa_ref,
               o_ref, sout_ref,
               st_ref, prev_ref, q_s, k_s, v_s, g_s, eg_s, beta_s, kt_s, egl_s, o_s, *, cfg):
    sb = pl.program_id(0)
    c = cfg.chunk
    n = SUPER_BLOCK
    nch = cfg.n_chunks
    w = C_W

    @pl.when(sb % cfg.sb_per_seq == 0)
    def _():
        for p in range(cfg.n_par):
            prev_ref[p, 0:8, :] = jnp.zeros((8, prev_ref.shape[2]), F32)

    ones = ones_ref[...]
    for p in range(cfg.n_par):
        r = slice(p * n, (p + 1) * n)
        qkv = _silu(_causal_conv(z_ref[p, :, 0:3 * w], prev_ref.at[p], cw_ref))
        q = qkv[:, 0:w]
        kx = qkv[:, w:2 * w]
        q = q * lax.rsqrt(_dot_sel(q * q, ones, passes=2) + EPS) * (HEAD_DIM ** -0.5)
        kx = kx * lax.rsqrt(_dot_sel(kx * kx, ones, passes=2) + EPS)
        pc = z_ref[p, :, 4 * w:4 * w + LANES]
        beta = _dot_sel(jax.nn.sigmoid(pc), expb_ref[...], passes=2)
        gdec = _dot_sel(-jnp.exp(alog_ref[...]) * _softplus(pc + dtb_ref[...]), expa_ref[...], passes=2)
        if cfg.masked:
            real = _real_mask(cfg, n)
            beta = jnp.where(real, beta, 0.0)
            gdec = jnp.where(real, gdec, 0.0)
        gcum = _sel_dot(ltri_ref[...], gdec)
        g3 = gcum.reshape(nch, c, w)
        gl3 = g3[:, c - 1:c, :]
        q_s[r, :] = q
        k_s[r, :] = kx
        v_s[r, :] = qkv[:, 2 * w:3 * w]
        g_s[r, :] = gcum
        eg_s[r, :] = jnp.exp(gcum)
        beta_s[r, :] = beta
        kt_s[r, :] = (kx.reshape(nch, c, w) * jnp.exp(gl3 - g3)).reshape(n, w)
        egl_s[p * nch:(p + 1) * nch, :] = jnp.exp(gl3).reshape(nch, w)

    iio = lax.broadcasted_iota(jnp.int32, (c, 1), 0)
    bd = bd_ref[...]

    def chunk(k, p, st):
        rows = pl.ds(pl.multiple_of(p * n + k * c, c), c)
        q_c = q_s[rows, :]
        k_c = k_s[rows, :]
        g_c = g_s[rows, :]
        eg_c = eg_s[rows, :]
        beta_c = beta_s[rows, :]
        qk_s = _state_matmul(jnp.concatenate([q_c, k_c], axis=0), st, _NN)
        q_st = qk_s[0:c, :]
        k_st = qk_s[c:2 * c, :]
        lo = min(c, 8)
        tail = lambda x: x[lo:c, :]
        d_rows = ([k_c * k_c[j:j + 1, :] for j in range(lo)] + [q_c * k_c[j:j + 1, :] for j in range(lo)]
                  + [tail(k_c) * k_c[j:j + 1, :] for j in range(lo, c)]
                  + [tail(q_c) * k_c[j:j + 1, :] for j in range(lo, c)])
        dots = _head_sums(jnp.concatenate(d_rows, axis=0), ones_ref)
        e = beta_c * (v_s[rows, :] - eg_c * k_st)
        o = eg_c * q_st
        for j in range(lo):
            dec = jnp.exp(jnp.minimum(g_c - g_c[j:j + 1, :], 0.0))
            m_col = jnp.where(iio > j, beta_c * dots[j * c:(j + 1) * c, :] * dec, 0.0)
            e_j = e[j:j + 1, :]
            e = e - m_col * e_j
            qk_col = jnp.where(iio >= j, dots[(lo + j) * c:(lo + j + 1) * c, :] * dec, 0.0)
            o = o + qk_col * e_j
        if c > lo:
            nt = c - lo
            e_t, o_t, g_t, beta_t = tail(e), tail(o), tail(g_c), tail(beta_c)
            base = 2 * lo * c
            for j in range(lo, c):
                dec = jnp.exp(jnp.minimum(g_t - g_c[j:j + 1, :], 0.0))
                kk = dots[base + (j - lo) * nt:base + (j - lo + 1) * nt, :]
                qk = dots[base + (nt + j - lo) * nt:base + (nt + j - lo + 1) * nt, :]
                e_j = e_t[j - lo:j - lo + 1, :]
                e_t = e_t - jnp.where(iio[0:nt] > j - lo, beta_t * kk * dec, 0.0) * e_j
                o_t = o_t + jnp.where(iio[0:nt] >= j - lo, qk * dec, 0.0) * e_j
            e = jnp.concatenate([e[0:lo, :], e_t], axis=0)
            o = jnp.concatenate([o[0:lo, :], o_t], axis=0)
        o_s[rows, :] = o
        return _state_update(st, egl_s[pl.ds(p * nch + k, 1), :], kt_s[rows, :], e, bd_ref)

    _run_chunks(cfg, sb, chunk, st_ref, s0_ref, sout_ref, tile_ref, tile_t_ref, bd_ref)
    for p in range(cfg.n_par):
        o_ref[p] = _head_norm_gate(o_s[p * n:(p + 1) * n, :], z_ref[p, :, 3 * w:4 * w], g_ref[...], ones_ref)


def _dn_call(z_c, cw, alog_exp, dtb_exp, g_exp, s0, cfg):
    rows = z_c.shape[0]
    w = C_W
    cst = _mixer_consts(C_HEADS, cfg)
    lane_head = np.arange(w) // HEAD_DIM
    expb = (np.arange(128)[:, None] == lane_head[None, :]).astype(np.float32)
    expa = (np.arange(128)[:, None] == (lane_head[None, :] + C_HEADS)).astype(np.float32)
    expb = jnp.asarray(expb, BF16)
    expa = jnp.asarray(expa, BF16)
    n_seq = s0.shape[0]
    par_rows = cfg.n_par * SUPER_BLOCK
    vm = lambda shape: pltpu.VMEM(shape, F32)
    z3 = _seq_view(z_c, cfg)
    oc, s_new = pl.pallas_call(
        functools.partial(_dn_kernel, cfg=cfg),
        grid=(rows // par_rows,),
        in_specs=[_seq_block_spec(cfg, ZC_W), _const_spec(cw.shape),
                  _const_spec((1, LANES)), _const_spec((1, LANES)), _const_spec((1, w)), _state_specs(cfg, w),
                  _const_spec(cst["ltri"].shape), _const_spec(cst["ones"].shape),
                  _const_spec(cst["tile"].shape), _const_spec(cst["tile_t"].shape),
                  _const_spec(cst["bd"].shape), _const_spec(expb.shape), _const_spec(expa.shape)],
        out_specs=[_seq_block_spec(cfg, w), _state_specs(cfg, w)],
        out_shape=[jax.ShapeDtypeStruct(z3.shape[:2] + (w,), F32), jax.ShapeDtypeStruct((n_seq, w, HEAD_DIM), F32)],
        scratch_shapes=[vm((cfg.n_par, w // LANES, LANES, LANES)), vm((cfg.n_par, 8 + SUPER_BLOCK, 3 * w))]
        + [vm((par_rows, w))] * 7
        + [vm((cfg.n_par * cfg.n_chunks, w)), vm((par_rows, w))],
        compiler_params=_params(1),
        name="deltanet",
    )(z3, cw, alog_exp, dtb_exp, g_exp, s0, cst["ltri"], cst["ones"], cst["tile"], cst["tile_t"],
      cst["bd"], expb, expa)
    return oc.reshape(rows, w), s_new


def _outproj_kernel(x_ref, oa_ref, ob_ref, oc_ref, gt_ref, g_ref, sc_ref, sh_ref,
                    wo_ref, wq_ref, x1_ref, h2_ref, q_ref):
    mo = _dot_b(oa_ref[...], wo_ref[0:A_W, :])
    mo = mo + _dot_b(ob_ref[...], wo_ref[A_W:A_W + B_W, :])
    mo = mo + _dot_b(oc_ref[...], wo_ref[A_W + B_W:, :])
    x1 = x_ref[...] + gt_ref[...] * mo
    x1_ref[...] = x1
    h2 = _rmsnorm(x1, g_ref[...]) * sc_ref[...] + sh_ref[...]
    h2_ref[...] = h2.astype(BF16)
    q_ref[...] = _dot_b(h2, wq_ref[...])


def _outproj_call(x, oa, ob, oc, gt, g, sc, sh, wo, wq, tiles_per_group):
    t = x.shape[0]
    tm = TOKEN_TILE
    row = lambda w: pl.BlockSpec((tm, w), lambda i: (i, 0))
    return pl.pallas_call(
        _outproj_kernel,
        grid=(t // tm,),
        in_specs=[row(D_MODEL), row(A_W), row(B_W), row(C_W), _mod_spec(gt, tiles_per_group),
                  _const_spec((1, D_MODEL)), _mod_spec(sc, tiles_per_group), _mod_spec(sh, tiles_per_group),
                  _const_spec(wo.shape), _const_spec(wq.shape)],
        out_specs=[row(D_MODEL), row(D_MODEL), row(D_MODEL)],
        out_shape=[jax.ShapeDtypeStruct((t, D_MODEL), F32), jax.ShapeDtypeStruct((t, D_MODEL), BF16),
                   jax.ShapeDtypeStruct((t, D_MODEL), F32)],
        compiler_params=_params(1),
        name="outproj",
    )(x, oa, ob, oc, gt, g, sc, sh, wo, wq)


def _top16(s):
    vals = []
    for r in range(PEER_TOPK):
        m = jnp.max(s, axis=0, keepdims=True)
        s = jnp.where(s == m, -(32.0 + r) * 2.0 ** 95, s)
        vals.append(m)
    rank = jnp.where(s <= -(2.0 ** 99), s * -(2.0 ** -95) - 32.0, float(PEER_TOPK))
    return jnp.concatenate(vals, axis=0), rank


def _route_tile(s1, s2):
    nk, t = s1.shape
    v1, rank1 = _top16(s1)
    v2, rank2 = _top16(s2)
    c3 = v1[:, None, :] + v2[None, :, :]
    jrow = lax.broadcasted_iota(jnp.int32, (8, 1), 0)
    parts = [v1[0:1, :] + v2, v1[1:2, :] + v2[0:8, :]]
    for i in range(2, 8):
        parts.append(jnp.where(jrow < PEER_TOPK // (i + 1), v1[i:i + 1, :] + v2[0:8, :], -jnp.inf))
    parts.append(v1[8:16, :] + v2[0:1, :])
    cand = jnp.concatenate(parts, axis=0)
    m = None
    for _ in range(PEER_TOPK):
        m = jnp.max(cand, axis=0, keepdims=True)
        cand = jnp.where(cand == m, -jnp.inf, cand)
    tau = m
    sel = c3 >= tau[None, :, :]
    m0 = v1[0:1, :] + v2[0:1, :]
    zsum = jnp.sum(jnp.where(sel, jnp.exp(c3 - m0[None, :, :]), 0.0).reshape(-1, t), axis=0, keepdims=True)
    n_i = jnp.sum(jnp.where(sel, 1.0, 0.0), axis=1)
    rank1 = rank1.astype(BF16)
    cnt = jnp.zeros((nk, t), BF16)
    for i in range(PEER_TOPK):
        cnt = cnt + jnp.where(rank1 == i, n_i[i:i + 1, :].astype(BF16), jnp.zeros((), BF16))
    return (cnt.astype(F32), jnp.exp(s1 - v1[0:1, :]), rank2.astype(BF16),
            (jnp.exp(s2 - v2[0:1, :]) * (0.5 / zsum)).astype(BF16))


def _peer_kernel(h2_ref, q_ref, x1_ref, gt_ref, fg_ref, k1_ref, k2_ref, u_ref, vt_ref, o_ref,
                 cnt_s, e1_s, rank2_s, e2_s, acc_s, s1_s, s2_s, coef_s, *, final, n_steps):
    g = pl.program_id(0)
    tt = PEER_TOKEN_TILE
    rt = PEER_ROUTE_TILE
    nk = PEER_NKEYS
    nj = PEER_N // PEER_EXPERT_BLOCK
    a_per_step = PEER_EXPERT_BLOCK // nk
    j = jnp.minimum(g, n_steps - 2) % nj
    gb = jnp.maximum(g - 1, 0)

    @pl.when(g == 0)
    def _():
        coef_s[...] = jnp.zeros_like(coef_s)

    @pl.when(gb % nj == 0)
    def _():
        acc_s[...] = jnp.zeros_like(acc_s)

    @pl.when(jnp.logical_and(g % nj == 0, g < n_steps - 1))
    def _():

        def head(h, carry):
            qh = q_ref[:, pl.ds(pl.multiple_of(h * nk, nk), nk)]
            s1_s[...] = _dot_x3(k1_ref[...], qh, _NT)
            s2_s[...] = _dot_x3(k2_ref[...], qh, _NT)

            def lane_tile(c, carry2):
                cols = pl.ds(pl.multiple_of(c * rt, rt), rt)
                (cnt_s[h, :, cols], e1_s[h, :, cols], rank2_s[h, :, cols],
                 e2_s[h, :, cols]) = _route_tile(s1_s[:, cols], s2_s[:, cols])
                return carry2

            lax.fori_loop(0, tt // rt, lane_tile, 0)
            return carry

        lax.fori_loop(0, PEER_HEADS, head, 0)

    h2 = h2_ref[...]
    n_sub = PEER_EXPERT_BLOCK // PEER_SUB_BLOCK
    a_per_sub = PEER_SUB_BLOCK // nk
    sub = lambda s: slice(s * PEER_SUB_BLOCK, (s + 1) * PEER_SUB_BLOCK)
    cur = coef_s.at[g % 2]
    prev = coef_s.at[(g + 1) % 2]
    hids = [_dot(u_ref[sub(s), :], h2, _NT).astype(BF16) for s in range(n_sub)]
    acc_s[...] += _dot(vt_ref[...], prev[...])
    pack = 16
    for s in range(n_sub):
        for al in range(a_per_sub):
            a = j * a_per_step + s * a_per_sub + al
            wsum = jnp.zeros((nk // pack, pack, tt), BF16)
            for h in range(PEER_HEADS):
                cnt_row = jnp.broadcast_to(cnt_s[h, pl.ds(a, 1), :], (pack, tt)).astype(BF16)
                e1_row = jnp.broadcast_to(e1_s[h, pl.ds(a, 1), :], (pack, tt)).astype(BF16)
                rank2 = rank2_s[h].reshape(nk // pack, pack, tt)
                e2 = e2_s[h].reshape(nk // pack, pack, tt)
                wsum = jnp.where(rank2 < cnt_row[None], wsum + e2 * e1_row[None], wsum)
            r0 = s * PEER_SUB_BLOCK + al * nk
            cur[r0:r0 + nk, :] = _gelu_x2(hids[s][al * nk:(al + 1) * nk, :]) * wsum.reshape(nk, tt)

    @pl.when(jnp.logical_and(g > 0, gb % nj == nj - 1))
    def _():
        x2 = x1_ref[...] + gt_ref[...] * acc_s[...].T
        if final:
            x2 = _rmsnorm(x2, fg_ref[...])
        o_ref[...] = x2


def _peer_call(h2b, q, x1, gt, fg, k1p, k2p, u_b, v_b, layer, tiles_per_group, final):
    t = h2b.shape[0]
    tt = PEER_TOKEN_TILE
    nb = PEER_EXPERT_BLOCK
    nj = PEER_N // nb
    n_tiles = t // tt
    n_steps = n_tiles * nj + 1
    front = lambda g: jnp.minimum(g, n_steps - 2)
    back = lambda g: jnp.maximum(g - 1, 0)
    row_f = lambda w: pl.BlockSpec((tt, w), lambda g: (front(g) // nj, 0))
    row_b = lambda w: pl.BlockSpec((tt, w), lambda g: (back(g) // nj, 0))
    _, r, w = gt.shape
    tab = lambda dt: pltpu.VMEM((PEER_HEADS, PEER_NKEYS, tt), dt)
    return pl.pallas_call(
        functools.partial(_peer_kernel, final=final, n_steps=n_steps),
        grid=(n_steps,),
        in_specs=[row_f(D_MODEL), row_f(D_MODEL), row_b(D_MODEL),
                  pl.BlockSpec((None, r, w), lambda g: (back(g) // nj // tiles_per_group, 0, 0)),
                  _const_spec((1, D_MODEL)),
                  _const_spec((PEER_NKEYS, PEER_NKEYS)),
                  _const_spec((PEER_NKEYS, PEER_NKEYS)),
                  pl.BlockSpec((None, nb, D_MODEL), lambda g: (layer, front(g) % nj, 0)),
                  pl.BlockSpec((None, D_MODEL, nb), lambda g: (layer, 0, back(g) % nj))],
        out_specs=row_b(D_MODEL),
        out_shape=jax.ShapeDtypeStruct((t, D_MODEL), F32),
        scratch_shapes=[tab(F32), tab(F32), tab(BF16), tab(BF16), pltpu.VMEM((D_MODEL, tt), F32),
                        pltpu.VMEM((PEER_NKEYS, tt), F32), pltpu.VMEM((PEER_NKEYS, tt), F32),
                        pltpu.VMEM((2, nb, tt), BF16)],
        compiler_params=_params(1),
        name="peer",
    )(h2b, q, x1, gt, fg, k1p, k2p, u_b, v_b)


def _block_diag(wblk):
    n, d, e = wblk.shape
    eye = jnp.eye(n, dtype=wblk.dtype)
    return (eye[:, None, :, None] * wblk[:, :, None, :]).reshape(n * d, n * e)


def _pad_sample(z, hist=None):
    bsz = z.shape[0] // 4
    w = z.shape[1]
    z3 = z.reshape(bsz, 4, w)
    h3 = jnp.zeros((bsz, SAMPLE_HIST, w), F32)
    if hist is not None:
        h3 = h3.at[:, :, :hist.shape[-1]].set(hist)
    return jnp.concatenate([h3, z3, jnp.zeros((bsz, 1, w), F32)], axis=1).reshape(bsz * SAMPLE_SEQ_ROWS, w)


def _unpad_sample(o):
    bsz = o.shape[0] // SAMPLE_SEQ_ROWS
    return o.reshape(bsz, SAMPLE_SEQ_ROWS, -1)[:, SAMPLE_HIST:SAMPLE_HIST + 4].reshape(bsz * 4, -1)


def _layer_weights(l, w_in, w_out, peer_wq, peer_k1, peer_k2, peer_u, peer_v, lru_wa, lru_wx):
    w_in_p = jnp.pad(w_in[l], ((0, 0), (0, N_IN_PAD - N_IN)))
    wih = w_in_p.astype(BF16)
    wo = w_out[l].astype(BF16)
    wq = peer_wq[l].astype(BF16)
    wg = jnp.concatenate([_block_diag(lru_wa[l]), _block_diag(lru_wx[l])], axis=1).astype(BF16)
    half = PEER_NKEYS // 2
    k1p = jnp.pad(peer_k1[l], ((0, 0), (0, half)))
    k2p = jnp.pad(peer_k2[l], ((0, 0), (half, 0)))
    return dict(wih=wih, wo=wo, wq=wq, wg=wg, k1p=k1p, k2p=k2p)


def _trunk(x, mods, states, is_prompt, p, lw, l, final_g):
    t = x.shape[0]
    if is_prompt:
        sh1, sc1, gt1, sh2, sc2, gt2 = mods
    else:
        sh1, sc1, gt1, sh2, sc2 = (m.reshape(-1, TOKEN_TILE, D_MODEL) for m in mods[:5])
        gt2 = mods[5].reshape(-1, PEER_TOKEN_TILE, D_MODEL)
    if is_prompt:
        cfg = PROMPT_CFG
        tiles_tok = 2048 // TOKEN_TILE
        tiles_peer = 2048 // PEER_TOKEN_TILE
        n_seq = t // 2048
    else:
        cfg = SAMPLE_CFG
        tiles_tok = 1
        tiles_peer = 1
        n_seq = t // 4
    row1 = lambda v: v.reshape(1, -1)
    z_a, z_b, z_c = _inproj_call(x, row1(p['norm1_g']), sc1, sh1, lw['wih'], tiles_tok)
    if is_prompt:
        s_a = jnp.zeros((n_seq, A_W, HEAD_DIM), F32)
        s_c = jnp.zeros((n_seq, C_W, HEAD_DIM), F32)
        za_m, zb_m, zc_m = z_a, z_b, z_c
        inj = None
    else:
        st_hgrn, st_lru_h, st_lru_conv, st_dn, st_dn_conv = states
        s_a = jnp.swapaxes(st_hgrn, -1, -2).reshape(n_seq, A_W, HEAD_DIM)
        s_c = st_dn.reshape(n_seq, C_W, HEAD_DIM)
        za_m = _pad_sample(z_a)
        zb_m = _pad_sample(z_b, st_lru_conv)
        zc_m = _pad_sample(z_c, st_dn_conv)
        inj = jnp.zeros((n_seq, SAMPLE_SEQ_ROWS, B_W), F32).at[:, 0].set(st_lru_h).reshape(-1, B_W)
    rep = lambda v, h: row1(jnp.tile(v, h))
    oa, s_a_new = _hgrn_call(za_m, p['lb_param'], rep(p['a_norm_g'], A_HEADS), s_a,
                             PROMPT_HGRN_CFG if is_prompt else cfg, l)
    ob, hs = _lru_call(zb_m, inj, p['lru_conv_w'], row1(p['lru_conv_b']), lw['wg'],
                       row1(p['lru_ba']), row1(p['lru_bx']), row1(p['lru_L']), cfg)
    decay_cols = lambda v: jnp.zeros((1, LANES), F32).at[0, C_HEADS:2 * C_HEADS].set(v)
    oc, s_c_new = _dn_call(zc_m, p['dn_conv_w'], decay_cols(p['dn_A_log']), decay_cols(p['dn_dt_bias']),
                           rep(p['dn_norm_g'], C_HEADS), s_c, cfg)
    if is_prompt:
        seq = 2048
        h_t = hs.reshape(n_seq, seq, B_W)[:, -1]
        buf_b = z_b.reshape(n_seq, seq, ZB_W)[:, -SAMPLE_HIST:, :B_W]
        buf_c = z_c.reshape(n_seq, seq, ZC_W)[:, -SAMPLE_HIST:, :3 * C_W]
    else:
        oa, ob, oc = _unpad_sample(oa), _unpad_sample(ob), _unpad_sample(oc)
        h_t = hs.reshape(n_seq, SAMPLE_SEQ_ROWS, B_W)[:, -1]
        buf_b = z_b.reshape(n_seq, 4, ZB_W)[:, 1:, :B_W]
        buf_c = z_c.reshape(n_seq, 4, ZC_W)[:, 1:, :3 * C_W]
    new_states = (jnp.swapaxes(s_a_new.reshape(n_seq, A_HEADS, HEAD_DIM, HEAD_DIM), -1, -2), h_t, buf_b,
                  s_c_new.reshape(n_seq, C_HEADS, HEAD_DIM, HEAD_DIM), buf_c)
    x1, h2b, q = _outproj_call(x, oa, ob, oc, gt1, row1(p['norm2_g']), sc2, sh2,
                               lw['wo'], lw['wq'], tiles_tok)
    x2 = _peer_call(h2b, q, x1, gt2, row1(final_g), lw['k1p'], lw['k2p'], lw['u_b'], lw['v_b'], l,
                    tiles_peer, final=(l == DEPTH - 1))
    return x2, new_states


def kernel(x_prompt, x_sample, state_hgrn, state_lru_h, state_lru_conv, state_dn, state_dn_conv,
           c_prompt, c_sample, w_ada, b_ada, norm1_g, norm2_g, w_in, lb_param, a_norm_g,
           lru_conv_w, lru_conv_b, lru_wa, lru_ba, lru_wx, lru_bx, lru_L,
           dn_conv_w, dn_A_log, dn_dt_bias, dn_norm_g, w_out,
           peer_wq, peer_k1, peer_k2, peer_u, peer_v, final_norm_g):
    n_p, seq, _ = x_prompt.shape
    n_s, dec_seq, _ = x_sample.shape
    mod = _ada_call(jnp.concatenate([c_prompt, c_sample], axis=0), w_ada, b_ada)
    xp = x_prompt.reshape(n_p * seq, D_MODEL)
    xs = x_sample.reshape(n_s * dec_seq, D_MODEL)
    sample_states = (state_hgrn, state_lru_h, state_lru_conv, state_dn, state_dn_conv)
    u_all = peer_u.astype(BF16)
    vt_all = jnp.swapaxes(peer_v, 1, 2).astype(BF16)
    p_new, s_new = [], []
    for l in range(DEPTH):
        p = dict(norm1_g=norm1_g[l], norm2_g=norm2_g[l], lb_param=lb_param, a_norm_g=a_norm_g[l],
                 lru_conv_w=lru_conv_w[l], lru_conv_b=lru_conv_b[l], lru_ba=lru_ba[l], lru_bx=lru_bx[l],
                 lru_L=lru_L[l], dn_conv_w=dn_conv_w[l], dn_A_log=dn_A_log[l], dn_dt_bias=dn_dt_bias[l],
                 dn_norm_g=dn_norm_g[l])
        lw = _layer_weights(l, w_in, w_out, peer_wq, peer_k1, peer_k2, peer_u, peer_v, lru_wa, lru_wx)
        lw['u_b'], lw['v_b'] = u_all, vt_all
        parts = jnp.split(mod[l], 6, axis=-1)
        parts = [m + 1.0 if i in (1, 4) else m for i, m in enumerate(parts)]
        mods_p = [m[:n_p].reshape(n_p, 1, D_MODEL) for m in parts]
        mods_s = [jnp.repeat(m[n_p:], dec_seq, axis=0) for m in parts]
        xp, st_p = _trunk(xp, mods_p, None, True, p, lw, l, final_norm_g)
        xs, st_s = _trunk(xs, mods_s, tuple(s[l] for s in sample_states), False, p, lw, l, final_norm_g)
        p_new.append(st_p)
        s_new.append(st_s)
    stack = lambda sts: [jnp.stack([s[i] for s in sts]) for i in range(5)]
    p_st = stack(p_new)
    s_st = stack(s_new)
    return (xp.reshape(n_p, seq, D_MODEL), xs.reshape(n_s, dec_seq, D_MODEL), *p_st, *s_st)
```

```python
import functools

import jax
import jax.numpy as jnp
import numpy as np
from jax import lax
from jax.experimental import pallas as pl
from jax.experimental.pallas import tpu as pltpu

F32 = jnp.float32
BF16 = jnp.bfloat16

D_MODEL = 1024
DEPTH = 2
HEAD_DIM = 64
A_HEADS = 4
A_W = 256
B_W = 384
C_HEADS = 6
C_W = 384
LRU_C = 8.0
CONV_W = 4
N_IN = 3340
N_IN_PAD = 3456
ZA_W = 1024
ZB_W = 768
ZC_W = 1664
PEER_HEADS = 8
PEER_NKEYS = 128
PEER_TOPK = 16
PEER_N = PEER_NKEYS * PEER_NKEYS
EPS = 1e-6

LANES = 128
SUPER_BLOCK = 256
SAMPLE_SEQ_ROWS = 8
SAMPLE_HIST = CONV_W - 1
TOKEN_TILE = 512
PEER_TOKEN_TILE = 512
PEER_EXPERT_BLOCK = 2048
PEER_SUB_BLOCK = 512
PEER_ROUTE_TILE = 256
VMEM_LIMIT_BYTES = 56 * 1024 * 1024
assert TOKEN_TILE == PEER_TOKEN_TILE

_NN = (((1,), (0,)), ((), ()))
_NT = (((1,), (1,)), ((), ()))
_TN = (((0,), (0,)), ((), ()))


def _dot(a, b, dims=_NN):
    return lax.dot_general(a, b, dims, preferred_element_type=F32)


def _split2(x):
    hi = x.astype(BF16)
    lo = (x - hi.astype(F32)).astype(BF16)
    return hi, lo


def _split3(x):
    hi = x.astype(BF16)
    r = x - hi.astype(F32)
    mid = r.astype(BF16)
    lo = (r - mid.astype(F32)).astype(BF16)
    return hi, mid, lo


def _dot_x3(a, b, dims=_NN):
    ah, al = _split2(a)
    bh, bl = _split2(b)
    return _dot(ah, bh, dims) + _dot(ah, bl, dims) + _dot(al, bh, dims)


def _dot_b(a, b, dims=_NN):
    return _dot(a.astype(BF16), b.astype(BF16), dims)


def _dot_sel(a, sel, dims=_NN, passes=3):
    parts = _split3(a)[:passes]
    out = _dot(parts[0], sel, dims)
    for p in parts[1:]:
        out = out + _dot(p, sel, dims)
    return out


def _head_sums(x, ones_ref):
    ones = ones_ref[0:LANES, 0:LANES]
    xb = x.astype(BF16)
    return jnp.concatenate([_dot(xb[:, t:t + LANES], ones) for t in range(0, x.shape[1], LANES)], axis=1)


def _sel_dot(sel, b, passes=3):
    parts = _split3(b)[:passes]
    out = _dot(sel, parts[0])
    for p in parts[1:]:
        out = out + _dot(sel, p)
    return out


def _softplus(x):
    return jnp.maximum(x, 0.0) + jnp.log(1.0 + jnp.exp(-jnp.abs(x)))


def _silu(x):
    return x * jax.nn.sigmoid(x)


def _gelu_x2(x):
    c = float(np.sqrt(2.0 / np.pi))
    return x + x * jnp.tanh(x * (c + (0.044715 * c) * (x * x)))


def _gelu(x):
    return 0.5 * x * (1.0 + jnp.tanh(float(np.sqrt(2.0 / np.pi)) * (x + 0.044715 * (x * x * x))))


def _rmsnorm(x, g):
    return x * lax.rsqrt(jnp.mean(x * x, axis=-1, keepdims=True) + EPS) * g


def _const_spec(shape):
    nd = len(shape)
    return pl.BlockSpec(shape, lambda *_: (0,) * nd)


def _params(n_grid):
    return pltpu.CompilerParams(dimension_semantics=("arbitrary",) * n_grid,
                                vmem_limit_bytes=VMEM_LIMIT_BYTES)


def _ada_kernel(c_ref, w_ref, b_ref, o_ref):
    c = c_ref[...]
    o_ref[0] = _dot_x3(_silu(c), w_ref[0]) + b_ref[0]


def _ada_call(c_all, w_ada, b_ada):
    nb = c_all.shape[0]
    nt = 6 * D_MODEL // 1024
    return pl.pallas_call(
        _ada_kernel,
        grid=(DEPTH, nt),
        in_specs=[
            _const_spec((nb, D_MODEL)),
            pl.BlockSpec((1, D_MODEL, 1024), lambda l, j: (l, 0, j)),
            pl.BlockSpec((1, 1, 1024), lambda l, j: (l, 0, j)),
        ],
        out_specs=pl.BlockSpec((1, nb, 1024), lambda l, j: (l, 0, j)),
        out_shape=jax.ShapeDtypeStruct((DEPTH, nb, 6 * D_MODEL), F32),
        compiler_params=_params(2),
        name="ada_mod",
    )(c_all, w_ada, b_ada.reshape(DEPTH, 1, 6 * D_MODEL))


def _inproj_kernel(x_ref, g_ref, sc_ref, sh_ref, w_ref, za_ref, zb_ref, zc_ref):
    h = (_rmsnorm(x_ref[...], g_ref[...]) * (1.0 + sc_ref[...]) + sh_ref[...]).astype(BF16)
    for o_ref, lo, hi in ((za_ref, 0, ZA_W), (zb_ref, ZA_W, ZA_W + ZB_W), (zc_ref, ZA_W + ZB_W, N_IN_PAD)):
        o_ref[...] = _dot(h, w_ref[:, lo:hi])


MOD_SH1, MOD_SC1, MOD_GT1, MOD_SH2, MOD_SC2, MOD_GT2 = range(6)


def _mod_spec(mod, tiles_per_group, chunk):
    _, r, _ = mod.shape
    return pl.BlockSpec((None, r, D_MODEL), lambda i: (i // tiles_per_group, 0, chunk))


def _inproj_call(x, g, mod, wh, tiles_per_group):
    t = x.shape[0]
    tm = TOKEN_TILE
    row = lambda w: pl.BlockSpec((tm, w), lambda i: (i, 0))
    return pl.pallas_call(
        _inproj_kernel,
        grid=(t // tm,),
        in_specs=[row(D_MODEL), _const_spec((1, D_MODEL)), _mod_spec(mod, tiles_per_group, MOD_SC1),
                  _mod_spec(mod, tiles_per_group, MOD_SH1), _const_spec(wh.shape)],
        out_specs=[row(ZA_W), row(ZB_W), row(ZC_W)],
        out_shape=[jax.ShapeDtypeStruct((t, ZA_W), F32), jax.ShapeDtypeStruct((t, ZB_W), F32),
                   jax.ShapeDtypeStruct((t, ZC_W), F32)],
        compiler_params=_params(1),
        name="inproj",
    )(x, g, mod, mod, wh)


class _SeqCfg:
    def __init__(self, seq_rows, hist, real, chunk, n_par):
        self.seq_rows = seq_rows
        self.n_par = n_par
        self.hist = hist
        self.real = real
        self.chunk = chunk
        self.masked = hist > 0 or hist + real < seq_rows
        self.n_chunks = SUPER_BLOCK // chunk
        self.long_seq = seq_rows > SUPER_BLOCK
        self.sb_per_seq = max(seq_rows // SUPER_BLOCK, 1)
        self.seq_per_sb = max(SUPER_BLOCK // seq_rows, 1)
        assert (self.long_seq or chunk == seq_rows) and (self.long_seq or n_par == 1)


PROMPT_CFG = _SeqCfg(seq_rows=2048, hist=0, real=2048, chunk=16, n_par=4)
PROMPT_HGRN_CFG = _SeqCfg(seq_rows=2048, hist=0, real=2048, chunk=16, n_par=8)
SAMPLE_CFG = _SeqCfg(seq_rows=SAMPLE_SEQ_ROWS, hist=SAMPLE_HIST, real=4, chunk=SAMPLE_SEQ_ROWS, n_par=1)


def _row_in_seq(cfg, n_rows):
    r = lax.broadcasted_iota(jnp.int32, (n_rows, 1), 0)
    return r % min(cfg.seq_rows, SUPER_BLOCK)


def _real_mask(cfg, n_rows):
    r = _row_in_seq(cfg, n_rows)
    return (r >= cfg.hist) & (r < cfg.hist + cfg.real)


def _causal_conv(x, xp_ref, w_ref):
    n = x.shape[0]
    xp_ref[8:8 + n, :] = x
    y = x * w_ref[CONV_W - 1:CONV_W, :]
    for k in range(CONV_W - 1):
        s = CONV_W - 1 - k
        y = y + xp_ref[8 - s:8 - s + n, :] * w_ref[k:k + 1, :]
    xp_ref[0:8, :] = x[n - 8:n, :]
    return y


def _lane_tiles(x):
    return [x[:, t:t + LANES] for t in range(0, x.shape[1], LANES)]


def _expand_state(s_cat, tile_ref, bd_ref):
    tile = tile_ref[:, 0:LANES]
    bd = bd_ref[0:LANES, 0:LANES]
    return jnp.stack([_dot_sel(s_cat[r:r + LANES, :], tile, passes=2) * bd
                      for r in range(0, s_cat.shape[0], LANES)])


def _compress_state(s_tiles, tile_t_ref):
    tile_t = tile_t_ref[0:LANES, :]
    return jnp.concatenate([_dot_sel(s_tiles[t], tile_t, passes=2) for t in range(s_tiles.shape[0])], axis=0)


def _state_matmul(x, s_tiles, dims):
    return jnp.concatenate([_dot_b(xt, s_tiles[t], dims) for t, xt in enumerate(_lane_tiles(x))], axis=1)


def _state_update(s_tiles, decay_row, a, b, bd_ref):
    bd = bd_ref[0:LANES, 0:LANES]
    return jnp.stack([s_tiles[t] * dt + bd * _dot_b(at, bt, _TN)
                      for t, (dt, at, bt) in enumerate(zip(_lane_tiles(decay_row), _lane_tiles(a), _lane_tiles(b)))])


def _head_norm_gate(o, z, g, ones_ref):
    ms = _dot_sel(o * o, ones_ref[...], passes=2) * (1.0 / HEAD_DIM)
    return o * lax.rsqrt(ms + EPS) * g * _silu(z)


def _run_chunks(cfg, sb, step, st_ref, s0_ref, sout_ref, tile_ref, tile_t_ref, bd_ref):
    if cfg.long_seq:
        @pl.when(sb % cfg.sb_per_seq == 0)
        def _():
            for p in range(cfg.n_par):
                st_ref[p] = _expand_state(s0_ref[p], tile_ref, bd_ref)

        def body(k, carry):
            for p in range(cfg.n_par):
                st_ref[p] = step(k, p, st_ref[p])
            return carry

        lax.fori_loop(0, cfg.n_chunks, body, 0)

        @pl.when(sb % cfg.sb_per_seq == cfg.sb_per_seq - 1)
        def _():
            for p in range(cfg.n_par):
                sout_ref[p] = _compress_state(st_ref[p], tile_t_ref)
    else:
        def body(k, carry):
            st_new = step(k, 0, _expand_state(s0_ref[k], tile_ref, bd_ref))
            sout_ref[k] = _compress_state(st_new, tile_t_ref)
            return carry

        lax.fori_loop(0, cfg.n_chunks, body, 0, unroll=4)


def _hgrn_kernel(z_ref, lbp_ref, g_ref, s0_ref, ltri_ref, ones_ref, tile_ref, tile_t_ref, bd_ref,
                 o_ref, sout_ref,
                 st_ref, q_s, k_s, g_s, qe_s, kt_s, egl_s, o_s, *, cfg, layer):
    sb = pl.program_id(0)
    c = cfg.chunk
    n = SUPER_BLOCK
    nch = cfg.n_chunks
    w = A_W
    lbp = lbp_ref[...]
    e = jnp.exp(lbp - jnp.max(lbp, axis=0, keepdims=True))
    lbs = e / jnp.sum(e, axis=0, keepdims=True)
    lb = jnp.sum(lbs[0:layer + 1, :], axis=0, keepdims=True) - lbs[0:1, :]

    for p in range(cfg.n_par):
        r = slice(p * n, (p + 1) * n)
        aq = z_ref[p, :, 0:w]
        af = z_ref[p, :, w:2 * w]
        f = lb + (1.0 - lb) * jax.nn.sigmoid(af)
        lf = jnp.log(f)
        kk = 1.0 - f
        if cfg.masked:
            real = _real_mask(cfg, n)
            lf = jnp.where(real, lf, 0.0)
            kk = jnp.where(real, kk, 0.0)
        q = _silu(aq) * (HEAD_DIM ** -0.5)
        gcum = _sel_dot(ltri_ref[...], lf)
        g3 = gcum.reshape(nch, c, w)
        gl3 = g3[:, c - 1:c, :]
        q_s[r, :] = q
        k_s[r, :] = kk
        g_s[r, :] = gcum
        qe_s[r, :] = q * jnp.exp(gcum)
        kt_s[r, :] = (kk.reshape(nch, c, w) * jnp.exp(gl3 - g3)).reshape(n, w)
        egl_s[p * nch:(p + 1) * nch, :] = jnp.exp(gl3).reshape(nch, w)

    jio = lax.broadcasted_iota(jnp.int32, (c, 1), 0)
    bd = bd_ref[...]

    def chunk(k, p, st):
        rows = pl.ds(pl.multiple_of(p * n + k * c, c), c)
        q_c = q_s[rows, :]
        k_c = k_s[rows, :]
        g_c = g_s[rows, :]
        v_c = z_ref[p, pl.ds(pl.multiple_of(k * c, c), c), 2 * w:3 * w]
        o_inter = _state_matmul(qe_s[rows, :], st, _NT)
        lo = min(c, 8)
        d_rows = []
        for i in range(c):
            nj = lo if i < lo else c
            dec = jnp.exp(jnp.minimum(g_c[i:i + 1, :] - g_c[0:nj, :], 0.0))
            d_rows.append(jnp.where(jio[0:nj] <= i, k_c[0:nj, :] * dec * q_c[i:i + 1, :], 0.0))
        a_b = _head_sums(jnp.concatenate(d_rows, axis=0), ones_ref)
        o_diag = jnp.sum(a_b[0:lo * lo, :].reshape(lo, lo, w) * v_c[None, 0:lo, :], axis=1)
        if c > lo:
            o_hi = jnp.sum(a_b[lo * lo:, :].reshape(c - lo, c, w) * v_c[None, :, :], axis=1)
            o_diag = jnp.concatenate([o_diag, o_hi], axis=0)
        o_s[rows, :] = o_inter + o_diag
        return _state_update(st, egl_s[pl.ds(p * nch + k, 1), :], v_c, kt_s[rows, :], bd_ref)

    _run_chunks(cfg, sb, chunk, st_ref, s0_ref, sout_ref, tile_ref, tile_t_ref, bd_ref)
    for p in range(cfg.n_par):
        o_ref[p] = _head_norm_gate(o_s[p * n:(p + 1) * n, :], z_ref[p, :, 3 * w:4 * w], g_ref[...], ones_ref)


def _mixer_consts(heads, cfg):
    w = heads * HEAD_DIM
    lane_head = np.arange(w) // HEAD_DIM
    ones = (lane_head[:, None] == lane_head[None, :]).astype(np.float32)
    tile = (np.arange(HEAD_DIM)[:, None] == (np.arange(w) % HEAD_DIM)[None, :]).astype(np.float32)
    r = np.arange(SUPER_BLOCK)
    ltri = ((r[:, None] // cfg.chunk == r[None, :] // cfg.chunk) & (r[None, :] <= r[:, None])).astype(np.float32)
    return dict(ltri=jnp.asarray(ltri, BF16), ones=jnp.asarray(ones, BF16), tile=jnp.asarray(tile, BF16),
                tile_t=jnp.asarray(tile.T, BF16), bd=jnp.asarray(ones, F32))


def _state_specs(cfg, w):
    blk = (cfg.n_par if cfg.long_seq else cfg.seq_per_sb, w, HEAD_DIM)
    if cfg.long_seq:
        imap = lambda i: (i // cfg.sb_per_seq, 0, 0)
    else:
        imap = lambda i: (i, 0, 0)
    return pl.BlockSpec(blk, imap)


def _seq_view(x, cfg):
    if not cfg.long_seq:
        return x[None]
    assert x.shape[0] % (cfg.n_par * cfg.seq_rows) == 0, "sequence count must be a multiple of n_par"
    return x.reshape(-1, cfg.seq_rows, x.shape[-1])


def _seq_block_spec(cfg, width):
    if cfg.long_seq:
        return pl.BlockSpec((cfg.n_par, SUPER_BLOCK, width), lambda i: (i // cfg.sb_per_seq, i % cfg.sb_per_seq, 0))
    return pl.BlockSpec((1, SUPER_BLOCK, width), lambda i: (0, i, 0))


def _hgrn_call(z_a, lb_param, g_exp, s0_t, cfg, layer):
    rows = z_a.shape[0]
    w = A_W
    cst = _mixer_consts(A_HEADS, cfg)
    n_seq = s0_t.shape[0]
    par_rows = cfg.n_par * SUPER_BLOCK
    vm = lambda shape: pltpu.VMEM(shape, F32)
    kern = functools.partial(_hgrn_kernel, cfg=cfg, layer=layer)
    z3 = _seq_view(z_a, cfg)
    oa, s_new = pl.pallas_call(
        kern,
        grid=(rows // par_rows,),
        in_specs=[_seq_block_spec(cfg, ZA_W), _const_spec((DEPTH, w)),
                  _const_spec((1, w)), _state_specs(cfg, w), _const_spec(cst["ltri"].shape),
                  _const_spec(cst["ones"].shape), _const_spec(cst["tile"].shape),
                  _const_spec(cst["tile_t"].shape), _const_spec(cst["bd"].shape)],
        out_specs=[_seq_block_spec(cfg, w), _state_specs(cfg, w)],
        out_shape=[jax.ShapeDtypeStruct(z3.shape[:2] + (w,), F32), jax.ShapeDtypeStruct((n_seq, w, HEAD_DIM), F32)],
        scratch_shapes=[vm((cfg.n_par, w // LANES, LANES, LANES))] + [vm((par_rows, w))] * 5
        + [vm((cfg.n_par * cfg.n_chunks, w)), vm((par_rows, w))],
        compiler_params=_params(1),
        name="hgrn2",
    )(z3, lb_param, g_exp, s0_t, cst["ltri"], cst["ones"], cst["tile"], cst["tile_t"], cst["bd"])
    return oa.reshape(rows, w), s_new


def _lru_kernel(*refs, cfg):
    if cfg.long_seq:
        (z_ref, cw_ref, cb_ref, wg_ref, ba_ref, bx_ref, lp_ref,
         ob_ref, hs_ref, prev_ref, hc_ref) = refs
        inj_ref = None
    else:
        (z_ref, inj_ref, cw_ref, cb_ref, wg_ref, ba_ref, bx_ref, lp_ref,
         ob_ref, hs_ref, prev_ref, hc_ref) = refs
    sb = pl.program_id(0)
    n = SUPER_BLOCK
    w = B_W

    @pl.when(sb % cfg.sb_per_seq == 0)
    def _():
        prev_ref[0:8, :] = jnp.zeros((8, prev_ref.shape[1]), F32)
        hc_ref[...] = jnp.zeros_like(hc_ref)

    xc = _causal_conv(z_ref[:, 0:w], prev_ref, cw_ref) + cb_ref[...]
    gates = _dot_b(xc, wg_ref[...])
    r = jax.nn.sigmoid(gates[:, 0:w] + ba_ref[...])
    ig = jax.nn.sigmoid(gates[:, w:2 * w] + bx_ref[...])
    log_a = -LRU_C * r * _softplus(-lp_ref[...])
    a = jnp.exp(log_a)
    b = jnp.sqrt(1.0 - jnp.exp(2.0 * log_a)) * ig * xc
    if cfg.masked:
        real = _real_mask(cfg, n)
        a = jnp.where(real, a, 1.0)
        b = jnp.where(real, b, 0.0)
    if inj_ref is not None:
        b = b + inj_ref[...]
    group = 8
    assert cfg.seq_rows % group == 0
    rig = lax.broadcasted_iota(jnp.int32, (n, 1), 0) % group
    d = 1
    while d < group:
        has = rig >= d
        a_sh = jnp.where(has, pltpu.roll(a, d, 0), 1.0)
        b_sh = jnp.where(has, pltpu.roll(b, d, 0), 0.0)
        b = b + a * b_sh
        a = a * a_sh
        d *= 2
    if cfg.long_seq:
        a3 = a.reshape(n // group, group, w)
        b3 = b.reshape(n // group, group, w)
        carry = hc_ref[...]
        rows = []
        for i in range(n // group):
            h_i = b3[i] + a3[i] * carry
            rows.append(h_i)
            carry = h_i[group - 1:group, :]
        hs = jnp.concatenate(rows, axis=0)
        hc_ref[...] = carry
    else:
        hs = b
    hs_ref[...] = hs
    ob_ref[...] = hs * _gelu(z_ref[:, w:2 * w])


def _lru_call(z_b, inj, cw, cb, wg, ba, bx, lp, cfg):
    rows = z_b.shape[0]
    w = B_W
    row = lambda width: pl.BlockSpec((SUPER_BLOCK, width), lambda i: (i, 0))
    ins = [z_b] + ([] if cfg.long_seq else [inj]) + [cw, cb, wg, ba, bx, lp]
    specs = [row(ZB_W)] + ([] if cfg.long_seq else [row(w)]) + [_const_spec(a.shape) for a in ins[-6:]]
    return pl.pallas_call(
        functools.partial(_lru_kernel, cfg=cfg),
        grid=(rows // SUPER_BLOCK,),
        in_specs=specs,
        out_specs=[row(w), row(w)],
        out_shape=[jax.ShapeDtypeStruct((rows, w), F32)] * 2,
        scratch_shapes=[pltpu.VMEM((8 + SUPER_BLOCK, w), F32), pltpu.VMEM((1, w), F32)],
        compiler_params=_params(1),
        name="rglru",
    )(*ins)


def _dn_kernel(z_ref, cw_ref, alog_ref, dtb_ref, g_ref, s0_ref, ltri_ref, ones_ref, tile_ref, tile_t_ref,
               bd_ref, expb_ref, expa_ref,
               o_ref, sout_ref,
               st_ref, prev_ref, q_s, k_s, v_s, g_s, eg_s, beta_s, kt_s, egl_s, o_s, *, cfg):
    sb = pl.program_id(0)
    c = cfg.chunk
    n = SUPER_BLOCK
    nch = cfg.n_chunks
    w = C_W

    @pl.when(sb % cfg.sb_per_seq == 0)
    def _():
        for p in range(cfg.n_par):
            prev_ref[p, 0:8, :] = jnp.zeros((8, prev_ref.shape[2]), F32)

    ones = ones_ref[...]
    for p in range(cfg.n_par):
        r = slice(p * n, (p + 1) * n)
        qkv = _silu(_causal_conv(z_ref[p, :, 0:3 * w], prev_ref.at[p], cw_ref))
        q = qkv[:, 0:w]
        kx = qkv[:, w:2 * w]
        q = q * lax.rsqrt(_dot_sel(q * q, ones, passes=2) + EPS) * (HEAD_DIM ** -0.5)
        kx = kx * lax.rsqrt(_dot_sel(kx * kx, ones, passes=2) + EPS)
        pc = z_ref[p, :, 4 * w:4 * w + LANES]
        beta = _dot_sel(jax.nn.sigmoid(pc), expb_ref[...], passes=2)
        gdec = _dot_sel(-jnp.exp(alog_ref[...]) * _softplus(pc + dtb_ref[...]), expa_ref[...], passes=2)
        if cfg.masked:
            real = _real_mask(cfg, n)
            beta = jnp.where(real, beta, 0.0)
            gdec = jnp.where(real, gdec, 0.0)
        gcum = _sel_dot(ltri_ref[...], gdec)
        g3 = gcum.reshape(nch, c, w)
        gl3 = g3[:, c - 1:c, :]
        q_s[r, :] = q
        k_s[r, :] = kx
        v_s[r, :] = qkv[:, 2 * w:3 * w]
        g_s[r, :] = gcum
        eg_s[r, :] = jnp.exp(gcum)
        beta_s[r, :] = beta
        kt_s[r, :] = (kx.reshape(nch, c, w) * jnp.exp(gl3 - g3)).reshape(n, w)
        egl_s[p * nch:(p + 1) * nch, :] = jnp.exp(gl3).reshape(nch, w)

    iio = lax.broadcasted_iota(jnp.int32, (c, 1), 0)
    bd = bd_ref[...]

    def chunk(k, p, st):
        rows = pl.ds(pl.multiple_of(p * n + k * c, c), c)
        q_c = q_s[rows, :]
        k_c = k_s[rows, :]
        g_c = g_s[rows, :]
        eg_c = eg_s[rows, :]
        beta_c = beta_s[rows, :]
        qk_s = _state_matmul(jnp.concatenate([q_c, k_c], axis=0), st, _NN)
        q_st = qk_s[0:c, :]
        k_st = qk_s[c:2 * c, :]
        lo = min(c, 8)
        tail = lambda x: x[lo:c, :]
        d_rows = ([k_c * k_c[j:j + 1, :] for j in range(lo)] + [q_c * k_c[j:j + 1, :] for j in range(lo)]
                  + [tail(k_c) * k_c[j:j + 1, :] for j in range(lo, c)]
                  + [tail(q_c) * k_c[j:j + 1, :] for j in range(lo, c)])
        dots = _head_sums(jnp.concatenate(d_rows, axis=0), ones_ref)
        e = beta_c * (v_s[rows, :] - eg_c * k_st)
        o = eg_c * q_st
        for j in range(lo):
            dec = jnp.exp(jnp.minimum(g_c - g_c[j:j + 1, :], 0.0))
            m_col = jnp.where(iio > j, beta_c * dots[j * c:(j + 1) * c, :] * dec, 0.0)
            e_j = e[j:j + 1, :]
            e = e - m_col * e_j
            qk_col = jnp.where(iio >= j, dots[(lo + j) * c:(lo + j + 1) * c, :] * dec, 0.0)
            o = o + qk_col * e_j
        if c > lo:
            nt = c - lo
            e_t, o_t, g_t, beta_t = tail(e), tail(o), tail(g_c), tail(beta_c)
            base = 2 * lo * c
            for j in range(lo, c):
                dec = jnp.exp(jnp.minimum(g_t - g_c[j:j + 1, :], 0.0))
                kk = dots[base + (j - lo) * nt:base + (j - lo + 1) * nt, :]
                qk = dots[base + (nt + j - lo) * nt:base + (nt + j - lo + 1) * nt, :]
                e_j = e_t[j - lo:j - lo + 1, :]
                e_t = e_t - jnp.where(iio[0:nt] > j - lo, beta_t * kk * dec, 0.0) * e_j
                o_t = o_t + jnp.where(iio[0:nt] >= j - lo, qk * dec, 0.0) * e_j
            e = jnp.concatenate([e[0:lo, :], e_t], axis=0)
            o = jnp.concatenate([o[0:lo, :], o_t], axis=0)
        o_s[rows, :] = o
        return _state_update(st, egl_s[pl.ds(p * nch + k, 1), :], kt_s[rows, :], e, bd_ref)

    _run_chunks(cfg, sb, chunk, st_ref, s0_ref, sout_ref, tile_ref, tile_t_ref, bd_ref)
    for p in range(cfg.n_par):
        o_ref[p] = _head_norm_gate(o_s[p * n:(p + 1) * n, :], z_ref[p, :, 3 * w:4 * w], g_ref[...], ones_ref)


def _dn_call(z_c, cw, alog_exp, dtb_exp, g_exp, s0, cfg):
    rows = z_c.shape[0]
    w = C_W
    cst = _mixer_consts(C_HEADS, cfg)
    lane_head = np.arange(w) // HEAD_DIM
    expb = (np.arange(128)[:, None] == lane_head[None, :]).astype(np.float32)
    expa = (np.arange(128)[:, None] == (lane_head[None, :] + C_HEADS)).astype(np.float32)
    expb = jnp.asarray(expb, BF16)
    expa = jnp.asarray(expa, BF16)
    n_seq = s0.shape[0]
    par_rows = cfg.n_par * SUPER_BLOCK
    vm = lambda shape: pltpu.VMEM(shape, F32)
    z3 = _seq_view(z_c, cfg)
    oc, s_new = pl.pallas_call(
        functools.partial(_dn_kernel, cfg=cfg),
        grid=(rows // par_rows,),
        in_specs=[_seq_block_spec(cfg, ZC_W), _const_spec(cw.shape),
                  _const_spec((1, LANES)), _const_spec((1, LANES)), _const_spec((1, w)), _state_specs(cfg, w),
                  _const_spec(cst["ltri"].shape), _const_spec(cst["ones"].shape),
                  _const_spec(cst["tile"].shape), _const_spec(cst["tile_t"].shape),
                  _const_spec(cst["bd"].shape), _const_spec(expb.shape), _const_spec(expa.shape)],
        out_specs=[_seq_block_spec(cfg, w), _state_specs(cfg, w)],
        out_shape=[jax.ShapeDtypeStruct(z3.shape[:2] + (w,), F32), jax.ShapeDtypeStruct((n_seq, w, HEAD_DIM), F32)],
        scratch_shapes=[vm((cfg.n_par, w // LANES, LANES, LANES)), vm((cfg.n_par, 8 + SUPER_BLOCK, 3 * w))]
        + [vm((par_rows, w))] * 7
        + [vm((cfg.n_par * cfg.n_chunks, w)), vm((par_rows, w))],
        compiler_params=_params(1),
        name="deltanet",
    )(z3, cw, alog_exp, dtb_exp, g_exp, s0, cst["ltri"], cst["ones"], cst["tile"], cst["tile_t"],
      cst["bd"], expb, expa)
    return oc.reshape(rows, w), s_new


def _outproj_kernel(x_ref, oa_ref, ob_ref, oc_ref, gt_ref, g_ref, sc_ref, sh_ref,
                    wo_ref, wq_ref, x1_ref, h2_ref, q_ref):
    mo = _dot_b(oa_ref[...], wo_ref[0:A_W, :])
    mo = mo + _dot_b(ob_ref[...], wo_ref[A_W:A_W + B_W, :])
    mo = mo + _dot_b(oc_ref[...], wo_ref[A_W + B_W:, :])
    x1 = x_ref[...] + gt_ref[...] * mo
    x1_ref[...] = x1
    h2 = _rmsnorm(x1, g_ref[...]) * (1.0 + sc_ref[...]) + sh_ref[...]
    h2_ref[...] = h2.astype(BF16)
    q_ref[...] = _dot_b(h2, wq_ref[...])


def _outproj_call(x, oa, ob, oc, mod, g, wo, wq, tiles_per_group):
    t = x.shape[0]
    tm = TOKEN_TILE
    row = lambda w: pl.BlockSpec((tm, w), lambda i: (i, 0))
    return pl.pallas_call(
        _outproj_kernel,
        grid=(t // tm,),
        in_specs=[row(D_MODEL), row(A_W), row(B_W), row(C_W), _mod_spec(mod, tiles_per_group, MOD_GT1),
                  _const_spec((1, D_MODEL)), _mod_spec(mod, tiles_per_group, MOD_SC2),
                  _mod_spec(mod, tiles_per_group, MOD_SH2),
                  _const_spec(wo.shape), _const_spec(wq.shape)],
        out_specs=[row(D_MODEL), row(D_MODEL), row(D_MODEL)],
        out_shape=[jax.ShapeDtypeStruct((t, D_MODEL), F32), jax.ShapeDtypeStruct((t, D_MODEL), BF16),
                   jax.ShapeDtypeStruct((t, D_MODEL), F32)],
        compiler_params=_params(1),
        name="outproj",
    )(x, oa, ob, oc, mod, g, mod, mod, wo, wq)


def _top16(s):
    vals = []
    for r in range(PEER_TOPK):
        m = jnp.max(s, axis=0, keepdims=True)
        s = jnp.where(s == m, -(32.0 + r) * 2.0 ** 95, s)
        vals.append(m)
    rank = jnp.where(s <= -(2.0 ** 99), s * -(2.0 ** -95) - 32.0, float(PEER_TOPK))
    return jnp.concatenate(vals, axis=0), rank


def _route_tile(s1, s2):
    nk, t = s1.shape
    v1, rank1 = _top16(s1)
    v2, rank2 = _top16(s2)
    c3 = v1[:, None, :] + v2[None, :, :]
    jrow = lax.broadcasted_iota(jnp.int32, (8, 1), 0)
    parts = [v1[0:1, :] + v2, v1[1:2, :] + v2[0:8, :]]
    for i in range(2, 8):
        parts.append(jnp.where(jrow < PEER_TOPK // (i + 1), v1[i:i + 1, :] + v2[0:8, :], -jnp.inf))
    parts.append(v1[8:16, :] + v2[0:1, :])
    cand = jnp.concatenate(parts, axis=0)
    m = None
    for _ in range(PEER_TOPK):
        m = jnp.max(cand, axis=0, keepdims=True)
        cand = jnp.where(cand == m, -jnp.inf, cand)
    tau = m
    sel = c3 >= tau[None, :, :]
    m0 = v1[0:1, :] + v2[0:1, :]
    zsum = jnp.sum(jnp.where(sel, jnp.exp(c3 - m0[None, :, :]), 0.0).reshape(-1, t), axis=0, keepdims=True)
    n_i = jnp.sum(jnp.where(sel, 1.0, 0.0), axis=1)
    rank1 = rank1.astype(BF16)
    cnt = jnp.zeros((nk, t), BF16)
    for i in range(PEER_TOPK):
        cnt = cnt + jnp.where(rank1 == i, n_i[i:i + 1, :].astype(BF16), jnp.zeros((), BF16))
    return (cnt.astype(F32), jnp.exp(s1 - v1[0:1, :]), rank2.astype(BF16),
            (jnp.exp(s2 - v2[0:1, :]) * (0.5 / zsum)).astype(BF16))


def _peer_kernel(h2_ref, q_ref, x1_ref, gt_ref, fg_ref, k1_ref, k2_ref, u_ref, vt_ref, o_ref,
                 cnt_s, e1_s, rank2_s, e2_s, acc_s, s1_s, s2_s, coef_s, *, final, n_steps):
    g = pl.program_id(0)
    tt = PEER_TOKEN_TILE
    rt = PEER_ROUTE_TILE
    nk = PEER_NKEYS
    nj = PEER_N // PEER_EXPERT_BLOCK
    a_per_step = PEER_EXPERT_BLOCK // nk
    j = jnp.minimum(g, n_steps - 2) % nj
    gb = jnp.maximum(g - 1, 0)

    @pl.when(g == 0)
    def _():
        coef_s[...] = jnp.zeros_like(coef_s)

    @pl.when(gb % nj == 0)
    def _():
        acc_s[...] = jnp.zeros_like(acc_s)

    @pl.when(jnp.logical_and(g % nj == 0, g < n_steps - 1))
    def _():

        def head(h, carry):
            qh = q_ref[:, pl.ds(pl.multiple_of(h * nk, nk), nk)]
            s1_s[...] = _dot_x3(k1_ref[...], qh, _NT)
            s2_s[...] = _dot_x3(k2_ref[...], qh, _NT)

            def lane_tile(c, carry2):
                cols = pl.ds(pl.multiple_of(c * rt, rt), rt)
                (cnt_s[h, :, cols], e1_s[h, :, cols], rank2_s[h, :, cols],
                 e2_s[h, :, cols]) = _route_tile(s1_s[:, cols], s2_s[:, cols])
                return carry2

            lax.fori_loop(0, tt // rt, lane_tile, 0)
            return carry

        lax.fori_loop(0, PEER_HEADS, head, 0)

    h2 = h2_ref[...]
    n_sub = PEER_EXPERT_BLOCK // PEER_SUB_BLOCK
    a_per_sub = PEER_SUB_BLOCK // nk
    sub = lambda s: slice(s * PEER_SUB_BLOCK, (s + 1) * PEER_SUB_BLOCK)
    cur = coef_s.at[g % 2]
    prev = coef_s.at[(g + 1) % 2]
    hids = [_dot(u_ref[sub(s), :], h2, _NT).astype(BF16) for s in range(n_sub)]
    acc_s[...] += _dot(vt_ref[...], prev[...])
    pack = 16
    for s in range(n_sub):
        for al in range(a_per_sub):
            a = j * a_per_step + s * a_per_sub + al
            wsum = jnp.zeros((nk // pack, pack, tt), BF16)
            for h in range(PEER_HEADS):
                cnt_row = jnp.broadcast_to(cnt_s[h, pl.ds(a, 1), :], (pack, tt)).astype(BF16)
                e1_row = jnp.broadcast_to(e1_s[h, pl.ds(a, 1), :], (pack, tt)).astype(BF16)
                rank2 = rank2_s[h].reshape(nk // pack, pack, tt)
                e2 = e2_s[h].reshape(nk // pack, pack, tt)
                wsum = jnp.where(rank2 < cnt_row[None], wsum + e2 * e1_row[None], wsum)
            r0 = s * PEER_SUB_BLOCK + al * nk
            cur[r0:r0 + nk, :] = _gelu_x2(hids[s][al * nk:(al + 1) * nk, :]) * wsum.reshape(nk, tt)

    @pl.when(jnp.logical_and(g > 0, gb % nj == nj - 1))
    def _():
        x2 = x1_ref[...] + gt_ref[...] * acc_s[...].T
        if final:
            x2 = _rmsnorm(x2, fg_ref[...])
        o_ref[...] = x2


def _peer_call(h2b, q, x1, mod, fg, k1p, k2p, u_b, v_b, layer, tiles_per_group, final):
    t = h2b.shape[0]
    tt = PEER_TOKEN_TILE
    nb = PEER_EXPERT_BLOCK
    nj = PEER_N // nb
    n_tiles = t // tt
    n_steps = n_tiles * nj + 1
    front = lambda g: jnp.minimum(g, n_steps - 2)
    back = lambda g: jnp.maximum(g - 1, 0)
    row_f = lambda w: pl.BlockSpec((tt, w), lambda g: (front(g) // nj, 0))
    row_b = lambda w: pl.BlockSpec((tt, w), lambda g: (back(g) // nj, 0))
    r = mod.shape[1]
    tab = lambda dt: pltpu.VMEM((PEER_HEADS, PEER_NKEYS, tt), dt)
    return pl.pallas_call(
        functools.partial(_peer_kernel, final=final, n_steps=n_steps),
        grid=(n_steps,),
        in_specs=[row_f(D_MODEL), row_f(D_MODEL), row_b(D_MODEL),
                  pl.BlockSpec((None, r, D_MODEL), lambda g: (back(g) // nj // tiles_per_group, 0, MOD_GT2)),
                  _const_spec((1, D_MODEL)),
                  _const_spec((PEER_NKEYS, PEER_NKEYS)),
                  _const_spec((PEER_NKEYS, PEER_NKEYS)),
                  pl.BlockSpec((None, nb, D_MODEL), lambda g: (layer, front(g) % nj, 0)),
                  pl.BlockSpec((None, D_MODEL, nb), lambda g: (layer, 0, back(g) % nj))],
        out_specs=row_b(D_MODEL),
        out_shape=jax.ShapeDtypeStruct((t, D_MODEL), F32),
        scratch_shapes=[tab(F32), tab(F32), tab(BF16), tab(BF16), pltpu.VMEM((D_MODEL, tt), F32),
                        pltpu.VMEM((PEER_NKEYS, tt), F32), pltpu.VMEM((PEER_NKEYS, tt), F32),
                        pltpu.VMEM((2, nb, tt), BF16)],
        compiler_params=_params(1),
        name="peer",
    )(h2b, q, x1, mod, fg, k1p, k2p, u_b, v_b)


def _block_diag(wblk):
    n, d, e = wblk.shape
    eye = jnp.eye(n, dtype=wblk.dtype)
    return (eye[:, None, :, None] * wblk[:, :, None, :]).reshape(n * d, n * e)


def _pad_sample(z, hist=None):
    bsz = z.shape[0] // 4
    w = z.shape[1]
    z3 = z.reshape(bsz, 4, w)
    h3 = jnp.zeros((bsz, SAMPLE_HIST, w), F32)
    if hist is not None:
        h3 = h3.at[:, :, :hist.shape[-1]].set(hist)
    return jnp.concatenate([h3, z3, jnp.zeros((bsz, 1, w), F32)], axis=1).reshape(bsz * SAMPLE_SEQ_ROWS, w)


def _unpad_sample(o):
    bsz = o.shape[0] // SAMPLE_SEQ_ROWS
    return o.reshape(bsz, SAMPLE_SEQ_ROWS, -1)[:, SAMPLE_HIST:SAMPLE_HIST + 4].reshape(bsz * 4, -1)


def _layer_weights(l, w_in, w_out, peer_wq, peer_k1, peer_k2, peer_u, peer_v, lru_wa, lru_wx):
    w_in_p = jnp.pad(w_in[l], ((0, 0), (0, N_IN_PAD - N_IN)))
    wih = w_in_p.astype(BF16)
    wo = w_out[l].astype(BF16)
    wq = peer_wq[l].astype(BF16)
    wg = jnp.concatenate([_block_diag(lru_wa[l]), _block_diag(lru_wx[l])], axis=1).astype(BF16)
    half = PEER_NKEYS // 2
    k1p = jnp.pad(peer_k1[l], ((0, 0), (0, half)))
    k2p = jnp.pad(peer_k2[l], ((0, 0), (half, 0)))
    return dict(wih=wih, wo=wo, wq=wq, wg=wg, k1p=k1p, k2p=k2p)


def _trunk(x, mod, states, is_prompt, p, lw, l, final_g):
    t = x.shape[0]
    if is_prompt:
        cfg = PROMPT_CFG
        tiles_tok = 2048 // TOKEN_TILE
        tiles_peer = 2048 // PEER_TOKEN_TILE
        n_seq = t // 2048
    else:
        cfg = SAMPLE_CFG
        tiles_tok = 1
        tiles_peer = 1
        n_seq = t // 4
    row1 = lambda v: v.reshape(1, -1)
    z_a, z_b, z_c = _inproj_call(x, row1(p['norm1_g']), mod, lw['wih'], tiles_tok)
    if is_prompt:
        s_a = jnp.zeros((n_seq, A_W, HEAD_DIM), F32)
        s_c = jnp.zeros((n_seq, C_W, HEAD_DIM), F32)
        za_m, zb_m, zc_m = z_a, z_b, z_c
        inj = None
    else:
        st_hgrn, st_lru_h, st_lru_conv, st_dn, st_dn_conv = states
        s_a = jnp.swapaxes(st_hgrn, -1, -2).reshape(n_seq, A_W, HEAD_DIM)
        s_c = st_dn.reshape(n_seq, C_W, HEAD_DIM)
        za_m = _pad_sample(z_a)
        zb_m = _pad_sample(z_b, st_lru_conv)
        zc_m = _pad_sample(z_c, st_dn_conv)
        inj = jnp.zeros((n_seq, SAMPLE_SEQ_ROWS, B_W), F32).at[:, 0].set(st_lru_h).reshape(-1, B_W)
    rep = lambda v, h: row1(jnp.tile(v, h))
    oa, s_a_new = _hgrn_call(za_m, p['lb_param'], rep(p['a_norm_g'], A_HEADS), s_a,
                             PROMPT_HGRN_CFG if is_prompt else cfg, l)
    ob, hs = _lru_call(zb_m, inj, p['lru_conv_w'], row1(p['lru_conv_b']), lw['wg'],
                       row1(p['lru_ba']), row1(p['lru_bx']), row1(p['lru_L']), cfg)
    decay_cols = lambda v: jnp.zeros((1, LANES), F32).at[0, C_HEADS:2 * C_HEADS].set(v)
    oc, s_c_new = _dn_call(zc_m, p['dn_conv_w'], decay_cols(p['dn_A_log']), decay_cols(p['dn_dt_bias']),
                           rep(p['dn_norm_g'], C_HEADS), s_c, cfg)
    if is_prompt:
        seq = 2048
        h_t = hs.reshape(n_seq, seq, B_W)[:, -1]
        buf_b = z_b.reshape(n_seq, seq, ZB_W)[:, -SAMPLE_HIST:, :B_W]
        buf_c = z_c.reshape(n_seq, seq, ZC_W)[:, -SAMPLE_HIST:, :3 * C_W]
    else:
        oa, ob, oc = _unpad_sample(oa), _unpad_sample(ob), _unpad_sample(oc)
        h_t = hs.reshape(n_seq, SAMPLE_SEQ_ROWS, B_W)[:, -1]
        buf_b = z_b.reshape(n_seq, 4, ZB_W)[:, 1:, :B_W]
        buf_c = z_c.reshape(n_seq, 4, ZC_W)[:, 1:, :3 * C_W]
    new_states = (jnp.swapaxes(s_a_new.reshape(n_seq, A_HEADS, HEAD_DIM, HEAD_DIM), -1, -2), h_t, buf_b,
                  s_c_new.reshape(n_seq, C_HEADS, HEAD_DIM, HEAD_DIM), buf_c)
    x1, h2b, q = _outproj_call(x, oa, ob, oc, mod, row1(p['norm2_g']), lw['wo'], lw['wq'], tiles_tok)
    x2 = _peer_call(h2b, q, x1, mod, row1(final_g), lw['k1p'], lw['k2p'], lw['u_b'], lw['v_b'], l,
                    tiles_peer, final=(l == DEPTH - 1))
    return x2, new_states


def kernel(x_prompt, x_sample, state_hgrn, state_lru_h, state_lru_conv, state_dn, state_dn_conv,
           c_prompt, c_sample, w_ada, b_ada, norm1_g, norm2_g, w_in, lb_param, a_norm_g,
           lru_conv_w, lru_conv_b, lru_wa, lru_ba, lru_wx, lru_bx, lru_L,
           dn_conv_w, dn_A_log, dn_dt_bias, dn_norm_g, w_out,
           peer_wq, peer_k1, peer_k2, peer_u, peer_v, final_norm_g):
    n_p, seq, _ = x_prompt.shape
    n_s, dec_seq, _ = x_sample.shape
    mod = _ada_call(jnp.concatenate([c_prompt, c_sample], axis=0), w_ada, b_ada)
    xp = x_prompt.reshape(n_p * seq, D_MODEL)
    xs = x_sample.reshape(n_s * dec_seq, D_MODEL)
    sample_states = (state_hgrn, state_lru_h, state_lru_conv, state_dn, state_dn_conv)
    u_all = peer_u.astype(BF16)
    vt_all = jnp.swapaxes(peer_v, 1, 2).astype(BF16)
    p_new, s_new = [], []
    for l in range(DEPTH):
        p = dict(norm1_g=norm1_g[l], norm2_g=norm2_g[l], lb_param=lb_param, a_norm_g=a_norm_g[l],
                 lru_conv_w=lru_conv_w[l], lru_conv_b=lru_conv_b[l], lru_ba=lru_ba[l], lru_bx=lru_bx[l],
                 lru_L=lru_L[l], dn_conv_w=dn_conv_w[l], dn_A_log=dn_A_log[l], dn_dt_bias=dn_dt_bias[l],
                 dn_norm_g=dn_norm_g[l])
        lw = _layer_weights(l, w_in, w_out, peer_wq, peer_k1, peer_k2, peer_u, peer_v, lru_wa, lru_wx)
        lw['u_b'], lw['v_b'] = u_all, vt_all
        mod_p = mod[l, :n_p].reshape(n_p, 1, 6 * D_MODEL)
        mod_s = jnp.repeat(mod[l, n_p:], dec_seq, axis=0).reshape(-1, TOKEN_TILE, 6 * D_MODEL)
        xp, st_p = _trunk(xp, mod_p, None, True, p, lw, l, final_norm_g)
        xs, st_s = _trunk(xs, mod_s, tuple(s[l] for s in sample_states), False, p, lw, l, final_norm_g)
        p_new.append(st_p)
        s_new.append(st_s)
    stack = lambda sts: [jnp.stack([s[i] for s in sts]) for i in range(5)]
    p_st = stack(p_new)
    s_st = stack(s_new)
    return (xp.reshape(n_p, seq, D_MODEL), xs.reshape(n_s, dec_seq, D_MODEL), *p_st, *s_st)
```

```python
import functools

import jax
import jax.numpy as jnp
import numpy as np
from jax import lax
from jax.experimental import pallas as pl
from jax.experimental.pallas import tpu as pltpu

F32 = jnp.float32
BF16 = jnp.bfloat16

D_MODEL = 1024
DEPTH = 2
HEAD_DIM = 64
A_HEADS = 4
A_W = 256
B_W = 384
C_HEADS = 6
C_W = 384
LRU_C = 8.0
CONV_W = 4
N_IN = 3340
N_IN_PAD = 3456
ZA_W = 1024
ZB_W = 768
ZC_W = 1664
PEER_HEADS = 8
PEER_NKEYS = 128
PEER_TOPK = 16
PEER_N = PEER_NKEYS * PEER_NKEYS
EPS = 1e-6

LANES = 128
SUPER_BLOCK = 256
SAMPLE_SEQ_ROWS = 8
SAMPLE_HIST = CONV_W - 1
TOKEN_TILE = 512
PEER_TOKEN_TILE = 512
PEER_EXPERT_BLOCK = 2048
PEER_SUB_BLOCK = 256
PEER_ROUTE_TILE = 256
VMEM_LIMIT_BYTES = 56 * 1024 * 1024
assert TOKEN_TILE == PEER_TOKEN_TILE

_NN = (((1,), (0,)), ((), ()))
_NT = (((1,), (1,)), ((), ()))
_TN = (((0,), (0,)), ((), ()))


def _dot(a, b, dims=_NN):
    return lax.dot_general(a, b, dims, preferred_element_type=F32)


def _split2(x):
    hi = x.astype(BF16)
    lo = (x - hi.astype(F32)).astype(BF16)
    return hi, lo


def _split3(x):
    hi = x.astype(BF16)
    r = x - hi.astype(F32)
    mid = r.astype(BF16)
    lo = (r - mid.astype(F32)).astype(BF16)
    return hi, mid, lo


def _dot_x3(a, b, dims=_NN):
    ah, al = _split2(a)
    bh, bl = _split2(b)
    return _dot(ah, bh, dims) + _dot(ah, bl, dims) + _dot(al, bh, dims)


def _dot_b(a, b, dims=_NN):
    return _dot(a.astype(BF16), b.astype(BF16), dims)


def _dot_sel(a, sel, dims=_NN, passes=3):
    parts = _split3(a)[:passes]
    out = _dot(parts[0], sel, dims)
    for p in parts[1:]:
        out = out + _dot(p, sel, dims)
    return out


def _head_sums(x, ones_ref):
    ones = ones_ref[0:LANES, 0:LANES]
    xb = x.astype(BF16)
    return jnp.concatenate([_dot(xb[:, t:t + LANES], ones) for t in range(0, x.shape[1], LANES)], axis=1)


def _sel_dot(sel, b, passes=3):
    parts = _split3(b)[:passes]
    out = _dot(sel, parts[0])
    for p in parts[1:]:
        out = out + _dot(sel, p)
    return out


def _softplus(x):
    return jnp.maximum(x, 0.0) + jnp.log(1.0 + jnp.exp(-jnp.abs(x)))


def _silu(x):
    return x * jax.nn.sigmoid(x)


def _gelu_x2(x):
    c = float(np.sqrt(2.0 / np.pi))
    return x + x * jnp.tanh(x * (c + (0.044715 * c) * (x * x)))


def _gelu(x):
    return 0.5 * x * (1.0 + jnp.tanh(float(np.sqrt(2.0 / np.pi)) * (x + 0.044715 * (x * x * x))))


def _rmsnorm(x, g):
    return x * lax.rsqrt(jnp.mean(x * x, axis=-1, keepdims=True) + EPS) * g


def _const_spec(shape):
    nd = len(shape)
    return pl.BlockSpec(shape, lambda *_: (0,) * nd)


def _params(n_grid):
    return pltpu.CompilerParams(dimension_semantics=("arbitrary",) * n_grid,
                                vmem_limit_bytes=VMEM_LIMIT_BYTES)


def _ada_kernel(c_ref, w_ref, b_ref, o_ref):
    c = c_ref[...]
    o_ref[0] = _dot_x3(_silu(c), w_ref[0]) + b_ref[0]


def _ada_call(c_all, w_ada, b_ada):
    nb = c_all.shape[0]
    nt = 6 * D_MODEL // 1024
    return pl.pallas_call(
        _ada_kernel,
        grid=(DEPTH, nt),
        in_specs=[
            _const_spec((nb, D_MODEL)),
            pl.BlockSpec((1, D_MODEL, 1024), lambda l, j: (l, 0, j)),
            pl.BlockSpec((1, 1, 1024), lambda l, j: (l, 0, j)),
        ],
        out_specs=pl.BlockSpec((1, nb, 1024), lambda l, j: (l, 0, j)),
        out_shape=jax.ShapeDtypeStruct((DEPTH, nb, 6 * D_MODEL), F32),
        compiler_params=_params(2),
        name="ada_mod",
    )(c_all, w_ada, b_ada.reshape(DEPTH, 1, 6 * D_MODEL))


def _inproj_kernel(x_ref, g_ref, sc_ref, sh_ref, w_ref, za_ref, zb_ref, zc_ref):
    h = (_rmsnorm(x_ref[...], g_ref[...]) * (1.0 + sc_ref[...]) + sh_ref[...]).astype(BF16)
    for o_ref, lo, hi in ((za_ref, 0, ZA_W), (zb_ref, ZA_W, ZA_W + ZB_W), (zc_ref, ZA_W + ZB_W, N_IN_PAD)):
        o_ref[...] = _dot(h, w_ref[:, lo:hi])


MOD_SH1, MOD_SC1, MOD_GT1, MOD_SH2, MOD_SC2, MOD_GT2 = range(6)


def _mod_spec(mod, tiles_per_group, chunk):
    _, r, _ = mod.shape
    return pl.BlockSpec((None, r, D_MODEL), lambda i: (i // tiles_per_group, 0, chunk))


def _inproj_call(x, g, mod, wh, tiles_per_group):
    t = x.shape[0]
    tm = TOKEN_TILE
    row = lambda w: pl.BlockSpec((tm, w), lambda i: (i, 0))
    return pl.pallas_call(
        _inproj_kernel,
        grid=(t // tm,),
        in_specs=[row(D_MODEL), _const_spec((1, D_MODEL)), _mod_spec(mod, tiles_per_group, MOD_SC1),
                  _mod_spec(mod, tiles_per_group, MOD_SH1), _const_spec(wh.shape)],
        out_specs=[row(ZA_W), row(ZB_W), row(ZC_W)],
        out_shape=[jax.ShapeDtypeStruct((t, ZA_W), F32), jax.ShapeDtypeStruct((t, ZB_W), F32),
                   jax.ShapeDtypeStruct((t, ZC_W), F32)],
        compiler_params=_params(1),
        name="inproj",
    )(x, g, mod, mod, wh)


class _SeqCfg:
    def __init__(self, seq_rows, hist, real, chunk, n_par):
        self.seq_rows = seq_rows
        self.n_par = n_par
        self.hist = hist
        self.real = real
        self.chunk = chunk
        self.masked = hist > 0 or hist + real < seq_rows
        self.n_chunks = SUPER_BLOCK // chunk
        self.long_seq = seq_rows > SUPER_BLOCK
        self.sb_per_seq = max(seq_rows // SUPER_BLOCK, 1)
        self.seq_per_sb = max(SUPER_BLOCK // seq_rows, 1)
        assert (self.long_seq or chunk == seq_rows) and (self.long_seq or n_par == 1)


PROMPT_CFG = _SeqCfg(seq_rows=2048, hist=0, real=2048, chunk=16, n_par=4)
PROMPT_HGRN_CFG = _SeqCfg(seq_rows=2048, hist=0, real=2048, chunk=16, n_par=8)
SAMPLE_CFG = _SeqCfg(seq_rows=SAMPLE_SEQ_ROWS, hist=SAMPLE_HIST, real=4, chunk=SAMPLE_SEQ_ROWS, n_par=1)


def _row_in_seq(cfg, n_rows):
    r = lax.broadcasted_iota(jnp.int32, (n_rows, 1), 0)
    return r % min(cfg.seq_rows, SUPER_BLOCK)


def _real_mask(cfg, n_rows):
    r = _row_in_seq(cfg, n_rows)
    return (r >= cfg.hist) & (r < cfg.hist + cfg.real)


def _causal_conv(x, xp_ref, w_ref):
    n = x.shape[0]
    xp_ref[8:8 + n, :] = x
    y = x * w_ref[CONV_W - 1:CONV_W, :]
    for k in range(CONV_W - 1):
        s = CONV_W - 1 - k
        y = y + xp_ref[8 - s:8 - s + n, :] * w_ref[k:k + 1, :]
    xp_ref[0:8, :] = x[n - 8:n, :]
    return y


def _lane_tiles(x):
    return [x[:, t:t + LANES] for t in range(0, x.shape[1], LANES)]


def _expand_state(s_cat, tile_ref, bd_ref):
    tile = tile_ref[:, 0:LANES]
    bd = bd_ref[0:LANES, 0:LANES]
    return jnp.stack([_dot_sel(s_cat[r:r + LANES, :], tile, passes=2) * bd
                      for r in range(0, s_cat.shape[0], LANES)])


def _compress_state(s_tiles, tile_t_ref):
    tile_t = tile_t_ref[0:LANES, :]
    return jnp.concatenate([_dot_sel(s_tiles[t], tile_t, passes=2) for t in range(s_tiles.shape[0])], axis=0)


def _state_matmul(x, s_tiles, dims):
    return jnp.concatenate([_dot_b(xt, s_tiles[t], dims) for t, xt in enumerate(_lane_tiles(x))], axis=1)


def _state_update(s_tiles, decay_row, a, b, bd_ref):
    bd = bd_ref[0:LANES, 0:LANES]
    return jnp.stack([s_tiles[t] * dt + bd * _dot_b(at, bt, _TN)
                      for t, (dt, at, bt) in enumerate(zip(_lane_tiles(decay_row), _lane_tiles(a), _lane_tiles(b)))])


def _head_norm_gate(o, z, g, ones_ref):
    ms = _dot_sel(o * o, ones_ref[...], passes=2) * (1.0 / HEAD_DIM)
    return o * lax.rsqrt(ms + EPS) * g * _silu(z)


def _run_chunks(cfg, sb, step, st_ref, s0_ref, sout_ref, tile_ref, tile_t_ref, bd_ref):
    if cfg.long_seq:
        @pl.when(sb % cfg.sb_per_seq == 0)
        def _():
            for p in range(cfg.n_par):
                st_ref[p] = _expand_state(s0_ref[p], tile_ref, bd_ref)

        def body(k, carry):
            for p in range(cfg.n_par):
                st_ref[p] = step(k, p, st_ref[p])
            return carry

        lax.fori_loop(0, cfg.n_chunks, body, 0)

        @pl.when(sb % cfg.sb_per_seq == cfg.sb_per_seq - 1)
        def _():
            for p in range(cfg.n_par):
                sout_ref[p] = _compress_state(st_ref[p], tile_t_ref)
    else:
        def body(k, carry):
            st_new = step(k, 0, _expand_state(s0_ref[k], tile_ref, bd_ref))
            sout_ref[k] = _compress_state(st_new, tile_t_ref)
            return carry

        lax.fori_loop(0, cfg.n_chunks, body, 0, unroll=4)


def _hgrn_kernel(z_ref, lbp_ref, g_ref, s0_ref, ltri_ref, ones_ref, tile_ref, tile_t_ref, bd_ref,
                 o_ref, sout_ref,
                 st_ref, q_s, k_s, g_s, qe_s, kt_s, egl_s, o_s, *, cfg, layer):
    sb = pl.program_id(0)
    c = cfg.chunk
    n = SUPER_BLOCK
    nch = cfg.n_chunks
    w = A_W
    lbp = lbp_ref[...]
    e = jnp.exp(lbp - jnp.max(lbp, axis=0, keepdims=True))
    lbs = e / jnp.sum(e, axis=0, keepdims=True)
    lb = jnp.sum(lbs[0:layer + 1, :], axis=0, keepdims=True) - lbs[0:1, :]

    for p in range(cfg.n_par):
        r = slice(p * n, (p + 1) * n)
        aq = z_ref[p, :, 0:w]
        af = z_ref[p, :, w:2 * w]
        f = lb + (1.0 - lb) * jax.nn.sigmoid(af)
        lf = jnp.log(f)
        kk = 1.0 - f
        if cfg.masked:
            real = _real_mask(cfg, n)
            lf = jnp.where(real, lf, 0.0)
            kk = jnp.where(real, kk, 0.0)
        q = _silu(aq) * (HEAD_DIM ** -0.5)
        gcum = _sel_dot(ltri_ref[...], lf)
        g3 = gcum.reshape(nch, c, w)
        gl3 = g3[:, c - 1:c, :]
        q_s[r, :] = q
        k_s[r, :] = kk
        g_s[r, :] = gcum
        qe_s[r, :] = q * jnp.exp(gcum)
        kt_s[r, :] = (kk.reshape(nch, c, w) * jnp.exp(gl3 - g3)).reshape(n, w)
        egl_s[p * nch:(p + 1) * nch, :] = jnp.exp(gl3).reshape(nch, w)

    jio = lax.broadcasted_iota(jnp.int32, (c, 1), 0)
    bd = bd_ref[...]

    def chunk(k, p, st):
        rows = pl.ds(pl.multiple_of(p * n + k * c, c), c)
        q_c = q_s[rows, :]
        k_c = k_s[rows, :]
        g_c = g_s[rows, :]
        v_c = z_ref[p, pl.ds(pl.multiple_of(k * c, c), c), 2 * w:3 * w]
        o_inter = _state_matmul(qe_s[rows, :], st, _NT)
        lo = min(c, 8)
        d_rows = []
        for i in range(c):
            nj = lo if i < lo else c
            dec = jnp.exp(jnp.minimum(g_c[i:i + 1, :] - g_c[0:nj, :], 0.0))
            d_rows.append(jnp.where(jio[0:nj] <= i, k_c[0:nj, :] * dec * q_c[i:i + 1, :], 0.0))
        a_b = _head_sums(jnp.concatenate(d_rows, axis=0), ones_ref)
        o_diag = jnp.sum(a_b[0:lo * lo, :].reshape(lo, lo, w) * v_c[None, 0:lo, :], axis=1)
        if c > lo:
            o_hi = jnp.sum(a_b[lo * lo:, :].reshape(c - lo, c, w) * v_c[None, :, :], axis=1)
            o_diag = jnp.concatenate([o_diag, o_hi], axis=0)
        o_s[rows, :] = o_inter + o_diag
        return _state_update(st, egl_s[pl.ds(p * nch + k, 1), :], v_c, kt_s[rows, :], bd_ref)

    _run_chunks(cfg, sb, chunk, st_ref, s0_ref, sout_ref, tile_ref, tile_t_ref, bd_ref)
    for p in range(cfg.n_par):
        o_ref[p] = _head_norm_gate(o_s[p * n:(p + 1) * n, :], z_ref[p, :, 3 * w:4 * w], g_ref[...], ones_ref)


def _mixer_consts(heads, cfg):
    w = heads * HEAD_DIM
    lane_head = np.arange(w) // HEAD_DIM
    ones = (lane_head[:, None] == lane_head[None, :]).astype(np.float32)
    tile = (np.arange(HEAD_DIM)[:, None] == (np.arange(w) % HEAD_DIM)[None, :]).astype(np.float32)
    r = np.arange(SUPER_BLOCK)
    ltri = ((r[:, None] // cfg.chunk == r[None, :] // cfg.chunk) & (r[None, :] <= r[:, None])).astype(np.float32)
    return dict(ltri=jnp.asarray(ltri, BF16), ones=jnp.asarray(ones, BF16), tile=jnp.asarray(tile, BF16),
                tile_t=jnp.asarray(tile.T, BF16), bd=jnp.asarray(ones, F32))


def _state_specs(cfg, w):
    blk = (cfg.n_par if cfg.long_seq else cfg.seq_per_sb, w, HEAD_DIM)
    if cfg.long_seq:
        imap = lambda i: (i // cfg.sb_per_seq, 0, 0)
    else:
        imap = lambda i: (i, 0, 0)
    return pl.BlockSpec(blk, imap)


def _seq_view(x, cfg):
    if not cfg.long_seq:
        return x[None]
    assert x.shape[0] % (cfg.n_par * cfg.seq_rows) == 0, "sequence count must be a multiple of n_par"
    return x.reshape(-1, cfg.seq_rows, x.shape[-1])


def _seq_block_spec(cfg, width):
    if cfg.long_seq:
        return pl.BlockSpec((cfg.n_par, SUPER_BLOCK, width), lambda i: (i // cfg.sb_per_seq, i % cfg.sb_per_seq, 0))
    return pl.BlockSpec((1, SUPER_BLOCK, width), lambda i: (0, i, 0))


def _hgrn_call(z_a, lb_param, g_exp, s0_t, cfg, layer):
    rows = z_a.shape[0]
    w = A_W
    cst = _mixer_consts(A_HEADS, cfg)
    n_seq = s0_t.shape[0]
    par_rows = cfg.n_par * SUPER_BLOCK
    vm = lambda shape: pltpu.VMEM(shape, F32)
    kern = functools.partial(_hgrn_kernel, cfg=cfg, layer=layer)
    z3 = _seq_view(z_a, cfg)
    oa, s_new = pl.pallas_call(
        kern,
        grid=(rows // par_rows,),
        in_specs=[_seq_block_spec(cfg, ZA_W), _const_spec((DEPTH, w)),
                  _const_spec((1, w)), _state_specs(cfg, w), _const_spec(cst["ltri"].shape),
                  _const_spec(cst["ones"].shape), _const_spec(cst["tile"].shape),
                  _const_spec(cst["tile_t"].shape), _const_spec(cst["bd"].shape)],
        out_specs=[_seq_block_spec(cfg, w), _state_specs(cfg, w)],
        out_shape=[jax.ShapeDtypeStruct(z3.shape[:2] + (w,), F32), jax.ShapeDtypeStruct((n_seq, w, HEAD_DIM), F32)],
        scratch_shapes=[vm((cfg.n_par, w // LANES, LANES, LANES))] + [vm((par_rows, w))] * 5
        + [vm((cfg.n_par * cfg.n_chunks, w)), vm((par_rows, w))],
        compiler_params=_params(1),
        name="hgrn2",
    )(z3, lb_param, g_exp, s0_t, cst["ltri"], cst["ones"], cst["tile"], cst["tile_t"], cst["bd"])
    return oa.reshape(rows, w), s_new


def _lru_kernel(*refs, cfg):
    if cfg.long_seq:
        (z_ref, cw_ref, cb_ref, wg_ref, ba_ref, bx_ref, lp_ref,
         ob_ref, hs_ref, prev_ref, hc_ref) = refs
        inj_ref = None
    else:
        (z_ref, inj_ref, cw_ref, cb_ref, wg_ref, ba_ref, bx_ref, lp_ref,
         ob_ref, hs_ref, prev_ref, hc_ref) = refs
    sb = pl.program_id(0)
    n = SUPER_BLOCK
    w = B_W

    @pl.when(sb % cfg.sb_per_seq == 0)
    def _():
        prev_ref[0:8, :] = jnp.zeros((8, prev_ref.shape[1]), F32)
        hc_ref[...] = jnp.zeros_like(hc_ref)

    xc = _causal_conv(z_ref[:, 0:w], prev_ref, cw_ref) + cb_ref[...]
    gates = _dot_b(xc, wg_ref[...])
    r = jax.nn.sigmoid(gates[:, 0:w] + ba_ref[...])
    ig = jax.nn.sigmoid(gates[:, w:2 * w] + bx_ref[...])
    log_a = -LRU_C * r * _softplus(-lp_ref[...])
    a = jnp.exp(log_a)
    b = jnp.sqrt(1.0 - jnp.exp(2.0 * log_a)) * ig * xc
    if cfg.masked:
        real = _real_mask(cfg, n)
        a = jnp.where(real, a, 1.0)
        b = jnp.where(real, b, 0.0)
    if inj_ref is not None:
        b = b + inj_ref[...]
    group = 8
    assert cfg.seq_rows % group == 0
    rig = lax.broadcasted_iota(jnp.int32, (n, 1), 0) % group
    d = 1
    while d < group:
        has = rig >= d
        a_sh = jnp.where(has, pltpu.roll(a, d, 0), 1.0)
        b_sh = jnp.where(has, pltpu.roll(b, d, 0), 0.0)
        b = b + a * b_sh
        a = a * a_sh
        d *= 2
    if cfg.long_seq:
        a3 = a.reshape(n // group, group, w)
        b3 = b.reshape(n // group, group, w)
        carry = hc_ref[...]
        rows = []
        for i in range(n // group):
            h_i = b3[i] + a3[i] * carry
            rows.append(h_i)
            carry = h_i[group - 1:group, :]
        hs = jnp.concatenate(rows, axis=0)
        hc_ref[...] = carry
    else:
        hs = b
    hs_ref[...] = hs
    ob_ref[...] = hs * _gelu(z_ref[:, w:2 * w])


def _lru_call(z_b, inj, cw, cb, wg, ba, bx, lp, cfg):
    rows = z_b.shape[0]
    w = B_W
    row = lambda width: pl.BlockSpec((SUPER_BLOCK, width), lambda i: (i, 0))
    ins = [z_b] + ([] if cfg.long_seq else [inj]) + [cw, cb, wg, ba, bx, lp]
    specs = [row(ZB_W)] + ([] if cfg.long_seq else [row(w)]) + [_const_spec(a.shape) for a in ins[-6:]]
    return pl.pallas_call(
        functools.partial(_lru_kernel, cfg=cfg),
        grid=(rows // SUPER_BLOCK,),
        in_specs=specs,
        out_specs=[row(w), row(w)],
        out_shape=[jax.ShapeDtypeStruct((rows, w), F32)] * 2,
        scratch_shapes=[pltpu.VMEM((8 + SUPER_BLOCK, w), F32), pltpu.VMEM((1, w), F32)],
        compiler_params=_params(1),
        name="rglru",
    )(*ins)


def _dn_kernel(z_ref, cw_ref, alog_ref, dtb_ref, g_ref, s0_ref, ltri_ref, ones_ref, tile_ref, tile_t_ref,
               bd_ref, expb_ref, expa_ref,
               o_ref, sout_ref,
               st_ref, prev_ref, q_s, k_s, v_s, g_s, eg_s, beta_s, kt_s, egl_s, o_s, *, cfg):
    sb = pl.program_id(0)
    c = cfg.chunk
    n = SUPER_BLOCK
    nch = cfg.n_chunks
    w = C_W

    @pl.when(sb % cfg.sb_per_seq == 0)
    def _():
        for p in range(cfg.n_par):
            prev_ref[p, 0:8, :] = jnp.zeros((8, prev_ref.shape[2]), F32)

    ones = ones_ref[...]
    for p in range(cfg.n_par):
        r = slice(p * n, (p + 1) * n)
        qkv = _silu(_causal_conv(z_ref[p, :, 0:3 * w], prev_ref.at[p], cw_ref))
        q = qkv[:, 0:w]
        kx = qkv[:, w:2 * w]
        q = q * lax.rsqrt(_dot_sel(q * q, ones, passes=2) + EPS) * (HEAD_DIM ** -0.5)
        kx = kx * lax.rsqrt(_dot_sel(kx * kx, ones, passes=2) + EPS)
        pc = z_ref[p, :, 4 * w:4 * w + LANES]
        beta = _dot_sel(jax.nn.sigmoid(pc), expb_ref[...], passes=2)
        gdec = _dot_sel(-jnp.exp(alog_ref[...]) * _softplus(pc + dtb_ref[...]), expa_ref[...], passes=2)
        if cfg.masked:
            real = _real_mask(cfg, n)
            beta = jnp.where(real, beta, 0.0)
            gdec = jnp.where(real, gdec, 0.0)
        gcum = _sel_dot(ltri_ref[...], gdec)
        g3 = gcum.reshape(nch, c, w)
        gl3 = g3[:, c - 1:c, :]
        q_s[r, :] = q
        k_s[r, :] = kx
        v_s[r, :] = qkv[:, 2 * w:3 * w]
        g_s[r, :] = gcum
        eg_s[r, :] = jnp.exp(gcum)
        beta_s[r, :] = beta
        kt_s[r, :] = (kx.reshape(nch, c, w) * jnp.exp(gl3 - g3)).reshape(n, w)
        egl_s[p * nch:(p + 1) * nch, :] = jnp.exp(gl3).reshape(nch, w)

    iio = lax.broadcasted_iota(jnp.int32, (c, 1), 0)
    bd = bd_ref[...]

    def chunk(k, p, st):
        rows = pl.ds(pl.multiple_of(p * n + k * c, c), c)
        q_c = q_s[rows, :]
        k_c = k_s[rows, :]
        g_c = g_s[rows, :]
        eg_c = eg_s[rows, :]
        beta_c = beta_s[rows, :]
        qk_s = _state_matmul(jnp.concatenate([q_c, k_c], axis=0), st, _NN)
        q_st = qk_s[0:c, :]
        k_st = qk_s[c:2 * c, :]
        lo = min(c, 8)
        tail = lambda x: x[lo:c, :]
        d_rows = ([k_c * k_c[j:j + 1, :] for j in range(lo)] + [q_c * k_c[j:j + 1, :] for j in range(lo)]
                  + [tail(k_c) * k_c[j:j + 1, :] for j in range(lo, c)]
                  + [tail(q_c) * k_c[j:j + 1, :] for j in range(lo, c)])
        dots = _head_sums(jnp.concatenate(d_rows, axis=0), ones_ref)
        e = beta_c * (v_s[rows, :] - eg_c * k_st)
        o = eg_c * q_st
        for j in range(lo):
            dec = jnp.exp(jnp.minimum(g_c - g_c[j:j + 1, :], 0.0))
            m_col = jnp.where(iio > j, beta_c * dots[j * c:(j + 1) * c, :] * dec, 0.0)
            e_j = e[j:j + 1, :]
            e = e - m_col * e_j
            qk_col = jnp.where(iio >= j, dots[(lo + j) * c:(lo + j + 1) * c, :] * dec, 0.0)
            o = o + qk_col * e_j
        if c > lo:
            nt = c - lo
            e_t, o_t, g_t, beta_t = tail(e), tail(o), tail(g_c), tail(beta_c)
            base = 2 * lo * c
            for j in range(lo, c):
                dec = jnp.exp(jnp.minimum(g_t - g_c[j:j + 1, :], 0.0))
                kk = dots[base + (j - lo) * nt:base + (j - lo + 1) * nt, :]
                qk = dots[base + (nt + j - lo) * nt:base + (nt + j - lo + 1) * nt, :]
                e_j = e_t[j - lo:j - lo + 1, :]
                e_t = e_t - jnp.where(iio[0:nt] > j - lo, beta_t * kk * dec, 0.0) * e_j
                o_t = o_t + jnp.where(iio[0:nt] >= j - lo, qk * dec, 0.0) * e_j
            e = jnp.concatenate([e[0:lo, :], e_t], axis=0)
            o = jnp.concatenate([o[0:lo, :], o_t], axis=0)
        o_s[rows, :] = o
        return _state_update(st, egl_s[pl.ds(p * nch + k, 1), :], kt_s[rows, :], e, bd_ref)

    _run_chunks(cfg, sb, chunk, st_ref, s0_ref, sout_ref, tile_ref, tile_t_ref, bd_ref)
    for p in range(cfg.n_par):
        o_ref[p] = _head_norm_gate(o_s[p * n:(p + 1) * n, :], z_ref[p, :, 3 * w:4 * w], g_ref[...], ones_ref)


def _dn_call(z_c, cw, alog_exp, dtb_exp, g_exp, s0, cfg):
    rows = z_c.shape[0]
    w = C_W
    cst = _mixer_consts(C_HEADS, cfg)
    lane_head = np.arange(w) // HEAD_DIM
    expb = (np.arange(128)[:, None] == lane_head[None, :]).astype(np.float32)
    expa = (np.arange(128)[:, None] == (lane_head[None, :] + C_HEADS)).astype(np.float32)
    expb = jnp.asarray(expb, BF16)
    expa = jnp.asarray(expa, BF16)
    n_seq = s0.shape[0]
    par_rows = cfg.n_par * SUPER_BLOCK
    vm = lambda shape: pltpu.VMEM(shape, F32)
    z3 = _seq_view(z_c, cfg)
    oc, s_new = pl.pallas_call(
        functools.partial(_dn_kernel, cfg=cfg),
        grid=(rows // par_rows,),
        in_specs=[_seq_block_spec(cfg, ZC_W), _const_spec(cw.shape),
                  _const_spec((1, LANES)), _const_spec((1, LANES)), _const_spec((1, w)), _state_specs(cfg, w),
                  _const_spec(cst["ltri"].shape), _const_spec(cst["ones"].shape),
                  _const_spec(cst["tile"].shape), _const_spec(cst["tile_t"].shape),
                  _const_spec(cst["bd"].shape), _const_spec(expb.shape), _const_spec(expa.shape)],
        out_specs=[_seq_block_spec(cfg, w), _state_specs(cfg, w)],
        out_shape=[jax.ShapeDtypeStruct(z3.shape[:2] + (w,), F32), jax.ShapeDtypeStruct((n_seq, w, HEAD_DIM), F32)],
        scratch_shapes=[vm((cfg.n_par, w // LANES, LANES, LANES)), vm((cfg.n_par, 8 + SUPER_BLOCK, 3 * w))]
        + [vm((par_rows, w))] * 7
        + [vm((cfg.n_par * cfg.n_chunks, w)), vm((par_rows, w))],
        compiler_params=_params(1),
        name="deltanet",
    )(z3, cw, alog_exp, dtb_exp, g_exp, s0, cst["ltri"], cst["ones"], cst["tile"], cst["tile_t"],
      cst["bd"], expb, expa)
    return oc.reshape(rows, w), s_new


def _outproj_kernel(x_ref, oa_ref, ob_ref, oc_ref, gt_ref, g_ref, sc_ref, sh_ref,
                    wo_ref, wq_ref, x1_ref, h2_ref, q_ref):
    mo = _dot_b(oa_ref[...], wo_ref[0:A_W, :])
    mo = mo + _dot_b(ob_ref[...], wo_ref[A_W:A_W + B_W, :])
    mo = mo + _dot_b(oc_ref[...], wo_ref[A_W + B_W:, :])
    x1 = x_ref[...] + gt_ref[...] * mo
    x1_ref[...] = x1
    h2 = _rmsnorm(x1, g_ref[...]) * (1.0 + sc_ref[...]) + sh_ref[...]
    h2_ref[...] = h2.astype(BF16)
    q_ref[...] = _dot_b(h2, wq_ref[...])


def _outproj_call(x, oa, ob, oc, mod, g, wo, wq, tiles_per_group):
    t = x.shape[0]
    tm = TOKEN_TILE
    row = lambda w: pl.BlockSpec((tm, w), lambda i: (i, 0))
    return pl.pallas_call(
        _outproj_kernel,
        grid=(t // tm,),
        in_specs=[row(D_MODEL), row(A_W), row(B_W), row(C_W), _mod_spec(mod, tiles_per_group, MOD_GT1),
                  _const_spec((1, D_MODEL)), _mod_spec(mod, tiles_per_group, MOD_SC2),
                  _mod_spec(mod, tiles_per_group, MOD_SH2),
                  _const_spec(wo.shape), _const_spec(wq.shape)],
        out_specs=[row(D_MODEL), row(D_MODEL), row(D_MODEL)],
        out_shape=[jax.ShapeDtypeStruct((t, D_MODEL), F32), jax.ShapeDtypeStruct((t, D_MODEL), BF16),
                   jax.ShapeDtypeStruct((t, D_MODEL), F32)],
        compiler_params=_params(1),
        name="outproj",
    )(x, oa, ob, oc, mod, g, mod, mod, wo, wq)


def _top16(s):
    vals = []
    for r in range(PEER_TOPK):
        m = jnp.max(s, axis=0, keepdims=True)
        s = jnp.where(s == m, -(32.0 + r) * 2.0 ** 95, s)
        vals.append(m)
    rank = jnp.where(s <= -(2.0 ** 99), s * -(2.0 ** -95) - 32.0, float(PEER_TOPK))
    return jnp.concatenate(vals, axis=0), rank


def _route_tile(s1, s2):
    nk, t = s1.shape
    v1, rank1 = _top16(s1)
    v2, rank2 = _top16(s2)
    c3 = v1[:, None, :] + v2[None, :, :]
    jrow = lax.broadcasted_iota(jnp.int32, (8, 1), 0)
    parts = [v1[0:1, :] + v2, v1[1:2, :] + v2[0:8, :]]
    for i in range(2, 8):
        parts.append(jnp.where(jrow < PEER_TOPK // (i + 1), v1[i:i + 1, :] + v2[0:8, :], -jnp.inf))
    parts.append(v1[8:16, :] + v2[0:1, :])
    cand = jnp.concatenate(parts, axis=0)
    m = None
    for _ in range(PEER_TOPK):
        m = jnp.max(cand, axis=0, keepdims=True)
        cand = jnp.where(cand == m, -jnp.inf, cand)
    tau = m
    sel = c3 >= tau[None, :, :]
    m0 = v1[0:1, :] + v2[0:1, :]
    zsum = jnp.sum(jnp.where(sel, jnp.exp(c3 - m0[None, :, :]), 0.0).reshape(-1, t), axis=0, keepdims=True)
    n_i = jnp.sum(jnp.where(sel, 1.0, 0.0), axis=1)
    rank1 = rank1.astype(BF16)
    cnt = jnp.zeros((nk, t), BF16)
    for i in range(PEER_TOPK):
        cnt = cnt + jnp.where(rank1 == i, n_i[i:i + 1, :].astype(BF16), jnp.zeros((), BF16))
    return (cnt.astype(F32), jnp.exp(s1 - v1[0:1, :]), rank2.astype(BF16),
            (jnp.exp(s2 - v2[0:1, :]) * (0.5 / zsum)).astype(BF16))


def _peer_kernel(h2_ref, q_ref, x1_ref, gt_ref, fg_ref, k1_ref, k2_ref, u_ref, vt_ref, o_ref,
                 cnt_s, e1_s, rank2_s, e2_s, acc_s, s1_s, s2_s, coef_s, *, final, n_steps):
    g = pl.program_id(0)
    tt = PEER_TOKEN_TILE
    rt = PEER_ROUTE_TILE
    nk = PEER_NKEYS
    nj = PEER_N // PEER_EXPERT_BLOCK
    a_per_step = PEER_EXPERT_BLOCK // nk
    j = jnp.minimum(g, n_steps - 2) % nj
    gb = jnp.maximum(g - 1, 0)

    @pl.when(g == 0)
    def _():
        coef_s[...] = jnp.zeros_like(coef_s)

    @pl.when(gb % nj == 0)
    def _():
        acc_s[...] = jnp.zeros_like(acc_s)

    @pl.when(jnp.logical_and(g % nj == 0, g < n_steps - 1))
    def _():

        def head(h, carry):
            qh = q_ref[:, pl.ds(pl.multiple_of(h * nk, nk), nk)]
            s1_s[...] = _dot_x3(k1_ref[...], qh, _NT)
            s2_s[...] = _dot_x3(k2_ref[...], qh, _NT)

            def lane_tile(c, carry2):
                cols = pl.ds(pl.multiple_of(c * rt, rt), rt)
                (cnt_s[h, :, cols], e1_s[h, :, cols], rank2_s[h, :, cols],
                 e2_s[h, :, cols]) = _route_tile(s1_s[:, cols], s2_s[:, cols])
                return carry2

            lax.fori_loop(0, tt // rt, lane_tile, 0)
            return carry

        lax.fori_loop(0, PEER_HEADS, head, 0)

    h2 = h2_ref[...]
    n_sub = PEER_EXPERT_BLOCK // PEER_SUB_BLOCK
    a_per_sub = PEER_SUB_BLOCK // nk
    sub = lambda s: slice(s * PEER_SUB_BLOCK, (s + 1) * PEER_SUB_BLOCK)
    cur = coef_s.at[g % 2]
    prev = coef_s.at[(g + 1) % 2]
    hids = [_dot(u_ref[sub(s), :], h2, _NT).astype(BF16) for s in range(n_sub)]
    acc_s[...] += _dot(vt_ref[...], prev[...])
    pack = 16
    for s in range(n_sub):
        for al in range(a_per_sub):
            a = j * a_per_step + s * a_per_sub + al
            wsum = jnp.zeros((nk // pack, pack, tt), BF16)
            for h in range(PEER_HEADS):
                cnt_row = jnp.broadcast_to(cnt_s[h, pl.ds(a, 1), :], (pack, tt)).astype(BF16)
                e1_row = jnp.broadcast_to(e1_s[h, pl.ds(a, 1), :], (pack, tt)).astype(BF16)
                rank2 = rank2_s[h].reshape(nk // pack, pack, tt)
                e2 = e2_s[h].reshape(nk // pack, pack, tt)
                wsum = jnp.where(rank2 < cnt_row[None], wsum + e2 * e1_row[None], wsum)
            r0 = s * PEER_SUB_BLOCK + al * nk
            cur[r0:r0 + nk, :] = _gelu_x2(hids[s][al * nk:(al + 1) * nk, :]) * wsum.reshape(nk, tt)

    @pl.when(jnp.logical_and(g > 0, gb % nj == nj - 1))
    def _():
        x2 = x1_ref[...] + gt_ref[...] * acc_s[...].T
        if final:
            x2 = _rmsnorm(x2, fg_ref[...])
        o_ref[...] = x2


def _peer_call(h2b, q, x1, mod, fg, k1p, k2p, u_b, v_b, layer, tiles_per_group, final):
    t = h2b.shape[0]
    tt = PEER_TOKEN_TILE
    nb = PEER_EXPERT_BLOCK
    nj = PEER_N // nb
    n_tiles = t // tt
    n_steps = n_tiles * nj + 1
    front = lambda g: jnp.minimum(g, n_steps - 2)
    back = lambda g: jnp.maximum(g - 1, 0)
    row_f = lambda w: pl.BlockSpec((tt, w), lambda g: (front(g) // nj, 0))
    row_b = lambda w: pl.BlockSpec((tt, w), lambda g: (back(g) // nj, 0))
    r = mod.shape[1]
    tab = lambda dt: pltpu.VMEM((PEER_HEADS, PEER_NKEYS, tt), dt)
    return pl.pallas_call(
        functools.partial(_peer_kernel, final=final, n_steps=n_steps),
        grid=(n_steps,),
        in_specs=[row_f(D_MODEL), row_f(D_MODEL), row_b(D_MODEL),
                  pl.BlockSpec((None, r, D_MODEL), lambda g: (back(g) // nj // tiles_per_group, 0, MOD_GT2)),
                  _const_spec((1, D_MODEL)),
                  _const_spec((PEER_NKEYS, PEER_NKEYS)),
                  _const_spec((PEER_NKEYS, PEER_NKEYS)),
                  pl.BlockSpec((None, nb, D_MODEL), lambda g: (layer, front(g) % nj, 0)),
                  pl.BlockSpec((None, D_MODEL, nb), lambda g: (layer, 0, back(g) % nj))],
        out_specs=row_b(D_MODEL),
        out_shape=jax.ShapeDtypeStruct((t, D_MODEL), F32),
        scratch_shapes=[tab(F32), tab(F32), tab(BF16), tab(BF16), pltpu.VMEM((D_MODEL, tt), F32),
                        pltpu.VMEM((PEER_NKEYS, tt), F32), pltpu.VMEM((PEER_NKEYS, tt), F32),
                        pltpu.VMEM((2, nb, tt), BF16)],
        compiler_params=_params(1),
        name="peer",
    )(h2b, q, x1, mod, fg, k1p, k2p, u_b, v_b)


def _block_diag(wblk):
    n, d, e = wblk.shape
    eye = jnp.eye(n, dtype=wblk.dtype)
    return (eye[:, None, :, None] * wblk[:, :, None, :]).reshape(n * d, n * e)


def _pad_sample(z, hist=None):
    bsz = z.shape[0] // 4
    w = z.shape[1]
    z3 = z.reshape(bsz, 4, w)
    h3 = jnp.zeros((bsz, SAMPLE_HIST, w), F32)
    if hist is not None:
        h3 = h3.at[:, :, :hist.shape[-1]].set(hist)
    return jnp.concatenate([h3, z3, jnp.zeros((bsz, 1, w), F32)], axis=1).reshape(bsz * SAMPLE_SEQ_ROWS, w)


def _unpad_sample(o):
    bsz = o.shape[0] // SAMPLE_SEQ_ROWS
    return o.reshape(bsz, SAMPLE_SEQ_ROWS, -1)[:, SAMPLE_HIST:SAMPLE_HIST + 4].reshape(bsz * 4, -1)


def _layer_weights(l, w_in, w_out, peer_wq, peer_k1, peer_k2, peer_u, peer_v, lru_wa, lru_wx):
    w_in_p = jnp.pad(w_in[l], ((0, 0), (0, N_IN_PAD - N_IN)))
    wih = w_in_p.astype(BF16)
    wo = w_out[l].astype(BF16)
    wq = peer_wq[l].astype(BF16)
    wg = jnp.concatenate([_block_diag(lru_wa[l]), _block_diag(lru_wx[l])], axis=1).astype(BF16)
    half = PEER_NKEYS // 2
    k1p = jnp.pad(peer_k1[l], ((0, 0), (0, half)))
    k2p = jnp.pad(peer_k2[l], ((0, 0), (half, 0)))
    return dict(wih=wih, wo=wo, wq=wq, wg=wg, k1p=k1p, k2p=k2p)


def _trunk(x, mod, states, is_prompt, p, lw, l, final_g):
    t = x.shape[0]
    if is_prompt:
        cfg = PROMPT_CFG
        tiles_tok = 2048 // TOKEN_TILE
        tiles_peer = 2048 // PEER_TOKEN_TILE
        n_seq = t // 2048
    else:
        cfg = SAMPLE_CFG
        tiles_tok = 1
        tiles_peer = 1
        n_seq = t // 4
    row1 = lambda v: v.reshape(1, -1)
    z_a, z_b, z_c = _inproj_call(x, row1(p['norm1_g']), mod, lw['wih'], tiles_tok)
    if is_prompt:
        s_a = jnp.zeros((n_seq, A_W, HEAD_DIM), F32)
        s_c = jnp.zeros((n_seq, C_W, HEAD_DIM), F32)
        za_m, zb_m, zc_m = z_a, z_b, z_c
        inj = None
    else:
        st_hgrn, st_lru_h, st_lru_conv, st_dn, st_dn_conv = states
        s_a = jnp.swapaxes(st_hgrn, -1, -2).reshape(n_seq, A_W, HEAD_DIM)
        s_c = st_dn.reshape(n_seq, C_W, HEAD_DIM)
        za_m = _pad_sample(z_a)
        zb_m = _pad_sample(z_b, st_lru_conv)
        zc_m = _pad_sample(z_c, st_dn_conv)
        inj = jnp.zeros((n_seq, SAMPLE_SEQ_ROWS, B_W), F32).at[:, 0].set(st_lru_h).reshape(-1, B_W)
    rep = lambda v, h: row1(jnp.tile(v, h))
    oa, s_a_new = _hgrn_call(za_m, p['lb_param'], rep(p['a_norm_g'], A_HEADS), s_a,
                             PROMPT_HGRN_CFG if is_prompt else cfg, l)
    ob, hs = _lru_call(zb_m, inj, p['lru_conv_w'], row1(p['lru_conv_b']), lw['wg'],
                       row1(p['lru_ba']), row1(p['lru_bx']), row1(p['lru_L']), cfg)
    decay_cols = lambda v: jnp.zeros((1, LANES), F32).at[0, C_HEADS:2 * C_HEADS].set(v)
    oc, s_c_new = _dn_call(zc_m, p['dn_conv_w'], decay_cols(p['dn_A_log']), decay_cols(p['dn_dt_bias']),
                           rep(p['dn_norm_g'], C_HEADS), s_c, cfg)
    if is_prompt:
        seq = 2048
        h_t = hs.reshape(n_seq, seq, B_W)[:, -1]
        buf_b = z_b.reshape(n_seq, seq, ZB_W)[:, -SAMPLE_HIST:, :B_W]
        buf_c = z_c.reshape(n_seq, seq, ZC_W)[:, -SAMPLE_HIST:, :3 * C_W]
    else:
        oa, ob, oc = _unpad_sample(oa), _unpad_sample(ob), _unpad_sample(oc)
        h_t = hs.reshape(n_seq, SAMPLE_SEQ_ROWS, B_W)[:, -1]
        buf_b = z_b.reshape(n_seq, 4, ZB_W)[:, 1:, :B_W]
        buf_c = z_c.reshape(n_seq, 4, ZC_W)[:, 1:, :3 * C_W]
    new_states = (jnp.swapaxes(s_a_new.reshape(n_seq, A_HEADS, HEAD_DIM, HEAD_DIM), -1, -2), h_t, buf_b,
                  s_c_new.reshape(n_seq, C_HEADS, HEAD_DIM, HEAD_DIM), buf_c)
    x1, h2b, q = _outproj_call(x, oa, ob, oc, mod, row1(p['norm2_g']), lw['wo'], lw['wq'], tiles_tok)
    x2 = _peer_call(h2b, q, x1, mod, row1(final_g), lw['k1p'], lw['k2p'], lw['u_b'], lw['v_b'], l,
                    tiles_peer, final=(l == DEPTH - 1))
    return x2, new_states


def kernel(x_prompt, x_sample, state_hgrn, state_lru_h, state_lru_conv, state_dn, state_dn_conv,
           c_prompt, c_sample, w_ada, b_ada, norm1_g, norm2_g, w_in, lb_param, a_norm_g,
           lru_conv_w, lru_conv_b, lru_wa, lru_ba, lru_wx, lru_bx, lru_L,
           dn_conv_w, dn_A_log, dn_dt_bias, dn_norm_g, w_out,
           peer_wq, peer_k1, peer_k2, peer_u, peer_v, final_norm_g):
    n_p, seq, _ = x_prompt.shape
    n_s, dec_seq, _ = x_sample.shape
    mod = _ada_call(jnp.concatenate([c_prompt, c_sample], axis=0), w_ada, b_ada)
    xp = x_prompt.reshape(n_p * seq, D_MODEL)
    xs = x_sample.reshape(n_s * dec_seq, D_MODEL)
    sample_states = (state_hgrn, state_lru_h, state_lru_conv, state_dn, state_dn_conv)
    u_all = peer_u.astype(BF16)
    vt_all = jnp.swapaxes(peer_v, 1, 2).astype(BF16)
    p_new, s_new = [], []
    for l in range(DEPTH):
        p = dict(norm1_g=norm1_g[l], norm2_g=norm2_g[l], lb_param=lb_param, a_norm_g=a_norm_g[l],
                 lru_conv_w=lru_conv_w[l], lru_conv_b=lru_conv_b[l], lru_ba=lru_ba[l], lru_bx=lru_bx[l],
                 lru_L=lru_L[l], dn_conv_w=dn_conv_w[l], dn_A_log=dn_A_log[l], dn_dt_bias=dn_dt_bias[l],
                 dn_norm_g=dn_norm_g[l])
        lw = _layer_weights(l, w_in, w_out, peer_wq, peer_k1, peer_k2, peer_u, peer_v, lru_wa, lru_wx)
        lw['u_b'], lw['v_b'] = u_all, vt_all
        mod_p = mod[l, :n_p].reshape(n_p, 1, 6 * D_MODEL)
        mod_s = jnp.repeat(mod[l, n_p:], dec_seq, axis=0).reshape(-1, TOKEN_TILE, 6 * D_MODEL)
        xp, st_p = _trunk(xp, mod_p, None, True, p, lw, l, final_norm_g)
        xs, st_s = _trunk(xs, mod_s, tuple(s[l] for s in sample_states), False, p, lw, l, final_norm_g)
        p_new.append(st_p)
        s_new.append(st_s)
    stack = lambda sts: [jnp.stack([s[i] for s in sts]) for i in range(5)]
    p_st = stack(p_new)
    s_st = stack(s_new)
    return (xp.reshape(n_p, seq, D_MODEL), xs.reshape(n_s, dec_seq, D_MODEL), *p_st, *s_st)
```

```python
import functools

import jax
import jax.numpy as jnp
import numpy as np
from jax import lax
from jax.experimental import pallas as pl
from jax.experimental.pallas import tpu as pltpu

F32 = jnp.float32
BF16 = jnp.bfloat16

D_MODEL = 1024
DEPTH = 2
HEAD_DIM = 64
A_HEADS = 4
A_W = 256
B_W = 384
C_HEADS = 6
C_W = 384
LRU_C = 8.0
CONV_W = 4
N_IN = 3340
N_IN_PAD = 3456
ZA_W = 1024
ZB_W = 768
ZC_W = 1664
PEER_HEADS = 8
PEER_NKEYS = 128
PEER_TOPK = 16
PEER_N = PEER_NKEYS * PEER_NKEYS
EPS = 1e-6

LANES = 128
SUPER_BLOCK = 256
SAMPLE_SEQ_ROWS = 8
SAMPLE_HIST = CONV_W - 1
TOKEN_TILE = 512
PEER_TOKEN_TILE = 512
PEER_EXPERT_BLOCK = 2048
PEER_SUB_BLOCK = 256
PEER_ROUTE_TILE = 256
VMEM_LIMIT_BYTES = 56 * 1024 * 1024
assert TOKEN_TILE == PEER_TOKEN_TILE

_NN = (((1,), (0,)), ((), ()))
_NT = (((1,), (1,)), ((), ()))
_TN = (((0,), (0,)), ((), ()))


def _dot(a, b, dims=_NN):
    return lax.dot_general(a, b, dims, preferred_element_type=F32)


def _split2(x):
    hi = x.astype(BF16)
    lo = (x - hi.astype(F32)).astype(BF16)
    return hi, lo


def _split3(x):
    hi = x.astype(BF16)
    r = x - hi.astype(F32)
    mid = r.astype(BF16)
    lo = (r - mid.astype(F32)).astype(BF16)
    return hi, mid, lo


def _dot_x3(a, b, dims=_NN):
    ah, al = _split2(a)
    bh, bl = _split2(b)
    return _dot(ah, bh, dims) + _dot(ah, bl, dims) + _dot(al, bh, dims)


def _dot_b(a, b, dims=_NN):
    return _dot(a.astype(BF16), b.astype(BF16), dims)


def _dot_sel(a, sel, dims=_NN, passes=3):
    parts = _split3(a)[:passes]
    out = _dot(parts[0], sel, dims)
    for p in parts[1:]:
        out = out + _dot(p, sel, dims)
    return out


def _head_sums(x, ones_ref):
    ones = ones_ref[0:LANES, 0:LANES]
    xb = x.astype(BF16)
    return jnp.concatenate([_dot(xb[:, t:t + LANES], ones) for t in range(0, x.shape[1], LANES)], axis=1)


def _sel_dot(sel, b, passes=3):
    parts = _split3(b)[:passes]
    out = _dot(sel, parts[0])
    for p in parts[1:]:
        out = out + _dot(sel, p)
    return out


def _softplus(x):
    return jnp.maximum(x, 0.0) + jnp.log(1.0 + jnp.exp(-jnp.abs(x)))


def _silu(x):
    return x * jax.nn.sigmoid(x)


def _gelu_x2(x):
    c = float(np.sqrt(2.0 / np.pi))
    return x + x * jnp.tanh(x * (c + (0.044715 * c) * (x * x)))


def _gelu(x):
    return 0.5 * x * (1.0 + jnp.tanh(float(np.sqrt(2.0 / np.pi)) * (x + 0.044715 * (x * x * x))))


def _rmsnorm(x, g):
    return x * lax.rsqrt(jnp.mean(x * x, axis=-1, keepdims=True) + EPS) * g


def _const_spec(shape):
    nd = len(shape)
    return pl.BlockSpec(shape, lambda *_: (0,) * nd)


def _params(n_grid):
    return pltpu.CompilerParams(dimension_semantics=("arbitrary",) * n_grid,
                                vmem_limit_bytes=VMEM_LIMIT_BYTES)


def _ada_kernel(c_ref, w_ref, b_ref, o_ref):
    c = c_ref[...]
    o_ref[0] = _dot_x3(_silu(c), w_ref[0]) + b_ref[0]


def _ada_call(c_all, w_ada, b_ada):
    nb = c_all.shape[0]
    nt = 6 * D_MODEL // 1024
    return pl.pallas_call(
        _ada_kernel,
        grid=(DEPTH, nt),
        in_specs=[
            _const_spec((nb, D_MODEL)),
            pl.BlockSpec((1, D_MODEL, 1024), lambda l, j: (l, 0, j)),
            pl.BlockSpec((1, 1, 1024), lambda l, j: (l, 0, j)),
        ],
        out_specs=pl.BlockSpec((1, nb, 1024), lambda l, j: (l, 0, j)),
        out_shape=jax.ShapeDtypeStruct((DEPTH, nb, 6 * D_MODEL), F32),
        compiler_params=_params(2),
        name="ada_mod",
    )(c_all, w_ada, b_ada.reshape(DEPTH, 1, 6 * D_MODEL))


def _inproj_kernel(x_ref, g_ref, sc_ref, sh_ref, w_ref, za_ref, zb_ref, zc_ref):
    h = (_rmsnorm(x_ref[...], g_ref[...]) * (1.0 + sc_ref[...]) + sh_ref[...]).astype(BF16)
    for o_ref, lo, hi in ((za_ref, 0, ZA_W), (zb_ref, ZA_W, ZA_W + ZB_W), (zc_ref, ZA_W + ZB_W, N_IN_PAD)):
        o_ref[...] = _dot(h, w_ref[:, lo:hi])


MOD_SH1, MOD_SC1, MOD_GT1, MOD_SH2, MOD_SC2, MOD_GT2 = range(6)


def _mod_spec(mod, tiles_per_group, chunk):
    _, r, _ = mod.shape
    return pl.BlockSpec((None, r, D_MODEL), lambda i: (i // tiles_per_group, 0, chunk))


def _inproj_call(x, g, mod, wh, tiles_per_group):
    t = x.shape[0]
    tm = TOKEN_TILE
    row = lambda w: pl.BlockSpec((tm, w), lambda i: (i, 0))
    return pl.pallas_call(
        _inproj_kernel,
        grid=(t // tm,),
        in_specs=[row(D_MODEL), _const_spec((1, D_MODEL)), _mod_spec(mod, tiles_per_group, MOD_SC1),
                  _mod_spec(mod, tiles_per_group, MOD_SH1), _const_spec(wh.shape)],
        out_specs=[row(ZA_W), row(ZB_W), row(ZC_W)],
        out_shape=[jax.ShapeDtypeStruct((t, ZA_W), F32), jax.ShapeDtypeStruct((t, ZB_W), F32),
                   jax.ShapeDtypeStruct((t, ZC_W), F32)],
        compiler_params=_params(1),
        name="inproj",
    )(x, g, mod, mod, wh)


class _SeqCfg:
    def __init__(self, seq_rows, hist, real, chunk, n_par):
        self.seq_rows = seq_rows
        self.n_par = n_par
        self.hist = hist
        self.real = real
        self.chunk = chunk
        self.masked = hist > 0 or hist + real < seq_rows
        self.n_chunks = SUPER_BLOCK // chunk
        self.long_seq = seq_rows > SUPER_BLOCK
        self.sb_per_seq = max(seq_rows // SUPER_BLOCK, 1)
        self.seq_per_sb = max(SUPER_BLOCK // seq_rows, 1)
        assert (self.long_seq or chunk == seq_rows) and (self.long_seq or n_par == 1)


PROMPT_CFG = _SeqCfg(seq_rows=2048, hist=0, real=2048, chunk=16, n_par=4)
PROMPT_HGRN_CFG = _SeqCfg(seq_rows=2048, hist=0, real=2048, chunk=16, n_par=8)
SAMPLE_CFG = _SeqCfg(seq_rows=SAMPLE_SEQ_ROWS, hist=SAMPLE_HIST, real=4, chunk=SAMPLE_SEQ_ROWS, n_par=1)


def _row_in_seq(cfg, n_rows):
    r = lax.broadcasted_iota(jnp.int32, (n_rows, 1), 0)
    return r % min(cfg.seq_rows, SUPER_BLOCK)


def _real_mask(cfg, n_rows):
    r = _row_in_seq(cfg, n_rows)
    return (r >= cfg.hist) & (r < cfg.hist + cfg.real)


def _causal_conv(x, xp_ref, w_ref):
    n = x.shape[0]
    xp_ref[8:8 + n, :] = x
    y = x * w_ref[CONV_W - 1:CONV_W, :]
    for k in range(CONV_W - 1):
        s = CONV_W - 1 - k
        y = y + xp_ref[8 - s:8 - s + n, :] * w_ref[k:k + 1, :]
    xp_ref[0:8, :] = x[n - 8:n, :]
    return y


def _lane_tiles(x):
    return [x[:, t:t + LANES] for t in range(0, x.shape[1], LANES)]


def _expand_state(s_cat, tile_ref, bd_ref):
    tile = tile_ref[:, 0:LANES]
    bd = bd_ref[0:LANES, 0:LANES]
    return jnp.stack([_dot_sel(s_cat[r:r + LANES, :], tile, passes=2) * bd
                      for r in range(0, s_cat.shape[0], LANES)])


def _compress_state(s_tiles, tile_t_ref):
    tile_t = tile_t_ref[0:LANES, :]
    return jnp.concatenate([_dot_sel(s_tiles[t], tile_t, passes=2) for t in range(s_tiles.shape[0])], axis=0)


def _state_matmul(x, s_tiles, dims):
    return jnp.concatenate([_dot_b(xt, s_tiles[t], dims) for t, xt in enumerate(_lane_tiles(x))], axis=1)


def _state_update(s_tiles, decay_row, a, b, bd_ref):
    bd = bd_ref[0:LANES, 0:LANES]
    return jnp.stack([s_tiles[t] * dt + bd * _dot_b(at, bt, _TN)
                      for t, (dt, at, bt) in enumerate(zip(_lane_tiles(decay_row), _lane_tiles(a), _lane_tiles(b)))])


def _head_norm_gate(o, z, g, ones_ref):
    ms = _dot_sel(o * o, ones_ref[...], passes=2) * (1.0 / HEAD_DIM)
    return o * lax.rsqrt(ms + EPS) * g * _silu(z)


def _run_chunks(cfg, sb, step, st_ref, s0_ref, sout_ref, tile_ref, tile_t_ref, bd_ref):
    if cfg.long_seq:
        @pl.when(sb % cfg.sb_per_seq == 0)
        def _():
            for p in range(cfg.n_par):
                st_ref[p] = _expand_state(s0_ref[p], tile_ref, bd_ref)

        def body(k, carry):
            for p in range(cfg.n_par):
                st_ref[p] = step(k, p, st_ref[p])
            return carry

        lax.fori_loop(0, cfg.n_chunks, body, 0, unroll=2)

        @pl.when(sb % cfg.sb_per_seq == cfg.sb_per_seq - 1)
        def _():
            for p in range(cfg.n_par):
                sout_ref[p] = _compress_state(st_ref[p], tile_t_ref)
    else:
        def body(k, carry):
            st_new = step(k, 0, _expand_state(s0_ref[k], tile_ref, bd_ref))
            sout_ref[k] = _compress_state(st_new, tile_t_ref)
            return carry

        lax.fori_loop(0, cfg.n_chunks, body, 0, unroll=4)


def _hgrn_kernel(z_ref, lbp_ref, g_ref, s0_ref, ltri_ref, ones_ref, tile_ref, tile_t_ref, bd_ref,
                 o_ref, sout_ref,
                 st_ref, q_s, k_s, g_s, qe_s, kt_s, egl_s, o_s, *, cfg, layer):
    sb = pl.program_id(0)
    c = cfg.chunk
    n = SUPER_BLOCK
    nch = cfg.n_chunks
    w = A_W
    lbp = lbp_ref[...]
    e = jnp.exp(lbp - jnp.max(lbp, axis=0, keepdims=True))
    lbs = e / jnp.sum(e, axis=0, keepdims=True)
    lb = jnp.sum(lbs[0:layer + 1, :], axis=0, keepdims=True) - lbs[0:1, :]

    for p in range(cfg.n_par):
        r = slice(p * n, (p + 1) * n)
        aq = z_ref[p, :, 0:w]
        af = z_ref[p, :, w:2 * w]
        f = lb + (1.0 - lb) * jax.nn.sigmoid(af)
        lf = jnp.log(f)
        kk = 1.0 - f
        if cfg.masked:
            real = _real_mask(cfg, n)
            lf = jnp.where(real, lf, 0.0)
            kk = jnp.where(real, kk, 0.0)
        q = _silu(aq) * (HEAD_DIM ** -0.5)
        gcum = _sel_dot(ltri_ref[...], lf)
        g3 = gcum.reshape(nch, c, w)
        gl3 = g3[:, c - 1:c, :]
        q_s[r, :] = q
        k_s[r, :] = kk
        g_s[r, :] = gcum
        qe_s[r, :] = q * jnp.exp(gcum)
        kt_s[r, :] = (kk.reshape(nch, c, w) * jnp.exp(gl3 - g3)).reshape(n, w)
        egl_s[p * nch:(p + 1) * nch, :] = jnp.exp(gl3).reshape(nch, w)

    jio = lax.broadcasted_iota(jnp.int32, (c, 1), 0)
    bd = bd_ref[...]

    def chunk(k, p, st):
        rows = pl.ds(pl.multiple_of(p * n + k * c, c), c)
        q_c = q_s[rows, :]
        k_c = k_s[rows, :]
        g_c = g_s[rows, :]
        v_c = z_ref[p, pl.ds(pl.multiple_of(k * c, c), c), 2 * w:3 * w]
        o_inter = _state_matmul(qe_s[rows, :], st, _NT)
        lo = min(c, 8)
        d_rows = []
        for i in range(c):
            nj = lo if i < lo else c
            dec = jnp.exp(jnp.minimum(g_c[i:i + 1, :] - g_c[0:nj, :], 0.0))
            d_rows.append(jnp.where(jio[0:nj] <= i, k_c[0:nj, :] * dec * q_c[i:i + 1, :], 0.0))
        a_b = _head_sums(jnp.concatenate(d_rows, axis=0), ones_ref)
        o_diag = jnp.sum(a_b[0:lo * lo, :].reshape(lo, lo, w) * v_c[None, 0:lo, :], axis=1)
        if c > lo:
            o_hi = jnp.sum(a_b[lo * lo:, :].reshape(c - lo, c, w) * v_c[None, :, :], axis=1)
            o_diag = jnp.concatenate([o_diag, o_hi], axis=0)
        o_s[rows, :] = o_inter + o_diag
        return _state_update(st, egl_s[pl.ds(p * nch + k, 1), :], v_c, kt_s[rows, :], bd_ref)

    _run_chunks(cfg, sb, chunk, st_ref, s0_ref, sout_ref, tile_ref, tile_t_ref, bd_ref)
    for p in range(cfg.n_par):
        o_ref[p] = _head_norm_gate(o_s[p * n:(p + 1) * n, :], z_ref[p, :, 3 * w:4 * w], g_ref[...], ones_ref)


def _mixer_consts(heads, cfg):
    w = heads * HEAD_DIM
    lane_head = np.arange(w) // HEAD_DIM
    ones = (lane_head[:, None] == lane_head[None, :]).astype(np.float32)
    tile = (np.arange(HEAD_DIM)[:, None] == (np.arange(w) % HEAD_DIM)[None, :]).astype(np.float32)
    r = np.arange(SUPER_BLOCK)
    ltri = ((r[:, None] // cfg.chunk == r[None, :] // cfg.chunk) & (r[None, :] <= r[:, None])).astype(np.float32)
    return dict(ltri=jnp.asarray(ltri, BF16), ones=jnp.asarray(ones, BF16), tile=jnp.asarray(tile, BF16),
                tile_t=jnp.asarray(tile.T, BF16), bd=jnp.asarray(ones, F32))


def _state_specs(cfg, w):
    blk = (cfg.n_par if cfg.long_seq else cfg.seq_per_sb, w, HEAD_DIM)
    if cfg.long_seq:
        imap = lambda i: (i // cfg.sb_per_seq, 0, 0)
    else:
        imap = lambda i: (i, 0, 0)
    return pl.BlockSpec(blk, imap)


def _seq_view(x, cfg):
    if not cfg.long_seq:
        return x[None]
    assert x.shape[0] % (cfg.n_par * cfg.seq_rows) == 0, "sequence count must be a multiple of n_par"
    return x.reshape(-1, cfg.seq_rows, x.shape[-1])


def _seq_block_spec(cfg, width):
    if cfg.long_seq:
        return pl.BlockSpec((cfg.n_par, SUPER_BLOCK, width), lambda i: (i // cfg.sb_per_seq, i % cfg.sb_per_seq, 0))
    return pl.BlockSpec((1, SUPER_BLOCK, width), lambda i: (0, i, 0))


def _hgrn_call(z_a, lb_param, g_exp, s0_t, cfg, layer):
    rows = z_a.shape[0]
    w = A_W
    cst = _mixer_consts(A_HEADS, cfg)
    n_seq = s0_t.shape[0]
    par_rows = cfg.n_par * SUPER_BLOCK
    vm = lambda shape: pltpu.VMEM(shape, F32)
    kern = functools.partial(_hgrn_kernel, cfg=cfg, layer=layer)
    z3 = _seq_view(z_a, cfg)
    oa, s_new = pl.pallas_call(
        kern,
        grid=(rows // par_rows,),
        in_specs=[_seq_block_spec(cfg, ZA_W), _const_spec((DEPTH, w)),
                  _const_spec((1, w)), _state_specs(cfg, w), _const_spec(cst["ltri"].shape),
                  _const_spec(cst["ones"].shape), _const_spec(cst["tile"].shape),
                  _const_spec(cst["tile_t"].shape), _const_spec(cst["bd"].shape)],
        out_specs=[_seq_block_spec(cfg, w), _state_specs(cfg, w)],
        out_shape=[jax.ShapeDtypeStruct(z3.shape[:2] + (w,), F32), jax.ShapeDtypeStruct((n_seq, w, HEAD_DIM), F32)],
        scratch_shapes=[vm((cfg.n_par, w // LANES, LANES, LANES))] + [vm((par_rows, w))] * 5
        + [vm((cfg.n_par * cfg.n_chunks, w)), vm((par_rows, w))],
        compiler_params=_params(1),
        name="hgrn2",
    )(z3, lb_param, g_exp, s0_t, cst["ltri"], cst["ones"], cst["tile"], cst["tile_t"], cst["bd"])
    return oa.reshape(rows, w), s_new


def _lru_kernel(*refs, cfg):
    if cfg.long_seq:
        (z_ref, cw_ref, cb_ref, wg_ref, ba_ref, bx_ref, lp_ref,
         ob_ref, hs_ref, prev_ref, hc_ref) = refs
        inj_ref = None
    else:
        (z_ref, inj_ref, cw_ref, cb_ref, wg_ref, ba_ref, bx_ref, lp_ref,
         ob_ref, hs_ref, prev_ref, hc_ref) = refs
    sb = pl.program_id(0)
    n = SUPER_BLOCK
    w = B_W

    @pl.when(sb % cfg.sb_per_seq == 0)
    def _():
        prev_ref[0:8, :] = jnp.zeros((8, prev_ref.shape[1]), F32)
        hc_ref[...] = jnp.zeros_like(hc_ref)

    xc = _causal_conv(z_ref[:, 0:w], prev_ref, cw_ref) + cb_ref[...]
    gates = _dot_b(xc, wg_ref[...])
    r = jax.nn.sigmoid(gates[:, 0:w] + ba_ref[...])
    ig = jax.nn.sigmoid(gates[:, w:2 * w] + bx_ref[...])
    log_a = -LRU_C * r * _softplus(-lp_ref[...])
    a = jnp.exp(log_a)
    b = jnp.sqrt(1.0 - jnp.exp(2.0 * log_a)) * ig * xc
    if cfg.masked:
        real = _real_mask(cfg, n)
        a = jnp.where(real, a, 1.0)
        b = jnp.where(real, b, 0.0)
    if inj_ref is not None:
        b = b + inj_ref[...]
    group = 8
    assert cfg.seq_rows % group == 0
    rig = lax.broadcasted_iota(jnp.int32, (n, 1), 0) % group
    d = 1
    while d < group:
        has = rig >= d
        a_sh = jnp.where(has, pltpu.roll(a, d, 0), 1.0)
        b_sh = jnp.where(has, pltpu.roll(b, d, 0), 0.0)
        b = b + a * b_sh
        a = a * a_sh
        d *= 2
    if cfg.long_seq:
        a3 = a.reshape(n // group, group, w)
        b3 = b.reshape(n // group, group, w)
        carry = hc_ref[...]
        rows = []
        for i in range(n // group):
            h_i = b3[i] + a3[i] * carry
            rows.append(h_i)
            carry = h_i[group - 1:group, :]
        hs = jnp.concatenate(rows, axis=0)
        hc_ref[...] = carry
    else:
        hs = b
    hs_ref[...] = hs
    ob_ref[...] = hs * _gelu(z_ref[:, w:2 * w])


def _lru_call(z_b, inj, cw, cb, wg, ba, bx, lp, cfg):
    rows = z_b.shape[0]
    w = B_W
    row = lambda width: pl.BlockSpec((SUPER_BLOCK, width), lambda i: (i, 0))
    ins = [z_b] + ([] if cfg.long_seq else [inj]) + [cw, cb, wg, ba, bx, lp]
    specs = [row(ZB_W)] + ([] if cfg.long_seq else [row(w)]) + [_const_spec(a.shape) for a in ins[-6:]]
    return pl.pallas_call(
        functools.partial(_lru_kernel, cfg=cfg),
        grid=(rows // SUPER_BLOCK,),
        in_specs=specs,
        out_specs=[row(w), row(w)],
        out_shape=[jax.ShapeDtypeStruct((rows, w), F32)] * 2,
        scratch_shapes=[pltpu.VMEM((8 + SUPER_BLOCK, w), F32), pltpu.VMEM((1, w), F32)],
        compiler_params=_params(1),
        name="rglru",
    )(*ins)


def _dn_kernel(z_ref, cw_ref, alog_ref, dtb_ref, g_ref, s0_ref, ltri_ref, ones_ref, tile_ref, tile_t_ref,
               bd_ref, expb_ref, expa_ref,
               o_ref, sout_ref,
               st_ref, prev_ref, q_s, k_s, v_s, g_s, eg_s, beta_s, kt_s, egl_s, o_s, *, cfg):
    sb = pl.program_id(0)
    c = cfg.chunk
    n = SUPER_BLOCK
    nch = cfg.n_chunks
    w = C_W

    @pl.when(sb % cfg.sb_per_seq == 0)
    def _():
        for p in range(cfg.n_par):
            prev_ref[p, 0:8, :] = jnp.zeros((8, prev_ref.shape[2]), F32)

    ones = ones_ref[...]
    for p in range(cfg.n_par):
        r = slice(p * n, (p + 1) * n)
        qkv = _silu(_causal_conv(z_ref[p, :, 0:3 * w], prev_ref.at[p], cw_ref))
        q = qkv[:, 0:w]
        kx = qkv[:, w:2 * w]
        q = q * lax.rsqrt(_dot_sel(q * q, ones, passes=2) + EPS) * (HEAD_DIM ** -0.5)
        kx = kx * lax.rsqrt(_dot_sel(kx * kx, ones, passes=2) + EPS)
        pc = z_ref[p, :, 4 * w:4 * w + LANES]
        beta = _dot_sel(jax.nn.sigmoid(pc), expb_ref[...], passes=2)
        gdec = _dot_sel(-jnp.exp(alog_ref[...]) * _softplus(pc + dtb_ref[...]), expa_ref[...], passes=2)
        if cfg.masked:
            real = _real_mask(cfg, n)
            beta = jnp.where(real, beta, 0.0)
            gdec = jnp.where(real, gdec, 0.0)
        gcum = _sel_dot(ltri_ref[...], gdec)
        g3 = gcum.reshape(nch, c, w)
        gl3 = g3[:, c - 1:c, :]
        q_s[r, :] = q
        k_s[r, :] = kx
        v_s[r, :] = qkv[:, 2 * w:3 * w]
        g_s[r, :] = gcum
        eg_s[r, :] = jnp.exp(gcum)
        beta_s[r, :] = beta
        kt_s[r, :] = (kx.reshape(nch, c, w) * jnp.exp(gl3 - g3)).reshape(n, w)
        egl_s[p * nch:(p + 1) * nch, :] = jnp.exp(gl3).reshape(nch, w)

    iio = lax.broadcasted_iota(jnp.int32, (c, 1), 0)
    bd = bd_ref[...]

    def chunk(k, p, st):
        rows = pl.ds(pl.multiple_of(p * n + k * c, c), c)
        q_c = q_s[rows, :]
        k_c = k_s[rows, :]
        g_c = g_s[rows, :]
        eg_c = eg_s[rows, :]
        beta_c = beta_s[rows, :]
        qk_s = _state_matmul(jnp.concatenate([q_c, k_c], axis=0), st, _NN)
        q_st = qk_s[0:c, :]
        k_st = qk_s[c:2 * c, :]
        lo = min(c, 8)
        tail = lambda x: x[lo:c, :]
        d_rows = ([k_c * k_c[j:j + 1, :] for j in range(lo)] + [q_c * k_c[j:j + 1, :] for j in range(lo)]
                  + [tail(k_c) * k_c[j:j + 1, :] for j in range(lo, c)]
                  + [tail(q_c) * k_c[j:j + 1, :] for j in range(lo, c)])
        dots = _head_sums(jnp.concatenate(d_rows, axis=0), ones_ref)
        e = beta_c * (v_s[rows, :] - eg_c * k_st)
        o = eg_c * q_st
        for j in range(lo):
            dec = jnp.exp(jnp.minimum(g_c - g_c[j:j + 1, :], 0.0))
            m_col = jnp.where(iio > j, beta_c * dots[j * c:(j + 1) * c, :] * dec, 0.0)
            e_j = e[j:j + 1, :]
            e = e - m_col * e_j
            qk_col = jnp.where(iio >= j, dots[(lo + j) * c:(lo + j + 1) * c, :] * dec, 0.0)
            o = o + qk_col * e_j
        if c > lo:
            nt = c - lo
            e_t, o_t, g_t, beta_t = tail(e), tail(o), tail(g_c), tail(beta_c)
            base = 2 * lo * c
            for j in range(lo, c):
                dec = jnp.exp(jnp.minimum(g_t - g_c[j:j + 1, :], 0.0))
                kk = dots[base + (j - lo) * nt:base + (j - lo + 1) * nt, :]
                qk = dots[base + (nt + j - lo) * nt:base + (nt + j - lo + 1) * nt, :]
                e_j = e_t[j - lo:j - lo + 1, :]
                e_t = e_t - jnp.where(iio[0:nt] > j - lo, beta_t * kk * dec, 0.0) * e_j
                o_t = o_t + jnp.where(iio[0:nt] >= j - lo, qk * dec, 0.0) * e_j
            e = jnp.concatenate([e[0:lo, :], e_t], axis=0)
            o = jnp.concatenate([o[0:lo, :], o_t], axis=0)
        o_s[rows, :] = o
        return _state_update(st, egl_s[pl.ds(p * nch + k, 1), :], kt_s[rows, :], e, bd_ref)

    _run_chunks(cfg, sb, chunk, st_ref, s0_ref, sout_ref, tile_ref, tile_t_ref, bd_ref)
    for p in range(cfg.n_par):
        o_ref[p] = _head_norm_gate(o_s[p * n:(p + 1) * n, :], z_ref[p, :, 3 * w:4 * w], g_ref[...], ones_ref)


def _dn_call(z_c, cw, alog_exp, dtb_exp, g_exp, s0, cfg):
    rows = z_c.shape[0]
    w = C_W
    cst = _mixer_consts(C_HEADS, cfg)
    lane_head = np.arange(w) // HEAD_DIM
    expb = (np.arange(128)[:, None] == lane_head[None, :]).astype(np.float32)
    expa = (np.arange(128)[:, None] == (lane_head[None, :] + C_HEADS)).astype(np.float32)
    expb = jnp.asarray(expb, BF16)
    expa = jnp.asarray(expa, BF16)
    n_seq = s0.shape[0]
    par_rows = cfg.n_par * SUPER_BLOCK
    vm = lambda shape: pltpu.VMEM(shape, F32)
    z3 = _seq_view(z_c, cfg)
    oc, s_new = pl.pallas_call(
        functools.partial(_dn_kernel, cfg=cfg),
        grid=(rows // par_rows,),
        in_specs=[_seq_block_spec(cfg, ZC_W), _const_spec(cw.shape),
                  _const_spec((1, LANES)), _const_spec((1, LANES)), _const_spec((1, w)), _state_specs(cfg, w),
                  _const_spec(cst["ltri"].shape), _const_spec(cst["ones"].shape),
                  _const_spec(cst["tile"].shape), _const_spec(cst["tile_t"].shape),
                  _const_spec(cst["bd"].shape), _const_spec(expb.shape), _const_spec(expa.shape)],
        out_specs=[_seq_block_spec(cfg, w), _state_specs(cfg, w)],
        out_shape=[jax.ShapeDtypeStruct(z3.shape[:2] + (w,), F32), jax.ShapeDtypeStruct((n_seq, w, HEAD_DIM), F32)],
        scratch_shapes=[vm((cfg.n_par, w // LANES, LANES, LANES)), vm((cfg.n_par, 8 + SUPER_BLOCK, 3 * w))]
        + [vm((par_rows, w))] * 7
        + [vm((cfg.n_par * cfg.n_chunks, w)), vm((par_rows, w))],
        compiler_params=_params(1),
        name="deltanet",
    )(z3, cw, alog_exp, dtb_exp, g_exp, s0, cst["ltri"], cst["ones"], cst["tile"], cst["tile_t"],
      cst["bd"], expb, expa)
    return oc.reshape(rows, w), s_new


def _outproj_kernel(x_ref, oa_ref, ob_ref, oc_ref, gt_ref, g_ref, sc_ref, sh_ref,
                    wo_ref, wq_ref, x1_ref, h2_ref, q_ref):
    mo = _dot_b(oa_ref[...], wo_ref[0:A_W, :])
    mo = mo + _dot_b(ob_ref[...], wo_ref[A_W:A_W + B_W, :])
    mo = mo + _dot_b(oc_ref[...], wo_ref[A_W + B_W:, :])
    x1 = x_ref[...] + gt_ref[...] * mo
    x1_ref[...] = x1
    h2 = _rmsnorm(x1, g_ref[...]) * (1.0 + sc_ref[...]) + sh_ref[...]
    h2_ref[...] = h2.astype(BF16)
    q_ref[...] = _dot_b(h2, wq_ref[...])


def _outproj_call(x, oa, ob, oc, mod, g, wo, wq, tiles_per_group):
    t = x.shape[0]
    tm = TOKEN_TILE
    row = lambda w: pl.BlockSpec((tm, w), lambda i: (i, 0))
    return pl.pallas_call(
        _outproj_kernel,
        grid=(t // tm,),
        in_specs=[row(D_MODEL), row(A_W), row(B_W), row(C_W), _mod_spec(mod, tiles_per_group, MOD_GT1),
                  _const_spec((1, D_MODEL)), _mod_spec(mod, tiles_per_group, MOD_SC2),
                  _mod_spec(mod, tiles_per_group, MOD_SH2),
                  _const_spec(wo.shape), _const_spec(wq.shape)],
        out_specs=[row(D_MODEL), row(D_MODEL), row(D_MODEL)],
        out_shape=[jax.ShapeDtypeStruct((t, D_MODEL), F32), jax.ShapeDtypeStruct((t, D_MODEL), BF16),
                   jax.ShapeDtypeStruct((t, D_MODEL), F32)],
        compiler_params=_params(1),
        name="outproj",
    )(x, oa, ob, oc, mod, g, mod, mod, wo, wq)


def _top16(s):
    vals = []
    for r in range(PEER_TOPK):
        m = jnp.max(s, axis=0, keepdims=True)
        s = jnp.where(s == m, -(32.0 + r) * 2.0 ** 95, s)
        vals.append(m)
    rank = jnp.where(s <= -(2.0 ** 99), s * -(2.0 ** -95) - 32.0, float(PEER_TOPK))
    return jnp.concatenate(vals, axis=0), rank


def _route_tile(s1, s2):
    nk, t = s1.shape
    v1, rank1 = _top16(s1)
    v2, rank2 = _top16(s2)
    c3 = v1[:, None, :] + v2[None, :, :]
    jrow = lax.broadcasted_iota(jnp.int32, (8, 1), 0)
    parts = [v1[0:1, :] + v2, v1[1:2, :] + v2[0:8, :]]
    for i in range(2, 8):
        parts.append(jnp.where(jrow < PEER_TOPK // (i + 1), v1[i:i + 1, :] + v2[0:8, :], -jnp.inf))
    parts.append(v1[8:16, :] + v2[0:1, :])
    cand = jnp.concatenate(parts, axis=0)
    m = None
    for _ in range(PEER_TOPK):
        m = jnp.max(cand, axis=0, keepdims=True)
        cand = jnp.where(cand == m, -jnp.inf, cand)
    tau = m
    sel = c3 >= tau[None, :, :]
    m0 = v1[0:1, :] + v2[0:1, :]
    zsum = jnp.sum(jnp.where(sel, jnp.exp(c3 - m0[None, :, :]), 0.0).reshape(-1, t), axis=0, keepdims=True)
    n_i = jnp.sum(jnp.where(sel, 1.0, 0.0), axis=1)
    rank1 = rank1.astype(BF16)
    cnt = jnp.zeros((nk, t), BF16)
    for i in range(PEER_TOPK):
        cnt = cnt + jnp.where(rank1 == i, n_i[i:i + 1, :].astype(BF16), jnp.zeros((), BF16))
    return (cnt.astype(F32), jnp.exp(s1 - v1[0:1, :]), rank2.astype(BF16),
            (jnp.exp(s2 - v2[0:1, :]) * (0.5 / zsum)).astype(BF16))


def _peer_kernel(h2_ref, q_ref, x1_ref, gt_ref, fg_ref, k1_ref, k2_ref, u_ref, vt_ref, o_ref,
                 cnt_s, e1_s, rank2_s, e2_s, acc_s, s1_s, s2_s, coef_s, *, final, n_steps):
    g = pl.program_id(0)
    tt = PEER_TOKEN_TILE
    rt = PEER_ROUTE_TILE
    nk = PEER_NKEYS
    nj = PEER_N // PEER_EXPERT_BLOCK
    a_per_step = PEER_EXPERT_BLOCK // nk
    j = jnp.minimum(g, n_steps - 2) % nj
    gb = jnp.maximum(g - 1, 0)

    @pl.when(g == 0)
    def _():
        coef_s[...] = jnp.zeros_like(coef_s)

    @pl.when(gb % nj == 0)
    def _():
        acc_s[...] = jnp.zeros_like(acc_s)

    @pl.when(jnp.logical_and(g % nj == 0, g < n_steps - 1))
    def _():

        def head(h, carry):
            qh = q_ref[:, pl.ds(pl.multiple_of(h * nk, nk), nk)]
            s1_s[...] = _dot_x3(k1_ref[...], qh, _NT)
            s2_s[...] = _dot_x3(k2_ref[...], qh, _NT)

            def lane_tile(c, carry2):
                cols = pl.ds(pl.multiple_of(c * rt, rt), rt)
                (cnt_s[h, :, cols], e1_s[h, :, cols], rank2_s[h, :, cols],
                 e2_s[h, :, cols]) = _route_tile(s1_s[:, cols], s2_s[:, cols])
                return carry2

            lax.fori_loop(0, tt // rt, lane_tile, 0)
            return carry

        lax.fori_loop(0, PEER_HEADS, head, 0)

    h2 = h2_ref[...]
    n_sub = PEER_EXPERT_BLOCK // PEER_SUB_BLOCK
    a_per_sub = PEER_SUB_BLOCK // nk
    sub = lambda s: slice(s * PEER_SUB_BLOCK, (s + 1) * PEER_SUB_BLOCK)
    cur = coef_s.at[g % 2]
    prev = coef_s.at[(g + 1) % 2]
    hids = [_dot(u_ref[sub(s), :], h2, _NT).astype(BF16) for s in range(n_sub)]
    acc_s[...] += _dot(vt_ref[...], prev[...])
    pack = 16
    for s in range(n_sub):
        for al in range(a_per_sub):
            a = j * a_per_step + s * a_per_sub + al
            wsum = jnp.zeros((nk // pack, pack, tt), BF16)
            for h in range(PEER_HEADS):
                cnt_row = jnp.broadcast_to(cnt_s[h, pl.ds(a, 1), :], (pack, tt)).astype(BF16)
                e1_row = jnp.broadcast_to(e1_s[h, pl.ds(a, 1), :], (pack, tt)).astype(BF16)
                rank2 = rank2_s[h].reshape(nk // pack, pack, tt)
                e2 = e2_s[h].reshape(nk // pack, pack, tt)
                wsum = jnp.where(rank2 < cnt_row[None], wsum + e2 * e1_row[None], wsum)
            r0 = s * PEER_SUB_BLOCK + al * nk
            cur[r0:r0 + nk, :] = _gelu_x2(hids[s][al * nk:(al + 1) * nk, :]) * wsum.reshape(nk, tt)

    @pl.when(jnp.logical_and(g > 0, gb % nj == nj - 1))
    def _():
        x2 = x1_ref[...] + gt_ref[...] * acc_s[...].T
        if final:
            x2 = _rmsnorm(x2, fg_ref[...])
        o_ref[...] = x2


def _peer_call(h2b, q, x1, mod, fg, k1p, k2p, u_b, v_b, layer, tiles_per_group, final):
    t = h2b.shape[0]
    tt = PEER_TOKEN_TILE
    nb = PEER_EXPERT_BLOCK
    nj = PEER_N // nb
    n_tiles = t // tt
    n_steps = n_tiles * nj + 1
    front = lambda g: jnp.minimum(g, n_steps - 2)
    back = lambda g: jnp.maximum(g - 1, 0)
    row_f = lambda w: pl.BlockSpec((tt, w), lambda g: (front(g) // nj, 0))
    row_b = lambda w: pl.BlockSpec((tt, w), lambda g: (back(g) // nj, 0))
    r = mod.shape[1]
    tab = lambda dt: pltpu.VMEM((PEER_HEADS, PEER_NKEYS, tt), dt)
    return pl.pallas_call(
        functools.partial(_peer_kernel, final=final, n_steps=n_steps),
        grid=(n_steps,),
        in_specs=[row_f(D_MODEL), row_f(D_MODEL), row_b(D_MODEL),
                  pl.BlockSpec((None, r, D_MODEL), lambda g: (back(g) // nj // tiles_per_group, 0, MOD_GT2)),
                  _const_spec((1, D_MODEL)),
                  _const_spec((PEER_NKEYS, PEER_NKEYS)),
                  _const_spec((PEER_NKEYS, PEER_NKEYS)),
                  pl.BlockSpec((None, nb, D_MODEL), lambda g: (layer, front(g) % nj, 0)),
                  pl.BlockSpec((None, D_MODEL, nb), lambda g: (layer, 0, back(g) % nj))],
        out_specs=row_b(D_MODEL),
        out_shape=jax.ShapeDtypeStruct((t, D_MODEL), F32),
        scratch_shapes=[tab(F32), tab(F32), tab(BF16), tab(BF16), pltpu.VMEM((D_MODEL, tt), F32),
                        pltpu.VMEM((PEER_NKEYS, tt), F32), pltpu.VMEM((PEER_NKEYS, tt), F32),
                        pltpu.VMEM((2, nb, tt), BF16)],
        compiler_params=_params(1),
        name="peer",
    )(h2b, q, x1, mod, fg, k1p, k2p, u_b, v_b)


def _block_diag(wblk):
    n, d, e = wblk.shape
    eye = jnp.eye(n, dtype=wblk.dtype)
    return (eye[:, None, :, None] * wblk[:, :, None, :]).reshape(n * d, n * e)


def _pad_sample(z, hist=None):
    bsz = z.shape[0] // 4
    w = z.shape[1]
    z3 = z.reshape(bsz, 4, w)
    h3 = jnp.zeros((bsz, SAMPLE_HIST, w), F32)
    if hist is not None:
        h3 = h3.at[:, :, :hist.shape[-1]].set(hist)
    return jnp.concatenate([h3, z3, jnp.zeros((bsz, 1, w), F32)], axis=1).reshape(bsz * SAMPLE_SEQ_ROWS, w)


def _unpad_sample(o):
    bsz = o.shape[0] // SAMPLE_SEQ_ROWS
    return o.reshape(bsz, SAMPLE_SEQ_ROWS, -1)[:, SAMPLE_HIST:SAMPLE_HIST + 4].reshape(bsz * 4, -1)


def _layer_weights(l, w_in, w_out, peer_wq, peer_k1, peer_k2, peer_u, peer_v, lru_wa, lru_wx):
    w_in_p = jnp.pad(w_in[l], ((0, 0), (0, N_IN_PAD - N_IN)))
    wih = w_in_p.astype(BF16)
    wo = w_out[l].astype(BF16)
    wq = peer_wq[l].astype(BF16)
    wg = jnp.concatenate([_block_diag(lru_wa[l]), _block_diag(lru_wx[l])], axis=1).astype(BF16)
    half = PEER_NKEYS // 2
    k1p = jnp.pad(peer_k1[l], ((0, 0), (0, half)))
    k2p = jnp.pad(peer_k2[l], ((0, 0), (half, 0)))
    return dict(wih=wih, wo=wo, wq=wq, wg=wg, k1p=k1p, k2p=k2p)


def _trunk(x, mod, states, is_prompt, p, lw, l, final_g):
    t = x.shape[0]
    if is_prompt:
        cfg = PROMPT_CFG
        tiles_tok = 2048 // TOKEN_TILE
        tiles_peer = 2048 // PEER_TOKEN_TILE
        n_seq = t // 2048
    else:
        cfg = SAMPLE_CFG
        tiles_tok = 1
        tiles_peer = 1
        n_seq = t // 4
    row1 = lambda v: v.reshape(1, -1)
    z_a, z_b, z_c = _inproj_call(x, row1(p['norm1_g']), mod, lw['wih'], tiles_tok)
    if is_prompt:
        s_a = jnp.zeros((n_seq, A_W, HEAD_DIM), F32)
        s_c = jnp.zeros((n_seq, C_W, HEAD_DIM), F32)
        za_m, zb_m, zc_m = z_a, z_b, z_c
        inj = None
    else:
        st_hgrn, st_lru_h, st_lru_conv, st_dn, st_dn_conv = states
        s_a = jnp.swapaxes(st_hgrn, -1, -2).reshape(n_seq, A_W, HEAD_DIM)
        s_c = st_dn.reshape(n_seq, C_W, HEAD_DIM)
        za_m = _pad_sample(z_a)
        zb_m = _pad_sample(z_b, st_lru_conv)
        zc_m = _pad_sample(z_c, st_dn_conv)
        inj = jnp.zeros((n_seq, SAMPLE_SEQ_ROWS, B_W), F32).at[:, 0].set(st_lru_h).reshape(-1, B_W)
    rep = lambda v, h: row1(jnp.tile(v, h))
    oa, s_a_new = _hgrn_call(za_m, p['lb_param'], rep(p['a_norm_g'], A_HEADS), s_a,
                             PROMPT_HGRN_CFG if is_prompt else cfg, l)
    ob, hs = _lru_call(zb_m, inj, p['lru_conv_w'], row1(p['lru_conv_b']), lw['wg'],
                       row1(p['lru_ba']), row1(p['lru_bx']), row1(p['lru_L']), cfg)
    decay_cols = lambda v: jnp.zeros((1, LANES), F32).at[0, C_HEADS:2 * C_HEADS].set(v)
    oc, s_c_new = _dn_call(zc_m, p['dn_conv_w'], decay_cols(p['dn_A_log']), decay_cols(p['dn_dt_bias']),
                           rep(p['dn_norm_g'], C_HEADS), s_c, cfg)
    if is_prompt:
        seq = 2048
        h_t = hs.reshape(n_seq, seq, B_W)[:, -1]
        buf_b = z_b.reshape(n_seq, seq, ZB_W)[:, -SAMPLE_HIST:, :B_W]
        buf_c = z_c.reshape(n_seq, seq, ZC_W)[:, -SAMPLE_HIST:, :3 * C_W]
    else:
        oa, ob, oc = _unpad_sample(oa), _unpad_sample(ob), _unpad_sample(oc)
        h_t = hs.reshape(n_seq, SAMPLE_SEQ_ROWS, B_W)[:, -1]
        buf_b = z_b.reshape(n_seq, 4, ZB_W)[:, 1:, :B_W]
        buf_c = z_c.reshape(n_seq, 4, ZC_W)[:, 1:, :3 * C_W]
    new_states = (jnp.swapaxes(s_a_new.reshape(n_seq, A_HEADS, HEAD_DIM, HEAD_DIM), -1, -2), h_t, buf_b,
                  s_c_new.reshape(n_seq, C_HEADS, HEAD_DIM, HEAD_DIM), buf_c)
    x1, h2b, q = _outproj_call(x, oa, ob, oc, mod, row1(p['norm2_g']), lw['wo'], lw['wq'], tiles_tok)
    x2 = _peer_call(h2b, q, x1, mod, row1(final_g), lw['k1p'], lw['k2p'], lw['u_b'], lw['v_b'], l,
                    tiles_peer, final=(l == DEPTH - 1))
    return x2, new_states


def kernel(x_prompt, x_sample, state_hgrn, state_lru_h, state_lru_conv, state_dn, state_dn_conv,
           c_prompt, c_sample, w_ada, b_ada, norm1_g, norm2_g, w_in, lb_param, a_norm_g,
           lru_conv_w, lru_conv_b, lru_wa, lru_ba, lru_wx, lru_bx, lru_L,
           dn_conv_w, dn_A_log, dn_dt_bias, dn_norm_g, w_out,
           peer_wq, peer_k1, peer_k2, peer_u, peer_v, final_norm_g):
    n_p, seq, _ = x_prompt.shape
    n_s, dec_seq, _ = x_sample.shape
    mod = _ada_call(jnp.concatenate([c_prompt, c_sample], axis=0), w_ada, b_ada)
    xp = x_prompt.reshape(n_p * seq, D_MODEL)
    xs = x_sample.reshape(n_s * dec_seq, D_MODEL)
    sample_states = (state_hgrn, state_lru_h, state_lru_conv, state_dn, state_dn_conv)
    u_all = peer_u.astype(BF16)
    vt_all = jnp.swapaxes(peer_v, 1, 2).astype(BF16)
    p_new, s_new = [], []
    for l in range(DEPTH):
        p = dict(norm1_g=norm1_g[l], norm2_g=norm2_g[l], lb_param=lb_param, a_norm_g=a_norm_g[l],
                 lru_conv_w=lru_conv_w[l], lru_conv_b=lru_conv_b[l], lru_ba=lru_ba[l], lru_bx=lru_bx[l],
                 lru_L=lru_L[l], dn_conv_w=dn_conv_w[l], dn_A_log=dn_A_log[l], dn_dt_bias=dn_dt_bias[l],
                 dn_norm_g=dn_norm_g[l])
        lw = _layer_weights(l, w_in, w_out, peer_wq, peer_k1, peer_k2, peer_u, peer_v, lru_wa, lru_wx)
        lw['u_b'], lw['v_b'] = u_all, vt_all
        mod_p = mod[l, :n_p].reshape(n_p, 1, 6 * D_MODEL)
        mod_s = jnp.repeat(mod[l, n_p:], dec_seq, axis=0).reshape(-1, TOKEN_TILE, 6 * D_MODEL)
        xp, st_p = _trunk(xp, mod_p, None, True, p, lw, l, final_norm_g)
        xs, st_s = _trunk(xs, mod_s, tuple(s[l] for s in sample_states), False, p, lw, l, final_norm_g)
        p_new.append(st_p)
        s_new.append(st_s)
    stack = lambda sts: [jnp.stack([s[i] for s in sts]) for i in range(5)]
    p_st = stack(p_new)
    s_st = stack(s_new)
    return (xp.reshape(n_p, seq, D_MODEL), xs.reshape(n_s, dec_seq, D_MODEL), *p_st, *s_st)
```

```python
import functools

import jax
import jax.numpy as jnp
import numpy as np
from jax import lax
from jax.experimental import pallas as pl
from jax.experimental.pallas import tpu as pltpu

F32 = jnp.float32
BF16 = jnp.bfloat16

D_MODEL = 1024
DEPTH = 2
HEAD_DIM = 64
A_HEADS = 4
A_W = 256
B_W = 384
C_HEADS = 6
C_W = 384
LRU_C = 8.0
CONV_W = 4
N_IN = 3340
N_IN_PAD = 3456
ZA_W = 1024
ZB_W = 768
ZC_W = 1664
PEER_HEADS = 8
PEER_NKEYS = 128
PEER_TOPK = 16
PEER_N = PEER_NKEYS * PEER_NKEYS
EPS = 1e-6

LANES = 128
SUPER_BLOCK = 256
SAMPLE_SEQ_ROWS = 8
SAMPLE_HIST = CONV_W - 1
TOKEN_TILE = 512
PEER_TOKEN_TILE = 512
PEER_EXPERT_BLOCK = 2048
PEER_SUB_BLOCK = 256
PEER_ROUTE_TILE = 256
VMEM_LIMIT_BYTES = 56 * 1024 * 1024
assert TOKEN_TILE == PEER_TOKEN_TILE

_NN = (((1,), (0,)), ((), ()))
_NT = (((1,), (1,)), ((), ()))
_TN = (((0,), (0,)), ((), ()))


def _dot(a, b, dims=_NN):
    return lax.dot_general(a, b, dims, preferred_element_type=F32)


def _split2(x):
    hi = x.astype(BF16)
    lo = (x - hi.astype(F32)).astype(BF16)
    return hi, lo


def _split3(x):
    hi = x.astype(BF16)
    r = x - hi.astype(F32)
    mid = r.astype(BF16)
    lo = (r - mid.astype(F32)).astype(BF16)
    return hi, mid, lo


def _dot_x3(a, b, dims=_NN):
    ah, al = _split2(a)
    bh, bl = _split2(b)
    return _dot(ah, bh, dims) + _dot(ah, bl, dims) + _dot(al, bh, dims)


def _dot_b(a, b, dims=_NN):
    return _dot(a.astype(BF16), b.astype(BF16), dims)


def _dot_sel(a, sel, dims=_NN, passes=3):
    parts = _split3(a)[:passes]
    out = _dot(parts[0], sel, dims)
    for p in parts[1:]:
        out = out + _dot(p, sel, dims)
    return out


def _head_sums(x, ones_ref):
    ones = ones_ref[0:LANES, 0:LANES]
    xb = x.astype(BF16)
    return jnp.concatenate([_dot(xb[:, t:t + LANES], ones) for t in range(0, x.shape[1], LANES)], axis=1)


def _sel_dot(sel, b, passes=3):
    parts = _split3(b)[:passes]
    out = _dot(sel, parts[0])
    for p in parts[1:]:
        out = out + _dot(sel, p)
    return out


def _softplus(x):
    return jnp.maximum(x, 0.0) + jnp.log(1.0 + jnp.exp(-jnp.abs(x)))


def _silu(x):
    return x * jax.nn.sigmoid(x)


def _gelu_x2(x):
    c = float(np.sqrt(2.0 / np.pi))
    return x + x * jnp.tanh(x * (c + (0.044715 * c) * (x * x)))


def _gelu(x):
    return 0.5 * x * (1.0 + jnp.tanh(float(np.sqrt(2.0 / np.pi)) * (x + 0.044715 * (x * x * x))))


def _rmsnorm(x, g):
    return x * lax.rsqrt(jnp.mean(x * x, axis=-1, keepdims=True) + EPS) * g


def _const_spec(shape):
    nd = len(shape)
    return pl.BlockSpec(shape, lambda *_: (0,) * nd)


def _params(n_grid):
    return pltpu.CompilerParams(dimension_semantics=("arbitrary",) * n_grid,
                                vmem_limit_bytes=VMEM_LIMIT_BYTES)


def _ada_kernel(c_ref, w_ref, b_ref, o_ref):
    c = c_ref[...]
    o_ref[0] = _dot_x3(_silu(c), w_ref[0]) + b_ref[0]


def _ada_call(c_all, w_ada, b_ada):
    nb = c_all.shape[0]
    nt = 6 * D_MODEL // 1024
    return pl.pallas_call(
        _ada_kernel,
        grid=(DEPTH, nt),
        in_specs=[
            _const_spec((nb, D_MODEL)),
            pl.BlockSpec((1, D_MODEL, 1024), lambda l, j: (l, 0, j)),
            pl.BlockSpec((1, 1, 1024), lambda l, j: (l, 0, j)),
        ],
        out_specs=pl.BlockSpec((1, nb, 1024), lambda l, j: (l, 0, j)),
        out_shape=jax.ShapeDtypeStruct((DEPTH, nb, 6 * D_MODEL), F32),
        compiler_params=_params(2),
        name="ada_mod",
    )(c_all, w_ada, b_ada.reshape(DEPTH, 1, 6 * D_MODEL))


def _inproj_kernel(x_ref, g_ref, sc_ref, sh_ref, w_ref, za_ref, zb_ref, zc_ref):
    h = (_rmsnorm(x_ref[...], g_ref[...]) * (1.0 + sc_ref[...]) + sh_ref[...]).astype(BF16)
    for o_ref, lo, hi in ((za_ref, 0, ZA_W), (zb_ref, ZA_W, ZA_W + ZB_W), (zc_ref, ZA_W + ZB_W, N_IN_PAD)):
        o_ref[...] = _dot(h, w_ref[:, lo:hi])


MOD_SH1, MOD_SC1, MOD_GT1, MOD_SH2, MOD_SC2, MOD_GT2 = range(6)


def _mod_spec(mod, tiles_per_group, chunk):
    _, r, _ = mod.shape
    return pl.BlockSpec((None, r, D_MODEL), lambda i: (i // tiles_per_group, 0, chunk))


def _inproj_call(x, g, mod, wh, tiles_per_group):
    t = x.shape[0]
    tm = TOKEN_TILE
    row = lambda w: pl.BlockSpec((tm, w), lambda i: (i, 0))
    return pl.pallas_call(
        _inproj_kernel,
        grid=(t // tm,),
        in_specs=[row(D_MODEL), _const_spec((1, D_MODEL)), _mod_spec(mod, tiles_per_group, MOD_SC1),
                  _mod_spec(mod, tiles_per_group, MOD_SH1), _const_spec(wh.shape)],
        out_specs=[row(ZA_W), row(ZB_W), row(ZC_W)],
        out_shape=[jax.ShapeDtypeStruct((t, ZA_W), F32), jax.ShapeDtypeStruct((t, ZB_W), F32),
                   jax.ShapeDtypeStruct((t, ZC_W), F32)],
        compiler_params=_params(1),
        name="inproj",
    )(x, g, mod, mod, wh)


class _SeqCfg:
    def __init__(self, seq_rows, hist, real, chunk, n_par):
        self.seq_rows = seq_rows
        self.n_par = n_par
        self.hist = hist
        self.real = real
        self.chunk = chunk
        self.masked = hist > 0 or hist + real < seq_rows
        self.n_chunks = SUPER_BLOCK // chunk
        self.long_seq = seq_rows > SUPER_BLOCK
        self.sb_per_seq = max(seq_rows // SUPER_BLOCK, 1)
        self.seq_per_sb = max(SUPER_BLOCK // seq_rows, 1)
        self.n_states = n_par if self.long_seq else self.seq_per_sb
        assert (self.long_seq or chunk == seq_rows) and (self.long_seq or n_par == 1)


PROMPT_CFG = _SeqCfg(seq_rows=2048, hist=0, real=2048, chunk=16, n_par=4)
PROMPT_HGRN_CFG = _SeqCfg(seq_rows=2048, hist=0, real=2048, chunk=16, n_par=8)
SAMPLE_CFG = _SeqCfg(seq_rows=SAMPLE_SEQ_ROWS, hist=SAMPLE_HIST, real=4, chunk=SAMPLE_SEQ_ROWS, n_par=1)


def _row_in_seq(cfg, n_rows):
    r = lax.broadcasted_iota(jnp.int32, (n_rows, 1), 0)
    return r % min(cfg.seq_rows, SUPER_BLOCK)


def _real_mask(cfg, n_rows):
    r = _row_in_seq(cfg, n_rows)
    return (r >= cfg.hist) & (r < cfg.hist + cfg.real)


def _causal_conv(x, xp_ref, w_ref):
    n = x.shape[0]
    xp_ref[8:8 + n, :] = x
    y = x * w_ref[CONV_W - 1:CONV_W, :]
    for k in range(CONV_W - 1):
        s = CONV_W - 1 - k
        y = y + xp_ref[8 - s:8 - s + n, :] * w_ref[k:k + 1, :]
    xp_ref[0:8, :] = x[n - 8:n, :]
    return y


def _lane_tiles(x):
    return [x[:, t:t + LANES] for t in range(0, x.shape[1], LANES)]


def _expand_state(s_cat, tile_ref, bd_ref):
    tile = tile_ref[:, 0:LANES]
    bd = bd_ref[0:LANES, 0:LANES]
    return jnp.stack([_dot_sel(s_cat[r:r + LANES, :], tile, passes=2) * bd
                      for r in range(0, s_cat.shape[0], LANES)])


def _compress_state(s_tiles, tile_t_ref):
    tile_t = tile_t_ref[0:LANES, :]
    return jnp.concatenate([_dot_sel(s_tiles[t], tile_t, passes=2) for t in range(s_tiles.shape[0])], axis=0)


def _state_matmul(x, s_tiles, dims):
    return jnp.concatenate([_dot_b(xt, s_tiles[t], dims) for t, xt in enumerate(_lane_tiles(x))], axis=1)


def _state_update(s_tiles, decay_row, a, b, bd_ref):
    bd = bd_ref[0:LANES, 0:LANES]
    return jnp.stack([s_tiles[t] * dt + bd * _dot_b(at, bt, _TN)
                      for t, (dt, at, bt) in enumerate(zip(_lane_tiles(decay_row), _lane_tiles(a), _lane_tiles(b)))])


def _head_norm_gate(o, z, g, ones_ref):
    ms = _dot_sel(o * o, ones_ref[...], passes=2) * (1.0 / HEAD_DIM)
    return o * lax.rsqrt(ms + EPS) * g * _silu(z)


def _run_chunks(cfg, sb, step, st_ref, s0_ref, sout_ref, tile_ref, tile_t_ref, bd_ref):
    if cfg.long_seq:
        @pl.when(sb % cfg.sb_per_seq == 0)
        def _():
            for p in range(cfg.n_par):
                st_ref[p] = _expand_state(s0_ref[p], tile_ref, bd_ref)

        def body(k, carry):
            for p in range(cfg.n_par):
                st_ref[p] = step(k, p, st_ref[p])
            return carry

        lax.fori_loop(0, cfg.n_chunks, body, 0, unroll=2)

        @pl.when(sb % cfg.sb_per_seq == cfg.sb_per_seq - 1)
        def _():
            for p in range(cfg.n_par):
                sout_ref[p] = _compress_state(st_ref[p], tile_t_ref)
    else:
        n_seq, rows, _ = s0_ref.shape
        bd = bd_ref[0:LANES, 0:LANES]
        wide = _dot_sel(s0_ref[...].reshape(n_seq * rows, HEAD_DIM), tile_ref[:, 0:LANES], passes=2)
        st_ref[...] = wide.reshape(n_seq, rows // LANES, LANES, LANES) * bd

        def body(k, carry):
            st_ref[k] = step(k, 0, st_ref[k])
            return carry

        lax.fori_loop(0, cfg.n_chunks, body, 0, unroll=4)
        narrow = _dot_sel(st_ref[...].reshape(n_seq * rows, LANES), tile_t_ref[0:LANES, :], passes=2)
        sout_ref[...] = narrow.reshape(n_seq, rows, HEAD_DIM)


def _hgrn_kernel(z_ref, lbp_ref, g_ref, s0_ref, ltri_ref, ones_ref, tile_ref, tile_t_ref, bd_ref,
                 o_ref, sout_ref,
                 st_ref, q_s, k_s, g_s, qe_s, kt_s, egl_s, o_s, *, cfg, layer):
    sb = pl.program_id(0)
    c = cfg.chunk
    n = SUPER_BLOCK
    nch = cfg.n_chunks
    w = A_W
    lbp = lbp_ref[...]
    e = jnp.exp(lbp - jnp.max(lbp, axis=0, keepdims=True))
    lbs = e / jnp.sum(e, axis=0, keepdims=True)
    lb = jnp.sum(lbs[0:layer + 1, :], axis=0, keepdims=True) - lbs[0:1, :]

    for p in range(cfg.n_par):
        r = slice(p * n, (p + 1) * n)
        aq = z_ref[p, :, 0:w]
        af = z_ref[p, :, w:2 * w]
        f = lb + (1.0 - lb) * jax.nn.sigmoid(af)
        lf = jnp.log(f)
        kk = 1.0 - f
        if cfg.masked:
            real = _real_mask(cfg, n)
            lf = jnp.where(real, lf, 0.0)
            kk = jnp.where(real, kk, 0.0)
        q = _silu(aq) * (HEAD_DIM ** -0.5)
        gcum = _sel_dot(ltri_ref[...], lf)
        g3 = gcum.reshape(nch, c, w)
        gl3 = g3[:, c - 1:c, :]
        q_s[r, :] = q
        k_s[r, :] = kk
        g_s[r, :] = gcum
        qe_s[r, :] = q * jnp.exp(gcum)
        kt_s[r, :] = (kk.reshape(nch, c, w) * jnp.exp(gl3 - g3)).reshape(n, w)
        egl_s[p * nch:(p + 1) * nch, :] = jnp.exp(gl3).reshape(nch, w)

    jio = lax.broadcasted_iota(jnp.int32, (c, 1), 0)
    bd = bd_ref[...]

    def chunk(k, p, st):
        rows = pl.ds(pl.multiple_of(p * n + k * c, c), c)
        q_c = q_s[rows, :]
        k_c = k_s[rows, :]
        g_c = g_s[rows, :]
        v_c = z_ref[p, pl.ds(pl.multiple_of(k * c, c), c), 2 * w:3 * w]
        o_inter = _state_matmul(qe_s[rows, :], st, _NT)
        lo = min(c, 8)
        d_rows = []
        for i in range(c):
            nj = lo if i < lo else c
            dec = jnp.exp(jnp.minimum(g_c[i:i + 1, :] - g_c[0:nj, :], 0.0))
            d_rows.append(jnp.where(jio[0:nj] <= i, k_c[0:nj, :] * dec * q_c[i:i + 1, :], 0.0))
        a_b = _head_sums(jnp.concatenate(d_rows, axis=0), ones_ref)
        o_diag = jnp.sum(a_b[0:lo * lo, :].reshape(lo, lo, w) * v_c[None, 0:lo, :], axis=1)
        if c > lo:
            o_hi = jnp.sum(a_b[lo * lo:, :].reshape(c - lo, c, w) * v_c[None, :, :], axis=1)
            o_diag = jnp.concatenate([o_diag, o_hi], axis=0)
        o_s[rows, :] = o_inter + o_diag
        return _state_update(st, egl_s[pl.ds(p * nch + k, 1), :], v_c, kt_s[rows, :], bd_ref)

    _run_chunks(cfg, sb, chunk, st_ref, s0_ref, sout_ref, tile_ref, tile_t_ref, bd_ref)
    for p in range(cfg.n_par):
        o_ref[p] = _head_norm_gate(o_s[p * n:(p + 1) * n, :], z_ref[p, :, 3 * w:4 * w], g_ref[...], ones_ref)


def _mixer_consts(heads, cfg):
    w = heads * HEAD_DIM
    lane_head = np.arange(w) // HEAD_DIM
    ones = (lane_head[:, None] == lane_head[None, :]).astype(np.float32)
    tile = (np.arange(HEAD_DIM)[:, None] == (np.arange(w) % HEAD_DIM)[None, :]).astype(np.float32)
    r = np.arange(SUPER_BLOCK)
    ltri = ((r[:, None] // cfg.chunk == r[None, :] // cfg.chunk) & (r[None, :] <= r[:, None])).astype(np.float32)
    return dict(ltri=jnp.asarray(ltri, BF16), ones=jnp.asarray(ones, BF16), tile=jnp.asarray(tile, BF16),
                tile_t=jnp.asarray(tile.T, BF16), bd=jnp.asarray(ones, F32))


def _state_specs(cfg, w):
    blk = (cfg.n_par if cfg.long_seq else cfg.seq_per_sb, w, HEAD_DIM)
    if cfg.long_seq:
        imap = lambda i: (i // cfg.sb_per_seq, 0, 0)
    else:
        imap = lambda i: (i, 0, 0)
    return pl.BlockSpec(blk, imap)


def _seq_view(x, cfg):
    if not cfg.long_seq:
        return x[None]
    assert x.shape[0] % (cfg.n_par * cfg.seq_rows) == 0, "sequence count must be a multiple of n_par"
    return x.reshape(-1, cfg.seq_rows, x.shape[-1])


def _seq_block_spec(cfg, width):
    if cfg.long_seq:
        return pl.BlockSpec((cfg.n_par, SUPER_BLOCK, width), lambda i: (i // cfg.sb_per_seq, i % cfg.sb_per_seq, 0))
    return pl.BlockSpec((1, SUPER_BLOCK, width), lambda i: (0, i, 0))


def _hgrn_call(z_a, lb_param, g_exp, s0_t, cfg, layer):
    rows = z_a.shape[0]
    w = A_W
    cst = _mixer_consts(A_HEADS, cfg)
    n_seq = s0_t.shape[0]
    par_rows = cfg.n_par * SUPER_BLOCK
    vm = lambda shape: pltpu.VMEM(shape, F32)
    kern = functools.partial(_hgrn_kernel, cfg=cfg, layer=layer)
    z3 = _seq_view(z_a, cfg)
    oa, s_new = pl.pallas_call(
        kern,
        grid=(rows // par_rows,),
        in_specs=[_seq_block_spec(cfg, ZA_W), _const_spec((DEPTH, w)),
                  _const_spec((1, w)), _state_specs(cfg, w), _const_spec(cst["ltri"].shape),
                  _const_spec(cst["ones"].shape), _const_spec(cst["tile"].shape),
                  _const_spec(cst["tile_t"].shape), _const_spec(cst["bd"].shape)],
        out_specs=[_seq_block_spec(cfg, w), _state_specs(cfg, w)],
        out_shape=[jax.ShapeDtypeStruct(z3.shape[:2] + (w,), F32), jax.ShapeDtypeStruct((n_seq, w, HEAD_DIM), F32)],
        scratch_shapes=[vm((cfg.n_states, w // LANES, LANES, LANES))] + [vm((par_rows, w))] * 5
        + [vm((cfg.n_par * cfg.n_chunks, w)), vm((par_rows, w))],
        compiler_params=_params(1),
        name="hgrn2",
    )(z3, lb_param, g_exp, s0_t, cst["ltri"], cst["ones"], cst["tile"], cst["tile_t"], cst["bd"])
    return oa.reshape(rows, w), s_new


def _lru_kernel(*refs, cfg):
    if cfg.long_seq:
        (z_ref, cw_ref, cb_ref, wg_ref, ba_ref, bx_ref, lp_ref,
         ob_ref, hs_ref, prev_ref, hc_ref) = refs
        inj_ref = None
    else:
        (z_ref, inj_ref, cw_ref, cb_ref, wg_ref, ba_ref, bx_ref, lp_ref,
         ob_ref, hs_ref, prev_ref, hc_ref) = refs
    sb = pl.program_id(0)
    n = SUPER_BLOCK
    w = B_W

    @pl.when(sb % cfg.sb_per_seq == 0)
    def _():
        prev_ref[0:8, :] = jnp.zeros((8, prev_ref.shape[1]), F32)
        hc_ref[...] = jnp.zeros_like(hc_ref)

    xc = _causal_conv(z_ref[:, 0:w], prev_ref, cw_ref) + cb_ref[...]
    gates = _dot_b(xc, wg_ref[...])
    r = jax.nn.sigmoid(gates[:, 0:w] + ba_ref[...])
    ig = jax.nn.sigmoid(gates[:, w:2 * w] + bx_ref[...])
    log_a = -LRU_C * r * _softplus(-lp_ref[...])
    a = jnp.exp(log_a)
    b = jnp.sqrt(1.0 - jnp.exp(2.0 * log_a)) * ig * xc
    if cfg.masked:
        real = _real_mask(cfg, n)
        a = jnp.where(real, a, 1.0)
        b = jnp.where(real, b, 0.0)
    if inj_ref is not None:
        b = b + inj_ref[...]
    group = 8
    assert cfg.seq_rows % group == 0
    rig = lax.broadcasted_iota(jnp.int32, (n, 1), 0) % group
    d = 1
    while d < group:
        has = rig >= d
        a_sh = jnp.where(has, pltpu.roll(a, d, 0), 1.0)
        b_sh = jnp.where(has, pltpu.roll(b, d, 0), 0.0)
        b = b + a * b_sh
        a = a * a_sh
        d *= 2
    if cfg.long_seq:
        a3 = a.reshape(n // group, group, w)
        b3 = b.reshape(n // group, group, w)
        carry = hc_ref[...]
        rows = []
        for i in range(n // group):
            h_i = b3[i] + a3[i] * carry
            rows.append(h_i)
            carry = h_i[group - 1:group, :]
        hs = jnp.concatenate(rows, axis=0)
        hc_ref[...] = carry
    else:
        hs = b
    hs_ref[...] = hs
    ob_ref[...] = hs * _gelu(z_ref[:, w:2 * w])


def _lru_call(z_b, inj, cw, cb, wg, ba, bx, lp, cfg):
    rows = z_b.shape[0]
    w = B_W
    row = lambda width: pl.BlockSpec((SUPER_BLOCK, width), lambda i: (i, 0))
    ins = [z_b] + ([] if cfg.long_seq else [inj]) + [cw, cb, wg, ba, bx, lp]
    specs = [row(ZB_W)] + ([] if cfg.long_seq else [row(w)]) + [_const_spec(a.shape) for a in ins[-6:]]
    return pl.pallas_call(
        functools.partial(_lru_kernel, cfg=cfg),
        grid=(rows // SUPER_BLOCK,),
        in_specs=specs,
        out_specs=[row(w), row(w)],
        out_shape=[jax.ShapeDtypeStruct((rows, w), F32)] * 2,
        scratch_shapes=[pltpu.VMEM((8 + SUPER_BLOCK, w), F32), pltpu.VMEM((1, w), F32)],
        compiler_params=_params(1),
        name="rglru",
    )(*ins)


def _dn_kernel(z_ref, cw_ref, alog_ref, dtb_ref, g_ref, s0_ref, ltri_ref, ones_ref, tile_ref, tile_t_ref,
               bd_ref, expb_ref, expa_ref,
               o_ref, sout_ref,
               st_ref, prev_ref, q_s, k_s, v_s, g_s, eg_s, beta_s, kt_s, egl_s, o_s, *, cfg):
    sb = pl.program_id(0)
    c = cfg.chunk
    n = SUPER_BLOCK
    nch = cfg.n_chunks
    w = C_W

    @pl.when(sb % cfg.sb_per_seq == 0)
    def _():
        for p in range(cfg.n_par):
            prev_ref[p, 0:8, :] = jnp.zeros((8, prev_ref.shape[2]), F32)

    ones = ones_ref[...]
    for p in range(cfg.n_par):
        r = slice(p * n, (p + 1) * n)
        qkv = _silu(_causal_conv(z_ref[p, :, 0:3 * w], prev_ref.at[p], cw_ref))
        q = qkv[:, 0:w]
        kx = qkv[:, w:2 * w]
        q = q * lax.rsqrt(_dot_sel(q * q, ones, passes=2) + EPS) * (HEAD_DIM ** -0.5)
        kx = kx * lax.rsqrt(_dot_sel(kx * kx, ones, passes=2) + EPS)
        pc = z_ref[p, :, 4 * w:4 * w + LANES]
        beta = _dot_sel(jax.nn.sigmoid(pc), expb_ref[...], passes=2)
        gdec = _dot_sel(-jnp.exp(alog_ref[...]) * _softplus(pc + dtb_ref[...]), expa_ref[...], passes=2)
        if cfg.masked:
            real = _real_mask(cfg, n)
            beta = jnp.where(real, beta, 0.0)
            gdec = jnp.where(real, gdec, 0.0)
        gcum = _sel_dot(ltri_ref[...], gdec)
        g3 = gcum.reshape(nch, c, w)
        gl3 = g3[:, c - 1:c, :]
        q_s[r, :] = q
        k_s[r, :] = kx
        v_s[r, :] = qkv[:, 2 * w:3 * w]
        g_s[r, :] = gcum
        eg_s[r, :] = jnp.exp(gcum)
        beta_s[r, :] = beta
        kt_s[r, :] = (kx.reshape(nch, c, w) * jnp.exp(gl3 - g3)).reshape(n, w)
        egl_s[p * nch:(p + 1) * nch, :] = jnp.exp(gl3).reshape(nch, w)

    iio = lax.broadcasted_iota(jnp.int32, (c, 1), 0)
    bd = bd_ref[...]

    def chunk(k, p, st):
        rows = pl.ds(pl.multiple_of(p * n + k * c, c), c)
        q_c = q_s[rows, :]
        k_c = k_s[rows, :]
        g_c = g_s[rows, :]
        eg_c = eg_s[rows, :]
        beta_c = beta_s[rows, :]
        qk_s = _state_matmul(jnp.concatenate([q_c, k_c], axis=0), st, _NN)
        q_st = qk_s[0:c, :]
        k_st = qk_s[c:2 * c, :]
        lo = min(c, 8)
        tail = lambda x: x[lo:c, :]
        d_rows = ([k_c * k_c[j:j + 1, :] for j in range(lo)] + [q_c * k_c[j:j + 1, :] for j in range(lo)]
                  + [tail(k_c) * k_c[j:j + 1, :] for j in range(lo, c)]
                  + [tail(q_c) * k_c[j:j + 1, :] for j in range(lo, c)])
        dots = _head_sums(jnp.concatenate(d_rows, axis=0), ones_ref)
        e = beta_c * (v_s[rows, :] - eg_c * k_st)
        o = eg_c * q_st
        for j in range(lo):
            dec = jnp.exp(jnp.minimum(g_c - g_c[j:j + 1, :], 0.0))
            m_col = jnp.where(iio > j, beta_c * dots[j * c:(j + 1) * c, :] * dec, 0.0)
            e_j = e[j:j + 1, :]
            e = e - m_col * e_j
            qk_col = jnp.where(iio >= j, dots[(lo + j) * c:(lo + j + 1) * c, :] * dec, 0.0)
            o = o + qk_col * e_j
        if c > lo:
            nt = c - lo
            e_t, o_t, g_t, beta_t = tail(e), tail(o), tail(g_c), tail(beta_c)
            base = 2 * lo * c
            for j in range(lo, c):
                dec = jnp.exp(jnp.minimum(g_t - g_c[j:j + 1, :], 0.0))
                kk = dots[base + (j - lo) * nt:base + (j - lo + 1) * nt, :]
                qk = dots[base + (nt + j - lo) * nt:base + (nt + j - lo + 1) * nt, :]
                e_j = e_t[j - lo:j - lo + 1, :]
                e_t = e_t - jnp.where(iio[0:nt] > j - lo, beta_t * kk * dec, 0.0) * e_j
                o_t = o_t + jnp.where(iio[0:nt] >= j - lo, qk * dec, 0.0) * e_j
            e = jnp.concatenate([e[0:lo, :], e_t], axis=0)
            o = jnp.concatenate([o[0:lo, :], o_t], axis=0)
        o_s[rows, :] = o
        return _state_update(st, egl_s[pl.ds(p * nch + k, 1), :], kt_s[rows, :], e, bd_ref)

    _run_chunks(cfg, sb, chunk, st_ref, s0_ref, sout_ref, tile_ref, tile_t_ref, bd_ref)
    for p in range(cfg.n_par):
        o_ref[p] = _head_norm_gate(o_s[p * n:(p + 1) * n, :], z_ref[p, :, 3 * w:4 * w], g_ref[...], ones_ref)


def _dn_call(z_c, cw, alog_exp, dtb_exp, g_exp, s0, cfg):
    rows = z_c.shape[0]
    w = C_W
    cst = _mixer_consts(C_HEADS, cfg)
    lane_head = np.arange(w) // HEAD_DIM
    expb = (np.arange(128)[:, None] == lane_head[None, :]).astype(np.float32)
    expa = (np.arange(128)[:, None] == (lane_head[None, :] + C_HEADS)).astype(np.float32)
    expb = jnp.asarray(expb, BF16)
    expa = jnp.asarray(expa, BF16)
    n_seq = s0.shape[0]
    par_rows = cfg.n_par * SUPER_BLOCK
    vm = lambda shape: pltpu.VMEM(shape, F32)
    z3 = _seq_view(z_c, cfg)
    oc, s_new = pl.pallas_call(
        functools.partial(_dn_kernel, cfg=cfg),
        grid=(rows // par_rows,),
        in_specs=[_seq_block_spec(cfg, ZC_W), _const_spec(cw.shape),
                  _const_spec((1, LANES)), _const_spec((1, LANES)), _const_spec((1, w)), _state_specs(cfg, w),
                  _const_spec(cst["ltri"].shape), _const_spec(cst["ones"].shape),
                  _const_spec(cst["tile"].shape), _const_spec(cst["tile_t"].shape),
                  _const_spec(cst["bd"].shape), _const_spec(expb.shape), _const_spec(expa.shape)],
        out_specs=[_seq_block_spec(cfg, w), _state_specs(cfg, w)],
        out_shape=[jax.ShapeDtypeStruct(z3.shape[:2] + (w,), F32), jax.ShapeDtypeStruct((n_seq, w, HEAD_DIM), F32)],
        scratch_shapes=[vm((cfg.n_states, w // LANES, LANES, LANES)), vm((cfg.n_par, 8 + SUPER_BLOCK, 3 * w))]
        + [vm((par_rows, w))] * 7
        + [vm((cfg.n_par * cfg.n_chunks, w)), vm((par_rows, w))],
        compiler_params=_params(1),
        name="deltanet",
    )(z3, cw, alog_exp, dtb_exp, g_exp, s0, cst["ltri"], cst["ones"], cst["tile"], cst["tile_t"],
      cst["bd"], expb, expa)
    return oc.reshape(rows, w), s_new


def _outproj_kernel(x_ref, oa_ref, ob_ref, oc_ref, gt_ref, g_ref, sc_ref, sh_ref,
                    wo_ref, wq_ref, x1_ref, h2_ref, q_ref):
    mo = _dot_b(oa_ref[...], wo_ref[0:A_W, :])
    mo = mo + _dot_b(ob_ref[...], wo_ref[A_W:A_W + B_W, :])
    mo = mo + _dot_b(oc_ref[...], wo_ref[A_W + B_W:, :])
    x1 = x_ref[...] + gt_ref[...] * mo
    x1_ref[...] = x1
    h2 = _rmsnorm(x1, g_ref[...]) * (1.0 + sc_ref[...]) + sh_ref[...]
    h2_ref[...] = h2.astype(BF16)
    q_ref[...] = _dot_b(h2, wq_ref[...])


def _outproj_call(x, oa, ob, oc, mod, g, wo, wq, tiles_per_group):
    t = x.shape[0]
    tm = TOKEN_TILE
    row = lambda w: pl.BlockSpec((tm, w), lambda i: (i, 0))
    return pl.pallas_call(
        _outproj_kernel,
        grid=(t // tm,),
        in_specs=[row(D_MODEL), row(A_W), row(B_W), row(C_W), _mod_spec(mod, tiles_per_group, MOD_GT1),
                  _const_spec((1, D_MODEL)), _mod_spec(mod, tiles_per_group, MOD_SC2),
                  _mod_spec(mod, tiles_per_group, MOD_SH2),
                  _const_spec(wo.shape), _const_spec(wq.shape)],
        out_specs=[row(D_MODEL), row(D_MODEL), row(D_MODEL)],
        out_shape=[jax.ShapeDtypeStruct((t, D_MODEL), F32), jax.ShapeDtypeStruct((t, D_MODEL), BF16),
                   jax.ShapeDtypeStruct((t, D_MODEL), F32)],
        compiler_params=_params(1),
        name="outproj",
    )(x, oa, ob, oc, mod, g, mod, mod, wo, wq)


def _top16(s):
    vals = []
    for r in range(PEER_TOPK):
        m = jnp.max(s, axis=0, keepdims=True)
        s = jnp.where(s == m, -(32.0 + r) * 2.0 ** 95, s)
        vals.append(m)
    rank = jnp.where(s <= -(2.0 ** 99), s * -(2.0 ** -95) - 32.0, float(PEER_TOPK))
    return jnp.concatenate(vals, axis=0), rank


def _route_tile(s1, s2):
    nk, t = s1.shape
    v1, rank1 = _top16(s1)
    v2, rank2 = _top16(s2)
    c3 = v1[:, None, :] + v2[None, :, :]
    jrow = lax.broadcasted_iota(jnp.int32, (8, 1), 0)
    parts = [v1[0:1, :] + v2, v1[1:2, :] + v2[0:8, :]]
    for i in range(2, 8):
        parts.append(jnp.where(jrow < PEER_TOPK // (i + 1), v1[i:i + 1, :] + v2[0:8, :], -jnp.inf))
    parts.append(v1[8:16, :] + v2[0:1, :])
    cand = jnp.concatenate(parts, axis=0)
    m = None
    for _ in range(PEER_TOPK):
        m = jnp.max(cand, axis=0, keepdims=True)
        cand = jnp.where(cand == m, -jnp.inf, cand)
    tau = m
    sel = c3 >= tau[None, :, :]
    m0 = v1[0:1, :] + v2[0:1, :]
    zsum = jnp.sum(jnp.where(sel, jnp.exp(c3 - m0[None, :, :]), 0.0).reshape(-1, t), axis=0, keepdims=True)
    n_i = jnp.sum(jnp.where(sel, 1.0, 0.0), axis=1)
    rank1 = rank1.astype(BF16)
    cnt = jnp.zeros((nk, t), BF16)
    for i in range(PEER_TOPK):
        cnt = cnt + jnp.where(rank1 == i, n_i[i:i + 1, :].astype(BF16), jnp.zeros((), BF16))
    return (cnt.astype(F32), jnp.exp(s1 - v1[0:1, :]), rank2.astype(BF16),
            (jnp.exp(s2 - v2[0:1, :]) * (0.5 / zsum)).astype(BF16))


def _peer_kernel(h2_ref, q_ref, x1_ref, gt_ref, fg_ref, k1_ref, k2_ref, u_ref, vt_ref, o_ref,
                 cnt_s, e1_s, rank2_s, e2_s, acc_s, s1_s, s2_s, coef_s, *, final, n_steps):
    g = pl.program_id(0)
    tt = PEER_TOKEN_TILE
    rt = PEER_ROUTE_TILE
    nk = PEER_NKEYS
    nj = PEER_N // PEER_EXPERT_BLOCK
    a_per_step = PEER_EXPERT_BLOCK // nk
    j = jnp.minimum(g, n_steps - 2) % nj
    gb = jnp.maximum(g - 1, 0)

    @pl.when(g == 0)
    def _():
        coef_s[...] = jnp.zeros_like(coef_s)

    @pl.when(gb % nj == 0)
    def _():
        acc_s[...] = jnp.zeros_like(acc_s)

    @pl.when(jnp.logical_and(g % nj == 0, g < n_steps - 1))
    def _():

        def head(h, carry):
            qh = q_ref[:, pl.ds(pl.multiple_of(h * nk, nk), nk)]
            s1_s[...] = _dot_x3(k1_ref[...], qh, _NT)
            s2_s[...] = _dot_x3(k2_ref[...], qh, _NT)

            def lane_tile(c, carry2):
                cols = pl.ds(pl.multiple_of(c * rt, rt), rt)
                (cnt_s[h, :, cols], e1_s[h, :, cols], rank2_s[h, :, cols],
                 e2_s[h, :, cols]) = _route_tile(s1_s[:, cols], s2_s[:, cols])
                return carry2

            lax.fori_loop(0, tt // rt, lane_tile, 0)
            return carry

        lax.fori_loop(0, PEER_HEADS, head, 0)

    h2 = h2_ref[...]
    n_sub = PEER_EXPERT_BLOCK // PEER_SUB_BLOCK
    a_per_sub = PEER_SUB_BLOCK // nk
    sub = lambda s: slice(s * PEER_SUB_BLOCK, (s + 1) * PEER_SUB_BLOCK)
    cur = coef_s.at[g % 2]
    prev = coef_s.at[(g + 1) % 2]
    hids = [_dot(u_ref[sub(s), :], h2, _NT).astype(BF16) for s in range(n_sub)]
    acc_s[...] += _dot(vt_ref[...], prev[...])
    pack = 16
    for s in range(n_sub):
        for al in range(a_per_sub):
            a = j * a_per_step + s * a_per_sub + al
            wsum = jnp.zeros((nk // pack, pack, tt), BF16)
            for h in range(PEER_HEADS):
                cnt_row = jnp.broadcast_to(cnt_s[h, pl.ds(a, 1), :], (pack, tt)).astype(BF16)
                e1_row = jnp.broadcast_to(e1_s[h, pl.ds(a, 1), :], (pack, tt)).astype(BF16)
                rank2 = rank2_s[h].reshape(nk // pack, pack, tt)
                e2 = e2_s[h].reshape(nk // pack, pack, tt)
                wsum = jnp.where(rank2 < cnt_row[None], wsum + e2 * e1_row[None], wsum)
            r0 = s * PEER_SUB_BLOCK + al * nk
            cur[r0:r0 + nk, :] = _gelu_x2(hids[s][al * nk:(al + 1) * nk, :]) * wsum.reshape(nk, tt)

    @pl.when(jnp.logical_and(g > 0, gb % nj == nj - 1))
    def _():
        x2 = x1_ref[...] + gt_ref[...] * acc_s[...].T
        if final:
            x2 = _rmsnorm(x2, fg_ref[...])
        o_ref[...] = x2


def _peer_call(h2b, q, x1, mod, fg, k1p, k2p, u_b, v_b, layer, tiles_per_group, final):
    t = h2b.shape[0]
    tt = PEER_TOKEN_TILE
    nb = PEER_EXPERT_BLOCK
    nj = PEER_N // nb
    n_tiles = t // tt
    n_steps = n_tiles * nj + 1
    front = lambda g: jnp.minimum(g, n_steps - 2)
    back = lambda g: jnp.maximum(g - 1, 0)
    row_f = lambda w: pl.BlockSpec((tt, w), lambda g: (front(g) // nj, 0))
    row_b = lambda w: pl.BlockSpec((tt, w), lambda g: (back(g) // nj, 0))
    r = mod.shape[1]
    tab = lambda dt: pltpu.VMEM((PEER_HEADS, PEER_NKEYS, tt), dt)
    return pl.pallas_call(
        functools.partial(_peer_kernel, final=final, n_steps=n_steps),
        grid=(n_steps,),
        in_specs=[row_f(D_MODEL), row_f(D_MODEL), row_b(D_MODEL),
                  pl.BlockSpec((None, r, D_MODEL), lambda g: (back(g) // nj // tiles_per_group, 0, MOD_GT2)),
                  _const_spec((1, D_MODEL)),
                  _const_spec((PEER_NKEYS, PEER_NKEYS)),
                  _const_spec((PEER_NKEYS, PEER_NKEYS)),
                  pl.BlockSpec((None, nb, D_MODEL), lambda g: (layer, front(g) % nj, 0)),
                  pl.BlockSpec((None, D_MODEL, nb), lambda g: (layer, 0, back(g) % nj))],
        out_specs=row_b(D_MODEL),
        out_shape=jax.ShapeDtypeStruct((t, D_MODEL), F32),
        scratch_shapes=[tab(F32), tab(F32), tab(BF16), tab(BF16), pltpu.VMEM((D_MODEL, tt), F32),
                        pltpu.VMEM((PEER_NKEYS, tt), F32), pltpu.VMEM((PEER_NKEYS, tt), F32),
                        pltpu.VMEM((2, nb, tt), BF16)],
        compiler_params=_params(1),
        name="peer",
    )(h2b, q, x1, mod, fg, k1p, k2p, u_b, v_b)


def _block_diag(wblk):
    n, d, e = wblk.shape
    eye = jnp.eye(n, dtype=wblk.dtype)
    return (eye[:, None, :, None] * wblk[:, :, None, :]).reshape(n * d, n * e)


def _pad_sample(z, hist=None):
    bsz = z.shape[0] // 4
    w = z.shape[1]
    z3 = z.reshape(bsz, 4, w)
    h3 = jnp.zeros((bsz, SAMPLE_HIST, w), F32)
    if hist is not None:
        h3 = h3.at[:, :, :hist.shape[-1]].set(hist)
    return jnp.concatenate([h3, z3, jnp.zeros((bsz, 1, w), F32)], axis=1).reshape(bsz * SAMPLE_SEQ_ROWS, w)


def _unpad_sample(o):
    bsz = o.shape[0] // SAMPLE_SEQ_ROWS
    return o.reshape(bsz, SAMPLE_SEQ_ROWS, -1)[:, SAMPLE_HIST:SAMPLE_HIST + 4].reshape(bsz * 4, -1)


def _layer_weights(l, w_in, w_out, peer_wq, peer_k1, peer_k2, peer_u, peer_v, lru_wa, lru_wx):
    w_in_p = jnp.pad(w_in[l], ((0, 0), (0, N_IN_PAD - N_IN)))
    wih = w_in_p.astype(BF16)
    wo = w_out[l].astype(BF16)
    wq = peer_wq[l].astype(BF16)
    wg = jnp.concatenate([_block_diag(lru_wa[l]), _block_diag(lru_wx[l])], axis=1).astype(BF16)
    half = PEER_NKEYS // 2
    k1p = jnp.pad(peer_k1[l], ((0, 0), (0, half)))
    k2p = jnp.pad(peer_k2[l], ((0, 0), (half, 0)))
    return dict(wih=wih, wo=wo, wq=wq, wg=wg, k1p=k1p, k2p=k2p)


def _trunk(x, mod, states, is_prompt, p, lw, l, final_g):
    t = x.shape[0]
    if is_prompt:
        cfg = PROMPT_CFG
        tiles_tok = 2048 // TOKEN_TILE
        tiles_peer = 2048 // PEER_TOKEN_TILE
        n_seq = t // 2048
    else:
        cfg = SAMPLE_CFG
        tiles_tok = 1
        tiles_peer = 1
        n_seq = t // 4
    row1 = lambda v: v.reshape(1, -1)
    z_a, z_b, z_c = _inproj_call(x, row1(p['norm1_g']), mod, lw['wih'], tiles_tok)
    if is_prompt:
        s_a = jnp.zeros((n_seq, A_W, HEAD_DIM), F32)
        s_c = jnp.zeros((n_seq, C_W, HEAD_DIM), F32)
        za_m, zb_m, zc_m = z_a, z_b, z_c
        inj = None
    else:
        st_hgrn, st_lru_h, st_lru_conv, st_dn, st_dn_conv = states
        s_a = jnp.swapaxes(st_hgrn, -1, -2).reshape(n_seq, A_W, HEAD_DIM)
        s_c = st_dn.reshape(n_seq, C_W, HEAD_DIM)
        za_m = _pad_sample(z_a)
        zb_m = _pad_sample(z_b, st_lru_conv)
        zc_m = _pad_sample(z_c, st_dn_conv)
        inj = jnp.zeros((n_seq, SAMPLE_SEQ_ROWS, B_W), F32).at[:, 0].set(st_lru_h).reshape(-1, B_W)
    rep = lambda v, h: row1(jnp.tile(v, h))
    oa, s_a_new = _hgrn_call(za_m, p['lb_param'], rep(p['a_norm_g'], A_HEADS), s_a,
                             PROMPT_HGRN_CFG if is_prompt else cfg, l)
    ob, hs = _lru_call(zb_m, inj, p['lru_conv_w'], row1(p['lru_conv_b']), lw['wg'],
                       row1(p['lru_ba']), row1(p['lru_bx']), row1(p['lru_L']), cfg)
    decay_cols = lambda v: jnp.zeros((1, LANES), F32).at[0, C_HEADS:2 * C_HEADS].set(v)
    oc, s_c_new = _dn_call(zc_m, p['dn_conv_w'], decay_cols(p['dn_A_log']), decay_cols(p['dn_dt_bias']),
                           rep(p['dn_norm_g'], C_HEADS), s_c, cfg)
    if is_prompt:
        seq = 2048
        h_t = hs.reshape(n_seq, seq, B_W)[:, -1]
        buf_b = z_b.reshape(n_seq, seq, ZB_W)[:, -SAMPLE_HIST:, :B_W]
        buf_c = z_c.reshape(n_seq, seq, ZC_W)[:, -SAMPLE_HIST:, :3 * C_W]
    else:
        oa, ob, oc = _unpad_sample(oa), _unpad_sample(ob), _unpad_sample(oc)
        h_t = hs.reshape(n_seq, SAMPLE_SEQ_ROWS, B_W)[:, -1]
        buf_b = z_b.reshape(n_seq, 4, ZB_W)[:, 1:, :B_W]
        buf_c = z_c.reshape(n_seq, 4, ZC_W)[:, 1:, :3 * C_W]
    new_states = (jnp.swapaxes(s_a_new.reshape(n_seq, A_HEADS, HEAD_DIM, HEAD_DIM), -1, -2), h_t, buf_b,
                  s_c_new.reshape(n_seq, C_HEADS, HEAD_DIM, HEAD_DIM), buf_c)
    x1, h2b, q = _outproj_call(x, oa, ob, oc, mod, row1(p['norm2_g']), lw['wo'], lw['wq'], tiles_tok)
    x2 = _peer_call(h2b, q, x1, mod, row1(final_g), lw['k1p'], lw['k2p'], lw['u_b'], lw['v_b'], l,
                    tiles_peer, final=(l == DEPTH - 1))
    return x2, new_states


def kernel(x_prompt, x_sample, state_hgrn, state_lru_h, state_lru_conv, state_dn, state_dn_conv,
           c_prompt, c_sample, w_ada, b_ada, norm1_g, norm2_g, w_in, lb_param, a_norm_g,
           lru_conv_w, lru_conv_b, lru_wa, lru_ba, lru_wx, lru_bx, lru_L,
           dn_conv_w, dn_A_log, dn_dt_bias, dn_norm_g, w_out,
           peer_wq, peer_k1, peer_k2, peer_u, peer_v, final_norm_g):
    n_p, seq, _ = x_prompt.shape
    n_s, dec_seq, _ = x_sample.shape
    mod = _ada_call(jnp.concatenate([c_prompt, c_sample], axis=0), w_ada, b_ada)
    xp = x_prompt.reshape(n_p * seq, D_MODEL)
    xs = x_sample.reshape(n_s * dec_seq, D_MODEL)
    sample_states = (state_hgrn, state_lru_h, state_lru_conv, state_dn, state_dn_conv)
    u_all = peer_u.astype(BF16)
    vt_all = jnp.swapaxes(peer_v, 1, 2).astype(BF16)
    p_new, s_new = [], []
    for l in range(DEPTH):
        p = dict(norm1_g=norm1_g[l], norm2_g=norm2_g[l], lb_param=lb_param, a_norm_g=a_norm_g[l],
                 lru_conv_w=lru_conv_w[l], lru_conv_b=lru_conv_b[l], lru_ba=lru_ba[l], lru_bx=lru_bx[l],
                 lru_L=lru_L[l], dn_conv_w=dn_conv_w[l], dn_A_log=dn_A_log[l], dn_dt_bias=dn_dt_bias[l],
                 dn_norm_g=dn_norm_g[l])
        lw = _layer_weights(l, w_in, w_out, peer_wq, peer_k1, peer_k2, peer_u, peer_v, lru_wa, lru_wx)
        lw['u_b'], lw['v_b'] = u_all, vt_all
        mod_p = mod[l, :n_p].reshape(n_p, 1, 6 * D_MODEL)
        mod_s = jnp.repeat(mod[l, n_p:], dec_seq, axis=0).reshape(-1, TOKEN_TILE, 6 * D_MODEL)
        xp, st_p = _trunk(xp, mod_p, None, True, p, lw, l, final_norm_g)
        xs, st_s = _trunk(xs, mod_s, tuple(s[l] for s in sample_states), False, p, lw, l, final_norm_g)
        p_new.append(st_p)
        s_new.append(st_s)
    stack = lambda sts: [jnp.stack([s[i] for s in sts]) for i in range(5)]
    p_st = stack(p_new)
    s_st = stack(s_new)
    return (xp.reshape(n_p, seq, D_MODEL), xs.reshape(n_s, dec_seq, D_MODEL), *p_st, *s_st)
```

```python
import functools

import jax
import jax.numpy as jnp
import numpy as np
from jax import lax
from jax.experimental import pallas as pl
from jax.experimental.pallas import tpu as pltpu

F32 = jnp.float32
BF16 = jnp.bfloat16

D_MODEL = 1024
DEPTH = 2
HEAD_DIM = 64
A_HEADS = 4
A_W = 256
B_W = 384
C_HEADS = 6
C_W = 384
LRU_C = 8.0
CONV_W = 4
N_IN = 3340
N_IN_PAD = 3456
ZA_W = 1024
ZB_W = 768
ZC_W = 1664
PEER_HEADS = 8
PEER_NKEYS = 128
PEER_TOPK = 16
PEER_N = PEER_NKEYS * PEER_NKEYS
EPS = 1e-6

LANES = 128
SUPER_BLOCK = 256
SAMPLE_SEQ_ROWS = 8
SAMPLE_HIST = CONV_W - 1
TOKEN_TILE = 512
PEER_TOKEN_TILE = 512
PEER_EXPERT_BLOCK = 2048
PEER_SUB_BLOCK = 256
PEER_ROUTE_TILE = 256
VMEM_LIMIT_BYTES = 56 * 1024 * 1024
assert TOKEN_TILE == PEER_TOKEN_TILE

_NN = (((1,), (0,)), ((), ()))
_NT = (((1,), (1,)), ((), ()))
_TN = (((0,), (0,)), ((), ()))


def _dot(a, b, dims=_NN):
    return lax.dot_general(a, b, dims, preferred_element_type=F32)


def _split2(x):
    hi = x.astype(BF16)
    lo = (x - hi.astype(F32)).astype(BF16)
    return hi, lo


def _split3(x):
    hi = x.astype(BF16)
    r = x - hi.astype(F32)
    mid = r.astype(BF16)
    lo = (r - mid.astype(F32)).astype(BF16)
    return hi, mid, lo


def _dot_x3(a, b, dims=_NN):
    ah, al = _split2(a)
    bh, bl = _split2(b)
    return _dot(ah, bh, dims) + _dot(ah, bl, dims) + _dot(al, bh, dims)


def _dot_b(a, b, dims=_NN):
    return _dot(a.astype(BF16), b.astype(BF16), dims)


def _dot_sel(a, sel, dims=_NN, passes=3):
    parts = _split3(a)[:passes]
    out = _dot(parts[0], sel, dims)
    for p in parts[1:]:
        out = out + _dot(p, sel, dims)
    return out


def _head_sums(x, ones_ref):
    ones = ones_ref[0:LANES, 0:LANES]
    xb = x.astype(BF16)
    return jnp.concatenate([_dot(xb[:, t:t + LANES], ones) for t in range(0, x.shape[1], LANES)], axis=1)


def _sel_dot(sel, b, passes=3):
    parts = _split3(b)[:passes]
    out = _dot(sel, parts[0])
    for p in parts[1:]:
        out = out + _dot(sel, p)
    return out


def _softplus(x):
    return jnp.maximum(x, 0.0) + jnp.log(1.0 + jnp.exp(-jnp.abs(x)))


def _silu(x):
    return x * jax.nn.sigmoid(x)


def _gelu_x2(x):
    c = float(np.sqrt(2.0 / np.pi))
    return x + x * jnp.tanh(x * (c + (0.044715 * c) * (x * x)))


def _gelu(x):
    return 0.5 * x * (1.0 + jnp.tanh(float(np.sqrt(2.0 / np.pi)) * (x + 0.044715 * (x * x * x))))


def _rmsnorm(x, g):
    return x * lax.rsqrt(jnp.mean(x * x, axis=-1, keepdims=True) + EPS) * g


def _const_spec(shape):
    nd = len(shape)
    return pl.BlockSpec(shape, lambda *_: (0,) * nd)


def _params(n_grid):
    return pltpu.CompilerParams(dimension_semantics=("arbitrary",) * n_grid,
                                vmem_limit_bytes=VMEM_LIMIT_BYTES)


def _ada_kernel(c_ref, w_ref, b_ref, o_ref):
    c = c_ref[...]
    o_ref[0] = _dot_x3(_silu(c), w_ref[0]) + b_ref[0]


def _ada_call(c_all, w_ada, b_ada):
    nb = c_all.shape[0]
    nt = 6 * D_MODEL // 1024
    return pl.pallas_call(
        _ada_kernel,
        grid=(DEPTH, nt),
        in_specs=[
            _const_spec((nb, D_MODEL)),
            pl.BlockSpec((1, D_MODEL, 1024), lambda l, j: (l, 0, j)),
            pl.BlockSpec((1, 1, 1024), lambda l, j: (l, 0, j)),
        ],
        out_specs=pl.BlockSpec((1, nb, 1024), lambda l, j: (l, 0, j)),
        out_shape=jax.ShapeDtypeStruct((DEPTH, nb, 6 * D_MODEL), F32),
        compiler_params=_params(2),
        name="ada_mod",
    )(c_all, w_ada, b_ada.reshape(DEPTH, 1, 6 * D_MODEL))


def _inproj_kernel(x_ref, g_ref, sc_ref, sh_ref, w_ref, za_ref, zb_ref, zc_ref):
    h = (_rmsnorm(x_ref[...], g_ref[...]) * (1.0 + sc_ref[...]) + sh_ref[...]).astype(BF16)
    for o_ref, lo, hi in ((za_ref, 0, ZA_W), (zb_ref, ZA_W, ZA_W + ZB_W), (zc_ref, ZA_W + ZB_W, N_IN_PAD)):
        o_ref[...] = _dot(h, w_ref[:, lo:hi])


MOD_SH1, MOD_SC1, MOD_GT1, MOD_SH2, MOD_SC2, MOD_GT2 = range(6)


def _mod_spec(mod, tiles_per_group, chunk):
    _, r, _ = mod.shape
    return pl.BlockSpec((None, r, D_MODEL), lambda i: (i // tiles_per_group, 0, chunk))


def _inproj_call(x, g, mod, wh, tiles_per_group):
    t = x.shape[0]
    tm = TOKEN_TILE
    row = lambda w: pl.BlockSpec((tm, w), lambda i: (i, 0))
    return pl.pallas_call(
        _inproj_kernel,
        grid=(t // tm,),
        in_specs=[row(D_MODEL), _const_spec((1, D_MODEL)), _mod_spec(mod, tiles_per_group, MOD_SC1),
                  _mod_spec(mod, tiles_per_group, MOD_SH1), _const_spec(wh.shape)],
        out_specs=[row(ZA_W), row(ZB_W), row(ZC_W)],
        out_shape=[jax.ShapeDtypeStruct((t, ZA_W), F32), jax.ShapeDtypeStruct((t, ZB_W), F32),
                   jax.ShapeDtypeStruct((t, ZC_W), F32)],
        compiler_params=_params(1),
        name="inproj",
    )(x, g, mod, mod, wh)


class _SeqCfg:
    def __init__(self, seq_rows, hist, real, chunk, n_par):
        self.seq_rows = seq_rows
        self.n_par = n_par
        self.hist = hist
        self.real = real
        self.chunk = chunk
        self.masked = hist > 0 or hist + real < seq_rows
        self.n_chunks = SUPER_BLOCK // chunk
        self.long_seq = seq_rows > SUPER_BLOCK
        self.sb_per_seq = max(seq_rows // SUPER_BLOCK, 1)
        self.seq_per_sb = max(SUPER_BLOCK // seq_rows, 1)
        self.n_states = n_par if self.long_seq else self.seq_per_sb
        assert (self.long_seq or chunk == seq_rows) and (self.long_seq or n_par == 1)


PROMPT_CFG = _SeqCfg(seq_rows=2048, hist=0, real=2048, chunk=16, n_par=4)
PROMPT_HGRN_CFG = _SeqCfg(seq_rows=2048, hist=0, real=2048, chunk=16, n_par=8)
SAMPLE_CFG = _SeqCfg(seq_rows=SAMPLE_SEQ_ROWS, hist=SAMPLE_HIST, real=4, chunk=SAMPLE_SEQ_ROWS, n_par=1)


def _row_in_seq(cfg, n_rows):
    r = lax.broadcasted_iota(jnp.int32, (n_rows, 1), 0)
    return r % min(cfg.seq_rows, SUPER_BLOCK)


def _real_mask(cfg, n_rows):
    r = _row_in_seq(cfg, n_rows)
    return (r >= cfg.hist) & (r < cfg.hist + cfg.real)


def _causal_conv(x, xp_ref, w_ref):
    n = x.shape[0]
    xp_ref[8:8 + n, :] = x
    y = x * w_ref[CONV_W - 1:CONV_W, :]
    for k in range(CONV_W - 1):
        s = CONV_W - 1 - k
        y = y + xp_ref[8 - s:8 - s + n, :] * w_ref[k:k + 1, :]
    xp_ref[0:8, :] = x[n - 8:n, :]
    return y


def _lane_tiles(x):
    return [x[:, t:t + LANES] for t in range(0, x.shape[1], LANES)]


def _expand_state(s_cat, tile_ref, bd_ref):
    tile = tile_ref[:, 0:LANES]
    bd = bd_ref[0:LANES, 0:LANES]
    return jnp.stack([_dot_sel(s_cat[r:r + LANES, :], tile, passes=2) * bd
                      for r in range(0, s_cat.shape[0], LANES)])


def _compress_state(s_tiles, tile_t_ref):
    tile_t = tile_t_ref[0:LANES, :]
    return jnp.concatenate([_dot_sel(s_tiles[t], tile_t, passes=2) for t in range(s_tiles.shape[0])], axis=0)


def _state_matmul(x, s_tiles, dims):
    return jnp.concatenate([_dot_b(xt, s_tiles[t], dims) for t, xt in enumerate(_lane_tiles(x))], axis=1)


def _state_update(s_tiles, decay_row, a, b, bd_ref):
    bd = bd_ref[0:LANES, 0:LANES]
    return jnp.stack([s_tiles[t] * dt + bd * _dot_b(at, bt, _TN)
                      for t, (dt, at, bt) in enumerate(zip(_lane_tiles(decay_row), _lane_tiles(a), _lane_tiles(b)))])


def _head_norm_gate(o, z, g, ones_ref):
    ms = _dot_sel(o * o, ones_ref[...], passes=2) * (1.0 / HEAD_DIM)
    return o * lax.rsqrt(ms + EPS) * g * _silu(z)


def _run_chunks(cfg, sb, step, st_ref, s0_ref, sout_ref, tile_ref, tile_t_ref, bd_ref):
    if cfg.long_seq:
        @pl.when(sb % cfg.sb_per_seq == 0)
        def _():
            for p in range(cfg.n_par):
                st_ref[p] = _expand_state(s0_ref[p], tile_ref, bd_ref)

        def body(k, carry):
            for p in range(cfg.n_par):
                st_ref[p] = step(k, p, st_ref[p])
            return carry

        lax.fori_loop(0, cfg.n_chunks, body, 0, unroll=2)

        @pl.when(sb % cfg.sb_per_seq == cfg.sb_per_seq - 1)
        def _():
            for p in range(cfg.n_par):
                sout_ref[p] = _compress_state(st_ref[p], tile_t_ref)
    else:
        n_seq, rows, _ = s0_ref.shape
        bd = bd_ref[0:LANES, 0:LANES]
        wide = _dot_sel(s0_ref[...].reshape(n_seq * rows, HEAD_DIM), tile_ref[:, 0:LANES], passes=2)
        st_ref[...] = wide.reshape(n_seq, rows // LANES, LANES, LANES) * bd

        def body(k, carry):
            st_ref[k] = step(k, 0, st_ref[k])
            return carry

        lax.fori_loop(0, cfg.n_chunks, body, 0, unroll=8)
        narrow = _dot_sel(st_ref[...].reshape(n_seq * rows, LANES), tile_t_ref[0:LANES, :], passes=2)
        sout_ref[...] = narrow.reshape(n_seq, rows, HEAD_DIM)


def _hgrn_kernel(z_ref, lbp_ref, g_ref, s0_ref, ltri_ref, ones_ref, tile_ref, tile_t_ref, bd_ref,
                 o_ref, sout_ref,
                 st_ref, q_s, k_s, g_s, qe_s, kt_s, egl_s, o_s, *, cfg, layer):
    sb = pl.program_id(0)
    c = cfg.chunk
    n = SUPER_BLOCK
    nch = cfg.n_chunks
    w = A_W
    lbp = lbp_ref[...]
    e = jnp.exp(lbp - jnp.max(lbp, axis=0, keepdims=True))
    lbs = e / jnp.sum(e, axis=0, keepdims=True)
    lb = jnp.sum(lbs[0:layer + 1, :], axis=0, keepdims=True) - lbs[0:1, :]

    for p in range(cfg.n_par):
        r = slice(p * n, (p + 1) * n)
        aq = z_ref[p, :, 0:w]
        af = z_ref[p, :, w:2 * w]
        f = lb + (1.0 - lb) * jax.nn.sigmoid(af)
        lf = jnp.log(f)
        kk = 1.0 - f
        if cfg.masked:
            real = _real_mask(cfg, n)
            lf = jnp.where(real, lf, 0.0)
            kk = jnp.where(real, kk, 0.0)
        q = _silu(aq) * (HEAD_DIM ** -0.5)
        gcum = _sel_dot(ltri_ref[...], lf)
        g3 = gcum.reshape(nch, c, w)
        gl3 = g3[:, c - 1:c, :]
        q_s[r, :] = q
        k_s[r, :] = kk
        g_s[r, :] = gcum
        qe_s[r, :] = q * jnp.exp(gcum)
        kt_s[r, :] = (kk.reshape(nch, c, w) * jnp.exp(gl3 - g3)).reshape(n, w)
        egl_s[p * nch:(p + 1) * nch, :] = jnp.exp(gl3).reshape(nch, w)

    jio = lax.broadcasted_iota(jnp.int32, (c, 1), 0)
    bd = bd_ref[...]

    def chunk(k, p, st):
        rows = pl.ds(pl.multiple_of(p * n + k * c, c), c)
        q_c = q_s[rows, :]
        k_c = k_s[rows, :]
        g_c = g_s[rows, :]
        v_c = z_ref[p, pl.ds(pl.multiple_of(k * c, c), c), 2 * w:3 * w]
        o_inter = _state_matmul(qe_s[rows, :], st, _NT)
        lo = min(c, 8)
        d_rows = []
        for i in range(c):
            nj = lo if i < lo else c
            dec = jnp.exp(jnp.minimum(g_c[i:i + 1, :] - g_c[0:nj, :], 0.0))
            d_rows.append(jnp.where(jio[0:nj] <= i, k_c[0:nj, :] * dec * q_c[i:i + 1, :], 0.0))
        a_b = _head_sums(jnp.concatenate(d_rows, axis=0), ones_ref)
        o_diag = jnp.sum(a_b[0:lo * lo, :].reshape(lo, lo, w) * v_c[None, 0:lo, :], axis=1)
        if c > lo:
            o_hi = jnp.sum(a_b[lo * lo:, :].reshape(c - lo, c, w) * v_c[None, :, :], axis=1)
            o_diag = jnp.concatenate([o_diag, o_hi], axis=0)
        o_s[rows, :] = o_inter + o_diag
        return _state_update(st, egl_s[pl.ds(p * nch + k, 1), :], v_c, kt_s[rows, :], bd_ref)

    _run_chunks(cfg, sb, chunk, st_ref, s0_ref, sout_ref, tile_ref, tile_t_ref, bd_ref)
    for p in range(cfg.n_par):
        o_ref[p] = _head_norm_gate(o_s[p * n:(p + 1) * n, :], z_ref[p, :, 3 * w:4 * w], g_ref[...], ones_ref)


def _mixer_consts(heads, cfg):
    w = heads * HEAD_DIM
    lane_head = np.arange(w) // HEAD_DIM
    ones = (lane_head[:, None] == lane_head[None, :]).astype(np.float32)
    tile = (np.arange(HEAD_DIM)[:, None] == (np.arange(w) % HEAD_DIM)[None, :]).astype(np.float32)
    r = np.arange(SUPER_BLOCK)
    ltri = ((r[:, None] // cfg.chunk == r[None, :] // cfg.chunk) & (r[None, :] <= r[:, None])).astype(np.float32)
    return dict(ltri=jnp.asarray(ltri, BF16), ones=jnp.asarray(ones, BF16), tile=jnp.asarray(tile, BF16),
                tile_t=jnp.asarray(tile.T, BF16), bd=jnp.asarray(ones, F32))


def _state_specs(cfg, w):
    blk = (cfg.n_par if cfg.long_seq else cfg.seq_per_sb, w, HEAD_DIM)
    if cfg.long_seq:
        imap = lambda i: (i // cfg.sb_per_seq, 0, 0)
    else:
        imap = lambda i: (i, 0, 0)
    return pl.BlockSpec(blk, imap)


def _seq_view(x, cfg):
    if not cfg.long_seq:
        return x[None]
    assert x.shape[0] % (cfg.n_par * cfg.seq_rows) == 0, "sequence count must be a multiple of n_par"
    return x.reshape(-1, cfg.seq_rows, x.shape[-1])


def _seq_block_spec(cfg, width):
    if cfg.long_seq:
        return pl.BlockSpec((cfg.n_par, SUPER_BLOCK, width), lambda i: (i // cfg.sb_per_seq, i % cfg.sb_per_seq, 0))
    return pl.BlockSpec((1, SUPER_BLOCK, width), lambda i: (0, i, 0))


def _hgrn_call(z_a, lb_param, g_exp, s0_t, cfg, layer):
    rows = z_a.shape[0]
    w = A_W
    cst = _mixer_consts(A_HEADS, cfg)
    n_seq = s0_t.shape[0]
    par_rows = cfg.n_par * SUPER_BLOCK
    vm = lambda shape: pltpu.VMEM(shape, F32)
    kern = functools.partial(_hgrn_kernel, cfg=cfg, layer=layer)
    z3 = _seq_view(z_a, cfg)
    oa, s_new = pl.pallas_call(
        kern,
        grid=(rows // par_rows,),
        in_specs=[_seq_block_spec(cfg, ZA_W), _const_spec((DEPTH, w)),
                  _const_spec((1, w)), _state_specs(cfg, w), _const_spec(cst["ltri"].shape),
                  _const_spec(cst["ones"].shape), _const_spec(cst["tile"].shape),
                  _const_spec(cst["tile_t"].shape), _const_spec(cst["bd"].shape)],
        out_specs=[_seq_block_spec(cfg, w), _state_specs(cfg, w)],
        out_shape=[jax.ShapeDtypeStruct(z3.shape[:2] + (w,), F32), jax.ShapeDtypeStruct((n_seq, w, HEAD_DIM), F32)],
        scratch_shapes=[vm((cfg.n_states, w // LANES, LANES, LANES))] + [vm((par_rows, w))] * 5
        + [vm((cfg.n_par * cfg.n_chunks, w)), vm((par_rows, w))],
        compiler_params=_params(1),
        name="hgrn2",
    )(z3, lb_param, g_exp, s0_t, cst["ltri"], cst["ones"], cst["tile"], cst["tile_t"], cst["bd"])
    return oa.reshape(rows, w), s_new


def _lru_kernel(*refs, cfg):
    if cfg.long_seq:
        (z_ref, cw_ref, cb_ref, wg_ref, ba_ref, bx_ref, lp_ref,
         ob_ref, hs_ref, prev_ref, hc_ref) = refs
        inj_ref = None
    else:
        (z_ref, inj_ref, cw_ref, cb_ref, wg_ref, ba_ref, bx_ref, lp_ref,
         ob_ref, hs_ref, prev_ref, hc_ref) = refs
    sb = pl.program_id(0)
    n = SUPER_BLOCK
    w = B_W

    @pl.when(sb % cfg.sb_per_seq == 0)
    def _():
        prev_ref[0:8, :] = jnp.zeros((8, prev_ref.shape[1]), F32)
        hc_ref[...] = jnp.zeros_like(hc_ref)

    xc = _causal_conv(z_ref[:, 0:w], prev_ref, cw_ref) + cb_ref[...]
    gates = _dot_b(xc, wg_ref[...])
    r = jax.nn.sigmoid(gates[:, 0:w] + ba_ref[...])
    ig = jax.nn.sigmoid(gates[:, w:2 * w] + bx_ref[...])
    log_a = -LRU_C * r * _softplus(-lp_ref[...])
    a = jnp.exp(log_a)
    b = jnp.sqrt(1.0 - jnp.exp(2.0 * log_a)) * ig * xc
    if cfg.masked:
        real = _real_mask(cfg, n)
        a = jnp.where(real, a, 1.0)
        b = jnp.where(real, b, 0.0)
    if inj_ref is not None:
        b = b + inj_ref[...]
    group = 8
    assert cfg.seq_rows % group == 0
    rig = lax.broadcasted_iota(jnp.int32, (n, 1), 0) % group
    d = 1
    while d < group:
        has = rig >= d
        a_sh = jnp.where(has, pltpu.roll(a, d, 0), 1.0)
        b_sh = jnp.where(has, pltpu.roll(b, d, 0), 0.0)
        b = b + a * b_sh
        a = a * a_sh
        d *= 2
    if cfg.long_seq:
        a3 = a.reshape(n // group, group, w)
        b3 = b.reshape(n // group, group, w)
        carry = hc_ref[...]
        rows = []
        for i in range(n // group):
            h_i = b3[i] + a3[i] * carry
            rows.append(h_i)
            carry = h_i[group - 1:group, :]
        hs = jnp.concatenate(rows, axis=0)
        hc_ref[...] = carry
    else:
        hs = b
    hs_ref[...] = hs
    ob_ref[...] = hs * _gelu(z_ref[:, w:2 * w])


def _lru_call(z_b, inj, cw, cb, wg, ba, bx, lp, cfg):
    rows = z_b.shape[0]
    w = B_W
    row = lambda width: pl.BlockSpec((SUPER_BLOCK, width), lambda i: (i, 0))
    ins = [z_b] + ([] if cfg.long_seq else [inj]) + [cw, cb, wg, ba, bx, lp]
    specs = [row(ZB_W)] + ([] if cfg.long_seq else [row(w)]) + [_const_spec(a.shape) for a in ins[-6:]]
    return pl.pallas_call(
        functools.partial(_lru_kernel, cfg=cfg),
        grid=(rows // SUPER_BLOCK,),
        in_specs=specs,
        out_specs=[row(w), row(w)],
        out_shape=[jax.ShapeDtypeStruct((rows, w), F32)] * 2,
        scratch_shapes=[pltpu.VMEM((8 + SUPER_BLOCK, w), F32), pltpu.VMEM((1, w), F32)],
        compiler_params=_params(1),
        name="rglru",
    )(*ins)


def _dn_kernel(z_ref, cw_ref, alog_ref, dtb_ref, g_ref, s0_ref, ltri_ref, ones_ref, tile_ref, tile_t_ref,
               bd_ref, expb_ref, expa_ref,
               o_ref, sout_ref,
               st_ref, prev_ref, q_s, k_s, v_s, g_s, eg_s, beta_s, kt_s, egl_s, o_s, *, cfg):
    sb = pl.program_id(0)
    c = cfg.chunk
    n = SUPER_BLOCK
    nch = cfg.n_chunks
    w = C_W

    @pl.when(sb % cfg.sb_per_seq == 0)
    def _():
        for p in range(cfg.n_par):
            prev_ref[p, 0:8, :] = jnp.zeros((8, prev_ref.shape[2]), F32)

    ones = ones_ref[...]
    for p in range(cfg.n_par):
        r = slice(p * n, (p + 1) * n)
        qkv = _silu(_causal_conv(z_ref[p, :, 0:3 * w], prev_ref.at[p], cw_ref))
        q = qkv[:, 0:w]
        kx = qkv[:, w:2 * w]
        q = q * lax.rsqrt(_dot_sel(q * q, ones, passes=2) + EPS) * (HEAD_DIM ** -0.5)
        kx = kx * lax.rsqrt(_dot_sel(kx * kx, ones, passes=2) + EPS)
        pc = z_ref[p, :, 4 * w:4 * w + LANES]
        beta = _dot_sel(jax.nn.sigmoid(pc), expb_ref[...], passes=2)
        gdec = _dot_sel(-jnp.exp(alog_ref[...]) * _softplus(pc + dtb_ref[...]), expa_ref[...], passes=2)
        if cfg.masked:
            real = _real_mask(cfg, n)
            beta = jnp.where(real, beta, 0.0)
            gdec = jnp.where(real, gdec, 0.0)
        gcum = _sel_dot(ltri_ref[...], gdec)
        g3 = gcum.reshape(nch, c, w)
        gl3 = g3[:, c - 1:c, :]
        q_s[r, :] = q
        k_s[r, :] = kx
        v_s[r, :] = qkv[:, 2 * w:3 * w]
        g_s[r, :] = gcum
        eg_s[r, :] = jnp.exp(gcum)
        beta_s[r, :] = beta
        kt_s[r, :] = (kx.reshape(nch, c, w) * jnp.exp(gl3 - g3)).reshape(n, w)
        egl_s[p * nch:(p + 1) * nch, :] = jnp.exp(gl3).reshape(nch, w)

    iio = lax.broadcasted_iota(jnp.int32, (c, 1), 0)
    bd = bd_ref[...]

    def chunk(k, p, st):
        rows = pl.ds(pl.multiple_of(p * n + k * c, c), c)
        q_c = q_s[rows, :]
        k_c = k_s[rows, :]
        g_c = g_s[rows, :]
        eg_c = eg_s[rows, :]
        beta_c = beta_s[rows, :]
        qk_s = _state_matmul(jnp.concatenate([q_c, k_c], axis=0), st, _NN)
        q_st = qk_s[0:c, :]
        k_st = qk_s[c:2 * c, :]
        lo = min(c, 8)
        tail = lambda x: x[lo:c, :]
        d_rows = ([k_c * k_c[j:j + 1, :] for j in range(lo)] + [q_c * k_c[j:j + 1, :] for j in range(lo)]
                  + [tail(k_c) * k_c[j:j + 1, :] for j in range(lo, c)]
                  + [tail(q_c) * k_c[j:j + 1, :] for j in range(lo, c)])
        dots = _head_sums(jnp.concatenate(d_rows, axis=0), ones_ref)
        e = beta_c * (v_s[rows, :] - eg_c * k_st)
        o = eg_c * q_st
        for j in range(lo):
            dec = jnp.exp(jnp.minimum(g_c - g_c[j:j + 1, :], 0.0))
            m_col = jnp.where(iio > j, beta_c * dots[j * c:(j + 1) * c, :] * dec, 0.0)
            e_j = e[j:j + 1, :]
            e = e - m_col * e_j
            qk_col = jnp.where(iio >= j, dots[(lo + j) * c:(lo + j + 1) * c, :] * dec, 0.0)
            o = o + qk_col * e_j
        if c > lo:
            nt = c - lo
            e_t, o_t, g_t, beta_t = tail(e), tail(o), tail(g_c), tail(beta_c)
            base = 2 * lo * c
            for j in range(lo, c):
                dec = jnp.exp(jnp.minimum(g_t - g_c[j:j + 1, :], 0.0))
                kk = dots[base + (j - lo) * nt:base + (j - lo + 1) * nt, :]
                qk = dots[base + (nt + j - lo) * nt:base + (nt + j - lo + 1) * nt, :]
                e_j = e_t[j - lo:j - lo + 1, :]
                e_t = e_t - jnp.where(iio[0:nt] > j - lo, beta_t * kk * dec, 0.0) * e_j
                o_t = o_t + jnp.where(iio[0:nt] >= j - lo, qk * dec, 0.0) * e_j
            e = jnp.concatenate([e[0:lo, :], e_t], axis=0)
            o = jnp.concatenate([o[0:lo, :], o_t], axis=0)
        o_s[rows, :] = o
        return _state_update(st, egl_s[pl.ds(p * nch + k, 1), :], kt_s[rows, :], e, bd_ref)

    _run_chunks(cfg, sb, chunk, st_ref, s0_ref, sout_ref, tile_ref, tile_t_ref, bd_ref)
    for p in range(cfg.n_par):
        o_ref[p] = _head_norm_gate(o_s[p * n:(p + 1) * n, :], z_ref[p, :, 3 * w:4 * w], g_ref[...], ones_ref)


def _dn_call(z_c, cw, alog_exp, dtb_exp, g_exp, s0, cfg):
    rows = z_c.shape[0]
    w = C_W
    cst = _mixer_consts(C_HEADS, cfg)
    lane_head = np.arange(w) // HEAD_DIM
    expb = (np.arange(128)[:, None] == lane_head[None, :]).astype(np.float32)
    expa = (np.arange(128)[:, None] == (lane_head[None, :] + C_HEADS)).astype(np.float32)
    expb = jnp.asarray(expb, BF16)
    expa = jnp.asarray(expa, BF16)
    n_seq = s0.shape[0]
    par_rows = cfg.n_par * SUPER_BLOCK
    vm = lambda shape: pltpu.VMEM(shape, F32)
    z3 = _seq_view(z_c, cfg)
    oc, s_new = pl.pallas_call(
        functools.partial(_dn_kernel, cfg=cfg),
        grid=(rows // par_rows,),
        in_specs=[_seq_block_spec(cfg, ZC_W), _const_spec(cw.shape),
                  _const_spec((1, LANES)), _const_spec((1, LANES)), _const_spec((1, w)), _state_specs(cfg, w),
                  _const_spec(cst["ltri"].shape), _const_spec(cst["ones"].shape),
                  _const_spec(cst["tile"].shape), _const_spec(cst["tile_t"].shape),
                  _const_spec(cst["bd"].shape), _const_spec(expb.shape), _const_spec(expa.shape)],
        out_specs=[_seq_block_spec(cfg, w), _state_specs(cfg, w)],
        out_shape=[jax.ShapeDtypeStruct(z3.shape[:2] + (w,), F32), jax.ShapeDtypeStruct((n_seq, w, HEAD_DIM), F32)],
        scratch_shapes=[vm((cfg.n_states, w // LANES, LANES, LANES)), vm((cfg.n_par, 8 + SUPER_BLOCK, 3 * w))]
        + [vm((par_rows, w))] * 7
        + [vm((cfg.n_par * cfg.n_chunks, w)), vm((par_rows, w))],
        compiler_params=_params(1),
        name="deltanet",
    )(z3, cw, alog_exp, dtb_exp, g_exp, s0, cst["ltri"], cst["ones"], cst["tile"], cst["tile_t"],
      cst["bd"], expb, expa)
    return oc.reshape(rows, w), s_new


def _outproj_kernel(x_ref, oa_ref, ob_ref, oc_ref, gt_ref, g_ref, sc_ref, sh_ref,
                    wo_ref, wq_ref, x1_ref, h2_ref, q_ref):
    mo = _dot_b(oa_ref[...], wo_ref[0:A_W, :])
    mo = mo + _dot_b(ob_ref[...], wo_ref[A_W:A_W + B_W, :])
    mo = mo + _dot_b(oc_ref[...], wo_ref[A_W + B_W:, :])
    x1 = x_ref[...] + gt_ref[...] * mo
    x1_ref[...] = x1
    h2 = _rmsnorm(x1, g_ref[...]) * (1.0 + sc_ref[...]) + sh_ref[...]
    h2_ref[...] = h2.astype(BF16)
    q_ref[...] = _dot_b(h2, wq_ref[...])


def _outproj_call(x, oa, ob, oc, mod, g, wo, wq, tiles_per_group):
    t = x.shape[0]
    tm = TOKEN_TILE
    row = lambda w: pl.BlockSpec((tm, w), lambda i: (i, 0))
    return pl.pallas_call(
        _outproj_kernel,
        grid=(t // tm,),
        in_specs=[row(D_MODEL), row(A_W), row(B_W), row(C_W), _mod_spec(mod, tiles_per_group, MOD_GT1),
                  _const_spec((1, D_MODEL)), _mod_spec(mod, tiles_per_group, MOD_SC2),
                  _mod_spec(mod, tiles_per_group, MOD_SH2),
                  _const_spec(wo.shape), _const_spec(wq.shape)],
        out_specs=[row(D_MODEL), row(D_MODEL), row(D_MODEL)],
        out_shape=[jax.ShapeDtypeStruct((t, D_MODEL), F32), jax.ShapeDtypeStruct((t, D_MODEL), BF16),
                   jax.ShapeDtypeStruct((t, D_MODEL), F32)],
        compiler_params=_params(1),
        name="outproj",
    )(x, oa, ob, oc, mod, g, mod, mod, wo, wq)


def _top16(s):
    vals = []
    for r in range(PEER_TOPK):
        m = jnp.max(s, axis=0, keepdims=True)
        s = jnp.where(s == m, -(32.0 + r) * 2.0 ** 95, s)
        vals.append(m)
    rank = jnp.where(s <= -(2.0 ** 99), s * -(2.0 ** -95) - 32.0, float(PEER_TOPK))
    return jnp.concatenate(vals, axis=0), rank


def _route_tile(s1, s2):
    nk, t = s1.shape
    v1, rank1 = _top16(s1)
    v2, rank2 = _top16(s2)
    c3 = v1[:, None, :] + v2[None, :, :]
    jrow = lax.broadcasted_iota(jnp.int32, (8, 1), 0)
    parts = [v1[0:1, :] + v2, v1[1:2, :] + v2[0:8, :]]
    for i in range(2, 8):
        parts.append(jnp.where(jrow < PEER_TOPK // (i + 1), v1[i:i + 1, :] + v2[0:8, :], -jnp.inf))
    parts.append(v1[8:16, :] + v2[0:1, :])
    cand = jnp.concatenate(parts, axis=0)
    m = None
    for _ in range(PEER_TOPK):
        m = jnp.max(cand, axis=0, keepdims=True)
        cand = jnp.where(cand == m, -jnp.inf, cand)
    tau = m
    sel = c3 >= tau[None, :, :]
    m0 = v1[0:1, :] + v2[0:1, :]
    zsum = jnp.sum(jnp.where(sel, jnp.exp(c3 - m0[None, :, :]), 0.0).reshape(-1, t), axis=0, keepdims=True)
    n_i = jnp.sum(jnp.where(sel, 1.0, 0.0), axis=1)
    rank1 = rank1.astype(BF16)
    cnt = jnp.zeros((nk, t), BF16)
    for i in range(PEER_TOPK):
        cnt = cnt + jnp.where(rank1 == i, n_i[i:i + 1, :].astype(BF16), jnp.zeros((), BF16))
    return (cnt.astype(F32), jnp.exp(s1 - v1[0:1, :]), rank2.astype(BF16),
            (jnp.exp(s2 - v2[0:1, :]) * (0.5 / zsum)).astype(BF16))


def _peer_kernel(h2_ref, q_ref, x1_ref, gt_ref, fg_ref, k1_ref, k2_ref, u_ref, vt_ref, o_ref,
                 cnt_s, e1_s, rank2_s, e2_s, acc_s, s1_s, s2_s, coef_s, *, final, n_steps):
    g = pl.program_id(0)
    tt = PEER_TOKEN_TILE
    rt = PEER_ROUTE_TILE
    nk = PEER_NKEYS
    nj = PEER_N // PEER_EXPERT_BLOCK
    a_per_step = PEER_EXPERT_BLOCK // nk
    j = jnp.minimum(g, n_steps - 2) % nj
    gb = jnp.maximum(g - 1, 0)

    @pl.when(g == 0)
    def _():
        coef_s[...] = jnp.zeros_like(coef_s)

    @pl.when(gb % nj == 0)
    def _():
        acc_s[...] = jnp.zeros_like(acc_s)

    @pl.when(jnp.logical_and(g % nj == 0, g < n_steps - 1))
    def _():

        def head(h, carry):
            qh = q_ref[:, pl.ds(pl.multiple_of(h * nk, nk), nk)]
            s1_s[...] = _dot_x3(k1_ref[...], qh, _NT)
            s2_s[...] = _dot_x3(k2_ref[...], qh, _NT)

            def lane_tile(c, carry2):
                cols = pl.ds(pl.multiple_of(c * rt, rt), rt)
                (cnt_s[h, :, cols], e1_s[h, :, cols], rank2_s[h, :, cols],
                 e2_s[h, :, cols]) = _route_tile(s1_s[:, cols], s2_s[:, cols])
                return carry2

            lax.fori_loop(0, tt // rt, lane_tile, 0)
            return carry

        lax.fori_loop(0, PEER_HEADS, head, 0)

    h2 = h2_ref[...]
    n_sub = PEER_EXPERT_BLOCK // PEER_SUB_BLOCK
    a_per_sub = PEER_SUB_BLOCK // nk
    sub = lambda s: slice(s * PEER_SUB_BLOCK, (s + 1) * PEER_SUB_BLOCK)
    cur = coef_s.at[g % 2]
    prev = coef_s.at[(g + 1) % 2]
    hids = [_dot(u_ref[sub(s), :], h2, _NT).astype(BF16) for s in range(n_sub)]
    acc_s[...] += _dot(vt_ref[...], prev[...])
    pack = 16
    for s in range(n_sub):
        for al in range(a_per_sub):
            a = j * a_per_step + s * a_per_sub + al
            wsum = jnp.zeros((nk // pack, pack, tt), BF16)
            for h in range(PEER_HEADS):
                cnt_row = jnp.broadcast_to(cnt_s[h, pl.ds(a, 1), :], (pack, tt)).astype(BF16)
                e1_row = jnp.broadcast_to(e1_s[h, pl.ds(a, 1), :], (pack, tt)).astype(BF16)
                rank2 = rank2_s[h].reshape(nk // pack, pack, tt)
                e2 = e2_s[h].reshape(nk // pack, pack, tt)
                wsum = jnp.where(rank2 < cnt_row[None], wsum + e2 * e1_row[None], wsum)
            r0 = s * PEER_SUB_BLOCK + al * nk
            cur[r0:r0 + nk, :] = _gelu_x2(hids[s][al * nk:(al + 1) * nk, :]) * wsum.reshape(nk, tt)

    @pl.when(jnp.logical_and(g > 0, gb % nj == nj - 1))
    def _():
        x2 = x1_ref[...] + gt_ref[...] * acc_s[...].T
        if final:
            x2 = _rmsnorm(x2, fg_ref[...])
        o_ref[...] = x2


def _peer_call(h2b, q, x1, mod, fg, k1p, k2p, u_b, v_b, layer, tiles_per_group, final):
    t = h2b.shape[0]
    tt = PEER_TOKEN_TILE
    nb = PEER_EXPERT_BLOCK
    nj = PEER_N // nb
    n_tiles = t // tt
    n_steps = n_tiles * nj + 1
    front = lambda g: jnp.minimum(g, n_steps - 2)
    back = lambda g: jnp.maximum(g - 1, 0)
    row_f = lambda w: pl.BlockSpec((tt, w), lambda g: (front(g) // nj, 0))
    row_b = lambda w: pl.BlockSpec((tt, w), lambda g: (back(g) // nj, 0))
    r = mod.shape[1]
    tab = lambda dt: pltpu.VMEM((PEER_HEADS, PEER_NKEYS, tt), dt)
    return pl.pallas_call(
        functools.partial(_peer_kernel, final=final, n_steps=n_steps),
        grid=(n_steps,),
        in_specs=[row_f(D_MODEL), row_f(D_MODEL), row_b(D_MODEL),
                  pl.BlockSpec((None, r, D_MODEL), lambda g: (back(g) // nj // tiles_per_group, 0, MOD_GT2)),
                  _const_spec((1, D_MODEL)),
                  _const_spec((PEER_NKEYS, PEER_NKEYS)),
                  _const_spec((PEER_NKEYS, PEER_NKEYS)),
                  pl.BlockSpec((None, nb, D_MODEL), lambda g: (layer, front(g) % nj, 0)),
                  pl.BlockSpec((None, D_MODEL, nb), lambda g: (layer, 0, back(g) % nj))],
        out_specs=row_b(D_MODEL),
        out_shape=jax.ShapeDtypeStruct((t, D_MODEL), F32),
        scratch_shapes=[tab(F32), tab(F32), tab(BF16), tab(BF16), pltpu.VMEM((D_MODEL, tt), F32),
                        pltpu.VMEM((PEER_NKEYS, tt), F32), pltpu.VMEM((PEER_NKEYS, tt), F32),
                        pltpu.VMEM((2, nb, tt), BF16)],
        compiler_params=_params(1),
        name="peer",
    )(h2b, q, x1, mod, fg, k1p, k2p, u_b, v_b)


def _block_diag(wblk):
    n, d, e = wblk.shape
    eye = jnp.eye(n, dtype=wblk.dtype)
    return (eye[:, None, :, None] * wblk[:, :, None, :]).reshape(n * d, n * e)


def _pad_sample(z, hist=None):
    bsz = z.shape[0] // 4
    w = z.shape[1]
    z3 = z.reshape(bsz, 4, w)
    h3 = jnp.zeros((bsz, SAMPLE_HIST, w), F32)
    if hist is not None:
        h3 = h3.at[:, :, :hist.shape[-1]].set(hist)
    return jnp.concatenate([h3, z3, jnp.zeros((bsz, 1, w), F32)], axis=1).reshape(bsz * SAMPLE_SEQ_ROWS, w)


def _unpad_sample(o):
    bsz = o.shape[0] // SAMPLE_SEQ_ROWS
    return o.reshape(bsz, SAMPLE_SEQ_ROWS, -1)[:, SAMPLE_HIST:SAMPLE_HIST + 4].reshape(bsz * 4, -1)


def _layer_weights(l, w_in, w_out, peer_wq, peer_k1, peer_k2, peer_u, peer_v, lru_wa, lru_wx):
    w_in_p = jnp.pad(w_in[l], ((0, 0), (0, N_IN_PAD - N_IN)))
    wih = w_in_p.astype(BF16)
    wo = w_out[l].astype(BF16)
    wq = peer_wq[l].astype(BF16)
    wg = jnp.concatenate([_block_diag(lru_wa[l]), _block_diag(lru_wx[l])], axis=1).astype(BF16)
    half = PEER_NKEYS // 2
    k1p = jnp.pad(peer_k1[l], ((0, 0), (0, half)))
    k2p = jnp.pad(peer_k2[l], ((0, 0), (half, 0)))
    return dict(wih=wih, wo=wo, wq=wq, wg=wg, k1p=k1p, k2p=k2p)


def _trunk(x, mod, states, is_prompt, p, lw, l, final_g):
    t = x.shape[0]
    if is_prompt:
        cfg = PROMPT_CFG
        tiles_tok = 2048 // TOKEN_TILE
        tiles_peer = 2048 // PEER_TOKEN_TILE
        n_seq = t // 2048
    else:
        cfg = SAMPLE_CFG
        tiles_tok = 1
        tiles_peer = 1
        n_seq = t // 4
    row1 = lambda v: v.reshape(1, -1)
    z_a, z_b, z_c = _inproj_call(x, row1(p['norm1_g']), mod, lw['wih'], tiles_tok)
    if is_prompt:
        s_a = jnp.zeros((n_seq, A_W, HEAD_DIM), F32)
        s_c = jnp.zeros((n_seq, C_W, HEAD_DIM), F32)
        za_m, zb_m, zc_m = z_a, z_b, z_c
        inj = None
    else:
        st_hgrn, st_lru_h, st_lru_conv, st_dn, st_dn_conv = states
        s_a = jnp.swapaxes(st_hgrn, -1, -2).reshape(n_seq, A_W, HEAD_DIM)
        s_c = st_dn.reshape(n_seq, C_W, HEAD_DIM)
        za_m = _pad_sample(z_a)
        zb_m = _pad_sample(z_b, st_lru_conv)
        zc_m = _pad_sample(z_c, st_dn_conv)
        inj = jnp.zeros((n_seq, SAMPLE_SEQ_ROWS, B_W), F32).at[:, 0].set(st_lru_h).reshape(-1, B_W)
    rep = lambda v, h: row1(jnp.tile(v, h))
    oa, s_a_new = _hgrn_call(za_m, p['lb_param'], rep(p['a_norm_g'], A_HEADS), s_a,
                             PROMPT_HGRN_CFG if is_prompt else cfg, l)
    ob, hs = _lru_call(zb_m, inj, p['lru_conv_w'], row1(p['lru_conv_b']), lw['wg'],
                       row1(p['lru_ba']), row1(p['lru_bx']), row1(p['lru_L']), cfg)
    decay_cols = lambda v: jnp.zeros((1, LANES), F32).at[0, C_HEADS:2 * C_HEADS].set(v)
    oc, s_c_new = _dn_call(zc_m, p['dn_conv_w'], decay_cols(p['dn_A_log']), decay_cols(p['dn_dt_bias']),
                           rep(p['dn_norm_g'], C_HEADS), s_c, cfg)
    if is_prompt:
        seq = 2048
        h_t = hs.reshape(n_seq, seq, B_W)[:, -1]
        buf_b = z_b.reshape(n_seq, seq, ZB_W)[:, -SAMPLE_HIST:, :B_W]
        buf_c = z_c.reshape(n_seq, seq, ZC_W)[:, -SAMPLE_HIST:, :3 * C_W]
    else:
        oa, ob, oc = _unpad_sample(oa), _unpad_sample(ob), _unpad_sample(oc)
        h_t = hs.reshape(n_seq, SAMPLE_SEQ_ROWS, B_W)[:, -1]
        buf_b = z_b.reshape(n_seq, 4, ZB_W)[:, 1:, :B_W]
        buf_c = z_c.reshape(n_seq, 4, ZC_W)[:, 1:, :3 * C_W]
    new_states = (jnp.swapaxes(s_a_new.reshape(n_seq, A_HEADS, HEAD_DIM, HEAD_DIM), -1, -2), h_t, buf_b,
                  s_c_new.reshape(n_seq, C_HEADS, HEAD_DIM, HEAD_DIM), buf_c)
    x1, h2b, q = _outproj_call(x, oa, ob, oc, mod, row1(p['norm2_g']), lw['wo'], lw['wq'], tiles_tok)
    x2 = _peer_call(h2b, q, x1, mod, row1(final_g), lw['k1p'], lw['k2p'], lw['u_b'], lw['v_b'], l,
                    tiles_peer, final=(l == DEPTH - 1))
    return x2, new_states


def kernel(x_prompt, x_sample, state_hgrn, state_lru_h, state_lru_conv, state_dn, state_dn_conv,
           c_prompt, c_sample, w_ada, b_ada, norm1_g, norm2_g, w_in, lb_param, a_norm_g,
           lru_conv_w, lru_conv_b, lru_wa, lru_ba, lru_wx, lru_bx, lru_L,
           dn_conv_w, dn_A_log, dn_dt_bias, dn_norm_g, w_out,
           peer_wq, peer_k1, peer_k2, peer_u, peer_v, final_norm_g):
    n_p, seq, _ = x_prompt.shape
    n_s, dec_seq, _ = x_sample.shape
    mod = _ada_call(jnp.concatenate([c_prompt, c_sample], axis=0), w_ada, b_ada)
    xp = x_prompt.reshape(n_p * seq, D_MODEL)
    xs = x_sample.reshape(n_s * dec_seq, D_MODEL)
    sample_states = (state_hgrn, state_lru_h, state_lru_conv, state_dn, state_dn_conv)
    u_all = peer_u.astype(BF16)
    vt_all = jnp.swapaxes(peer_v, 1, 2).astype(BF16)
    p_new, s_new = [], []
    for l in range(DEPTH):
        p = dict(norm1_g=norm1_g[l], norm2_g=norm2_g[l], lb_param=lb_param, a_norm_g=a_norm_g[l],
                 lru_conv_w=lru_conv_w[l], lru_conv_b=lru_conv_b[l], lru_ba=lru_ba[l], lru_bx=lru_bx[l],
                 lru_L=lru_L[l], dn_conv_w=dn_conv_w[l], dn_A_log=dn_A_log[l], dn_dt_bias=dn_dt_bias[l],
                 dn_norm_g=dn_norm_g[l])
        lw = _layer_weights(l, w_in, w_out, peer_wq, peer_k1, peer_k2, peer_u, peer_v, lru_wa, lru_wx)
        lw['u_b'], lw['v_b'] = u_all, vt_all
        mod_p = mod[l, :n_p].reshape(n_p, 1, 6 * D_MODEL)
        mod_s = jnp.repeat(mod[l, n_p:], dec_seq, axis=0).reshape(-1, TOKEN_TILE, 6 * D_MODEL)
        xp, st_p = _trunk(xp, mod_p, None, True, p, lw, l, final_norm_g)
        xs, st_s = _trunk(xs, mod_s, tuple(s[l] for s in sample_states), False, p, lw, l, final_norm_g)
        p_new.append(st_p)
        s_new.append(st_s)
    stack = lambda sts: [jnp.stack([s[i] for s in sts]) for i in range(5)]
    p_st = stack(p_new)
    s_st = stack(s_new)
    return (xp.reshape(n_p, seq, D_MODEL), xs.reshape(n_s, dec_seq, D_MODEL), *p_st, *s_st)
```
